```python
import jax, jax.numpy as jnp
from jax import lax
import numpy as np

D_MODEL = 1024
BATCH = 2
SEQ = 8192
DEPTH = 4

ATTN_HEADS = 8
ATTN_HEAD_DIM = 64
ATTN_WIDTH = ATTN_HEADS * ATTN_HEAD_DIM
IDX_HEADS = 4
IDX_HEAD_DIM = 64
MAX_TOPK = 256
Q_BLOCK = 128
HGRN_HEADS = 4
HGRN_KEY_DIM = 128
HGRN_VAL_DIM = 128
HGRN_F_WIDTH = HGRN_HEADS * HGRN_KEY_DIM
HGRN_WIDTH = HGRN_HEADS * HGRN_VAL_DIM
HGRN_CHUNK = 64
D_FF = -(-8 * D_MODEL // (3 * 256)) * 256
ROPE_THETA = 10000.0
EPS = 1e-6
IN_WIDTHS = (ATTN_WIDTH, ATTN_WIDTH, ATTN_WIDTH,
             IDX_HEADS * IDX_HEAD_DIM, IDX_HEAD_DIM, IDX_HEADS,
             HGRN_F_WIDTH, HGRN_F_WIDTH, HGRN_WIDTH, HGRN_WIDTH,
             D_MODEL, D_MODEL)
D_IN = sum(IN_WIDTHS)

kernel_name = "hybrid_dsa_hgrn2_gated_block"


def rmsnorm(x, g):
    xf = x.astype(jnp.float32)
    xf = xf * lax.rsqrt(jnp.mean(xf * xf, axis=-1, keepdims=True) + EPS)
    return xf.astype(x.dtype) * g


def rope_tables(positions, dim):
    inv = ROPE_THETA ** (-jnp.arange(0, dim, 2, dtype=jnp.float32) / dim)
    ang = positions.astype(jnp.float32)[..., None] * inv
    return jnp.cos(ang)[:, :, None, :], jnp.sin(ang)[:, :, None, :]


def apply_rope(x, cos, sin):
    half = x.shape[-1] // 2
    x1 = x[..., :half].astype(jnp.float32)
    x2 = x[..., half:].astype(jnp.float32)
    return jnp.concatenate([x1 * cos - x2 * sin, x2 * cos + x1 * sin], axis=-1).astype(x.dtype)


def dsa_attention(q, k, v, qi, ki, wi):
    B, T, H, Dh = q.shape
    topk = min(MAX_TOPK, T // 4)
    nb = T // Q_BLOCK
    scale = Dh ** -0.5
    key_pos = jnp.arange(T)
    ki32 = ki.astype(jnp.float32)
    gather = jax.vmap(lambda a, i: a[i])

    def to_blocks(a):
        return a.reshape((B, nb, Q_BLOCK) + a.shape[2:]).swapaxes(0, 1)

    def block_fn(args):
        qb, qib, wib, start = args
        qpos = start + jnp.arange(Q_BLOCK)
        visible = key_pos[None, :] <= qpos[:, None]
        rel = jax.nn.relu(jnp.einsum('bqhd,bsd->bqhs', qib.astype(jnp.float32), ki32))
        score = jnp.einsum('bqh,bqhs->bqs', wib.astype(jnp.float32), rel)
        score = jnp.where(visible[None], score, -jnp.inf)
        _, idx = lax.top_k(score, topk)
        k_sel = gather(k, idx).astype(jnp.float32)
        v_sel = gather(v, idx).astype(jnp.float32)
        logits = jnp.einsum('bqhd,bqkhd->bhqk', qb.astype(jnp.float32), k_sel) * scale
        valid = idx <= qpos[None, :, None]
        logits = jnp.where(valid[:, None], logits, -jnp.inf)
        p = jax.nn.softmax(logits, axis=-1)
        return jnp.einsum('bhqk,bqkhd->bqhd', p, v_sel).astype(q.dtype)

    starts = jnp.arange(nb) * Q_BLOCK
    out = lax.map(block_fn, (to_blocks(q), to_blocks(qi), to_blocks(wi), starts))
    return out.swapaxes(0, 1).reshape(B, T, H * Dh)


def hgrn2_chunked(q, k, v, log_f):
    B, T, H, dk = q.shape
    dv = v.shape[-1]
    C = HGRN_CHUNK
    nc = T // C

    def chunks(a):
        return a.reshape(B, nc, C, H, a.shape[-1]).transpose(1, 0, 3, 2, 4)

    b = jnp.cumsum(chunks(log_f), axis=3)
    causal = jnp.tril(jnp.ones((C, C), dtype=bool))

    def step(S, xs):
        qc, kc, vc, bc = xs
        inter = jnp.einsum('bhtd,bhde->bhte', qc * jnp.exp(bc), S)
        diff = bc[:, :, :, None, :] - bc[:, :, None, :, :]
        decay = jnp.exp(jnp.where(causal[:, :, None], diff, -jnp.inf))
        A = jnp.einsum('bhtd,bhsd,bhtsd->bhts', qc, kc, decay)
        intra = jnp.einsum('bhts,bhse->bhte', A, vc)
        b_last = bc[:, :, -1:, :]
        S = jnp.exp(b_last[:, :, 0, :])[..., None] * S + jnp.einsum(
            'bhsd,bhse->bhde', kc * jnp.exp(b_last - bc), vc)
        return S, inter + intra

    S0 = jnp.zeros((B, H, dk, dv), jnp.float32)
    _, o = lax.scan(step, S0, (chunks(q), chunks(k), chunks(v), b))
    return o.transpose(1, 0, 3, 2, 4).reshape(B, T, H, dv)


def setup_inputs(seed: int = 0) -> dict:
    key = jax.random.key(seed)
    ks = jax.random.split(key, 16)
    f32 = jnp.float32
    x = jax.random.normal(ks[0], (BATCH, SEQ, D_MODEL), f32)
    offset = jax.random.randint(ks[1], (BATCH, 1), 0, 4096, dtype=jnp.int32)
    positions = offset + jnp.arange(SEQ, dtype=jnp.int32)[None, :]
    res_scale = (2 * DEPTH) ** -0.5
    w_in = jax.random.normal(ks[2], (DEPTH, D_MODEL, D_IN), f32) * D_MODEL ** -0.5
    w_proj_attn = jax.random.normal(ks[3], (DEPTH, ATTN_WIDTH, D_MODEL), f32) * ATTN_WIDTH ** -0.5
    w_proj_hgrn = jax.random.normal(ks[4], (DEPTH, HGRN_WIDTH, D_MODEL), f32) * HGRN_WIDTH ** -0.5
    w_out = jax.random.normal(ks[5], (DEPTH, D_MODEL, D_MODEL), f32) * D_MODEL ** -0.5 * res_scale
    norm_mix = 1.0 + 0.02 * jax.random.normal(ks[6], (DEPTH, D_MODEL), f32)
    norm_ffn = 1.0 + 0.02 * jax.random.normal(ks[7], (DEPTH, D_MODEL), f32)
    q_norm = 1.0 + 0.02 * jax.random.normal(ks[8], (DEPTH, ATTN_HEAD_DIM), f32)
    k_norm = 1.0 + 0.02 * jax.random.normal(ks[9], (DEPTH, ATTN_HEAD_DIM), f32)
    hgrn_norm = 1.0 + 0.02 * jax.random.normal(ks[10], (DEPTH, HGRN_VAL_DIM), f32)
    hgrn_lower_bound = 0.5 * jax.random.normal(ks[11], (DEPTH, HGRN_F_WIDTH), f32)
    w_ffn_in = jax.random.normal(ks[12], (DEPTH, D_MODEL, 2 * D_FF), f32) * D_MODEL ** -0.5
    w_ffn_out = jax.random.normal(ks[13], (DEPTH, D_FF, D_MODEL), f32) * D_FF ** -0.5 * res_scale
    return {"x": x, "positions": positions, "w_in": w_in, "w_proj_attn": w_proj_attn,
            "w_proj_hgrn": w_proj_hgrn, "w_out": w_out, "norm_mix": norm_mix, "norm_ffn": norm_ffn,
            "q_norm": q_norm, "k_norm": k_norm, "hgrn_norm": hgrn_norm,
            "hgrn_lower_bound": hgrn_lower_bound, "w_ffn_in": w_ffn_in, "w_ffn_out": w_ffn_out}


def reference(x, positions, w_in, w_proj_attn, w_proj_hgrn, w_out, norm_mix, norm_ffn,
              q_norm, k_norm, hgrn_norm, hgrn_lower_bound, w_ffn_in, w_ffn_out):
    B, T, _ = x.shape
    cos, sin = rope_tables(positions, ATTN_HEAD_DIM)
    split_points = list(np.cumsum(IN_WIDTHS)[:-1])
    lb_all = jnp.cumsum(jax.nn.softmax(hgrn_lower_bound.astype(jnp.float32), axis=0), axis=0)
    lb_all = lb_all - lb_all[:1]
    for l in range(DEPTH):
        h = rmsnorm(x, norm_mix[l])
        proj = h @ w_in[l]
        aq, ak, av, iq, ik, iw, hq, hf, hi, hg, ga, gb = jnp.split(proj, split_points, axis=-1)
        aq = apply_rope(rmsnorm(aq.reshape(B, T, ATTN_HEADS, ATTN_HEAD_DIM), q_norm[l]), cos, sin)
        ak = apply_rope(rmsnorm(ak.reshape(B, T, ATTN_HEADS, ATTN_HEAD_DIM), k_norm[l]), cos, sin)
        av = av.reshape(B, T, ATTN_HEADS, ATTN_HEAD_DIM)
        iq = apply_rope(iq.reshape(B, T, IDX_HEADS, IDX_HEAD_DIM), cos, sin) * (IDX_HEAD_DIM ** -0.5)
        ik = apply_rope(ik[:, :, None, :], cos, sin)[:, :, 0, :]
        iw = iw * (IDX_HEADS ** -0.5)
        y_attn = dsa_attention(aq, ak, av, iq, ik, iw)
        lb = lb_all[l].reshape(HGRN_HEADS, HGRN_KEY_DIM)
        hf32 = hf.astype(jnp.float32).reshape(B, T, HGRN_HEADS, HGRN_KEY_DIM)
        log_f = jnp.logaddexp(jnp.log(lb), jnp.log1p(-lb) + jax.nn.log_sigmoid(hf32))
        k_in = -jnp.expm1(log_f)
        q_h = jax.nn.silu(hq.astype(jnp.float32)).reshape(B, T, HGRN_HEADS, HGRN_KEY_DIM)
        v_h = hi.astype(jnp.float32).reshape(B, T, HGRN_HEADS, HGRN_VAL_DIM)
        o = hgrn2_chunked(q_h, k_in, v_h, log_f).astype(x.dtype)
        o = rmsnorm(o, hgrn_norm[l]) * jax.nn.silu(hg.reshape(B, T, HGRN_HEADS, HGRN_VAL_DIM))
        y_hgrn = o.reshape(B, T, HGRN_WIDTH)
        merged = jax.nn.sigmoid(ga) * (y_attn @ w_proj_attn[l]) + jax.nn.sigmoid(gb) * (y_hgrn @ w_proj_hgrn[l])
        x = x + merged @ w_out[l]
        h = rmsnorm(x, norm_ffn[l])
        g, u = jnp.split(h @ w_ffn_in[l], 2, axis=-1)
        x = x + (jax.nn.silu(g) * u) @ w_ffn_out[l]
    return x
```

```python
import functools

import numpy as np
import jax
import jax.numpy as jnp
from jax import lax
from jax.experimental import pallas as pl
from jax.experimental.pallas import tpu as pltpu

D_MODEL = 1024
ATTN_HEADS = 8
HEAD_DIM = 64
ATTN_WIDTH = ATTN_HEADS * HEAD_DIM
IDX_HEADS = 4
IDX_DIM = 64
MAX_TOPK = 256
HG_HEADS = 4
HG_DIM = 128
HG_WIDTH = HG_HEADS * HG_DIM
HG_CHUNK = 64
HG_SUB = 16
D_FF = 2816
ROPE_THETA = 10000.0
EPS = 1e-6

LANES = 128
VMEM_LIMIT_BYTES = 56 * 1024 * 1024

KEY_BLOCK = 256
INT_MIN = -(2 ** 31)
NEG_BIG = -1e30

F32 = jnp.float32
BF16 = jnp.bfloat16
HIGHEST = lax.Precision.HIGHEST


def _resident(shape):
    nd = len(shape)
    return pl.BlockSpec(shape, lambda *_: (0,) * nd, pipeline_mode=pl.Buffered(1))


def _params(n_axes):
    return pltpu.CompilerParams(dimension_semantics=("arbitrary",) * n_axes,
                                vmem_limit_bytes=VMEM_LIMIT_BYTES)


def _rmsnorm_rows(x, gain):
    return x * lax.rsqrt(jnp.mean(x * x, axis=-1, keepdims=True) + EPS) * gain


def _sigmoid(x):
    return 1.0 / (1.0 + jnp.exp(-x))


def _rope(x, cos, sin_signed):
    w = x.shape[-1]
    lane = lax.broadcasted_iota(jnp.int32, x.shape, 1)
    first_half = (lane & (HEAD_DIM - 1)) < HEAD_DIM // 2
    partner = jnp.where(first_half, pltpu.roll(x, w - HEAD_DIM // 2, 1), pltpu.roll(x, HEAD_DIM // 2, 1))
    return x * cos + partner * sin_signed


def _mix_in_kernel(x_ref, g_ref, wa_ref, wi_ref, wh_ref, qn_ref, kn_ref, cos_ref, sin_ref, hsum_ref, hexp_ref,
                   q_ref, k_ref, v_ref, iq_ref, ikw_ref, hg_ref):
    h = _rmsnorm_rows(x_ref[...], g_ref[...])
    hb = h.astype(BF16)
    cos1 = cos_ref[...]
    sin1 = sin_ref[...]
    cos4 = jnp.concatenate([cos1] * 4, axis=1)
    sin4 = jnp.concatenate([sin1] * 4, axis=1)

    def head_norm(a, gain):
        ss = jnp.dot(a * a, hsum_ref[...], precision=HIGHEST, preferred_element_type=F32)
        r = lax.rsqrt(ss * (1.0 / HEAD_DIM) + EPS)
        rf = jnp.dot(r, hexp_ref[...], precision=HIGHEST, preferred_element_type=F32)
        return a * rf * gain

    pa = jnp.dot(hb, wa_ref[...], preferred_element_type=F32)
    q = _rope(head_norm(pa[:, :ATTN_WIDTH], qn_ref[...]), cos4, sin4) * (HEAD_DIM ** -0.5)
    k = _rope(head_norm(pa[:, ATTN_WIDTH:2 * ATTN_WIDTH], kn_ref[...]), cos4, sin4)
    q_ref[...] = q.astype(BF16)
    k_ref[...] = k.astype(BF16)
    v_ref[...] = pa[:, 2 * ATTN_WIDTH:].astype(BF16)

    pi = jnp.dot(h, wi_ref[...], precision=HIGHEST, preferred_element_type=F32)
    nq = IDX_HEADS * IDX_DIM
    iq_ref[...] = _rope(pi[:, :nq], cos4[:, :nq], sin4[:, :nq]) * (IDX_DIM ** -0.5)
    slab = pi[:, nq:]
    lane = lax.broadcasted_iota(jnp.int32, slab.shape, 1)
    ikw_ref[...] = jnp.where(lane < IDX_DIM, _rope(slab, cos1, sin1), slab * (IDX_HEADS ** -0.5))

    hg_ref[...] = jnp.dot(hb, wh_ref[...], preferred_element_type=F32)


def _mix_in(x2, gain, wa, wi, wh, qn, kn, cos1, sin1, hsum, hexp, tm):
    nt = x2.shape[0]
    row = lambda w: pl.BlockSpec((tm, w), lambda i: (i, 0))
    return pl.pallas_call(
        _mix_in_kernel,
        grid=(nt // tm,),
        in_specs=[row(D_MODEL), _resident(gain.shape), _resident(wa.shape), _resident(wi.shape), _resident(wh.shape),
                  _resident(qn.shape), _resident(kn.shape), row(LANES), row(LANES),
                  _resident(hsum.shape), _resident(hexp.shape)],
        out_specs=[row(ATTN_WIDTH), row(ATTN_WIDTH), row(ATTN_WIDTH), row(IDX_HEADS * IDX_DIM), row(LANES),
                   row(4 * HG_WIDTH)],
        out_shape=[jax.ShapeDtypeStruct((nt, ATTN_WIDTH), BF16)] * 3
        + [jax.ShapeDtypeStruct((nt, IDX_HEADS * IDX_DIM), F32), jax.ShapeDtypeStruct((nt, LANES), F32),
           jax.ShapeDtypeStruct((nt, 4 * HG_WIDTH), F32)],
        compiler_params=_params(1),
        name="mix_in",
    )(x2, gain, wa, wi, wh, qn, kn, cos1, sin1, hsum, hexp)


def _dsa_kernel(iq3_ref, ik3_ref, iw_ref, qt_ref, k_ref, vt_ref, ltri_ref, o_ref,
                s_ref, qz_ref, m_ref, l_ref, acc_ref, *, topk):
    kb = KEY_BLOCK
    j = pl.program_id(1)
    nk = j + 1
    int_min = jnp.int32(INT_MIN)

    zeros = jnp.zeros((HEAD_DIM, kb), BF16)
    for h in range(ATTN_HEADS):
        qh = qt_ref[0, HEAD_DIM * h:HEAD_DIM * (h + 1), :]
        qz_ref[h] = jnp.concatenate([qh, zeros] if h % 2 == 0 else [zeros, qh], axis=0)

    def score_chunk(c, carry):
        k3 = ik3_ref[0, c]
        acc = jnp.zeros((kb, kb), F32)
        for h in range(IDX_HEADS):
            rel = jnp.dot(k3, iq3_ref[0, h], preferred_element_type=F32)
            acc = acc + iw_ref[0, h:h + 1, :] * jnp.maximum(rel, 0.0)
        acc = jnp.where(acc == 0.0, 0.0, acc)
        bits = pltpu.bitcast(acc, jnp.int32)
        key = jnp.where(bits < 0, bits ^ jnp.int32(0x7FFFFFFF), bits)
        kpos = c * kb + lax.broadcasted_iota(jnp.int32, (kb, kb), 0)
        qpos = j * kb + lax.broadcasted_iota(jnp.int32, (kb, kb), 1)
        s_ref[c] = jnp.where(kpos <= qpos, key, int_min)
        return carry

    lax.fori_loop(0, nk, score_chunk, 0)

    def count(pred_fn):
        def body(c, cnt):
            hit = jnp.where(pred_fn(s_ref[c]), 1, 0)
            return cnt + jnp.sum(hit.reshape(kb // 8, 8, kb), axis=0)
        cnt = lax.fori_loop(0, nk, body, jnp.zeros((8, kb), jnp.int32))
        return jnp.sum(cnt, axis=0, keepdims=True)

    def bit_step(i, thr_u):
        cand_u = thr_u | lax.shift_left(jnp.int32(1), 31 - i)
        cand = cand_u ^ int_min
        cnt = count(lambda s: s >= cand)
        return jnp.where(cnt >= topk, cand_u, thr_u)

    thr = lax.fori_loop(0, 32, bit_step, jnp.zeros((1, kb), jnp.int32)) ^ int_min
    n_gt = count(lambda s: s > thr)
    need = jnp.where(thr > int_min, (topk - n_gt).astype(F32), 0.0)

    m_ref[...] = jnp.full(m_ref.shape, NEG_BIG, F32)
    l_ref[...] = jnp.zeros(l_ref.shape, F32)
    acc_ref[...] = jnp.zeros(acc_ref.shape, F32)

    def attend_chunk(c, ties_before):
        sc = s_ref[c]
        eq = jnp.where(sc == thr, 1.0, 0.0)
        rank = jnp.dot(ltri_ref[...], eq.astype(BF16), preferred_element_type=F32) + ties_before
        sel = jnp.where(sc > thr, 1.0, jnp.where(rank < need, eq, 0.0)) > 0.5
        for h in range(ATTN_HEADS):
            kk = k_ref[0, c, :, LANES * (h // 2):LANES * (h // 2 + 1)]
            lg = jnp.dot(kk, qz_ref[h], preferred_element_type=F32)
            m_old = m_ref[h]
            m_new = jnp.maximum(m_old, jnp.max(jnp.where(sel, lg, NEG_BIG), axis=0, keepdims=True))
            alpha = jnp.exp(m_old - m_new)
            p = jnp.where(sel, jnp.exp(lg - m_new), 0.0)
            l_ref[h] = alpha * l_ref[h] + jnp.sum(p, axis=0, keepdims=True)
            pv = jnp.dot(vt_ref[0, c, HEAD_DIM * h:HEAD_DIM * (h + 1), :], p.astype(BF16),
                         preferred_element_type=F32)
            acc_ref[h] = alpha * acc_ref[h] + pv
            m_ref[h] = m_new
        return ties_before + jnp.sum(eq, axis=0, keepdims=True)

    lax.fori_loop(0, nk, attend_chunk, jnp.zeros((1, kb), F32))

    for h in range(ATTN_HEADS):
        o_ref[0, HEAD_DIM * h:HEAD_DIM * (h + 1), :] = (acc_ref[h] / l_ref[h]).astype(o_ref.dtype)


def _dsa(iq3t, ik3, iwt, qt, k4, vt4, ltri, topk):
    b, _, t = qt.shape
    kb = KEY_BLOCK
    nkc = t // kb
    return pl.pallas_call(
        functools.partial(_dsa_kernel, topk=topk),
        grid=(b, nkc),
        in_specs=[
            pl.BlockSpec((1, IDX_HEADS, kb, kb), lambda i, j: (i, 0, 0, j)),
            pl.BlockSpec((1, nkc, kb, kb), lambda i, j: (i, 0, 0, 0), pipeline_mode=pl.Buffered(1)),
            pl.BlockSpec((1, 8, kb), lambda i, j: (i, 0, j)),
            pl.BlockSpec((1, ATTN_WIDTH, kb), lambda i, j: (i, 0, j)),
            pl.BlockSpec((1, nkc, kb, ATTN_WIDTH), lambda i, j: (i, 0, 0, 0), pipeline_mode=pl.Buffered(1)),
            pl.BlockSpec((1, nkc, ATTN_WIDTH, kb), lambda i, j: (i, 0, 0, 0), pipeline_mode=pl.Buffered(1)),
            _resident(ltri.shape),
        ],
        out_specs=pl.BlockSpec((1, ATTN_WIDTH, kb), lambda i, j: (i, 0, j)),
        out_shape=jax.ShapeDtypeStruct((b, ATTN_WIDTH, t), BF16),
        scratch_shapes=[
            pltpu.VMEM((nkc, kb, kb), jnp.int32),
            pltpu.VMEM((ATTN_HEADS, LANES, kb), BF16),
            pltpu.VMEM((ATTN_HEADS, 1, kb), F32),
            pltpu.VMEM((ATTN_HEADS, 1, kb), F32),
            pltpu.VMEM((ATTN_HEADS, HEAD_DIM, kb), F32),
        ],
        compiler_params=_params(2),
        name="dsa",
    )(iq3t, ik3, iwt, qt, k4, vt4, ltri)


def _hgrn_kernel(hq_ref, hf_ref, hi_ref, hg_ref, hc_ref, lt_ref, o_ref, st_ref):
    c_len, sub = HG_CHUNK, HG_SUB

    @pl.when(pl.program_id(1) == 0)
    def _():
        st_ref[...] = jnp.zeros(st_ref.shape, F32)

    log_lb = hc_ref[0:1, :]
    log1m_lb = hc_ref[1:2, :]
    one_m_lb = hc_ref[2:3, :]
    out_gain = hc_ref[3:4, :]

    hf = hf_ref[...]
    hq = hq_ref[...]
    hg = hg_ref[...]
    c = log1m_lb + jnp.minimum(hf, 0.0) - jnp.log1p(jnp.exp(-jnp.abs(hf)))
    log_f = jnp.maximum(log_lb, c) + jnp.log1p(jnp.exp(-jnp.abs(log_lb - c)))
    k_all = one_m_lb * _sigmoid(-hf)
    q_all = hq * _sigmoid(hq)
    gate = hg * _sigmoid(hg)
    b_all = jnp.dot(lt_ref[...], log_f, precision=HIGHEST, preferred_element_type=F32)
    b_last = b_all[c_len - 1:c_len, :]
    q_inter = q_all * jnp.exp(b_all)
    k_carry = k_all * jnp.exp(b_last - b_all)

    row = lax.broadcasted_iota(jnp.int32, (sub, 1), 0)
    col = lax.broadcasted_iota(jnp.int32, (sub, sub), 1)
    outs = []
    for h in range(HG_HEADS):
        hs = slice(HG_DIM * h, HG_DIM * (h + 1))
        b, q, k, v = b_all[:, hs], q_all[:, hs], k_all[:, hs], hi_ref[:, hs]
        vb = v.astype(BF16)
        st = st_ref[h]
        inter = lax.dot_general(q_inter[:, hs].astype(BF16), st.astype(BF16), (((1,), (1,)), ((), ())),
                                preferred_element_type=F32)
        blocks = []
        for i in range(c_len // sub):
            rs = slice(sub * i, sub * (i + 1))
            b_i, q_i, k_i = b[rs], q[rs], k[rs]
            diag = jnp.zeros((sub, sub), F32)
            for s in range(sub):
                decay = jnp.exp(jnp.where(row >= s, b_i - b_i[s:s + 1, :], -jnp.inf))
                a_col = jnp.sum(q_i * k_i[s:s + 1, :] * decay, axis=1, keepdims=True)
                diag = jnp.where(col == s, a_col, diag)
            o_i = inter[rs] + jnp.dot(diag.astype(BF16), vb[rs], preferred_element_type=F32)
            if i > 0:
                prev = slice(0, sub * i)
                b_ref = b[sub * i - 1:sub * i, :]
                q_hat = (q_i * jnp.exp(b_i - b_ref)).astype(BF16)
                k_hat = (k[prev] * jnp.exp(b_ref - b[prev])).astype(BF16)
                a_off = lax.dot_general(q_hat, k_hat, (((1,), (1,)), ((), ())), preferred_element_type=F32)
                o_i = o_i + jnp.dot(a_off.astype(BF16), vb[prev], preferred_element_type=F32)
            blocks.append(o_i)
        o = jnp.concatenate(blocks, axis=0)
        st_ref[h] = jnp.exp(b_last[:, hs]) * st + jnp.dot(v.T.astype(BF16), k_carry[:, hs].astype(BF16),
                                                         preferred_element_type=F32)
        outs.append(_rmsnorm_rows(o, out_gain[:, hs]) * gate[:, hs])
    o_ref[...] = jnp.concatenate(outs, axis=1).astype(o_ref.dtype)


def _hgrn(hg4, hconst, ltri, b, t):
    c = HG_CHUNK
    nc = t // c
    col = lambda jcol: pl.BlockSpec((c, HG_WIDTH), lambda i, s, jcol=jcol: (i * nc + s, jcol))
    return pl.pallas_call(
        _hgrn_kernel,
        grid=(b, nc),
        in_specs=[col(0), col(1), col(2), col(3), _resident(hconst.shape), _resident(ltri.shape)],
        out_specs=pl.BlockSpec((c, HG_WIDTH), lambda i, s: (i * nc + s, 0)),
        out_shape=jax.ShapeDtypeStruct((b * t, HG_WIDTH), BF16),
        scratch_shapes=[pltpu.VMEM((HG_HEADS, HG_DIM, HG_DIM), F32)],
        compiler_params=_params(2),
        name="hgrn",
    )(hg4, hg4, hg4, hg4, hconst, ltri)


def _merge_kernel(x_ref, ya_ref, yh_ref, g_ref, wg_ref, wpa_ref, wph_ref, wo_ref, o_ref):
    x = x_ref[...]
    hb = _rmsnorm_rows(x, g_ref[...]).astype(BF16)
    gates = jnp.dot(hb, wg_ref[...], preferred_element_type=F32)
    pa = jnp.dot(ya_ref[...], wpa_ref[...], preferred_element_type=F32)
    ph = jnp.dot(yh_ref[...], wph_ref[...], preferred_element_type=F32)
    merged = _sigmoid(gates[:, :D_MODEL]) * pa + _sigmoid(gates[:, D_MODEL:]) * ph
    o_ref[...] = x + jnp.dot(merged.astype(BF16), wo_ref[...], preferred_element_type=F32)


def _merge(x2, ya, yh, gain, wg, wpa, wph, wo, tm):
    nt = x2.shape[0]
    row = lambda w: pl.BlockSpec((tm, w), lambda i: (i, 0))
    return pl.pallas_call(
        _merge_kernel,
        grid=(nt // tm,),
        in_specs=[row(D_MODEL), row(ATTN_WIDTH), row(HG_WIDTH), _resident(gain.shape), _resident(wg.shape),
                  _resident(wpa.shape), _resident(wph.shape), _resident(wo.shape)],
        out_specs=row(D_MODEL),
        out_shape=jax.ShapeDtypeStruct((nt, D_MODEL), F32),
        compiler_params=_params(1),
        name="merge",
    )(x2, ya, yh, gain, wg, wpa, wph, wo)


def _ffn_kernel(x_ref, g_ref, wi_ref, wo_ref, o_ref):
    x = x_ref[...]
    hb = _rmsnorm_rows(x, g_ref[...]).astype(BF16)
    gu = jnp.dot(hb, wi_ref[...], preferred_element_type=F32)
    g = gu[:, :D_FF]
    act = g * _sigmoid(g) * gu[:, D_FF:]
    o_ref[...] = x + jnp.dot(act.astype(BF16), wo_ref[...], preferred_element_type=F32)


def _ffn(x2, gain, wi, wo, tm):
    nt = x2.shape[0]
    row = pl.BlockSpec((tm, D_MODEL), lambda i: (i, 0))
    return pl.pallas_call(
        _ffn_kernel,
        grid=(nt // tm,),
        in_specs=[row, _resident(gain.shape), _resident(wi.shape), _resident(wo.shape)],
        out_specs=row,
        out_shape=jax.ShapeDtypeStruct((nt, D_MODEL), F32),
        compiler_params=_params(1),
        name="ffn",
    )(x2, gain, wi, wo)


def _split_hi_lo(a):
    hi = a.astype(BF16)
    lo = (a - hi.astype(F32)).astype(BF16)
    return hi, lo


def _layer(x2, b, t, cos1, sin1, consts, w):
    hsum, hexp, ltri_attn, ltri_hgrn = consts
    kb = KEY_BLOCK
    nkc = t // kb
    q, k, v, iq, ikw, hg4 = _mix_in(x2, w["norm_mix"], w["wa"], w["wi"], w["wh"], w["qn"], w["kn"],
                                    cos1, sin1, hsum, hexp, tm=512)
    qt = q.reshape(b, t, ATTN_WIDTH).transpose(0, 2, 1)
    k4 = k.reshape(b, nkc, kb, ATTN_WIDTH)
    vt4 = v.reshape(b, nkc, kb, ATTN_WIDTH).transpose(0, 1, 3, 2)
    iq_hi, iq_lo = _split_hi_lo(iq.reshape(b, t, IDX_HEADS, IDX_DIM))
    iq3t = jnp.concatenate([iq_hi, iq_hi, iq_lo, jnp.zeros_like(iq_hi)], axis=-1).transpose(0, 2, 3, 1)
    ik_hi, ik_lo = _split_hi_lo(ikw[:, :IDX_DIM])
    ik3 = jnp.concatenate([ik_hi, ik_lo, ik_hi, jnp.zeros_like(ik_hi)], axis=-1).reshape(b, nkc, kb, 4 * IDX_DIM)
    iwt = jnp.pad(ikw[:, IDX_DIM:IDX_DIM + IDX_HEADS].reshape(b, t, IDX_HEADS), ((0, 0), (0, 0), (0, 8 - IDX_HEADS))
                  ).transpose(0, 2, 1)
    y_attn_t = _dsa(iq3t, ik3, iwt, qt, k4, vt4, ltri_attn, topk=min(MAX_TOPK, t // 4))
    y_attn = y_attn_t.transpose(0, 2, 1).reshape(b * t, ATTN_WIDTH)
    y_hgrn = _hgrn(hg4, w["hconst"], ltri_hgrn, b, t)
    x2 = _merge(x2, y_attn, y_hgrn, w["norm_mix"], w["wg"], w["wpa"], w["wph"], w["wout"], tm=512)
    return _ffn(x2, w["norm_ffn"], w["wffn_in"], w["wffn_out"], tm=256)


def _constants():
    head_of_lane = np.arange(ATTN_WIDTH) // HEAD_DIM
    hsum = (head_of_lane[:, None] == np.arange(LANES)[None, :]).astype(np.float32)
    hexp = hsum.T.copy()
    r = np.arange(KEY_BLOCK)
    ltri_attn = (r[None, :] < r[:, None]).astype(np.float32)
    r = np.arange(HG_CHUNK)
    ltri_hgrn = (r[None, :] <= r[:, None]).astype(np.float32)
    return jnp.asarray(hsum), jnp.asarray(hexp), jnp.asarray(ltri_attn, BF16), jnp.asarray(ltri_hgrn)


def kernel(x, positions, w_in, w_proj_attn, w_proj_hgrn, w_out, norm_mix, norm_ffn, q_norm, k_norm, hgrn_norm,
           hgrn_lower_bound, w_ffn_in, w_ffn_out):
    b, t, d = x.shape
    depth = w_in.shape[0]
    nt = b * t

    inv = ROPE_THETA ** (-jnp.arange(0, HEAD_DIM, 2, dtype=F32) / HEAD_DIM)
    ang = positions.astype(F32)[..., None] * inv
    cos, sin = jnp.cos(ang), jnp.sin(ang)
    cos1 = jnp.concatenate([cos, cos, cos, cos], axis=-1).reshape(nt, LANES)
    sin1 = jnp.concatenate([-sin, sin, -sin, sin], axis=-1).reshape(nt, LANES)

    lb_all = jnp.cumsum(jax.nn.softmax(hgrn_lower_bound.astype(F32), axis=0), axis=0)
    lb_all = lb_all - lb_all[:1]

    consts = _constants()
    widths = (ATTN_WIDTH, ATTN_WIDTH, ATTN_WIDTH, IDX_HEADS * IDX_DIM, IDX_DIM, IDX_HEADS,
              HG_WIDTH, HG_WIDTH, HG_WIDTH, HG_WIDTH, D_MODEL, D_MODEL)
    off = np.concatenate([[0], np.cumsum(widths)])
    idx_cols = off[6] - off[3]

    x2 = x.reshape(nt, d)
    for l in range(depth):
        wl = w_in[l]
        lb = lb_all[l]
        hconst = jnp.zeros((8, HG_WIDTH), F32)
        hconst = hconst.at[0].set(jnp.log(lb)).at[1].set(jnp.log1p(-lb)).at[2].set(1.0 - lb)
        hconst = hconst.at[3].set(jnp.tile(hgrn_norm[l], HG_HEADS))
        w = {
            "norm_mix": norm_mix[l].reshape(1, d),
            "norm_ffn": norm_ffn[l].reshape(1, d),
            "wa": wl[:, off[0]:off[3]].astype(BF16),
            "wi": jnp.pad(wl[:, off[3]:off[6]], ((0, 0), (0, 3 * LANES - idx_cols))),
            "wh": wl[:, off[6]:off[10]].astype(BF16),
            "wg": wl[:, off[10]:off[12]].astype(BF16),
            "qn": jnp.tile(q_norm[l], ATTN_HEADS).reshape(1, ATTN_WIDTH),
            "kn": jnp.tile(k_norm[l], ATTN_HEADS).reshape(1, ATTN_WIDTH),
            "hconst": hconst,
            "wpa": w_proj_attn[l].astype(BF16),
            "wph": w_proj_hgrn[l].astype(BF16),
            "wout": w_out[l].astype(BF16),
            "wffn_in": w_ffn_in[l].astype(BF16),
            "wffn_out": w_ffn_out[l].astype(BF16),
        }
        x2 = _layer(x2, b, t, cos1, sin1, consts, w)
    return x2.reshape(b, t, d)
```

```python
import functools

import numpy as np
import jax
import jax.numpy as jnp
from jax import lax
from jax.experimental import pallas as pl
from jax.experimental.pallas import tpu as pltpu

D_MODEL = 1024
ATTN_HEADS = 8
HEAD_DIM = 64
ATTN_WIDTH = ATTN_HEADS * HEAD_DIM
IDX_HEADS = 4
IDX_DIM = 64
MAX_TOPK = 256
HG_HEADS = 4
HG_DIM = 128
HG_WIDTH = HG_HEADS * HG_DIM
HG_CHUNK = 64
HG_SUB = 16
D_FF = 2816
ROPE_THETA = 10000.0
EPS = 1e-6

LANES = 128
VMEM_LIMIT_BYTES = 56 * 1024 * 1024

KEY_BLOCK = 256
VT_ROWS = 80
INT_MIN = -(2 ** 31)
NEG_BIG = -1e30

F32 = jnp.float32
BF16 = jnp.bfloat16
HIGHEST = lax.Precision.HIGHEST


def _resident(shape):
    nd = len(shape)
    return pl.BlockSpec(shape, lambda *_: (0,) * nd, pipeline_mode=pl.Buffered(1))


def _params(n_axes):
    return pltpu.CompilerParams(dimension_semantics=("arbitrary",) * n_axes,
                                vmem_limit_bytes=VMEM_LIMIT_BYTES)


def _rmsnorm_rows(x, gain):
    return x * lax.rsqrt(jnp.mean(x * x, axis=-1, keepdims=True) + EPS) * gain


def _sigmoid(x):
    return 1.0 / (1.0 + jnp.exp(-x))


def _rope(x, cos, sin_signed):
    w = x.shape[-1]
    lane = lax.broadcasted_iota(jnp.int32, x.shape, 1)
    first_half = (lane & (HEAD_DIM - 1)) < HEAD_DIM // 2
    partner = jnp.where(first_half, pltpu.roll(x, w - HEAD_DIM // 2, 1), pltpu.roll(x, HEAD_DIM // 2, 1))
    return x * cos + partner * sin_signed


def _mix_in_kernel(x_ref, g_ref, wa_ref, wi_ref, wh_ref, qn_ref, kn_ref, cos_ref, sin_ref, hsum_ref, hexp_ref,
                   q_ref, k_ref, v_ref, iq_ref, ikw_ref, hg_ref):
    h = _rmsnorm_rows(x_ref[...], g_ref[...])
    hb = h.astype(BF16)
    cos1 = cos_ref[...]
    sin1 = sin_ref[...]
    cos4 = jnp.concatenate([cos1] * 4, axis=1)
    sin4 = jnp.concatenate([sin1] * 4, axis=1)

    def head_norm(a, gain):
        ss = jnp.dot(a * a, hsum_ref[...], precision=HIGHEST, preferred_element_type=F32)
        r = lax.rsqrt(ss * (1.0 / HEAD_DIM) + EPS)
        rf = jnp.dot(r, hexp_ref[...], precision=HIGHEST, preferred_element_type=F32)
        return a * rf * gain

    pa = jnp.dot(hb, wa_ref[...], preferred_element_type=F32)
    q = _rope(head_norm(pa[:, :ATTN_WIDTH], qn_ref[...]), cos4, sin4) * (HEAD_DIM ** -0.5)
    k = _rope(head_norm(pa[:, ATTN_WIDTH:2 * ATTN_WIDTH], kn_ref[...]), cos4, sin4)
    q_ref[...] = q.astype(BF16)
    k_ref[...] = k.astype(BF16)
    v_ref[...] = pa[:, 2 * ATTN_WIDTH:].astype(BF16)

    pi = jnp.dot(h, wi_ref[...], precision=HIGHEST, preferred_element_type=F32)
    nq = IDX_HEADS * IDX_DIM
    iq_ref[...] = _rope(pi[:, :nq], cos4[:, :nq], sin4[:, :nq]) * (IDX_DIM ** -0.5)
    slab = pi[:, nq:]
    lane = lax.broadcasted_iota(jnp.int32, slab.shape, 1)
    ikw_ref[...] = jnp.where(lane < IDX_DIM, _rope(slab, cos1, sin1), slab * (IDX_HEADS ** -0.5))

    hg_ref[...] = jnp.dot(hb, wh_ref[...], preferred_element_type=F32)


def _mix_in(x2, gain, wa, wi, wh, qn, kn, cos1, sin1, hsum, hexp, tm):
    nt = x2.shape[0]
    row = lambda w: pl.BlockSpec((tm, w), lambda i: (i, 0))
    return pl.pallas_call(
        _mix_in_kernel,
        grid=(nt // tm,),
        in_specs=[row(D_MODEL), _resident(gain.shape), _resident(wa.shape), _resident(wi.shape), _resident(wh.shape),
                  _resident(qn.shape), _resident(kn.shape), row(LANES), row(LANES),
                  _resident(hsum.shape), _resident(hexp.shape)],
        out_specs=[row(ATTN_WIDTH), row(ATTN_WIDTH), row(ATTN_WIDTH), row(IDX_HEADS * IDX_DIM), row(LANES),
                   row(4 * HG_WIDTH)],
        out_shape=[jax.ShapeDtypeStruct((nt, ATTN_WIDTH), BF16)] * 3
        + [jax.ShapeDtypeStruct((nt, IDX_HEADS * IDX_DIM), F32), jax.ShapeDtypeStruct((nt, LANES), F32),
           jax.ShapeDtypeStruct((nt, 4 * HG_WIDTH), F32)],
        compiler_params=_params(1),
        name="mix_in",
    )(x2, gain, wa, wi, wh, qn, kn, cos1, sin1, hsum, hexp)


def _dsa_kernel(iq3_ref, ik3_ref, iw_ref, qt_ref, k_ref, vt_ref, ltri_ref, o_ref,
                s_ref, qz_ref, lg_ref, m_ref, alpha_ref, acc_ref, *, topk):
    kb = KEY_BLOCK
    j = pl.program_id(1)
    nk = j + 1
    int_min = jnp.int32(INT_MIN)

    zeros = jnp.zeros((HEAD_DIM, kb), BF16)
    for h in range(ATTN_HEADS):
        qh = qt_ref[0, HEAD_DIM * h:HEAD_DIM * (h + 1), :]
        qz_ref[h] = jnp.concatenate([qh, zeros] if h % 2 == 0 else [zeros, qh], axis=0)

    def score_chunk(c, carry):
        k3 = ik3_ref[0, c]
        acc = jnp.zeros((kb, kb), F32)
        for h in range(IDX_HEADS):
            rel = jnp.dot(k3, iq3_ref[0, h], preferred_element_type=F32)
            acc = acc + iw_ref[0, h:h + 1, :] * jnp.maximum(rel, 0.0)
        acc = jnp.where(acc == 0.0, 0.0, acc)
        bits = pltpu.bitcast(acc, jnp.int32)
        key = jnp.where(bits < 0, bits ^ jnp.int32(0x7FFFFFFF), bits)
        kpos = c * kb + lax.broadcasted_iota(jnp.int32, (kb, kb), 0)
        qpos = j * kb + lax.broadcasted_iota(jnp.int32, (kb, kb), 1)
        s_ref[c] = jnp.where(kpos <= qpos, key, int_min)
        return carry

    lax.fori_loop(0, nk, score_chunk, 0)

    def count(pred_fn):
        def body(c, cnt):
            hit = jnp.where(pred_fn(s_ref[c]), 1, 0)
            return cnt + jnp.sum(hit.reshape(kb // 8, 8, kb), axis=0)
        cnt = lax.fori_loop(0, nk, body, jnp.zeros((8, kb), jnp.int32))
        return jnp.sum(cnt, axis=0, keepdims=True)

    def bit_step(i, thr_u):
        cand_u = thr_u | lax.shift_left(jnp.int32(1), 31 - i)
        cand = cand_u ^ int_min
        cnt = count(lambda s: s >= cand)
        return jnp.where(cnt >= topk, cand_u, thr_u)

    thr = lax.fori_loop(0, 32, bit_step, jnp.zeros((1, kb), jnp.int32)) ^ int_min
    n_gt = count(lambda s: s > thr)
    need = jnp.where(thr > int_min, (topk - n_gt).astype(F32), 0.0)

    def bias_chunk(c, ties_before):
        sc = s_ref[c]
        eq = jnp.where(sc == thr, 1.0, 0.0)
        rank = jnp.dot(ltri_ref[...], eq.astype(BF16), preferred_element_type=F32) + ties_before
        take_tie = jnp.where(rank < need, eq, 0.0) > 0.5
        bias = jnp.where(sc > thr, 0.0, jnp.where(take_tie, 0.0, NEG_BIG))
        s_ref[c] = pltpu.bitcast(bias, jnp.int32)
        return ties_before + jnp.sum(eq, axis=0, keepdims=True)

    lax.fori_loop(0, nk, bias_chunk, jnp.zeros((1, kb), F32))

    m_ref[...] = jnp.full(m_ref.shape, NEG_BIG, F32)
    acc_ref[...] = jnp.zeros(acc_ref.shape, F32)

    def logits_chunk(c):
        bias = pltpu.bitcast(s_ref[c], F32)
        for h in range(ATTN_HEADS):
            kk = k_ref[0, c, :, LANES * (h // 2):LANES * (h // 2 + 1)]
            lgb = jnp.dot(kk, qz_ref[h], preferred_element_type=F32) + bias
            lg_ref[h] = lgb
            m_old = m_ref[h]
            m_new = jnp.maximum(m_old, jnp.max(lgb, axis=0, keepdims=True))
            alpha_ref[h] = jnp.exp(m_old - m_new)
            m_ref[h] = m_new

    def softmax_pv_chunk(c):
        for h in range(ATTN_HEADS):
            p = jnp.exp(lg_ref[h] - m_ref[h])
            pv = jnp.dot(vt_ref[0, c, VT_ROWS * h:VT_ROWS * (h + 1), :], p.astype(BF16),
                         preferred_element_type=F32)
            acc_ref[h] = alpha_ref[h] * acc_ref[h] + pv

    logits_chunk(0)

    def pipelined(c, carry):
        softmax_pv_chunk(c - 1)
        logits_chunk(c)
        return carry

    lax.fori_loop(1, nk, pipelined, 0)
    softmax_pv_chunk(nk - 1)

    for h in range(ATTN_HEADS):
        acc = acc_ref[h]
        o_ref[0, HEAD_DIM * h:HEAD_DIM * (h + 1), :] = (
            acc[:HEAD_DIM] / acc[HEAD_DIM:HEAD_DIM + 1]).astype(o_ref.dtype)


def _dsa(iq3t, ik3, iwt, qt, k4, vt4, ltri, topk):
    b, _, t = qt.shape
    kb = KEY_BLOCK
    nkc = t // kb
    return pl.pallas_call(
        functools.partial(_dsa_kernel, topk=topk),
        grid=(b, nkc),
        in_specs=[
            pl.BlockSpec((1, IDX_HEADS, kb, kb), lambda i, j: (i, 0, 0, j)),
            pl.BlockSpec((1, nkc, kb, 4 * IDX_DIM), lambda i, j: (i, 0, 0, 0), pipeline_mode=pl.Buffered(1)),
            pl.BlockSpec((1, 8, kb), lambda i, j: (i, 0, j)),
            pl.BlockSpec((1, ATTN_WIDTH, kb), lambda i, j: (i, 0, j)),
            pl.BlockSpec((1, nkc, kb, ATTN_WIDTH), lambda i, j: (i, 0, 0, 0), pipeline_mode=pl.Buffered(1)),
            pl.BlockSpec((1, nkc, ATTN_HEADS * VT_ROWS, kb), lambda i, j: (i, 0, 0, 0),
                         pipeline_mode=pl.Buffered(1)),
            _resident(ltri.shape),
        ],
        out_specs=pl.BlockSpec((1, ATTN_WIDTH, kb), lambda i, j: (i, 0, j)),
        out_shape=jax.ShapeDtypeStruct((b, ATTN_WIDTH, t), BF16),
        scratch_shapes=[
            pltpu.VMEM((nkc, kb, kb), jnp.int32),
            pltpu.VMEM((ATTN_HEADS, LANES, kb), BF16),
            pltpu.VMEM((ATTN_HEADS, kb, kb), F32),
            pltpu.VMEM((ATTN_HEADS, 1, kb), F32),
            pltpu.VMEM((ATTN_HEADS, 1, kb), F32),
            pltpu.VMEM((ATTN_HEADS, VT_ROWS, kb), F32),
        ],
        compiler_params=_params(2),
        name="dsa",
    )(iq3t, ik3, iwt, qt, k4, vt4, ltri)


def _hgrn_kernel(hq_ref, hf_ref, hi_ref, hg_ref, hc_ref, lt_ref, o_ref, st_ref):
    c_len, sub = HG_CHUNK, HG_SUB

    @pl.when(pl.program_id(1) == 0)
    def _():
        st_ref[...] = jnp.zeros(st_ref.shape, F32)

    log_lb = hc_ref[0:1, :]
    log1m_lb = hc_ref[1:2, :]
    one_m_lb = hc_ref[2:3, :]
    out_gain = hc_ref[3:4, :]

    hf = hf_ref[...]
    hq = hq_ref[...]
    hg = hg_ref[...]
    c = log1m_lb + jnp.minimum(hf, 0.0) - jnp.log1p(jnp.exp(-jnp.abs(hf)))
    log_f = jnp.maximum(log_lb, c) + jnp.log1p(jnp.exp(-jnp.abs(log_lb - c)))
    k_all = one_m_lb * _sigmoid(-hf)
    q_all = hq * _sigmoid(hq)
    gate = hg * _sigmoid(hg)
    b_all = jnp.dot(lt_ref[...], log_f, precision=HIGHEST, preferred_element_type=F32)
    b_last = b_all[c_len - 1:c_len, :]
    q_inter = q_all * jnp.exp(b_all)
    k_carry = k_all * jnp.exp(b_last - b_all)

    row = lax.broadcasted_iota(jnp.int32, (sub, 1), 0)
    col = lax.broadcasted_iota(jnp.int32, (sub, sub), 1)
    outs = []
    for h in range(HG_HEADS):
        hs = slice(HG_DIM * h, HG_DIM * (h + 1))
        b, q, k, v = b_all[:, hs], q_all[:, hs], k_all[:, hs], hi_ref[:, hs]
        vb = v.astype(BF16)
        st = st_ref[h]
        inter = lax.dot_general(q_inter[:, hs].astype(BF16), st.astype(BF16), (((1,), (1,)), ((), ())),
                                preferred_element_type=F32)
        blocks = []
        for i in range(c_len // sub):
            rs = slice(sub * i, sub * (i + 1))
            b_i, q_i, k_i = b[rs], q[rs], k[rs]
            diag = jnp.zeros((sub, sub), F32)
            for s in range(sub):
                decay = jnp.exp(jnp.where(row >= s, b_i - b_i[s:s + 1, :], -jnp.inf))
                a_col = jnp.sum(q_i * k_i[s:s + 1, :] * decay, axis=1, keepdims=True)
                diag = jnp.where(col == s, a_col, diag)
            o_i = inter[rs] + jnp.dot(diag.astype(BF16), vb[rs], preferred_element_type=F32)
            if i > 0:
                prev = slice(0, sub * i)
                b_ref = b[sub * i - 1:sub * i, :]
                q_hat = (q_i * jnp.exp(b_i - b_ref)).astype(BF16)
                k_hat = (k[prev] * jnp.exp(b_ref - b[prev])).astype(BF16)
                a_off = lax.dot_general(q_hat, k_hat, (((1,), (1,)), ((), ())), preferred_element_type=F32)
                o_i = o_i + jnp.dot(a_off.astype(BF16), vb[prev], preferred_element_type=F32)
            blocks.append(o_i)
        o = jnp.concatenate(blocks, axis=0)
        st_ref[h] = jnp.exp(b_last[:, hs]) * st + jnp.dot(v.T.astype(BF16), k_carry[:, hs].astype(BF16),
                                                         preferred_element_type=F32)
        outs.append(_rmsnorm_rows(o, out_gain[:, hs]) * gate[:, hs])
    o_ref[...] = jnp.concatenate(outs, axis=1).astype(o_ref.dtype)


def _hgrn(hg4, hconst, ltri, b, t):
    c = HG_CHUNK
    nc = t // c
    col = lambda jcol: pl.BlockSpec((c, HG_WIDTH), lambda i, s, jcol=jcol: (i * nc + s, jcol))
    return pl.pallas_call(
        _hgrn_kernel,
        grid=(b, nc),
        in_specs=[col(0), col(1), col(2), col(3), _resident(hconst.shape), _resident(ltri.shape)],
        out_specs=pl.BlockSpec((c, HG_WIDTH), lambda i, s: (i * nc + s, 0)),
        out_shape=jax.ShapeDtypeStruct((b * t, HG_WIDTH), BF16),
        scratch_shapes=[pltpu.VMEM((HG_HEADS, HG_DIM, HG_DIM), F32)],
        compiler_params=_params(2),
        name="hgrn",
    )(hg4, hg4, hg4, hg4, hconst, ltri)


def _merge_kernel(x_ref, ya_ref, yh_ref, g_ref, wg_ref, wpa_ref, wph_ref, wo_ref, o_ref):
    x = x_ref[...]
    hb = _rmsnorm_rows(x, g_ref[...]).astype(BF16)
    gates = jnp.dot(hb, wg_ref[...], preferred_element_type=F32)
    pa = jnp.dot(ya_ref[...], wpa_ref[...], preferred_element_type=F32)
    ph = jnp.dot(yh_ref[...], wph_ref[...], preferred_element_type=F32)
    merged = _sigmoid(gates[:, :D_MODEL]) * pa + _sigmoid(gates[:, D_MODEL:]) * ph
    o_ref[...] = x + jnp.dot(merged.astype(BF16), wo_ref[...], preferred_element_type=F32)


def _merge(x2, ya, yh, gain, wg, wpa, wph, wo, tm):
    nt = x2.shape[0]
    row = lambda w: pl.BlockSpec((tm, w), lambda i: (i, 0))
    return pl.pallas_call(
        _merge_kernel,
        grid=(nt // tm,),
        in_specs=[row(D_MODEL), row(ATTN_WIDTH), row(HG_WIDTH), _resident(gain.shape), _resident(wg.shape),
                  _resident(wpa.shape), _resident(wph.shape), _resident(wo.shape)],
        out_specs=row(D_MODEL),
        out_shape=jax.ShapeDtypeStruct((nt, D_MODEL), F32),
        compiler_params=_params(1),
        name="merge",
    )(x2, ya, yh, gain, wg, wpa, wph, wo)


def _ffn_kernel(x_ref, g_ref, wi_ref, wo_ref, o_ref):
    x = x_ref[...]
    hb = _rmsnorm_rows(x, g_ref[...]).astype(BF16)
    gu = jnp.dot(hb, wi_ref[...], preferred_element_type=F32)
    g = gu[:, :D_FF]
    act = g * _sigmoid(g) * gu[:, D_FF:]
    o_ref[...] = x + jnp.dot(act.astype(BF16), wo_ref[...], preferred_element_type=F32)


def _ffn(x2, gain, wi, wo, tm):
    nt = x2.shape[0]
    row = pl.BlockSpec((tm, D_MODEL), lambda i: (i, 0))
    return pl.pallas_call(
        _ffn_kernel,
        grid=(nt // tm,),
        in_specs=[row, _resident(gain.shape), _resident(wi.shape), _resident(wo.shape)],
        out_specs=row,
        out_shape=jax.ShapeDtypeStruct((nt, D_MODEL), F32),
        compiler_params=_params(1),
        name="ffn",
    )(x2, gain, wi, wo)


def _split_hi_lo(a):
    hi = a.astype(BF16)
    lo = (a - hi.astype(F32)).astype(BF16)
    return hi, lo


def _layer(x2, b, t, cos1, sin1, consts, w):
    hsum, hexp, ltri_attn, ltri_hgrn = consts
    kb = KEY_BLOCK
    nkc = t // kb
    q, k, v, iq, ikw, hg4 = _mix_in(x2, w["norm_mix"], w["wa"], w["wi"], w["wh"], w["qn"], w["kn"],
                                    cos1, sin1, hsum, hexp, tm=512)
    qt = q.reshape(b, t, ATTN_WIDTH).transpose(0, 2, 1)
    k4 = k.reshape(b, nkc, kb, ATTN_WIDTH)
    vt = v.reshape(b, nkc, kb, ATTN_HEADS, HEAD_DIM).transpose(0, 1, 3, 4, 2)
    ones_pad = jnp.zeros((b, nkc, ATTN_HEADS, VT_ROWS - HEAD_DIM, kb), BF16).at[:, :, :, 0, :].set(1.0)
    vt4 = jnp.concatenate([vt, ones_pad], axis=3).reshape(b, nkc, ATTN_HEADS * VT_ROWS, kb)
    iq_hi, iq_lo = _split_hi_lo(iq.reshape(b, t, IDX_HEADS, IDX_DIM))
    iq3t = jnp.concatenate([iq_hi, iq_hi, iq_lo, jnp.zeros_like(iq_hi)], axis=-1).transpose(0, 2, 3, 1)
    ik_hi, ik_lo = _split_hi_lo(ikw[:, :IDX_DIM])
    ik3 = jnp.concatenate([ik_hi, ik_lo, ik_hi, jnp.zeros_like(ik_hi)], axis=-1).reshape(b, nkc, kb, 4 * IDX_DIM)
    iwt = jnp.pad(ikw[:, IDX_DIM:IDX_DIM + IDX_HEADS].reshape(b, t, IDX_HEADS), ((0, 0), (0, 0), (0, 8 - IDX_HEADS))
                  ).transpose(0, 2, 1)
    y_attn_t = _dsa(iq3t, ik3, iwt, qt, k4, vt4, ltri_attn, topk=min(MAX_TOPK, t // 4))
    y_attn = y_attn_t.transpose(0, 2, 1).reshape(b * t, ATTN_WIDTH)
    y_hgrn = _hgrn(hg4, w["hconst"], ltri_hgrn, b, t)
    x2 = _merge(x2, y_attn, y_hgrn, w["norm_mix"], w["wg"], w["wpa"], w["wph"], w["wout"], tm=512)
    return _ffn(x2, w["norm_ffn"], w["wffn_in"], w["wffn_out"], tm=256)


def _constants():
    head_of_lane = np.arange(ATTN_WIDTH) // HEAD_DIM
    hsum = (head_of_lane[:, None] == np.arange(LANES)[None, :]).astype(np.float32)
    hexp = hsum.T.copy()
    r = np.arange(KEY_BLOCK)
    ltri_attn = (r[None, :] < r[:, None]).astype(np.float32)
    r = np.arange(HG_CHUNK)
    ltri_hgrn = (r[None, :] <= r[:, None]).astype(np.float32)
    return jnp.asarray(hsum), jnp.asarray(hexp), jnp.asarray(ltri_attn, BF16), jnp.asarray(ltri_hgrn)


def kernel(x, positions, w_in, w_proj_attn, w_proj_hgrn, w_out, norm_mix, norm_ffn, q_norm, k_norm, hgrn_norm,
           hgrn_lower_bound, w_ffn_in, w_ffn_out):
    b, t, d = x.shape
    depth = w_in.shape[0]
    nt = b * t

    inv = ROPE_THETA ** (-jnp.arange(0, HEAD_DIM, 2, dtype=F32) / HEAD_DIM)
    ang = positions.astype(F32)[..., None] * inv
    cos, sin = jnp.cos(ang), jnp.sin(ang)
    cos1 = jnp.concatenate([cos, cos, cos, cos], axis=-1).reshape(nt, LANES)
    sin1 = jnp.concatenate([-sin, sin, -sin, sin], axis=-1).reshape(nt, LANES)

    lb_all = jnp.cumsum(jax.nn.softmax(hgrn_lower_bound.astype(F32), axis=0), axis=0)
    lb_all = lb_all - lb_all[:1]

    consts = _constants()
    widths = (ATTN_WIDTH, ATTN_WIDTH, ATTN_WIDTH, IDX_HEADS * IDX_DIM, IDX_DIM, IDX_HEADS,
              HG_WIDTH, HG_WIDTH, HG_WIDTH, HG_WIDTH, D_MODEL, D_MODEL)
    off = np.concatenate([[0], np.cumsum(widths)])
    idx_cols = off[6] - off[3]

    x2 = x.reshape(nt, d)
    for l in range(depth):
        wl = w_in[l]
        lb = lb_all[l]
        hconst = jnp.zeros((8, HG_WIDTH), F32)
        hconst = hconst.at[0].set(jnp.log(lb)).at[1].set(jnp.log1p(-lb)).at[2].set(1.0 - lb)
        hconst = hconst.at[3].set(jnp.tile(hgrn_norm[l], HG_HEADS))
        w = {
            "norm_mix": norm_mix[l].reshape(1, d),
            "norm_ffn": norm_ffn[l].reshape(1, d),
            "wa": wl[:, off[0]:off[3]].astype(BF16),
            "wi": jnp.pad(wl[:, off[3]:off[6]], ((0, 0), (0, 3 * LANES - idx_cols))),
            "wh": wl[:, off[6]:off[10]].astype(BF16),
            "wg": wl[:, off[10]:off[12]].astype(BF16),
            "qn": jnp.tile(q_norm[l], ATTN_HEADS).reshape(1, ATTN_WIDTH),
            "kn": jnp.tile(k_norm[l], ATTN_HEADS).reshape(1, ATTN_WIDTH),
            "hconst": hconst,
            "wpa": w_proj_attn[l].astype(BF16),
            "wph": w_proj_hgrn[l].astype(BF16),
            "wout": w_out[l].astype(BF16),
            "wffn_in": w_ffn_in[l].astype(BF16),
            "wffn_out": w_ffn_out[l].astype(BF16),
        }
        x2 = _layer(x2, b, t, cos1, sin1, consts, w)
    return x2.reshape(b, t, d)
```

```python
import functools

import numpy as np
import jax
import jax.numpy as jnp
from jax import lax
from jax.experimental import pallas as pl
from jax.experimental.pallas import tpu as pltpu

D_MODEL = 1024
ATTN_HEADS = 8
HEAD_DIM = 64
ATTN_WIDTH = ATTN_HEADS * HEAD_DIM
IDX_HEADS = 4
IDX_DIM = 64
MAX_TOPK = 256
HG_HEADS = 4
HG_DIM = 128
HG_WIDTH = HG_HEADS * HG_DIM
HG_CHUNK = 64
HG_SUB = 16
D_FF = 2816
ROPE_THETA = 10000.0
EPS = 1e-6
LOG2_E = 1.4426950408889634

LANES = 128
VMEM_LIMIT_BYTES = 56 * 1024 * 1024

KEY_BLOCK = 256
VT_ROWS = 80
INT_MIN = -(2 ** 31)
HALF_MIN = -(2 ** 15)
NEG_BIG = -1e30

F32 = jnp.float32
BF16 = jnp.bfloat16
HIGHEST = lax.Precision.HIGHEST


def _resident(shape):
    nd = len(shape)
    return pl.BlockSpec(shape, lambda *_: (0,) * nd, pipeline_mode=pl.Buffered(1))


def _params(n_axes):
    return pltpu.CompilerParams(dimension_semantics=("arbitrary",) * n_axes,
                                vmem_limit_bytes=VMEM_LIMIT_BYTES)


def _rmsnorm_rows(x, gain):
    return x * lax.rsqrt(jnp.mean(x * x, axis=-1, keepdims=True) + EPS) * gain


def _sigmoid(x):
    return 1.0 / (1.0 + jnp.exp(-x))


def _rope(x, cos, sin_signed):
    w = x.shape[-1]
    lane = lax.broadcasted_iota(jnp.int32, x.shape, 1)
    first_half = (lane & (HEAD_DIM - 1)) < HEAD_DIM // 2
    partner = jnp.where(first_half, pltpu.roll(x, w - HEAD_DIM // 2, 1), pltpu.roll(x, HEAD_DIM // 2, 1))
    return x * cos + partner * sin_signed


def _mix_in_kernel(x_ref, g_ref, wa_ref, wi_ref, wh_ref, qn_ref, kn_ref, cos_ref, sin_ref, hsum_ref, hexp_ref,
                   q_ref, k_ref, v_ref, iq_ref, ikw_ref, hg_ref):
    h = _rmsnorm_rows(x_ref[...], g_ref[...])
    hb = h.astype(BF16)
    cos1 = cos_ref[...]
    sin1 = sin_ref[...]
    cos4 = jnp.concatenate([cos1] * 4, axis=1)
    sin4 = jnp.concatenate([sin1] * 4, axis=1)

    def head_norm(a, gain):
        ss = jnp.dot(a * a, hsum_ref[...], precision=HIGHEST, preferred_element_type=F32)
        r = lax.rsqrt(ss * (1.0 / HEAD_DIM) + EPS)
        rf = jnp.dot(r, hexp_ref[...], precision=HIGHEST, preferred_element_type=F32)
        return a * rf * gain

    pa = jnp.dot(hb, wa_ref[...], preferred_element_type=F32)
    q = _rope(head_norm(pa[:, :ATTN_WIDTH], qn_ref[...]), cos4, sin4) * (HEAD_DIM ** -0.5 * LOG2_E)
    k = _rope(head_norm(pa[:, ATTN_WIDTH:2 * ATTN_WIDTH], kn_ref[...]), cos4, sin4)
    q_ref[...] = q.astype(BF16)
    k_ref[...] = k.astype(BF16)
    v_ref[...] = pa[:, 2 * ATTN_WIDTH:].astype(BF16)

    pi = jnp.dot(h, wi_ref[...], precision=HIGHEST, preferred_element_type=F32)
    nq = IDX_HEADS * IDX_DIM
    iq_ref[...] = _rope(pi[:, :nq], cos4[:, :nq], sin4[:, :nq]) * (IDX_DIM ** -0.5)
    slab = pi[:, nq:]
    lane = lax.broadcasted_iota(jnp.int32, slab.shape, 1)
    ikw_ref[...] = jnp.where(lane < IDX_DIM, _rope(slab, cos1, sin1), slab * (IDX_HEADS ** -0.5))

    hg_ref[...] = jnp.dot(hb, wh_ref[...], preferred_element_type=F32)


def _mix_in(x2, gain, wa, wi, wh, qn, kn, cos1, sin1, hsum, hexp, tm):
    nt = x2.shape[0]
    row = lambda w: pl.BlockSpec((tm, w), lambda i: (i, 0))
    return pl.pallas_call(
        _mix_in_kernel,
        grid=(nt // tm,),
        in_specs=[row(D_MODEL), _resident(gain.shape), _resident(wa.shape), _resident(wi.shape), _resident(wh.shape),
                  _resident(qn.shape), _resident(kn.shape), row(LANES), row(LANES),
                  _resident(hsum.shape), _resident(hexp.shape)],
        out_specs=[row(ATTN_WIDTH), row(ATTN_WIDTH), row(ATTN_WIDTH), row(IDX_HEADS * IDX_DIM), row(LANES),
                   row(4 * HG_WIDTH)],
        out_shape=[jax.ShapeDtypeStruct((nt, ATTN_WIDTH), BF16)] * 3
        + [jax.ShapeDtypeStruct((nt, IDX_HEADS * IDX_DIM), F32), jax.ShapeDtypeStruct((nt, LANES), F32),
           jax.ShapeDtypeStruct((nt, 4 * HG_WIDTH), F32)],
        compiler_params=_params(1),
        name="mix_in",
    )(x2, gain, wa, wi, wh, qn, kn, cos1, sin1, hsum, hexp)


def _dsa_kernel(iq3_ref, ik3_ref, iw_ref, qt_ref, k_ref, vt_ref, ltri_ref, o_ref,
                hi_ref, lo_ref, qz_ref, lg_ref, m_ref, alpha_ref, acc_ref, *, topk):
    kb = KEY_BLOCK
    j = pl.program_id(1)
    nk = j + 1
    int_min = jnp.int32(INT_MIN)

    zeros = jnp.zeros((HEAD_DIM, kb), BF16)
    for h in range(ATTN_HEADS):
        qh = qt_ref[0, HEAD_DIM * h:HEAD_DIM * (h + 1), :]
        qz_ref[h] = jnp.concatenate([qh, zeros] if h % 2 == 0 else [zeros, qh], axis=0)

    def score_chunk(c, diagonal):
        k3 = ik3_ref[0, c]
        acc = jnp.zeros((kb, kb), F32)
        for h in range(IDX_HEADS):
            rel = jnp.dot(k3, iq3_ref[0, h], preferred_element_type=F32)
            acc = acc + iw_ref[0, h:h + 1, :] * jnp.maximum(rel, 0.0)
        bits = pltpu.bitcast(acc, jnp.int32)
        key = jnp.where(bits < 0, int_min - bits, bits)
        if diagonal:
            kpos = lax.broadcasted_iota(jnp.int32, (kb, kb), 0)
            qpos = lax.broadcasted_iota(jnp.int32, (kb, kb), 1)
            key = jnp.where(kpos <= qpos, key, int_min)
        hi_ref[c] = lax.shift_right_arithmetic(key, 16).astype(jnp.int16)
        lo_ref[c] = (key ^ 0x8000).astype(jnp.int16)

    def score_pair(i, carry):
        score_chunk(2 * i, False)
        score_chunk(2 * i + 1, False)
        return carry

    lax.fori_loop(0, j // 2, score_pair, 0)

    @pl.when(j % 2 == 1)
    def _():
        score_chunk(j - 1, False)

    score_chunk(j, True)

    one_b, zero_b = jnp.ones((), BF16), jnp.zeros((), BF16)
    one_i, zero_i = jnp.ones((), jnp.int16), jnp.zeros((), jnp.int16)
    rows16 = 16

    hi_ref[nk] = jnp.full((kb, kb), HALF_MIN, jnp.int16)
    lo_ref[nk] = jnp.full((kb, kb), HALF_MIN, jnp.int16)

    def count(hit_fn):
        def fold(hit):
            parts = [hit[rows16 * r:rows16 * (r + 1)] for r in range(kb // rows16)]
            while len(parts) > 1:
                parts = [a + b for a, b in zip(parts[::2], parts[1::2])]
            return parts[0]

        def body(i, cnt):
            return cnt + (fold(hit_fn(2 * i)) + fold(hit_fn(2 * i + 1)))

        cnt = lax.fori_loop(0, (nk + 1) // 2, body, jnp.zeros((rows16, kb), jnp.int16))
        return jnp.sum(cnt.astype(jnp.int32), axis=0, keepdims=True).astype(F32)

    def bisect_half(ref, target):
        def bit_step(i, thr_u):
            cand_u = thr_u | lax.shift_left(jnp.int32(1), 15 - i)
            cand = (cand_u + HALF_MIN).astype(jnp.int16)
            cnt = count(lambda c: jnp.where(ref[c] >= cand, one_i, zero_i))
            return jnp.where(cnt >= target, cand_u, thr_u)
        return lax.fori_loop(0, 16, bit_step, jnp.zeros((1, kb), jnp.int32)) + HALF_MIN

    thr_hi32 = bisect_half(hi_ref, float(topk))
    thr_hi = thr_hi32.astype(jnp.int16)
    n_gt_hi = count(lambda c: jnp.where(hi_ref[c] > thr_hi, one_i, zero_i))

    def keep_matching_low(c, carry):
        lo_ref[c] = jnp.where(hi_ref[c] == thr_hi, lo_ref[c], jnp.int16(HALF_MIN))
        return carry

    lax.fori_loop(0, nk, keep_matching_low, 0)
    thr_lo32 = bisect_half(lo_ref, topk - n_gt_hi)
    thr_lo = thr_lo32.astype(jnp.int16)
    n_gt = n_gt_hi + count(lambda c: jnp.where(lo_ref[c] > thr_lo, one_i, zero_i))
    is_marker = (thr_hi32 == HALF_MIN) & (thr_lo32 == HALF_MIN)
    need = jnp.where(is_marker, 0.0, topk - n_gt)

    def bias_chunk(c, ties_before):
        hi, lo = hi_ref[c], lo_ref[c]
        above = jnp.where(hi > thr_hi, one_b, jnp.where(lo > thr_lo, one_b, zero_b)).astype(F32)
        eq_b = jnp.where(hi == thr_hi, jnp.where(lo == thr_lo, one_b, zero_b), zero_b)
        eq = eq_b.astype(F32)
        rank = jnp.dot(ltri_ref[...], eq_b, preferred_element_type=F32) + ties_before
        chosen = above + jnp.where(rank < need, eq, 0.0)
        bias = jnp.where(chosen > 0.5, 0.0, NEG_BIG).astype(BF16)
        hi_ref[c] = pltpu.bitcast(bias, jnp.int16)
        return ties_before + jnp.sum(eq, axis=0, keepdims=True)

    def bias_pair(i, ties_before):
        return bias_chunk(2 * i + 1, bias_chunk(2 * i, ties_before))

    lax.fori_loop(0, (nk + 1) // 2, bias_pair, jnp.zeros((1, kb), F32))

    m_ref[...] = jnp.full(m_ref.shape, NEG_BIG, F32)
    acc_ref[...] = jnp.zeros(acc_ref.shape, F32)

    def logits_head(c, h, bias):
        kk = k_ref[0, c, :, LANES * (h // 2):LANES * (h // 2 + 1)]
        lgb = jnp.dot(kk, qz_ref[h], preferred_element_type=F32) + bias
        lg_ref[h] = lgb
        m_old = m_ref[h]
        m_new = jnp.maximum(m_old, jnp.max(lgb, axis=0, keepdims=True))
        alpha_ref[h] = jnp.exp2(m_old - m_new)
        m_ref[h] = m_new

    def softmax_pv_head(c, h):
        p = jnp.exp2(lg_ref[h] - m_ref[h])
        pv = jnp.dot(vt_ref[0, c, VT_ROWS * h:VT_ROWS * (h + 1), :], p.astype(BF16),
                     preferred_element_type=F32)
        acc_ref[h] = alpha_ref[h] * acc_ref[h] + pv

    def chunk_bias(c):
        return pltpu.bitcast(hi_ref[c], BF16).astype(F32)

    bias0 = chunk_bias(0)
    for h in range(ATTN_HEADS):
        logits_head(0, h, bias0)

    def pipelined(c, carry):
        bias = chunk_bias(c)
        for h in range(ATTN_HEADS):
            softmax_pv_head(c - 1, h)
            logits_head(c, h, bias)
        return carry

    lax.fori_loop(1, nk, pipelined, 0)
    for h in range(ATTN_HEADS):
        softmax_pv_head(nk - 1, h)

    for h in range(ATTN_HEADS):
        acc = acc_ref[h]
        o_ref[0, HEAD_DIM * h:HEAD_DIM * (h + 1), :] = (
            acc[:HEAD_DIM] / acc[HEAD_DIM:HEAD_DIM + 1]).astype(o_ref.dtype)


def _dsa(iq3t, ik3, iwt, qt, k4, vt4, ltri, topk):
    b, _, t = qt.shape
    kb = KEY_BLOCK
    nkc = t // kb
    return pl.pallas_call(
        functools.partial(_dsa_kernel, topk=topk),
        grid=(b, nkc),
        in_specs=[
            pl.BlockSpec((1, IDX_HEADS, kb, kb), lambda i, j: (i, 0, 0, j)),
            pl.BlockSpec((1, nkc, kb, 4 * IDX_DIM), lambda i, j: (i, 0, 0, 0), pipeline_mode=pl.Buffered(1)),
            pl.BlockSpec((1, 8, kb), lambda i, j: (i, 0, j)),
            pl.BlockSpec((1, ATTN_WIDTH, kb), lambda i, j: (i, 0, j)),
            pl.BlockSpec((1, nkc, kb, ATTN_WIDTH), lambda i, j: (i, 0, 0, 0), pipeline_mode=pl.Buffered(1)),
            pl.BlockSpec((1, nkc, ATTN_HEADS * VT_ROWS, kb), lambda i, j: (i, 0, 0, 0),
                         pipeline_mode=pl.Buffered(1)),
            _resident(ltri.shape),
        ],
        out_specs=pl.BlockSpec((1, ATTN_WIDTH, kb), lambda i, j: (i, 0, j)),
        out_shape=jax.ShapeDtypeStruct((b, ATTN_WIDTH, t), BF16),
        scratch_shapes=[
            pltpu.VMEM((nkc + 1, kb, kb), jnp.int16),
            pltpu.VMEM((nkc + 1, kb, kb), jnp.int16),
            pltpu.VMEM((ATTN_HEADS, LANES, kb), BF16),
            pltpu.VMEM((ATTN_HEADS, kb, kb), F32),
            pltpu.VMEM((ATTN_HEADS, 1, kb), F32),
            pltpu.VMEM((ATTN_HEADS, 1, kb), F32),
            pltpu.VMEM((ATTN_HEADS, VT_ROWS, kb), F32),
        ],
        compiler_params=_params(2),
        name="dsa",
    )(iq3t, ik3, iwt, qt, k4, vt4, ltri)


def _hgrn_kernel(hq_ref, hf_ref, hi_ref, hg_ref, hc_ref, lt_ref, o_ref, st_ref):
    c_len, sub = HG_CHUNK, HG_SUB

    @pl.when(pl.program_id(1) == 0)
    def _():
        st_ref[...] = jnp.zeros(st_ref.shape, F32)

    log_lb = hc_ref[0:1, :]
    log1m_lb = hc_ref[1:2, :]
    one_m_lb = hc_ref[2:3, :]
    out_gain = hc_ref[3:4, :]

    hf = hf_ref[...]
    hq = hq_ref[...]
    hg = hg_ref[...]
    c = log1m_lb + jnp.minimum(hf, 0.0) - jnp.log1p(jnp.exp(-jnp.abs(hf)))
    log_f = jnp.maximum(log_lb, c) + jnp.log1p(jnp.exp(-jnp.abs(log_lb - c)))
    k_all = one_m_lb * _sigmoid(-hf)
    q_all = hq * _sigmoid(hq)
    gate = hg * _sigmoid(hg)
    b_all = jnp.dot(lt_ref[...], log_f, precision=HIGHEST, preferred_element_type=F32)
    b_last = b_all[c_len - 1:c_len, :]
    q_inter = q_all * jnp.exp(b_all)
    k_carry = k_all * jnp.exp(b_last - b_all)

    row = lax.broadcasted_iota(jnp.int32, (sub, 1), 0)
    col = lax.broadcasted_iota(jnp.int32, (sub, sub), 1)
    outs = []
    for h in range(HG_HEADS):
        hs = slice(HG_DIM * h, HG_DIM * (h + 1))
        b, q, k, v = b_all[:, hs], q_all[:, hs], k_all[:, hs], hi_ref[:, hs]
        vb = v.astype(BF16)
        st = st_ref[h]
        inter = lax.dot_general(q_inter[:, hs].astype(BF16), st.astype(BF16), (((1,), (1,)), ((), ())),
                                preferred_element_type=F32)
        blocks = []
        for i in range(c_len // sub):
            rs = slice(sub * i, sub * (i + 1))
            b_i, q_i, k_i = b[rs], q[rs], k[rs]
            diag = jnp.zeros((sub, sub), F32)
            for s in range(sub):
                decay = jnp.exp(jnp.where(row >= s, b_i - b_i[s:s + 1, :], -jnp.inf))
                a_col = jnp.sum(q_i * k_i[s:s + 1, :] * decay, axis=1, keepdims=True)
                diag = jnp.where(col == s, a_col, diag)
            o_i = inter[rs] + jnp.dot(diag.astype(BF16), vb[rs], preferred_element_type=F32)
            if i > 0:
                prev = slice(0, sub * i)
                b_ref = b[sub * i - 1:sub * i, :]
                q_hat = (q_i * jnp.exp(b_i - b_ref)).astype(BF16)
                k_hat = (k[prev] * jnp.exp(b_ref - b[prev])).astype(BF16)
                a_off = lax.dot_general(q_hat, k_hat, (((1,), (1,)), ((), ())), preferred_element_type=F32)
                o_i = o_i + jnp.dot(a_off.astype(BF16), vb[prev], preferred_element_type=F32)
            blocks.append(o_i)
        o = jnp.concatenate(blocks, axis=0)
        st_ref[h] = jnp.exp(b_last[:, hs]) * st + jnp.dot(v.T.astype(BF16), k_carry[:, hs].astype(BF16),
                                                         preferred_element_type=F32)
        outs.append(_rmsnorm_rows(o, out_gain[:, hs]) * gate[:, hs])
    o_ref[...] = jnp.concatenate(outs, axis=1).astype(o_ref.dtype)


def _hgrn(hg4, hconst, ltri, b, t):
    c = HG_CHUNK
    nc = t // c
    col = lambda jcol: pl.BlockSpec((c, HG_WIDTH), lambda i, s, jcol=jcol: (i * nc + s, jcol))
    return pl.pallas_call(
        _hgrn_kernel,
        grid=(b, nc),
        in_specs=[col(0), col(1), col(2), col(3), _resident(hconst.shape), _resident(ltri.shape)],
        out_specs=pl.BlockSpec((c, HG_WIDTH), lambda i, s: (i * nc + s, 0)),
        out_shape=jax.ShapeDtypeStruct((b * t, HG_WIDTH), BF16),
        scratch_shapes=[pltpu.VMEM((HG_HEADS, HG_DIM, HG_DIM), F32)],
        compiler_params=_params(2),
        name="hgrn",
    )(hg4, hg4, hg4, hg4, hconst, ltri)


def _merge_kernel(x_ref, ya_ref, yh_ref, g_ref, wg_ref, wpa_ref, wph_ref, wo_ref, o_ref):
    x = x_ref[...]
    hb = _rmsnorm_rows(x, g_ref[...]).astype(BF16)
    gates = jnp.dot(hb, wg_ref[...], preferred_element_type=F32)
    pa = jnp.dot(ya_ref[...], wpa_ref[...], preferred_element_type=F32)
    ph = jnp.dot(yh_ref[...], wph_ref[...], preferred_element_type=F32)
    merged = _sigmoid(gates[:, :D_MODEL]) * pa + _sigmoid(gates[:, D_MODEL:]) * ph
    o_ref[...] = x + jnp.dot(merged.astype(BF16), wo_ref[...], preferred_element_type=F32)


def _merge(x2, ya, yh, gain, wg, wpa, wph, wo, tm):
    nt = x2.shape[0]
    row = lambda w: pl.BlockSpec((tm, w), lambda i: (i, 0))
    return pl.pallas_call(
        _merge_kernel,
        grid=(nt // tm,),
        in_specs=[row(D_MODEL), row(ATTN_WIDTH), row(HG_WIDTH), _resident(gain.shape), _resident(wg.shape),
                  _resident(wpa.shape), _resident(wph.shape), _resident(wo.shape)],
        out_specs=row(D_MODEL),
        out_shape=jax.ShapeDtypeStruct((nt, D_MODEL), F32),
        compiler_params=_params(1),
        name="merge",
    )(x2, ya, yh, gain, wg, wpa, wph, wo)


def _ffn_kernel(x_ref, g_ref, wi_ref, wo_ref, o_ref):
    x = x_ref[...]
    hb = _rmsnorm_rows(x, g_ref[...]).astype(BF16)
    gu = jnp.dot(hb, wi_ref[...], preferred_element_type=F32)
    g = gu[:, :D_FF]
    act = g * _sigmoid(g) * gu[:, D_FF:]
    o_ref[...] = x + jnp.dot(act.astype(BF16), wo_ref[...], preferred_element_type=F32)


def _ffn(x2, gain, wi, wo, tm):
    nt = x2.shape[0]
    row = pl.BlockSpec((tm, D_MODEL), lambda i: (i, 0))
    return pl.pallas_call(
        _ffn_kernel,
        grid=(nt // tm,),
        in_specs=[row, _resident(gain.shape), _resident(wi.shape), _resident(wo.shape)],
        out_specs=row,
        out_shape=jax.ShapeDtypeStruct((nt, D_MODEL), F32),
        compiler_params=_params(1),
        name="ffn",
    )(x2, gain, wi, wo)


def _split_hi_lo(a):
    hi = a.astype(BF16)
    lo = (a - hi.astype(F32)).astype(BF16)
    return hi, lo


def _layer(x2, b, t, cos1, sin1, consts, w):
    hsum, hexp, ltri_attn, ltri_hgrn = consts
    kb = KEY_BLOCK
    nkc = t // kb
    q, k, v, iq, ikw, hg4 = _mix_in(x2, w["norm_mix"], w["wa"], w["wi"], w["wh"], w["qn"], w["kn"],
                                    cos1, sin1, hsum, hexp, tm=512)
    qt = q.reshape(b, t, ATTN_WIDTH).transpose(0, 2, 1)
    k4 = k.reshape(b, nkc, kb, ATTN_WIDTH)
    vt = v.reshape(b, nkc, kb, ATTN_HEADS, HEAD_DIM).transpose(0, 1, 3, 4, 2)
    ones_pad = jnp.zeros((b, nkc, ATTN_HEADS, VT_ROWS - HEAD_DIM, kb), BF16).at[:, :, :, 0, :].set(1.0)
    vt4 = jnp.concatenate([vt, ones_pad], axis=3).reshape(b, nkc, ATTN_HEADS * VT_ROWS, kb)
    iq_hi, iq_lo = _split_hi_lo(iq.reshape(b, t, IDX_HEADS, IDX_DIM))
    iq3t = jnp.concatenate([iq_hi, iq_hi, iq_lo, jnp.zeros_like(iq_hi)], axis=-1).transpose(0, 2, 3, 1)
    ik_hi, ik_lo = _split_hi_lo(ikw[:, :IDX_DIM])
    ik3 = jnp.concatenate([ik_hi, ik_lo, ik_hi, jnp.zeros_like(ik_hi)], axis=-1).reshape(b, nkc, kb, 4 * IDX_DIM)
    iwt = jnp.pad(ikw[:, IDX_DIM:IDX_DIM + IDX_HEADS].reshape(b, t, IDX_HEADS), ((0, 0), (0, 0), (0, 8 - IDX_HEADS))
                  ).transpose(0, 2, 1)
    y_attn_t = _dsa(iq3t, ik3, iwt, qt, k4, vt4, ltri_attn, topk=min(MAX_TOPK, t // 4))
    y_attn = y_attn_t.transpose(0, 2, 1).reshape(b * t, ATTN_WIDTH)
    y_hgrn = _hgrn(hg4, w["hconst"], ltri_hgrn, b, t)
    x2 = _merge(x2, y_attn, y_hgrn, w["norm_mix"], w["wg"], w["wpa"], w["wph"], w["wout"], tm=512)
    return _ffn(x2, w["norm_ffn"], w["wffn_in"], w["wffn_out"], tm=256)


def _constants():
    head_of_lane = np.arange(ATTN_WIDTH) // HEAD_DIM
    hsum = (head_of_lane[:, None] == np.arange(LANES)[None, :]).astype(np.float32)
    hexp = hsum.T.copy()
    r = np.arange(KEY_BLOCK)
    ltri_attn = (r[None, :] < r[:, None]).astype(np.float32)
    r = np.arange(HG_CHUNK)
    ltri_hgrn = (r[None, :] <= r[:, None]).astype(np.float32)
    return jnp.asarray(hsum), jnp.asarray(hexp), jnp.asarray(ltri_attn, BF16), jnp.asarray(ltri_hgrn)


def kernel(x, positions, w_in, w_proj_attn, w_proj_hgrn, w_out, norm_mix, norm_ffn, q_norm, k_norm, hgrn_norm,
           hgrn_lower_bound, w_ffn_in, w_ffn_out):
    b, t, d = x.shape
    depth = w_in.shape[0]
    nt = b * t

    inv = ROPE_THETA ** (-jnp.arange(0, HEAD_DIM, 2, dtype=F32) / HEAD_DIM)
    ang = positions.astype(F32)[..., None] * inv
    cos, sin = jnp.cos(ang), jnp.sin(ang)
    cos1 = jnp.concatenate([cos, cos, cos, cos], axis=-1).reshape(nt, LANES)
    sin1 = jnp.concatenate([-sin, sin, -sin, sin], axis=-1).reshape(nt, LANES)

    lb_all = jnp.cumsum(jax.nn.softmax(hgrn_lower_bound.astype(F32), axis=0), axis=0)
    lb_all = lb_all - lb_all[:1]

    consts = _constants()
    widths = (ATTN_WIDTH, ATTN_WIDTH, ATTN_WIDTH, IDX_HEADS * IDX_DIM, IDX_DIM, IDX_HEADS,
              HG_WIDTH, HG_WIDTH, HG_WIDTH, HG_WIDTH, D_MODEL, D_MODEL)
    off = np.concatenate([[0], np.cumsum(widths)])
    idx_cols = off[6] - off[3]

    x2 = x.reshape(nt, d)
    for l in range(depth):
        wl = w_in[l]
        lb = lb_all[l]
        hconst = jnp.zeros((8, HG_WIDTH), F32)
        hconst = hconst.at[0].set(jnp.log(lb)).at[1].set(jnp.log1p(-lb)).at[2].set(1.0 - lb)
        hconst = hconst.at[3].set(jnp.tile(hgrn_norm[l], HG_HEADS))
        w = {
            "norm_mix": norm_mix[l].reshape(1, d),
            "norm_ffn": norm_ffn[l].reshape(1, d),
            "wa": wl[:, off[0]:off[3]].astype(BF16),
            "wi": jnp.pad(wl[:, off[3]:off[6]], ((0, 0), (0, 3 * LANES - idx_cols))),
            "wh": wl[:, off[6]:off[10]].astype(BF16),
            "wg": wl[:, off[10]:off[12]].astype(BF16),
            "qn": jnp.tile(q_norm[l], ATTN_HEADS).reshape(1, ATTN_WIDTH),
            "kn": jnp.tile(k_norm[l], ATTN_HEADS).reshape(1, ATTN_WIDTH),
            "hconst": hconst,
            "wpa": w_proj_attn[l].astype(BF16),
            "wph": w_proj_hgrn[l].astype(BF16),
            "wout": w_out[l].astype(BF16),
            "wffn_in": w_ffn_in[l].astype(BF16),
            "wffn_out": w_ffn_out[l].astype(BF16),
        }
        x2 = _layer(x2, b, t, cos1, sin1, consts, w)
    return x2.reshape(b, t, d)
```

```python
import functools

import numpy as np
import jax
import jax.numpy as jnp
from jax import lax
from jax.experimental import pallas as pl
from jax.experimental.pallas import tpu as pltpu

D_MODEL = 1024
ATTN_HEADS = 8
HEAD_DIM = 64
ATTN_WIDTH = ATTN_HEADS * HEAD_DIM
IDX_HEADS = 4
IDX_DIM = 64
MAX_TOPK = 256
HG_HEADS = 4
HG_DIM = 128
HG_WIDTH = HG_HEADS * HG_DIM
HG_CHUNK = 64
HG_STREAMS = 5
HG_STEP_CHUNKS = 2
HG_SUB = 16
HG_FACT = 32
MAX_SAFE_EXPONENT = 80.0
D_FF = 2816
ROPE_THETA = 10000.0
EPS = 1e-6
LOG2_E = 1.4426950408889634

LANES = 128
VMEM_LIMIT_BYTES = 56 * 1024 * 1024

KEY_BLOCK = 256
VT_ROWS = 80
INT_MIN = -(2 ** 31)
HALF_MIN = -(2 ** 15)
NEG_BIG = -1e30

F32 = jnp.float32
BF16 = jnp.bfloat16
HIGHEST = lax.Precision.HIGHEST


def _resident(shape):
    nd = len(shape)
    return pl.BlockSpec(shape, lambda *_: (0,) * nd, pipeline_mode=pl.Buffered(1))


def _params(n_axes):
    return pltpu.CompilerParams(dimension_semantics=("arbitrary",) * n_axes,
                                vmem_limit_bytes=VMEM_LIMIT_BYTES)


def _rmsnorm_rows(x, gain):
    return x * lax.rsqrt(jnp.mean(x * x, axis=-1, keepdims=True) + EPS) * gain


def _sigmoid(x):
    return 1.0 / (1.0 + jnp.exp(-x))


def _rope(x, cos, sin_signed):
    w = x.shape[-1]
    lane = lax.broadcasted_iota(jnp.int32, x.shape, 1)
    first_half = (lane & (HEAD_DIM - 1)) < HEAD_DIM // 2
    partner = jnp.where(first_half, pltpu.roll(x, w - HEAD_DIM // 2, 1), pltpu.roll(x, HEAD_DIM // 2, 1))
    return x * cos + partner * sin_signed


def _dot_hi_lo(a, b_hi, b_lo=None):
    a_hi = a.astype(BF16)
    a_lo = (a - a_hi.astype(F32)).astype(BF16)
    out = jnp.dot(a_hi, b_hi, preferred_element_type=F32) + jnp.dot(a_lo, b_hi, preferred_element_type=F32)
    if b_lo is not None:
        out = out + jnp.dot(a_hi, b_lo, preferred_element_type=F32)
    return out


def _mix_in_kernel(x_ref, g_ref, wa_ref, wi_ref, wil_ref, wh_ref, qn_ref, kn_ref, cos_ref, sin_ref, hsum_ref, hexp_ref,
                   hc_ref, q_ref, k_ref, v_ref, iq_ref, ikw_ref, hg_ref):
    h = _rmsnorm_rows(x_ref[...], g_ref[...])
    hb = h.astype(BF16)
    cos1 = cos_ref[...]
    sin1 = sin_ref[...]
    cos4 = jnp.concatenate([cos1] * 4, axis=1)
    sin4 = jnp.concatenate([sin1] * 4, axis=1)

    def head_norm(a, gain):
        ss = _dot_hi_lo(a * a, hsum_ref[...])
        r = lax.rsqrt(ss * (1.0 / HEAD_DIM) + EPS)
        return a * _dot_hi_lo(r, hexp_ref[...]) * gain

    pa = jnp.dot(hb, wa_ref[...], preferred_element_type=F32)
    q = _rope(head_norm(pa[:, :ATTN_WIDTH], qn_ref[...]), cos4, sin4) * (HEAD_DIM ** -0.5 * LOG2_E)
    k = _rope(head_norm(pa[:, ATTN_WIDTH:2 * ATTN_WIDTH], kn_ref[...]), cos4, sin4)
    q_ref[...] = q.astype(BF16)
    k_ref[...] = k.astype(BF16)
    v_ref[...] = pa[:, 2 * ATTN_WIDTH:].astype(BF16)

    pi = _dot_hi_lo(h, wi_ref[...], wil_ref[...])
    nq = IDX_HEADS * IDX_DIM
    iq_ref[...] = _rope(pi[:, :nq], cos4[:, :nq], sin4[:, :nq]) * (IDX_DIM ** -0.5)
    slab = pi[:, nq:]
    lane = lax.broadcasted_iota(jnp.int32, slab.shape, 1)
    ikw_ref[...] = jnp.where(lane < IDX_DIM, _rope(slab, cos1, sin1), slab * (IDX_HEADS ** -0.5))

    ph = jnp.dot(hb, wh_ref[...], preferred_element_type=F32)
    w = HG_WIDTH
    hq, hf, hg = ph[:, :w], ph[:, w:2 * w], ph[:, 3 * w:]
    log_lb, log1m_lb, one_m_lb = hc_ref[0:1, :], hc_ref[1:2, :], hc_ref[2:3, :]
    c = log1m_lb + jnp.minimum(hf, 0.0) - jnp.log1p(jnp.exp(-jnp.abs(hf)))
    hg_ref[:, :w] = hq * _sigmoid(hq)
    hg_ref[:, w:2 * w] = jnp.maximum(log_lb, c) + jnp.log1p(jnp.exp(-jnp.abs(log_lb - c)))
    hg_ref[:, 2 * w:3 * w] = one_m_lb * _sigmoid(-hf)
    hg_ref[:, 3 * w:4 * w] = ph[:, 2 * w:3 * w]
    hg_ref[:, 4 * w:] = hg * _sigmoid(hg)


def _mix_in(x2, gain, wa, wi, wil, wh, qn, kn, cos1, sin1, hsum, hexp, hconst, tm):
    nt = x2.shape[0]
    row = lambda w: pl.BlockSpec((tm, w), lambda i: (i, 0))
    return pl.pallas_call(
        _mix_in_kernel,
        grid=(nt // tm,),
        in_specs=[row(D_MODEL), _resident(gain.shape), _resident(wa.shape), _resident(wi.shape), _resident(wil.shape),
                  _resident(wh.shape),
                  _resident(qn.shape), _resident(kn.shape), row(LANES), row(LANES),
                  _resident(hsum.shape), _resident(hexp.shape), _resident(hconst.shape)],
        out_specs=[row(ATTN_WIDTH), row(ATTN_WIDTH), row(ATTN_WIDTH), row(IDX_HEADS * IDX_DIM), row(LANES),
                   row(HG_STREAMS * HG_WIDTH)],
        out_shape=[jax.ShapeDtypeStruct((nt, ATTN_WIDTH), BF16)] * 3
        + [jax.ShapeDtypeStruct((nt, IDX_HEADS * IDX_DIM), F32), jax.ShapeDtypeStruct((nt, LANES), F32),
           jax.ShapeDtypeStruct((nt, HG_STREAMS * HG_WIDTH), F32)],
        compiler_params=_params(1),
        name="mix_in",
    )(x2, gain, wa, wi, wil, wh, qn, kn, cos1, sin1, hsum, hexp, hconst)


def _dsa_kernel(iq3_ref, ik3_ref, iw_ref, qt_ref, k_ref, vt_ref, ltri_ref, o_ref,
                hi_ref, lo_ref, qz_ref, lg_ref, m_ref, alpha_ref, acc_ref, *, topk):
    kb = KEY_BLOCK
    j = pl.program_id(1)
    nk = j + 1
    int_min = jnp.int32(INT_MIN)

    zeros = jnp.zeros((HEAD_DIM, kb), BF16)
    for h in range(ATTN_HEADS):
        qh = qt_ref[0, HEAD_DIM * h:HEAD_DIM * (h + 1), :]
        qz_ref[h] = jnp.concatenate([qh, zeros] if h % 2 == 0 else [zeros, qh], axis=0)

    def score_chunk(c, diagonal):
        k3 = ik3_ref[0, c]
        acc = jnp.zeros((kb, kb), F32)
        for h in range(IDX_HEADS):
            rel = jnp.dot(k3, iq3_ref[0, h], preferred_element_type=F32)
            acc = acc + iw_ref[0, h:h + 1, :] * jnp.maximum(rel, 0.0)
        bits = pltpu.bitcast(acc, jnp.int32)
        key = jnp.where(bits < 0, int_min - bits, bits)
        if diagonal:
            kpos = lax.broadcasted_iota(jnp.int32, (kb, kb), 0)
            qpos = lax.broadcasted_iota(jnp.int32, (kb, kb), 1)
            key = jnp.where(kpos <= qpos, key, int_min)
        hi_ref[c] = lax.shift_right_arithmetic(key, 16).astype(jnp.int16)
        lo_ref[c] = (key ^ 0x8000).astype(jnp.int16)

    def score_pair(i, carry):
        score_chunk(2 * i, False)
        score_chunk(2 * i + 1, False)
        return carry

    lax.fori_loop(0, j // 2, score_pair, 0)

    @pl.when(j % 2 == 1)
    def _():
        score_chunk(j - 1, False)

    score_chunk(j, True)

    one_b, zero_b = jnp.ones((), BF16), jnp.zeros((), BF16)
    one_i, zero_i = jnp.ones((), jnp.int16), jnp.zeros((), jnp.int16)
    rows16 = 16

    hi_ref[nk] = jnp.full((kb, kb), HALF_MIN, jnp.int16)
    lo_ref[nk] = jnp.full((kb, kb), HALF_MIN, jnp.int16)

    def count(hit_fn):
        def fold(hit):
            parts = [hit[rows16 * r:rows16 * (r + 1)] for r in range(kb // rows16)]
            while len(parts) > 1:
                parts = [a + b for a, b in zip(parts[::2], parts[1::2])]
            return parts[0]

        def body(i, cnt):
            return cnt + (fold(hit_fn(2 * i)) + fold(hit_fn(2 * i + 1)))

        cnt = lax.fori_loop(0, (nk + 1) // 2, body, jnp.zeros((rows16, kb), jnp.int16))
        return jnp.sum(cnt.astype(jnp.int32), axis=0, keepdims=True).astype(F32)

    def bisect_half(ref, target):
        def bit_step(i, thr_u):
            cand_u = thr_u | lax.shift_left(jnp.int32(1), 15 - i)
            cand = (cand_u + HALF_MIN).astype(jnp.int16)
            cnt = count(lambda c: jnp.where(ref[c] >= cand, one_i, zero_i))
            return jnp.where(cnt >= target, cand_u, thr_u)
        return lax.fori_loop(0, 16, bit_step, jnp.zeros((1, kb), jnp.int32)) + HALF_MIN

    thr_hi32 = bisect_half(hi_ref, float(topk))
    thr_hi = thr_hi32.astype(jnp.int16)
    n_gt_hi = count(lambda c: jnp.where(hi_ref[c] > thr_hi, one_i, zero_i))

    def keep_matching_low(c, carry):
        lo_ref[c] = jnp.where(hi_ref[c] == thr_hi, lo_ref[c], jnp.int16(HALF_MIN))
        return carry

    lax.fori_loop(0, nk, keep_matching_low, 0)
    thr_lo32 = bisect_half(lo_ref, topk - n_gt_hi)
    thr_lo = thr_lo32.astype(jnp.int16)
    n_gt = n_gt_hi + count(lambda c: jnp.where(lo_ref[c] > thr_lo, one_i, zero_i))
    is_marker = (thr_hi32 == HALF_MIN) & (thr_lo32 == HALF_MIN)
    need = jnp.where(is_marker, 0.0, topk - n_gt)

    def bias_chunk(c, ties_before):
        hi, lo = hi_ref[c], lo_ref[c]
        above = jnp.where(hi > thr_hi, one_b, jnp.where(lo > thr_lo, one_b, zero_b)).astype(F32)
        eq_b = jnp.where(hi == thr_hi, jnp.where(lo == thr_lo, one_b, zero_b), zero_b)
        eq = eq_b.astype(F32)
        rank = jnp.dot(ltri_ref[...], eq_b, preferred_element_type=F32) + ties_before
        chosen = above + jnp.where(rank < need, eq, 0.0)
        bias = jnp.where(chosen > 0.5, 0.0, NEG_BIG).astype(BF16)
        hi_ref[c] = pltpu.bitcast(bias, jnp.int16)
        return ties_before + jnp.sum(eq, axis=0, keepdims=True)

    def bias_pair(i, ties_before):
        return bias_chunk(2 * i + 1, bias_chunk(2 * i, ties_before))

    lax.fori_loop(0, (nk + 1) // 2, bias_pair, jnp.zeros((1, kb), F32))

    m_ref[...] = jnp.full(m_ref.shape, NEG_BIG, F32)
    acc_ref[...] = jnp.zeros(acc_ref.shape, F32)

    def logits_head(c, h, bias):
        kk = k_ref[0, c, :, LANES * (h // 2):LANES * (h // 2 + 1)]
        lgb = jnp.dot(kk, qz_ref[h], preferred_element_type=F32) + bias
        lg_ref[h] = lgb
        m_old = m_ref[h]
        m_new = jnp.maximum(m_old, jnp.max(lgb, axis=0, keepdims=True))
        alpha_ref[h] = jnp.exp2(m_old - m_new)
        m_ref[h] = m_new

    def softmax_pv_head(c, h):
        p = jnp.exp2(lg_ref[h] - m_ref[h])
        pv = jnp.dot(vt_ref[0, c, VT_ROWS * h:VT_ROWS * (h + 1), :], p.astype(BF16),
                     preferred_element_type=F32)
        acc_ref[h] = alpha_ref[h] * acc_ref[h] + pv

    def chunk_bias(c):
        return pltpu.bitcast(hi_ref[c], BF16).astype(F32)

    bias0 = chunk_bias(0)
    for h in range(ATTN_HEADS):
        logits_head(0, h, bias0)

    def pipelined(c, carry):
        bias = chunk_bias(c)
        for h in range(ATTN_HEADS):
            softmax_pv_head(c - 1, h)
            logits_head(c, h, bias)
        return carry

    lax.fori_loop(1, nk, pipelined, 0)
    for h in range(ATTN_HEADS):
        softmax_pv_head(nk - 1, h)

    for h in range(ATTN_HEADS):
        acc = acc_ref[h]
        o_ref[0, HEAD_DIM * h:HEAD_DIM * (h + 1), :] = (
            acc[:HEAD_DIM] / acc[HEAD_DIM:HEAD_DIM + 1]).astype(o_ref.dtype)


def _dsa(iq3t, ik3, iwt, qt, k4, vt4, ltri, topk):
    b, _, t = qt.shape
    kb = KEY_BLOCK
    nkc = t // kb
    return pl.pallas_call(
        functools.partial(_dsa_kernel, topk=topk),
        grid=(b, nkc),
        in_specs=[
            pl.BlockSpec((1, IDX_HEADS, kb, kb), lambda i, j: (i, 0, 0, j)),
            pl.BlockSpec((1, nkc, kb, 4 * IDX_DIM), lambda i, j: (i, 0, 0, 0), pipeline_mode=pl.Buffered(1)),
            pl.BlockSpec((1, 8, kb), lambda i, j: (i, 0, j)),
            pl.BlockSpec((1, ATTN_WIDTH, kb), lambda i, j: (i, 0, j)),
            pl.BlockSpec((1, nkc, kb, ATTN_WIDTH), lambda i, j: (i, 0, 0, 0), pipeline_mode=pl.Buffered(1)),
            pl.BlockSpec((1, nkc, ATTN_HEADS * VT_ROWS, kb), lambda i, j: (i, 0, 0, 0),
                         pipeline_mode=pl.Buffered(1)),
            _resident(ltri.shape),
        ],
        out_specs=pl.BlockSpec((1, ATTN_WIDTH, kb), lambda i, j: (i, 0, j)),
        out_shape=jax.ShapeDtypeStruct((b, ATTN_WIDTH, t), BF16),
        scratch_shapes=[
            pltpu.VMEM((nkc + 1, kb, kb), jnp.int16),
            pltpu.VMEM((nkc + 1, kb, kb), jnp.int16),
            pltpu.VMEM((ATTN_HEADS, LANES, kb), BF16),
            pltpu.VMEM((ATTN_HEADS, kb, kb), F32),
            pltpu.VMEM((ATTN_HEADS, 1, kb), F32),
            pltpu.VMEM((ATTN_HEADS, 1, kb), F32),
            pltpu.VMEM((ATTN_HEADS, VT_ROWS, kb), F32),
        ],
        compiler_params=_params(2),
        name="dsa",
    )(iq3t, ik3, iwt, qt, k4, vt4, ltri)


def _hgrn_kernel(q_ref, lf_ref, k_ref, v_ref, gate_ref, hc_ref, lt_ref, o_ref, st_ref, intra_ref):
    @pl.when(pl.program_id(1) == 0)
    def _():
        st_ref[...] = jnp.zeros(st_ref.shape, F32)

    for ci in range(HG_STEP_CHUNKS):
        rows = slice(HG_CHUNK * ci, HG_CHUNK * (ci + 1))
        _hgrn_chunk(q_ref[rows, :], lf_ref[rows, :], k_ref[rows, :], v_ref.at[rows, :], gate_ref[rows, :],
                    hc_ref[3:4, :], lt_ref, o_ref.at[rows, :], st_ref, intra_ref)


def _hgrn_chunk(q_all, log_f, k_all, hi_ref, gate, out_gain, lt_ref, o_ref, st_ref, intra_ref):
    c_len, sub = HG_CHUNK, HG_SUB
    b_all = jnp.dot(lt_ref[...], log_f, precision=HIGHEST, preferred_element_type=F32)
    b_last = b_all[c_len - 1:c_len, :]
    q_inter = q_all * jnp.exp(b_all)
    k_carry = k_all * jnp.exp(b_last - b_all)

    n_sub = c_len // sub
    trans_b = (((1,), (1,)), ((), ()))

    def block_start(b, i):
        return b[sub * i - 1:sub * i, :] if i > 0 else jnp.zeros_like(b[0:1, :])

    def intra_factored_all():
        fs = HG_FACT
        heads = [slice(HG_DIM * h, HG_DIM * (h + 1)) for h in range(HG_HEADS)]
        scores = []
        for hs in heads:
            b, q, k = b_all[:, hs], q_all[:, hs], k_all[:, hs]
            for i in range(c_len // fs):
                rs, upto = slice(fs * i, fs * (i + 1)), slice(0, fs * (i + 1))
                b_ref = b[fs * i - 1:fs * i, :] if i > 0 else jnp.zeros_like(b[0:1, :])
                q_hat = (q[rs] * jnp.exp(b[rs] - b_ref)).astype(BF16)
                k_hat = (k[upto] * jnp.exp(b_ref - b[upto])).astype(BF16)
                scores.append(lax.dot_general(q_hat, k_hat, trans_b, preferred_element_type=F32))
        for h, hs in enumerate(heads):
            vb = hi_ref[:, hs].astype(BF16)
            blocks = []
            for i in range(c_len // fs):
                a = scores[h * (c_len // fs) + i]
                t_idx = fs * i + lax.broadcasted_iota(jnp.int32, a.shape, 0)
                s_idx = lax.broadcasted_iota(jnp.int32, a.shape, 1)
                a = jnp.where(s_idx <= t_idx, a, 0.0).astype(BF16)
                blocks.append(jnp.dot(a, vb[0:fs * (i + 1)], preferred_element_type=F32))
            intra_ref[h] = jnp.concatenate(blocks, axis=0)

    def intra_direct(b, q, k, vb):
        row = lax.broadcasted_iota(jnp.int32, (sub, 1), 0)
        col = lax.broadcasted_iota(jnp.int32, (sub, sub), 1)
        blocks = []
        for i in range(n_sub):
            rs = slice(sub * i, sub * (i + 1))
            b_i, q_i, k_i = b[rs], q[rs], k[rs]
            diag = jnp.zeros((sub, sub), F32)
            for s in range(sub):
                decay = jnp.exp(jnp.where(row >= s, b_i - b_i[s:s + 1, :], -jnp.inf))
                a_col = jnp.sum(q_i * k_i[s:s + 1, :] * decay, axis=1, keepdims=True)
                diag = jnp.where(col == s, a_col, diag)
            o_i = jnp.dot(diag.astype(BF16), vb[rs], preferred_element_type=F32)
            if i > 0:
                prev = slice(0, sub * i)
                b_ref = block_start(b, i)
                q_hat = (q_i * jnp.exp(b_i - b_ref)).astype(BF16)
                k_hat = (k[prev] * jnp.exp(b_ref - b[prev])).astype(BF16)
                a_off = lax.dot_general(q_hat, k_hat, trans_b, preferred_element_type=F32)
                o_i = o_i + jnp.dot(a_off.astype(BF16), vb[prev], preferred_element_type=F32)
            blocks.append(o_i)
        return jnp.concatenate(blocks, axis=0)

    ends = [b_all[HG_FACT * (i + 1) - 1:HG_FACT * (i + 1), :] for i in range(c_len // HG_FACT)]
    drops = [(ends[i - 1] if i > 0 else 0.0) - ends[i] for i in range(len(ends))]
    factored_ok = jnp.max(functools.reduce(jnp.maximum, drops)) < MAX_SAFE_EXPONENT

    @pl.when(factored_ok)
    def _():
        intra_factored_all()

    @pl.when(jnp.logical_not(factored_ok))
    def _():
        for h in range(HG_HEADS):
            hs = slice(HG_DIM * h, HG_DIM * (h + 1))
            intra_ref[h] = intra_direct(b_all[:, hs], q_all[:, hs], k_all[:, hs], hi_ref[:, hs].astype(BF16))

    outs = []
    for h in range(HG_HEADS):
        hs = slice(HG_DIM * h, HG_DIM * (h + 1))
        v = hi_ref[:, hs]
        st = st_ref[h]
        inter = lax.dot_general(q_inter[:, hs].astype(BF16), st.astype(BF16), trans_b,
                                preferred_element_type=F32)
        o = inter + intra_ref[h]
        st_ref[h] = jnp.exp(b_last[:, hs]) * st + jnp.dot(v.T.astype(BF16), k_carry[:, hs].astype(BF16),
                                                         preferred_element_type=F32)
        outs.append(_rmsnorm_rows(o, out_gain[:, hs]) * gate[:, hs])
    o_ref[...] = jnp.concatenate(outs, axis=1).astype(o_ref.dtype)


def _hgrn(hg5, hconst, ltri, b, t):
    rows = HG_CHUNK * HG_STEP_CHUNKS
    ns = t // rows
    col = lambda jcol: pl.BlockSpec((rows, HG_WIDTH), lambda i, s, jcol=jcol: (i * ns + s, jcol))
    return pl.pallas_call(
        _hgrn_kernel,
        grid=(b, ns),
        in_specs=[col(jcol) for jcol in range(HG_STREAMS)] + [_resident(hconst.shape), _resident(ltri.shape)],
        out_specs=pl.BlockSpec((rows, HG_WIDTH), lambda i, s: (i * ns + s, 0)),
        out_shape=jax.ShapeDtypeStruct((b * t, HG_WIDTH), BF16),
        scratch_shapes=[pltpu.VMEM((HG_HEADS, HG_DIM, HG_DIM), F32),
                        pltpu.VMEM((HG_HEADS, HG_CHUNK, HG_DIM), F32)],
        compiler_params=_params(2),
        name="hgrn",
    )(*([hg5] * HG_STREAMS), hconst, ltri)


def _merge_kernel(x_ref, ya_ref, yh_ref, g_ref, wg_ref, wpa_ref, wph_ref, wo_ref, o_ref):
    x = x_ref[...]
    hb = _rmsnorm_rows(x, g_ref[...]).astype(BF16)
    gates = jnp.dot(hb, wg_ref[...], preferred_element_type=F32)
    pa = jnp.dot(ya_ref[...], wpa_ref[...], preferred_element_type=F32)
    ph = jnp.dot(yh_ref[...], wph_ref[...], preferred_element_type=F32)
    merged = _sigmoid(gates[:, :D_MODEL]) * pa + _sigmoid(gates[:, D_MODEL:]) * ph
    o_ref[...] = x + jnp.dot(merged.astype(BF16), wo_ref[...], preferred_element_type=F32)


def _merge(x2, ya, yh, gain, wg, wpa, wph, wo, tm):
    nt = x2.shape[0]
    row = lambda w: pl.BlockSpec((tm, w), lambda i: (i, 0))
    return pl.pallas_call(
        _merge_kernel,
        grid=(nt // tm,),
        in_specs=[row(D_MODEL), row(ATTN_WIDTH), row(HG_WIDTH), _resident(gain.shape), _resident(wg.shape),
                  _resident(wpa.shape), _resident(wph.shape), _resident(wo.shape)],
        out_specs=row(D_MODEL),
        out_shape=jax.ShapeDtypeStruct((nt, D_MODEL), F32),
        compiler_params=_params(1),
        name="merge",
    )(x2, ya, yh, gain, wg, wpa, wph, wo)


def _ffn_kernel(x_ref, g_ref, wi_ref, wo_ref, o_ref):
    x = x_ref[...]
    hb = _rmsnorm_rows(x, g_ref[...]).astype(BF16)
    gu = jnp.dot(hb, wi_ref[...], preferred_element_type=F32)
    g = gu[:, :D_FF]
    act = g * _sigmoid(g) * gu[:, D_FF:]
    o_ref[...] = x + jnp.dot(act.astype(BF16), wo_ref[...], preferred_element_type=F32)


def _ffn(x2, gain, wi, wo, tm):
    nt = x2.shape[0]
    row = pl.BlockSpec((tm, D_MODEL), lambda i: (i, 0))
    return pl.pallas_call(
        _ffn_kernel,
        grid=(nt // tm,),
        in_specs=[row, _resident(gain.shape), _resident(wi.shape), _resident(wo.shape)],
        out_specs=row,
        out_shape=jax.ShapeDtypeStruct((nt, D_MODEL), F32),
        compiler_params=_params(1),
        name="ffn",
    )(x2, gain, wi, wo)


def _split_hi_lo(a):
    hi = a.astype(BF16)
    lo = (a - hi.astype(F32)).astype(BF16)
    return hi, lo


def _layer(x2, b, t, cos1, sin1, consts, w):
    hsum, hexp, ltri_attn, ltri_hgrn = consts
    kb = KEY_BLOCK
    nkc = t // kb
    q, k, v, iq, ikw, hg5 = _mix_in(x2, w["norm_mix"], w["wa"], *w["wi"], w["wh"], w["qn"], w["kn"],
                                    cos1, sin1, hsum, hexp, w["hconst"], tm=512)
    qt = q.reshape(b, t, ATTN_WIDTH).transpose(0, 2, 1)
    k4 = k.reshape(b, nkc, kb, ATTN_WIDTH)
    vt = v.reshape(b, nkc, kb, ATTN_HEADS, HEAD_DIM).transpose(0, 1, 3, 4, 2)
    ones_pad = jnp.zeros((b, nkc, ATTN_HEADS, VT_ROWS - HEAD_DIM, kb), BF16).at[:, :, :, 0, :].set(1.0)
    vt4 = jnp.concatenate([vt, ones_pad], axis=3).reshape(b, nkc, ATTN_HEADS * VT_ROWS, kb)
    iq_hi, iq_lo = _split_hi_lo(iq.reshape(b, t, IDX_HEADS, IDX_DIM))
    iq3t = jnp.concatenate([iq_hi, iq_hi, iq_lo, jnp.zeros_like(iq_hi)], axis=-1).transpose(0, 2, 3, 1)
    ik_hi, ik_lo = _split_hi_lo(ikw[:, :IDX_DIM])
    ik3 = jnp.concatenate([ik_hi, ik_lo, ik_hi, jnp.zeros_like(ik_hi)], axis=-1).reshape(b, nkc, kb, 4 * IDX_DIM)
    iwt = jnp.pad(ikw[:, IDX_DIM:IDX_DIM + IDX_HEADS].reshape(b, t, IDX_HEADS), ((0, 0), (0, 0), (0, 8 - IDX_HEADS))
                  ).transpose(0, 2, 1)
    y_attn_t = _dsa(iq3t, ik3, iwt, qt, k4, vt4, ltri_attn, topk=min(MAX_TOPK, t // 4))
    y_attn = y_attn_t.transpose(0, 2, 1).reshape(b * t, ATTN_WIDTH)
    y_hgrn = _hgrn(hg5, w["hconst"], ltri_hgrn, b, t)
    x2 = _merge(x2, y_attn, y_hgrn, w["norm_mix"], w["wg"], w["wpa"], w["wph"], w["wout"], tm=512)
    return _ffn(x2, w["norm_ffn"], w["wffn_in"], w["wffn_out"], tm=256)


def _constants():
    head_of_lane = np.arange(ATTN_WIDTH) // HEAD_DIM
    hsum = (head_of_lane[:, None] == np.arange(LANES)[None, :]).astype(np.float32)
    hexp = hsum.T.copy()
    r = np.arange(KEY_BLOCK)
    ltri_attn = (r[None, :] < r[:, None]).astype(np.float32)
    r = np.arange(HG_CHUNK)
    ltri_hgrn = (r[None, :] <= r[:, None]).astype(np.float32)
    return jnp.asarray(hsum, BF16), jnp.asarray(hexp, BF16), jnp.asarray(ltri_attn, BF16), jnp.asarray(ltri_hgrn)


def kernel(x, positions, w_in, w_proj_attn, w_proj_hgrn, w_out, norm_mix, norm_ffn, q_norm, k_norm, hgrn_norm,
           hgrn_lower_bound, w_ffn_in, w_ffn_out):
    b, t, d = x.shape
    depth = w_in.shape[0]
    nt = b * t

    inv = ROPE_THETA ** (-jnp.arange(0, HEAD_DIM, 2, dtype=F32) / HEAD_DIM)
    ang = positions.astype(F32)[..., None] * inv
    cos, sin = jnp.cos(ang), jnp.sin(ang)
    cos1 = jnp.concatenate([cos, cos, cos, cos], axis=-1).reshape(nt, LANES)
    sin1 = jnp.concatenate([-sin, sin, -sin, sin], axis=-1).reshape(nt, LANES)

    lb_all = jnp.cumsum(jax.nn.softmax(hgrn_lower_bound.astype(F32), axis=0), axis=0)
    lb_all = lb_all - lb_all[:1]

    consts = _constants()
    widths = (ATTN_WIDTH, ATTN_WIDTH, ATTN_WIDTH, IDX_HEADS * IDX_DIM, IDX_DIM, IDX_HEADS,
              HG_WIDTH, HG_WIDTH, HG_WIDTH, HG_WIDTH, D_MODEL, D_MODEL)
    off = np.concatenate([[0], np.cumsum(widths)])
    idx_cols = off[6] - off[3]

    x2 = x.reshape(nt, d)
    for l in range(depth):
        wl = w_in[l]
        lb = lb_all[l]
        hconst = jnp.zeros((8, HG_WIDTH), F32)
        hconst = hconst.at[0].set(jnp.log(lb)).at[1].set(jnp.log1p(-lb)).at[2].set(1.0 - lb)
        hconst = hconst.at[3].set(jnp.tile(hgrn_norm[l], HG_HEADS))
        w = {
            "norm_mix": norm_mix[l].reshape(1, d),
            "norm_ffn": norm_ffn[l].reshape(1, d),
            "wa": wl[:, off[0]:off[3]].astype(BF16),
            "wi": _split_hi_lo(jnp.pad(wl[:, off[3]:off[6]], ((0, 0), (0, 3 * LANES - idx_cols)))),
            "wh": wl[:, off[6]:off[10]].astype(BF16),
            "wg": wl[:, off[10]:off[12]].astype(BF16),
            "qn": jnp.tile(q_norm[l], ATTN_HEADS).reshape(1, ATTN_WIDTH),
            "kn": jnp.tile(k_norm[l], ATTN_HEADS).reshape(1, ATTN_WIDTH),
            "hconst": hconst,
            "wpa": w_proj_attn[l].astype(BF16),
            "wph": w_proj_hgrn[l].astype(BF16),
            "wout": w_out[l].astype(BF16),
            "wffn_in": w_ffn_in[l].astype(BF16),
            "wffn_out": w_ffn_out[l].astype(BF16),
        }
        x2 = _layer(x2, b, t, cos1, sin1, consts, w)
    return x2.reshape(b, t, d)
```

```python
import functools

import numpy as np
import jax
import jax.numpy as jnp
from jax import lax
from jax.experimental import pallas as pl
from jax.experimental.pallas import tpu as pltpu

D_MODEL = 1024
ATTN_HEADS = 8
HEAD_DIM = 64
ATTN_WIDTH = ATTN_HEADS * HEAD_DIM
IDX_HEADS = 4
IDX_DIM = 64
MAX_TOPK = 256
HG_HEADS = 4
HG_DIM = 128
HG_WIDTH = HG_HEADS * HG_DIM
HG_CHUNK = 64
HG_STREAMS = 5
HG_STEP_CHUNKS = 2
HG_SUB = 16
HG_FACT = 32
MAX_SAFE_LOGIT = 60.0
MAX_SAFE_EXPONENT = 80.0
D_FF = 2816
ROPE_THETA = 10000.0
EPS = 1e-6
LOG2_E = 1.4426950408889634

LANES = 128
VMEM_LIMIT_BYTES = 56 * 1024 * 1024

KEY_BLOCK = 256
VT_ROWS = 80
INT_MIN = -(2 ** 31)
HALF_MIN = -(2 ** 15)
NEG_BIG = -1e30

F32 = jnp.float32
BF16 = jnp.bfloat16
HIGHEST = lax.Precision.HIGHEST


def _resident(shape):
    nd = len(shape)
    return pl.BlockSpec(shape, lambda *_: (0,) * nd, pipeline_mode=pl.Buffered(1))


def _params(n_axes):
    return pltpu.CompilerParams(dimension_semantics=("arbitrary",) * n_axes,
                                vmem_limit_bytes=VMEM_LIMIT_BYTES)


def _rmsnorm_rows(x, gain):
    return x * lax.rsqrt(jnp.mean(x * x, axis=-1, keepdims=True) + EPS) * gain


def _sigmoid(x):
    return 1.0 / (1.0 + jnp.exp(-x))


def _rope(x, cos, sin_signed):
    w = x.shape[-1]
    lane = lax.broadcasted_iota(jnp.int32, x.shape, 1)
    first_half = (lane & (HEAD_DIM - 1)) < HEAD_DIM // 2
    partner = jnp.where(first_half, pltpu.roll(x, w - HEAD_DIM // 2, 1), pltpu.roll(x, HEAD_DIM // 2, 1))
    return x * cos + partner * sin_signed


def _dot_hi_lo(a, b_hi, b_lo=None):
    a_hi = a.astype(BF16)
    a_lo = (a - a_hi.astype(F32)).astype(BF16)
    out = jnp.dot(a_hi, b_hi, preferred_element_type=F32) + jnp.dot(a_lo, b_hi, preferred_element_type=F32)
    if b_lo is not None:
        out = out + jnp.dot(a_hi, b_lo, preferred_element_type=F32)
    return out


def _mix_in_kernel(x_ref, g_ref, wa_ref, wi_ref, wil_ref, wh_ref, qn_ref, kn_ref, cos_ref, sin_ref, hsum_ref, hexp_ref,
                   hc_ref, qt_ref, k_ref, vt_ref, iq3_ref, ik3_ref, iw_ref, hg_ref):
    h = _rmsnorm_rows(x_ref[...], g_ref[...])
    hb = h.astype(BF16)
    cos1 = cos_ref[...]
    sin1 = sin_ref[...]
    cos4 = jnp.concatenate([cos1] * 4, axis=1)
    sin4 = jnp.concatenate([sin1] * 4, axis=1)

    def head_norm(a, gain):
        ss = _dot_hi_lo(a * a, hsum_ref[...])
        r = lax.rsqrt(ss * (1.0 / HEAD_DIM) + EPS)
        return a * _dot_hi_lo(r, hexp_ref[...]) * gain

    pa = jnp.dot(hb, wa_ref[...], preferred_element_type=F32)
    q = _rope(head_norm(pa[:, :ATTN_WIDTH], qn_ref[...]), cos4, sin4) * (HEAD_DIM ** -0.5 * LOG2_E)
    k = _rope(head_norm(pa[:, ATTN_WIDTH:2 * ATTN_WIDTH], kn_ref[...]), cos4, sin4)
    qt_ref[...] = q.T.astype(BF16)
    k_ref[...] = k.astype(BF16)
    v = pa[:, 2 * ATTN_WIDTH:]
    ones_pad = jnp.where(lax.broadcasted_iota(jnp.int32, (VT_ROWS - HEAD_DIM, KEY_BLOCK), 0) == 0, 1.0, 0.0)
    for s in range(v.shape[0] // KEY_BLOCK):
        vt = v[KEY_BLOCK * s:KEY_BLOCK * (s + 1)].T
        rows = []
        for hd in range(ATTN_HEADS):
            rows += [vt[HEAD_DIM * hd:HEAD_DIM * (hd + 1)], ones_pad]
        vt_ref[s] = jnp.concatenate(rows, axis=0).astype(BF16)

    pi = _dot_hi_lo(h, wi_ref[...], wil_ref[...])
    nq = IDX_HEADS * IDX_DIM
    iq = _rope(pi[:, :nq], cos4[:, :nq], sin4[:, :nq]) * (IDX_DIM ** -0.5)
    slab = pi[:, nq:]
    ikw_t = _rope(slab, cos1, sin1).T
    iw_ref[...] = slab.T[IDX_DIM:IDX_DIM + 8] * (IDX_HEADS ** -0.5)

    def hi_lo(a):
        hi = a.astype(BF16).astype(F32)
        return hi, (a - hi).astype(BF16).astype(F32)

    k_hi, k_lo = hi_lo(ikw_t[:IDX_DIM])
    ik3_ref[...] = jnp.concatenate([k_hi, k_lo, k_hi, jnp.zeros_like(k_hi)], axis=0).T.astype(BF16)
    iq_t = iq.T
    rows = []
    for hd in range(IDX_HEADS):
        q_hi, q_lo = hi_lo(iq_t[IDX_DIM * hd:IDX_DIM * (hd + 1)])
        rows += [q_hi, q_hi, q_lo, jnp.zeros_like(q_hi)]
    iq3_ref[...] = jnp.concatenate(rows, axis=0).astype(BF16)

    ph = jnp.dot(hb, wh_ref[...], preferred_element_type=F32)
    w = HG_WIDTH
    hq, hf, hg = ph[:, :w], ph[:, w:2 * w], ph[:, 3 * w:]
    log_lb, log1m_lb, one_m_lb = hc_ref[0:1, :], hc_ref[1:2, :], hc_ref[2:3, :]
    c = log1m_lb + jnp.minimum(hf, 0.0) - jnp.log1p(jnp.exp(-jnp.abs(hf)))
    hg_ref[:, :w] = hq * _sigmoid(hq)
    hg_ref[:, w:2 * w] = jnp.maximum(log_lb, c) + jnp.log1p(jnp.exp(-jnp.abs(log_lb - c)))
    hg_ref[:, 2 * w:3 * w] = one_m_lb * _sigmoid(-hf)
    hg_ref[:, 3 * w:4 * w] = ph[:, 2 * w:3 * w]
    hg_ref[:, 4 * w:] = hg * _sigmoid(hg)


def _mix_in(x2, gain, wa, wi, wil, wh, qn, kn, cos1, sin1, hsum, hexp, hconst, tm):
    nt = x2.shape[0]
    kb = KEY_BLOCK
    row = lambda w: pl.BlockSpec((tm, w), lambda i: (i, 0))
    col = lambda w: pl.BlockSpec((w, tm), lambda i: (0, i))
    return pl.pallas_call(
        _mix_in_kernel,
        grid=(nt // tm,),
        in_specs=[row(D_MODEL), _resident(gain.shape), _resident(wa.shape), _resident(wi.shape), _resident(wil.shape),
                  _resident(wh.shape),
                  _resident(qn.shape), _resident(kn.shape), row(LANES), row(LANES),
                  _resident(hsum.shape), _resident(hexp.shape), _resident(hconst.shape)],
        out_specs=[col(ATTN_WIDTH), row(ATTN_WIDTH),
                   pl.BlockSpec((tm // kb, ATTN_HEADS * VT_ROWS, kb), lambda i: (i, 0, 0)),
                   col(IDX_HEADS * 4 * IDX_DIM), row(4 * IDX_DIM), col(8), row(HG_STREAMS * HG_WIDTH)],
        out_shape=[jax.ShapeDtypeStruct((ATTN_WIDTH, nt), BF16), jax.ShapeDtypeStruct((nt, ATTN_WIDTH), BF16),
                   jax.ShapeDtypeStruct((nt // kb, ATTN_HEADS * VT_ROWS, kb), BF16),
                   jax.ShapeDtypeStruct((IDX_HEADS * 4 * IDX_DIM, nt), BF16),
                   jax.ShapeDtypeStruct((nt, 4 * IDX_DIM), BF16), jax.ShapeDtypeStruct((8, nt), F32),
                   jax.ShapeDtypeStruct((nt, HG_STREAMS * HG_WIDTH), F32)],
        compiler_params=_params(1),
        name="mix_in",
    )(x2, gain, wa, wi, wil, wh, qn, kn, cos1, sin1, hsum, hexp, hconst)


def _dsa_kernel(bound_ref, iq3_ref, ik3_ref, iw_ref, qt_ref, k_ref, vt_ref, ltri_ref, o_ref,
                hi_ref, lo_ref, qz_ref, lg_ref, m_ref, alpha_ref, acc_ref, *, topk):
    kb = KEY_BLOCK
    j = pl.program_id(1)
    nk = j + 1
    int_min = jnp.int32(INT_MIN)

    zeros = jnp.zeros((HEAD_DIM, kb), BF16)
    for h in range(ATTN_HEADS):
        qh = qt_ref[HEAD_DIM * h:HEAD_DIM * (h + 1), :]
        qz_ref[h] = jnp.concatenate([qh, zeros] if h % 2 == 0 else [zeros, qh], axis=0)

    def score_chunk(c, diagonal):
        k3 = ik3_ref[c]
        acc = jnp.zeros((kb, kb), F32)
        for h in range(IDX_HEADS):
            rel = jnp.dot(k3, iq3_ref[4 * IDX_DIM * h:4 * IDX_DIM * (h + 1), :], preferred_element_type=F32)
            acc = acc + iw_ref[h:h + 1, :] * jnp.maximum(rel, 0.0)
        bits = pltpu.bitcast(acc, jnp.int32)
        key = jnp.where(bits < 0, int_min - bits, bits)
        if diagonal:
            kpos = lax.broadcasted_iota(jnp.int32, (kb, kb), 0)
            qpos = lax.broadcasted_iota(jnp.int32, (kb, kb), 1)
            key = jnp.where(kpos <= qpos, key, int_min)
        hi_ref[c] = lax.shift_right_arithmetic(key, 16).astype(jnp.int16)
        lo_ref[c] = (key ^ 0x8000).astype(jnp.int16)

    def score_pair(i, carry):
        score_chunk(2 * i, False)
        score_chunk(2 * i + 1, False)
        return carry

    lax.fori_loop(0, j // 2, score_pair, 0)

    @pl.when(j % 2 == 1)
    def _():
        score_chunk(j - 1, False)

    score_chunk(j, True)

    one_b, zero_b = jnp.ones((), BF16), jnp.zeros((), BF16)
    one_i, zero_i = jnp.ones((), jnp.int16), jnp.zeros((), jnp.int16)
    rows16 = 16

    hi_ref[nk] = jnp.full((kb, kb), HALF_MIN, jnp.int16)
    lo_ref[nk] = jnp.full((kb, kb), HALF_MIN, jnp.int16)

    def count(hit_fn):
        def fold(hit):
            parts = [hit[rows16 * r:rows16 * (r + 1)] for r in range(kb // rows16)]
            while len(parts) > 1:
                parts = [a + b for a, b in zip(parts[::2], parts[1::2])]
            return parts[0]

        def body(i, cnt):
            return cnt + (fold(hit_fn(2 * i)) + fold(hit_fn(2 * i + 1)))

        cnt = lax.fori_loop(0, (nk + 1) // 2, body, jnp.zeros((rows16, kb), jnp.int16))
        return jnp.sum(cnt.astype(jnp.int32), axis=0, keepdims=True).astype(F32)

    def bisect_half(ref, target):
        def bit_step(i, thr_u):
            cand_u = thr_u | lax.shift_left(jnp.int32(1), 15 - i)
            cand = (cand_u + HALF_MIN).astype(jnp.int16)
            cnt = count(lambda c: jnp.where(ref[c] >= cand, one_i, zero_i))
            return jnp.where(cnt >= target, cand_u, thr_u)
        return lax.fori_loop(0, 16, bit_step, jnp.zeros((1, kb), jnp.int32)) + HALF_MIN

    thr_hi32 = bisect_half(hi_ref, float(topk))
    thr_hi = thr_hi32.astype(jnp.int16)
    n_gt_hi = count(lambda c: jnp.where(hi_ref[c] > thr_hi, one_i, zero_i))

    def keep_matching_low(c, carry):
        lo_ref[c] = jnp.where(hi_ref[c] == thr_hi, lo_ref[c], jnp.int16(HALF_MIN))
        return carry

    lax.fori_loop(0, nk, keep_matching_low, 0)
    thr_lo32 = bisect_half(lo_ref, topk - n_gt_hi)
    thr_lo = thr_lo32.astype(jnp.int16)
    n_gt = n_gt_hi + count(lambda c: jnp.where(lo_ref[c] > thr_lo, one_i, zero_i))
    is_marker = (thr_hi32 == HALF_MIN) & (thr_lo32 == HALF_MIN)
    need = jnp.where(is_marker, 0.0, topk - n_gt)

    def bias_chunk(c, ties_before):
        hi, lo = hi_ref[c], lo_ref[c]
        above = jnp.where(hi > thr_hi, one_b, jnp.where(lo > thr_lo, one_b, zero_b)).astype(F32)
        eq_b = jnp.where(hi == thr_hi, jnp.where(lo == thr_lo, one_b, zero_b), zero_b)
        eq = eq_b.astype(F32)
        rank = jnp.dot(ltri_ref[...], eq_b, preferred_element_type=F32) + ties_before
        chosen = above + jnp.where(rank < need, eq, 0.0)
        bias = jnp.where(chosen > 0.5, 0.0, NEG_BIG).astype(BF16)
        hi_ref[c] = pltpu.bitcast(bias, jnp.int16)
        return ties_before + jnp.sum(eq, axis=0, keepdims=True)

    def bias_pair(i, ties_before):
        return bias_chunk(2 * i + 1, bias_chunk(2 * i, ties_before))

    lax.fori_loop(0, (nk + 1) // 2, bias_pair, jnp.zeros((1, kb), F32))

    def attend(bounded):
        acc_ref[...] = jnp.zeros(acc_ref.shape, F32)
        if not bounded:
            m_ref[...] = jnp.full(m_ref.shape, NEG_BIG, F32)

        def logits_head(c, h, bias):
            kk = k_ref[c, :, LANES * (h // 2):LANES * (h // 2 + 1)]
            lgb = jnp.dot(kk, qz_ref[h], preferred_element_type=F32) + bias
            lg_ref[h] = lgb
            if not bounded:
                m_old = m_ref[h]
                m_new = jnp.maximum(m_old, jnp.max(lgb, axis=0, keepdims=True))
                alpha_ref[h] = jnp.exp2(m_old - m_new)
                m_ref[h] = m_new

        def softmax_pv_head(c, h):
            p = jnp.exp2(lg_ref[h] if bounded else lg_ref[h] - m_ref[h])
            pv = jnp.dot(vt_ref[c, VT_ROWS * h:VT_ROWS * (h + 1), :], p.astype(BF16),
                         preferred_element_type=F32)
            acc_ref[h] = (acc_ref[h] if bounded else alpha_ref[h] * acc_ref[h]) + pv

        def chunk_bias(c):
            return pltpu.bitcast(hi_ref[c], BF16).astype(F32)

        bias0 = chunk_bias(0)
        for h in range(ATTN_HEADS):
            logits_head(0, h, bias0)

        def pipelined(c, carry):
            bias = chunk_bias(c)
            for h in range(ATTN_HEADS):
                softmax_pv_head(c - 1, h)
                logits_head(c, h, bias)
            return carry

        lax.fori_loop(1, nk, pipelined, 0)
        for h in range(ATTN_HEADS):
            softmax_pv_head(nk - 1, h)

    logits_bounded = bound_ref[0] < MAX_SAFE_LOGIT

    @pl.when(logits_bounded)
    def _():
        attend(True)

    @pl.when(jnp.logical_not(logits_bounded))
    def _():
        attend(False)

    out_t = [acc_ref[h][:HEAD_DIM] / acc_ref[h][HEAD_DIM:HEAD_DIM + 1] for h in range(ATTN_HEADS)]
    o_ref[...] = jnp.concatenate(out_t, axis=0).T.astype(o_ref.dtype)


def _dsa(logit_bound, iq3t, ik3, iwt, qt, k, vt4, ltri, b, t, topk):
    kb = KEY_BLOCK
    nkc = t // kb
    chunked = lambda a: a.reshape(b * nkc, kb, a.shape[-1])
    per_batch = lambda shape: pl.BlockSpec((nkc,) + shape, lambda i, j: (i, 0, 0), pipeline_mode=pl.Buffered(1))
    q_cols = lambda rows: pl.BlockSpec((rows, kb), lambda i, j: (0, i * nkc + j))
    return pl.pallas_call(
        functools.partial(_dsa_kernel, topk=topk),
        grid=(b, nkc),
        in_specs=[
            pl.BlockSpec(memory_space=pltpu.SMEM),
            q_cols(IDX_HEADS * 4 * IDX_DIM),
            per_batch((kb, 4 * IDX_DIM)),
            q_cols(8),
            q_cols(ATTN_WIDTH),
            per_batch((kb, ATTN_WIDTH)),
            per_batch((ATTN_HEADS * VT_ROWS, kb)),
            _resident(ltri.shape),
        ],
        out_specs=pl.BlockSpec((kb, ATTN_WIDTH), lambda i, j: (i * nkc + j, 0)),
        out_shape=jax.ShapeDtypeStruct((b * t, ATTN_WIDTH), BF16),
        scratch_shapes=[
            pltpu.VMEM((nkc + 1, kb, kb), jnp.int16),
            pltpu.VMEM((nkc + 1, kb, kb), jnp.int16),
            pltpu.VMEM((ATTN_HEADS, LANES, kb), BF16),
            pltpu.VMEM((ATTN_HEADS, kb, kb), F32),
            pltpu.VMEM((ATTN_HEADS, 1, kb), F32),
            pltpu.VMEM((ATTN_HEADS, 1, kb), F32),
            pltpu.VMEM((ATTN_HEADS, VT_ROWS, kb), F32),
        ],
        compiler_params=_params(2),
        name="dsa",
    )(logit_bound, iq3t, chunked(ik3), iwt, qt, chunked(k), vt4, ltri)


def _hgrn_kernel(q_ref, lf_ref, k_ref, v_ref, gate_ref, hc_ref, lt_ref, o_ref, st_ref, intra_ref):
    @pl.when(pl.program_id(1) == 0)
    def _():
        st_ref[...] = jnp.zeros(st_ref.shape, F32)

    for ci in range(HG_STEP_CHUNKS):
        rows = slice(HG_CHUNK * ci, HG_CHUNK * (ci + 1))
        _hgrn_chunk(q_ref[rows, :], lf_ref[rows, :], k_ref[rows, :], v_ref.at[rows, :], gate_ref[rows, :],
                    hc_ref[3:4, :], lt_ref, o_ref.at[rows, :], st_ref, intra_ref)


def _hgrn_chunk(q_all, log_f, k_all, hi_ref, gate, out_gain, lt_ref, o_ref, st_ref, intra_ref):
    c_len, sub = HG_CHUNK, HG_SUB
    b_all = jnp.dot(lt_ref[...], log_f, precision=HIGHEST, preferred_element_type=F32)
    b_last = b_all[c_len - 1:c_len, :]
    q_inter = q_all * jnp.exp(b_all)
    k_carry = k_all * jnp.exp(b_last - b_all)

    n_sub = c_len // sub
    trans_b = (((1,), (1,)), ((), ()))

    def block_start(b, i):
        return b[sub * i - 1:sub * i, :] if i > 0 else jnp.zeros_like(b[0:1, :])

    def intra_factored_all():
        fs = HG_FACT
        heads = [slice(HG_DIM * h, HG_DIM * (h + 1)) for h in range(HG_HEADS)]
        scores = []
        for hs in heads:
            b, q, k = b_all[:, hs], q_all[:, hs], k_all[:, hs]
            for i in range(c_len // fs):
                rs, upto = slice(fs * i, fs * (i + 1)), slice(0, fs * (i + 1))
                b_ref = b[fs * i - 1:fs * i, :] if i > 0 else jnp.zeros_like(b[0:1, :])
                q_hat = (q[rs] * jnp.exp(b[rs] - b_ref)).astype(BF16)
                k_hat = (k[upto] * jnp.exp(b_ref - b[upto])).astype(BF16)
                scores.append(lax.dot_general(q_hat, k_hat, trans_b, preferred_element_type=F32))
        for h, hs in enumerate(heads):
            vb = hi_ref[:, hs].astype(BF16)
            blocks = []
            for i in range(c_len // fs):
                a = scores[h * (c_len // fs) + i]
                t_idx = fs * i + lax.broadcasted_iota(jnp.int32, a.shape, 0)
                s_idx = lax.broadcasted_iota(jnp.int32, a.shape, 1)
                a = jnp.where(s_idx <= t_idx, a, 0.0).astype(BF16)
                blocks.append(jnp.dot(a, vb[0:fs * (i + 1)], preferred_element_type=F32))
            intra_ref[h] = jnp.concatenate(blocks, axis=0)

    def intra_direct(b, q, k, vb):
        row = lax.broadcasted_iota(jnp.int32, (sub, 1), 0)
        col = lax.broadcasted_iota(jnp.int32, (sub, sub), 1)
        blocks = []
        for i in range(n_sub):
            rs = slice(sub * i, sub * (i + 1))
            b_i, q_i, k_i = b[rs], q[rs], k[rs]
            diag = jnp.zeros((sub, sub), F32)
            for s in range(sub):
                decay = jnp.exp(jnp.where(row >= s, b_i - b_i[s:s + 1, :], -jnp.inf))
                a_col = jnp.sum(q_i * k_i[s:s + 1, :] * decay, axis=1, keepdims=True)
                diag = jnp.where(col == s, a_col, diag)
            o_i = jnp.dot(diag.astype(BF16), vb[rs], preferred_element_type=F32)
            if i > 0:
                prev = slice(0, sub * i)
                b_ref = block_start(b, i)
                q_hat = (q_i * jnp.exp(b_i - b_ref)).astype(BF16)
                k_hat = (k[prev] * jnp.exp(b_ref - b[prev])).astype(BF16)
                a_off = lax.dot_general(q_hat, k_hat, trans_b, preferred_element_type=F32)
                o_i = o_i + jnp.dot(a_off.astype(BF16), vb[prev], preferred_element_type=F32)
            blocks.append(o_i)
        return jnp.concatenate(blocks, axis=0)

    ends = [b_all[HG_FACT * (i + 1) - 1:HG_FACT * (i + 1), :] for i in range(c_len // HG_FACT)]
    drops = [(ends[i - 1] if i > 0 else 0.0) - ends[i] for i in range(len(ends))]
    factored_ok = jnp.max(functools.reduce(jnp.maximum, drops)) < MAX_SAFE_EXPONENT

    @pl.when(factored_ok)
    def _():
        intra_factored_all()

    @pl.when(jnp.logical_not(factored_ok))
    def _():
        for h in range(HG_HEADS):
            hs = slice(HG_DIM * h, HG_DIM * (h + 1))
            intra_ref[h] = intra_direct(b_all[:, hs], q_all[:, hs], k_all[:, hs], hi_ref[:, hs].astype(BF16))

    outs = []
    for h in range(HG_HEADS):
        hs = slice(HG_DIM * h, HG_DIM * (h + 1))
        v = hi_ref[:, hs]
        st = st_ref[h]
        inter = lax.dot_general(q_inter[:, hs].astype(BF16), st.astype(BF16), trans_b,
                                preferred_element_type=F32)
        o = inter + intra_ref[h]
        st_ref[h] = jnp.exp(b_last[:, hs]) * st + jnp.dot(v.T.astype(BF16), k_carry[:, hs].astype(BF16),
                                                         preferred_element_type=F32)
        outs.append(_rmsnorm_rows(o, out_gain[:, hs]) * gate[:, hs])
    o_ref[...] = jnp.concatenate(outs, axis=1).astype(o_ref.dtype)


def _hgrn(hg5, hconst, ltri, b, t):
    rows = HG_CHUNK * HG_STEP_CHUNKS
    ns = t // rows
    col = lambda jcol: pl.BlockSpec((rows, HG_WIDTH), lambda i, s, jcol=jcol: (i * ns + s, jcol))
    return pl.pallas_call(
        _hgrn_kernel,
        grid=(b, ns),
        in_specs=[col(jcol) for jcol in range(HG_STREAMS)] + [_resident(hconst.shape), _resident(ltri.shape)],
        out_specs=pl.BlockSpec((rows, HG_WIDTH), lambda i, s: (i * ns + s, 0)),
        out_shape=jax.ShapeDtypeStruct((b * t, HG_WIDTH), BF16),
        scratch_shapes=[pltpu.VMEM((HG_HEADS, HG_DIM, HG_DIM), F32),
                        pltpu.VMEM((HG_HEADS, HG_CHUNK, HG_DIM), F32)],
        compiler_params=_params(2),
        name="hgrn",
    )(*([hg5] * HG_STREAMS), hconst, ltri)


def _merge_kernel(x_ref, ya_ref, yh_ref, g_ref, wg_ref, wpa_ref, wph_ref, wo_ref, o_ref):
    x = x_ref[...]
    hb = _rmsnorm_rows(x, g_ref[...]).astype(BF16)
    gates = jnp.dot(hb, wg_ref[...], preferred_element_type=F32)
    pa = jnp.dot(ya_ref[...], wpa_ref[...], preferred_element_type=F32)
    ph = jnp.dot(yh_ref[...], wph_ref[...], preferred_element_type=F32)
    merged = _sigmoid(gates[:, :D_MODEL]) * pa + _sigmoid(gates[:, D_MODEL:]) * ph
    o_ref[...] = x + jnp.dot(merged.astype(BF16), wo_ref[...], preferred_element_type=F32)


def _merge(x2, ya, yh, gain, wg, wpa, wph, wo, tm):
    nt = x2.shape[0]
    row = lambda w: pl.BlockSpec((tm, w), lambda i: (i, 0))
    return pl.pallas_call(
        _merge_kernel,
        grid=(nt // tm,),
        in_specs=[row(D_MODEL), row(ATTN_WIDTH), row(HG_WIDTH), _resident(gain.shape), _resident(wg.shape),
                  _resident(wpa.shape), _resident(wph.shape), _resident(wo.shape)],
        out_specs=row(D_MODEL),
        out_shape=jax.ShapeDtypeStruct((nt, D_MODEL), F32),
        compiler_params=_params(1),
        name="merge",
    )(x2, ya, yh, gain, wg, wpa, wph, wo)


def _ffn_kernel(x_ref, g_ref, wi_ref, wo_ref, o_ref):
    x = x_ref[...]
    hb = _rmsnorm_rows(x, g_ref[...]).astype(BF16)
    gu = jnp.dot(hb, wi_ref[...], preferred_element_type=F32)
    g = gu[:, :D_FF]
    act = g * _sigmoid(g) * gu[:, D_FF:]
    o_ref[...] = x + jnp.dot(act.astype(BF16), wo_ref[...], preferred_element_type=F32)


def _ffn(x2, gain, wi, wo, tm):
    nt = x2.shape[0]
    row = pl.BlockSpec((tm, D_MODEL), lambda i: (i, 0))
    return pl.pallas_call(
        _ffn_kernel,
        grid=(nt // tm,),
        in_specs=[row, _resident(gain.shape), _resident(wi.shape), _resident(wo.shape)],
        out_specs=row,
        out_shape=jax.ShapeDtypeStruct((nt, D_MODEL), F32),
        compiler_params=_params(1),
        name="ffn",
    )(x2, gain, wi, wo)


def _split_hi_lo(a):
    hi = a.astype(BF16)
    lo = (a - hi.astype(F32)).astype(BF16)
    return hi, lo


def _layer(x2, b, t, cos1, sin1, consts, w):
    hsum, hexp, ltri_attn, ltri_hgrn = consts
    qt, k, vt4, iq3t, ik3, iwt, hg5 = _mix_in(x2, w["norm_mix"], w["wa"], *w["wi"], w["wh"], w["qn"], w["kn"],
                                              cos1, sin1, hsum, hexp, w["hconst"], tm=512)
    y_attn = _dsa(w["logit_bound"], iq3t, ik3, iwt, qt, k, vt4, ltri_attn, b, t, topk=min(MAX_TOPK, t // 4))
    y_hgrn = _hgrn(hg5, w["hconst"], ltri_hgrn, b, t)
    x2 = _merge(x2, y_attn, y_hgrn, w["norm_mix"], w["wg"], w["wpa"], w["wph"], w["wout"], tm=512)
    return _ffn(x2, w["norm_ffn"], w["wffn_in"], w["wffn_out"], tm=256)


def _constants():
    head_of_lane = np.arange(ATTN_WIDTH) // HEAD_DIM
    hsum = (head_of_lane[:, None] == np.arange(LANES)[None, :]).astype(np.float32)
    hexp = hsum.T.copy()
    r = np.arange(KEY_BLOCK)
    ltri_attn = (r[None, :] < r[:, None]).astype(np.float32)
    r = np.arange(HG_CHUNK)
    ltri_hgrn = (r[None, :] <= r[:, None]).astype(np.float32)
    return jnp.asarray(hsum, BF16), jnp.asarray(hexp, BF16), jnp.asarray(ltri_attn, BF16), jnp.asarray(ltri_hgrn)


def kernel(x, positions, w_in, w_proj_attn, w_proj_hgrn, w_out, norm_mix, norm_ffn, q_norm, k_norm, hgrn_norm,
           hgrn_lower_bound, w_ffn_in, w_ffn_out):
    b, t, d = x.shape
    depth = w_in.shape[0]
    nt = b * t

    inv = ROPE_THETA ** (-jnp.arange(0, HEAD_DIM, 2, dtype=F32) / HEAD_DIM)
    ang = positions.astype(F32)[..., None] * inv
    cos, sin = jnp.cos(ang), jnp.sin(ang)
    cos1 = jnp.concatenate([cos, cos, cos, cos], axis=-1).reshape(nt, LANES)
    sin1 = jnp.concatenate([-sin, sin, -sin, sin], axis=-1).reshape(nt, LANES)

    lb_all = jnp.cumsum(jax.nn.softmax(hgrn_lower_bound.astype(F32), axis=0), axis=0)
    lb_all = lb_all - lb_all[:1]

    consts = _constants()
    widths = (ATTN_WIDTH, ATTN_WIDTH, ATTN_WIDTH, IDX_HEADS * IDX_DIM, IDX_DIM, IDX_HEADS,
              HG_WIDTH, HG_WIDTH, HG_WIDTH, HG_WIDTH, D_MODEL, D_MODEL)
    off = np.concatenate([[0], np.cumsum(widths)])
    idx_cols = off[6] - off[3]

    x2 = x.reshape(nt, d)
    for l in range(depth):
        wl = w_in[l]
        lb = lb_all[l]
        hconst = jnp.zeros((8, HG_WIDTH), F32)
        hconst = hconst.at[0].set(jnp.log(lb)).at[1].set(jnp.log1p(-lb)).at[2].set(1.0 - lb)
        hconst = hconst.at[3].set(jnp.tile(hgrn_norm[l], HG_HEADS))
        w = {
            "norm_mix": norm_mix[l].reshape(1, d),
            "norm_ffn": norm_ffn[l].reshape(1, d),
            "wa": wl[:, off[0]:off[3]].astype(BF16),
            "wi": _split_hi_lo(jnp.pad(wl[:, off[3]:off[6]], ((0, 0), (0, 3 * LANES - idx_cols)))),
            "wh": wl[:, off[6]:off[10]].astype(BF16),
            "wg": wl[:, off[10]:off[12]].astype(BF16),
            "qn": jnp.tile(q_norm[l], ATTN_HEADS).reshape(1, ATTN_WIDTH),
            "kn": jnp.tile(k_norm[l], ATTN_HEADS).reshape(1, ATTN_WIDTH),
            "logit_bound": (HEAD_DIM ** 0.5 * LOG2_E * jnp.max(jnp.abs(q_norm[l])) * jnp.max(jnp.abs(k_norm[l]))
                            ).reshape(1).astype(F32),
            "hconst": hconst,
            "wpa": w_proj_attn[l].astype(BF16),
            "wph": w_proj_hgrn[l].astype(BF16),
            "wout": w_out[l].astype(BF16),
            "wffn_in": w_ffn_in[l].astype(BF16),
            "wffn_out": w_ffn_out[l].astype(BF16),
        }
        x2 = _layer(x2, b, t, cos1, sin1, consts, w)
    return x2.reshape(b, t, d)
```

```python
import functools

import numpy as np
import jax
import jax.numpy as jnp
from jax import lax
from jax.experimental import pallas as pl
from jax.experimental.pallas import tpu as pltpu

D_MODEL = 1024
ATTN_HEADS = 8
HEAD_DIM = 64
ATTN_WIDTH = ATTN_HEADS * HEAD_DIM
IDX_HEADS = 4
IDX_DIM = 64
MAX_TOPK = 256
HG_HEADS = 4
HG_DIM = 128
HG_WIDTH = HG_HEADS * HG_DIM
HG_CHUNK = 64
HG_STREAMS = 5
HG_STEP_CHUNKS = 2
HG_SUB = 16
HG_FACT = 32
MAX_SAFE_LOGIT = 60.0
MAX_SAFE_EXPONENT = 80.0
D_FF = 2816
ROPE_THETA = 10000.0
EPS = 1e-6
LOG2_E = 1.4426950408889634

LANES = 128
VMEM_LIMIT_BYTES = 56 * 1024 * 1024

KEY_BLOCK = 256
VT_ROWS = 80
INT_MIN = -(2 ** 31)
HALF_MIN = -(2 ** 15)
NEG_BIG = -1e30

F32 = jnp.float32
BF16 = jnp.bfloat16
HIGHEST = lax.Precision.HIGHEST


def _resident(shape):
    nd = len(shape)
    return pl.BlockSpec(shape, lambda *_: (0,) * nd, pipeline_mode=pl.Buffered(1))


def _params(n_axes):
    return pltpu.CompilerParams(dimension_semantics=("arbitrary",) * n_axes,
                                vmem_limit_bytes=VMEM_LIMIT_BYTES)


def _rmsnorm_rows(x, gain):
    return x * lax.rsqrt(jnp.mean(x * x, axis=-1, keepdims=True) + EPS) * gain


def _sigmoid(x):
    return 1.0 / (1.0 + jnp.exp(-x))


def _rope(x, cos, sin_signed):
    w = x.shape[-1]
    lane = lax.broadcasted_iota(jnp.int32, x.shape, 1)
    first_half = (lane & (HEAD_DIM - 1)) < HEAD_DIM // 2
    partner = jnp.where(first_half, pltpu.roll(x, w - HEAD_DIM // 2, 1), pltpu.roll(x, HEAD_DIM // 2, 1))
    return x * cos + partner * sin_signed


def _dot_hi_lo(a, b_hi, b_lo=None):
    a_hi = a.astype(BF16)
    a_lo = (a - a_hi.astype(F32)).astype(BF16)
    out = jnp.dot(a_hi, b_hi, preferred_element_type=F32) + jnp.dot(a_lo, b_hi, preferred_element_type=F32)
    if b_lo is not None:
        out = out + jnp.dot(a_hi, b_lo, preferred_element_type=F32)
    return out


def _mix_in_kernel(x_ref, g_ref, wa_ref, wi_ref, wil_ref, wh_ref, qn_ref, kn_ref, cos_ref, sin_ref, hsum_ref, hexp_ref,
                   hc_ref, qt_ref, k_ref, vt_ref, iq3_ref, ik3_ref, iw_ref, hg_ref):
    h = _rmsnorm_rows(x_ref[...], g_ref[...])
    hb = h.astype(BF16)
    cos1 = cos_ref[...]
    sin1 = sin_ref[...]
    cos4 = jnp.concatenate([cos1] * 4, axis=1)
    sin4 = jnp.concatenate([sin1] * 4, axis=1)

    def head_norm(a, gain):
        ss = _dot_hi_lo(a * a, hsum_ref[...])
        r = lax.rsqrt(ss * (1.0 / HEAD_DIM) + EPS)
        return a * _dot_hi_lo(r, hexp_ref[...]) * gain

    pa = jnp.dot(hb, wa_ref[...], preferred_element_type=F32)
    q = _rope(head_norm(pa[:, :ATTN_WIDTH], qn_ref[...]), cos4, sin4) * (HEAD_DIM ** -0.5 * LOG2_E)
    k = _rope(head_norm(pa[:, ATTN_WIDTH:2 * ATTN_WIDTH], kn_ref[...]), cos4, sin4)
    qt_ref[...] = q.T.astype(BF16)
    k_ref[...] = k.astype(BF16)
    v = pa[:, 2 * ATTN_WIDTH:]
    ones_pad = jnp.where(lax.broadcasted_iota(jnp.int32, (VT_ROWS - HEAD_DIM, KEY_BLOCK), 0) == 0, 1.0, 0.0)
    for s in range(v.shape[0] // KEY_BLOCK):
        vt = v[KEY_BLOCK * s:KEY_BLOCK * (s + 1)].T
        rows = []
        for hd in range(ATTN_HEADS):
            rows += [vt[HEAD_DIM * hd:HEAD_DIM * (hd + 1)], ones_pad]
        vt_ref[s] = jnp.concatenate(rows, axis=0).astype(BF16)

    pi = _dot_hi_lo(h, wi_ref[...], wil_ref[...])
    nq = IDX_HEADS * IDX_DIM
    iq = _rope(pi[:, :nq], cos4[:, :nq], sin4[:, :nq]) * (IDX_DIM ** -0.5)
    slab = pi[:, nq:]
    ikw_t = _rope(slab, cos1, sin1).T
    iw_ref[...] = slab.T[IDX_DIM:IDX_DIM + 8] * (IDX_HEADS ** -0.5)

    def hi_lo(a):
        hi = a.astype(BF16).astype(F32)
        return hi, (a - hi).astype(BF16).astype(F32)

    k_hi, k_lo = hi_lo(ikw_t[:IDX_DIM])
    ik3_ref[...] = jnp.concatenate([k_hi, k_lo, k_hi, jnp.zeros_like(k_hi)], axis=0).T.astype(BF16)
    iq_t = iq.T
    rows = []
    for hd in range(IDX_HEADS):
        q_hi, q_lo = hi_lo(iq_t[IDX_DIM * hd:IDX_DIM * (hd + 1)])
        rows += [q_hi, q_hi, q_lo, jnp.zeros_like(q_hi)]
    iq3_ref[...] = jnp.concatenate(rows, axis=0).astype(BF16)

    ph = jnp.dot(hb, wh_ref[...], preferred_element_type=F32)
    w = HG_WIDTH
    hq, hf, hg = ph[:, :w], ph[:, w:2 * w], ph[:, 3 * w:]
    log_lb, log1m_lb, one_m_lb = hc_ref[0:1, :], hc_ref[1:2, :], hc_ref[2:3, :]
    c = log1m_lb + jnp.minimum(hf, 0.0) - jnp.log1p(jnp.exp(-jnp.abs(hf)))
    hg_ref[:, :w] = hq * _sigmoid(hq)
    hg_ref[:, w:2 * w] = jnp.maximum(log_lb, c) + jnp.log1p(jnp.exp(-jnp.abs(log_lb - c)))
    hg_ref[:, 2 * w:3 * w] = one_m_lb * _sigmoid(-hf)
    hg_ref[:, 3 * w:4 * w] = ph[:, 2 * w:3 * w]
    hg_ref[:, 4 * w:] = hg * _sigmoid(hg)


def _mix_in(x2, gain, wa, wi, wil, wh, qn, kn, cos1, sin1, hsum, hexp, hconst, tm):
    nt = x2.shape[0]
    kb = KEY_BLOCK
    row = lambda w: pl.BlockSpec((tm, w), lambda i: (i, 0))
    col = lambda w: pl.BlockSpec((w, tm), lambda i: (0, i))
    return pl.pallas_call(
        _mix_in_kernel,
        grid=(nt // tm,),
        in_specs=[row(D_MODEL), _resident(gain.shape), _resident(wa.shape), _resident(wi.shape), _resident(wil.shape),
                  _resident(wh.shape),
                  _resident(qn.shape), _resident(kn.shape), row(LANES), row(LANES),
                  _resident(hsum.shape), _resident(hexp.shape), _resident(hconst.shape)],
        out_specs=[col(ATTN_WIDTH), row(ATTN_WIDTH),
                   pl.BlockSpec((tm // kb, ATTN_HEADS * VT_ROWS, kb), lambda i: (i, 0, 0)),
                   col(IDX_HEADS * 4 * IDX_DIM), row(4 * IDX_DIM), col(8), row(HG_STREAMS * HG_WIDTH)],
        out_shape=[jax.ShapeDtypeStruct((ATTN_WIDTH, nt), BF16), jax.ShapeDtypeStruct((nt, ATTN_WIDTH), BF16),
                   jax.ShapeDtypeStruct((nt // kb, ATTN_HEADS * VT_ROWS, kb), BF16),
                   jax.ShapeDtypeStruct((IDX_HEADS * 4 * IDX_DIM, nt), BF16),
                   jax.ShapeDtypeStruct((nt, 4 * IDX_DIM), BF16), jax.ShapeDtypeStruct((8, nt), F32),
                   jax.ShapeDtypeStruct((nt, HG_STREAMS * HG_WIDTH), F32)],
        compiler_params=_params(1),
        name="mix_in",
    )(x2, gain, wa, wi, wil, wh, qn, kn, cos1, sin1, hsum, hexp, hconst)


def _dsa_kernel(bound_ref, iq3_ref, iq3_next_ref, ik3_ref, iw_ref, iw_next_ref, qt_ref, k_ref, vt_ref, ltri_ref,
                o_ref, hi2_ref, lo2_ref, qz_ref, lg_ref, m_ref, alpha_ref, acc_ref, *, topk):
    kb = KEY_BLOCK
    j = pl.program_id(1)
    nk = j + 1
    int_min = jnp.int32(INT_MIN)
    slot = j % 2
    hi_ref, lo_ref = hi2_ref.at[slot], lo2_ref.at[slot]
    hi_next_ref, lo_next_ref = hi2_ref.at[1 - slot], lo2_ref.at[1 - slot]

    zeros = jnp.zeros((HEAD_DIM, kb), BF16)
    for h in range(ATTN_HEADS):
        qh = qt_ref[HEAD_DIM * h:HEAD_DIM * (h + 1), :]
        qz_ref[h] = jnp.concatenate([qh, zeros] if h % 2 == 0 else [zeros, qh], axis=0)

    def score_head(c, h, next_block):
        q3, w = (iq3_next_ref, iw_next_ref) if next_block else (iq3_ref, iw_ref)
        rel = jnp.dot(ik3_ref[c], q3[4 * IDX_DIM * h:4 * IDX_DIM * (h + 1), :], preferred_element_type=F32)
        return w[h:h + 1, :] * jnp.maximum(rel, 0.0)

    def store_keys(c, score, next_block, diagonal):
        bits = pltpu.bitcast(score, jnp.int32)
        key = jnp.where(bits < 0, int_min - bits, bits)
        if diagonal:
            kpos = lax.broadcasted_iota(jnp.int32, (kb, kb), 0)
            qpos = lax.broadcasted_iota(jnp.int32, (kb, kb), 1)
            key = jnp.where(kpos <= qpos, key, int_min)
        hi_dst, lo_dst = (hi_next_ref, lo_next_ref) if next_block else (hi_ref, lo_ref)
        hi_dst[c] = lax.shift_right_arithmetic(key, 16).astype(jnp.int16)
        lo_dst[c] = (key ^ 0x8000).astype(jnp.int16)

    store_keys(j, sum(score_head(j, h, False) for h in range(IDX_HEADS)), False, True)

    one_b, zero_b = jnp.ones((), BF16), jnp.zeros((), BF16)
    one_i, zero_i = jnp.ones((), jnp.int16), jnp.zeros((), jnp.int16)
    rows16 = 16

    hi_ref[nk] = jnp.full((kb, kb), HALF_MIN, jnp.int16)
    lo_ref[nk] = jnp.full((kb, kb), HALF_MIN, jnp.int16)

    def count(hit_fn):
        def fold(hit):
            parts = [hit[rows16 * r:rows16 * (r + 1)] for r in range(kb // rows16)]
            while len(parts) > 1:
                parts = [a + b for a, b in zip(parts[::2], parts[1::2])]
            return parts[0]

        def body(i, cnt):
            return cnt + (fold(hit_fn(2 * i)) + fold(hit_fn(2 * i + 1)))

        cnt = lax.fori_loop(0, (nk + 1) // 2, body, jnp.zeros((rows16, kb), jnp.int16))
        return jnp.sum(cnt.astype(jnp.int32), axis=0, keepdims=True).astype(F32)

    def bisect_half(ref, target):
        def bit_step(i, thr_u):
            cand_u = thr_u | lax.shift_left(jnp.int32(1), 15 - i)
            cand = (cand_u + HALF_MIN).astype(jnp.int16)
            cnt = count(lambda c: jnp.where(ref[c] >= cand, one_i, zero_i))
            return jnp.where(cnt >= target, cand_u, thr_u)
        return lax.fori_loop(0, 16, bit_step, jnp.zeros((1, kb), jnp.int32)) + HALF_MIN

    thr_hi32 = bisect_half(hi_ref, float(topk))
    thr_hi = thr_hi32.astype(jnp.int16)
    n_gt_hi = count(lambda c: jnp.where(hi_ref[c] > thr_hi, one_i, zero_i))

    def keep_matching_low(c, carry):
        lo_ref[c] = jnp.where(hi_ref[c] == thr_hi, lo_ref[c], jnp.int16(HALF_MIN))
        return carry

    lax.fori_loop(0, nk, keep_matching_low, 0)
    thr_lo32 = bisect_half(lo_ref, topk - n_gt_hi)
    thr_lo = thr_lo32.astype(jnp.int16)
    n_gt = n_gt_hi + count(lambda c: jnp.where(lo_ref[c] > thr_lo, one_i, zero_i))
    is_marker = (thr_hi32 == HALF_MIN) & (thr_lo32 == HALF_MIN)
    need = jnp.where(is_marker, 0.0, topk - n_gt)

    def bias_chunk(c, ties_before):
        hi, lo = hi_ref[c], lo_ref[c]
        above = jnp.where(hi > thr_hi, one_b, jnp.where(lo > thr_lo, one_b, zero_b)).astype(F32)
        eq_b = jnp.where(hi == thr_hi, jnp.where(lo == thr_lo, one_b, zero_b), zero_b)
        eq = eq_b.astype(F32)
        rank = jnp.dot(ltri_ref[...], eq_b, preferred_element_type=F32) + ties_before
        chosen = above + jnp.where(rank < need, eq, 0.0)
        bias = jnp.where(chosen > 0.5, 0.0, NEG_BIG).astype(BF16)
        hi_ref[c] = pltpu.bitcast(bias, jnp.int16)
        return ties_before + jnp.sum(eq, axis=0, keepdims=True)


    def attend(bounded):
        acc_ref[...] = jnp.zeros(acc_ref.shape, F32)
        if not bounded:
            m_ref[...] = jnp.full(m_ref.shape, NEG_BIG, F32)

        def logits_head(c, h, bias):
            kk = k_ref[c, :, LANES * (h // 2):LANES * (h // 2 + 1)]
            lgb = jnp.dot(kk, qz_ref[h], preferred_element_type=F32) + bias
            lg_ref[h] = lgb
            if not bounded:
                m_old = m_ref[h]
                m_new = jnp.maximum(m_old, jnp.max(lgb, axis=0, keepdims=True))
                alpha_ref[h] = jnp.exp2(m_old - m_new)
                m_ref[h] = m_new

        def softmax_pv_head(c, h):
            p = jnp.exp2(lg_ref[h] if bounded else lg_ref[h] - m_ref[h])
            pv = jnp.dot(vt_ref[c, VT_ROWS * h:VT_ROWS * (h + 1), :], p.astype(BF16),
                         preferred_element_type=F32)
            acc_ref[h] = (acc_ref[h] if bounded else alpha_ref[h] * acc_ref[h]) + pv

        def chunk_bias(c):
            return pltpu.bitcast(hi_ref[c], BF16).astype(F32)

        def next_block_scores(c, h, partial):
            if h % (ATTN_HEADS // IDX_HEADS) != ATTN_HEADS // IDX_HEADS - 1:
                return partial
            term = score_head(c, h // (ATTN_HEADS // IDX_HEADS), True)
            return term if partial is None else partial + term

        ties = bias_chunk(0, jnp.zeros((1, kb), F32))
        bias0 = chunk_bias(0)
        partial = None
        for h in range(ATTN_HEADS):
            logits_head(0, h, bias0)
            partial = next_block_scores(0, h, partial)
        store_keys(0, partial, True, False)
        ties = bias_chunk(1, ties)

        def pipelined(c, ties_before):
            bias = chunk_bias(c)
            ties_after = bias_chunk(c + 1, ties_before)
            partial = None
            for h in range(ATTN_HEADS):
                softmax_pv_head(c - 1, h)
                logits_head(c, h, bias)
                partial = next_block_scores(c, h, partial)
            store_keys(c, partial, True, False)
            return ties_after

        lax.fori_loop(1, nk, pipelined, ties)
        for h in range(ATTN_HEADS):
            softmax_pv_head(nk - 1, h)

    logits_bounded = bound_ref[0] < MAX_SAFE_LOGIT

    @pl.when(logits_bounded)
    def _():
        attend(True)

    @pl.when(jnp.logical_not(logits_bounded))
    def _():
        attend(False)

    out_t = [acc_ref[h][:HEAD_DIM] / acc_ref[h][HEAD_DIM:HEAD_DIM + 1] for h in range(ATTN_HEADS)]
    o_ref[...] = jnp.concatenate(out_t, axis=0).T.astype(o_ref.dtype)


def _dsa(logit_bound, iq3t, ik3, iwt, qt, k, vt4, ltri, b, t, topk):
    kb = KEY_BLOCK
    nkc = t // kb
    chunked = lambda a: a.reshape(b * nkc, kb, a.shape[-1])
    per_batch = lambda shape: pl.BlockSpec((nkc,) + shape, lambda i, j: (i, 0, 0), pipeline_mode=pl.Buffered(1))
    q_cols = lambda rows: pl.BlockSpec((rows, kb), lambda i, j: (0, i * nkc + j))
    next_q_cols = lambda rows: pl.BlockSpec((rows, kb), lambda i, j: (0, i * nkc + jnp.minimum(j + 1, nkc - 1)))
    return pl.pallas_call(
        functools.partial(_dsa_kernel, topk=topk),
        grid=(b, nkc),
        in_specs=[
            pl.BlockSpec(memory_space=pltpu.SMEM),
            q_cols(IDX_HEADS * 4 * IDX_DIM),
            next_q_cols(IDX_HEADS * 4 * IDX_DIM),
            per_batch((kb, 4 * IDX_DIM)),
            q_cols(8),
            next_q_cols(8),
            q_cols(ATTN_WIDTH),
            per_batch((kb, ATTN_WIDTH)),
            per_batch((ATTN_HEADS * VT_ROWS, kb)),
            _resident(ltri.shape),
        ],
        out_specs=pl.BlockSpec((kb, ATTN_WIDTH), lambda i, j: (i * nkc + j, 0)),
        out_shape=jax.ShapeDtypeStruct((b * t, ATTN_WIDTH), BF16),
        scratch_shapes=[
            pltpu.VMEM((2, nkc + 1, kb, kb), jnp.int16),
            pltpu.VMEM((2, nkc + 1, kb, kb), jnp.int16),
            pltpu.VMEM((ATTN_HEADS, LANES, kb), BF16),
            pltpu.VMEM((ATTN_HEADS, kb, kb), F32),
            pltpu.VMEM((ATTN_HEADS, 1, kb), F32),
            pltpu.VMEM((ATTN_HEADS, 1, kb), F32),
            pltpu.VMEM((ATTN_HEADS, VT_ROWS, kb), F32),
        ],
        compiler_params=_params(2),
        name="dsa",
    )(logit_bound, iq3t, iq3t, chunked(ik3), iwt, iwt, qt, chunked(k), vt4, ltri)


def _hgrn_kernel(q_ref, lf_ref, k_ref, v_ref, gate_ref, hc_ref, lt_ref, o_ref, st_ref, intra_ref):
    @pl.when(pl.program_id(1) == 0)
    def _():
        st_ref[...] = jnp.zeros(st_ref.shape, F32)

    for ci in range(HG_STEP_CHUNKS):
        rows = slice(HG_CHUNK * ci, HG_CHUNK * (ci + 1))
        _hgrn_chunk(q_ref[rows, :], lf_ref[rows, :], k_ref[rows, :], v_ref.at[rows, :], gate_ref[rows, :],
                    hc_ref[3:4, :], lt_ref, o_ref.at[rows, :], st_ref, intra_ref)


def _hgrn_chunk(q_all, log_f, k_all, hi_ref, gate, out_gain, lt_ref, o_ref, st_ref, intra_ref):
    c_len, sub = HG_CHUNK, HG_SUB
    b_all = jnp.dot(lt_ref[...], log_f, precision=HIGHEST, preferred_element_type=F32)
    b_last = b_all[c_len - 1:c_len, :]
    q_inter = q_all * jnp.exp(b_all)
    k_carry = k_all * jnp.exp(b_last - b_all)

    n_sub = c_len // sub
    trans_b = (((1,), (1,)), ((), ()))

    def block_start(b, i):
        return b[sub * i - 1:sub * i, :] if i > 0 else jnp.zeros_like(b[0:1, :])

    def intra_factored_all():
        fs = HG_FACT
        heads = [slice(HG_DIM * h, HG_DIM * (h + 1)) for h in range(HG_HEADS)]
        scores = []
        for hs in heads:
            b, q, k = b_all[:, hs], q_all[:, hs], k_all[:, hs]
            for i in range(c_len // fs):
                rs, upto = slice(fs * i, fs * (i + 1)), slice(0, fs * (i + 1))
                b_ref = b[fs * i - 1:fs * i, :] if i > 0 else jnp.zeros_like(b[0:1, :])
                q_hat = (q[rs] * jnp.exp(b[rs] - b_ref)).astype(BF16)
                k_hat = (k[upto] * jnp.exp(b_ref - b[upto])).astype(BF16)
                scores.append(lax.dot_general(q_hat, k_hat, trans_b, preferred_element_type=F32))
        for h, hs in enumerate(heads):
            vb = hi_ref[:, hs].astype(BF16)
            blocks = []
            for i in range(c_len // fs):
                a = scores[h * (c_len // fs) + i]
                t_idx = fs * i + lax.broadcasted_iota(jnp.int32, a.shape, 0)
                s_idx = lax.broadcasted_iota(jnp.int32, a.shape, 1)
                a = jnp.where(s_idx <= t_idx, a, 0.0).astype(BF16)
                blocks.append(jnp.dot(a, vb[0:fs * (i + 1)], preferred_element_type=F32))
            intra_ref[h] = jnp.concatenate(blocks, axis=0)

    def intra_direct(b, q, k, vb):
        row = lax.broadcasted_iota(jnp.int32, (sub, 1), 0)
        col = lax.broadcasted_iota(jnp.int32, (sub, sub), 1)
        blocks = []
        for i in range(n_sub):
            rs = slice(sub * i, sub * (i + 1))
            b_i, q_i, k_i = b[rs], q[rs], k[rs]
            diag = jnp.zeros((sub, sub), F32)
            for s in range(sub):
                decay = jnp.exp(jnp.where(row >= s, b_i - b_i[s:s + 1, :], -jnp.inf))
                a_col = jnp.sum(q_i * k_i[s:s + 1, :] * decay, axis=1, keepdims=True)
                diag = jnp.where(col == s, a_col, diag)
            o_i = jnp.dot(diag.astype(BF16), vb[rs], preferred_element_type=F32)
            if i > 0:
                prev = slice(0, sub * i)
                b_ref = block_start(b, i)
                q_hat = (q_i * jnp.exp(b_i - b_ref)).astype(BF16)
                k_hat = (k[prev] * jnp.exp(b_ref - b[prev])).astype(BF16)
                a_off = lax.dot_general(q_hat, k_hat, trans_b, preferred_element_type=F32)
                o_i = o_i + jnp.dot(a_off.astype(BF16), vb[prev], preferred_element_type=F32)
            blocks.append(o_i)
        return jnp.concatenate(blocks, axis=0)

    ends = [b_all[HG_FACT * (i + 1) - 1:HG_FACT * (i + 1), :] for i in range(c_len // HG_FACT)]
    drops = [(ends[i - 1] if i > 0 else 0.0) - ends[i] for i in range(len(ends))]
    factored_ok = jnp.max(functools.reduce(jnp.maximum, drops)) < MAX_SAFE_EXPONENT

    @pl.when(factored_ok)
    def _():
        intra_factored_all()

    @pl.when(jnp.logical_not(factored_ok))
    def _():
        for h in range(HG_HEADS):
            hs = slice(HG_DIM * h, HG_DIM * (h + 1))
            intra_ref[h] = intra_direct(b_all[:, hs], q_all[:, hs], k_all[:, hs], hi_ref[:, hs].astype(BF16))

    outs = []
    for h in range(HG_HEADS):
        hs = slice(HG_DIM * h, HG_DIM * (h + 1))
        v = hi_ref[:, hs]
        st = st_ref[h]
        inter = lax.dot_general(q_inter[:, hs].astype(BF16), st.astype(BF16), trans_b,
                                preferred_element_type=F32)
        o = inter + intra_ref[h]
        st_ref[h] = jnp.exp(b_last[:, hs]) * st + jnp.dot(v.T.astype(BF16), k_carry[:, hs].astype(BF16),
                                                         preferred_element_type=F32)
        outs.append(_rmsnorm_rows(o, out_gain[:, hs]) * gate[:, hs])
    o_ref[...] = jnp.concatenate(outs, axis=1).astype(o_ref.dtype)


def _hgrn(hg5, hconst, ltri, b, t):
    rows = HG_CHUNK * HG_STEP_CHUNKS
    ns = t // rows
    col = lambda jcol: pl.BlockSpec((rows, HG_WIDTH), lambda i, s, jcol=jcol: (i * ns + s, jcol))
    return pl.pallas_call(
        _hgrn_kernel,
        grid=(b, ns),
        in_specs=[col(jcol) for jcol in range(HG_STREAMS)] + [_resident(hconst.shape), _resident(ltri.shape)],
        out_specs=pl.BlockSpec((rows, HG_WIDTH), lambda i, s: (i * ns + s, 0)),
        out_shape=jax.ShapeDtypeStruct((b * t, HG_WIDTH), BF16),
        scratch_shapes=[pltpu.VMEM((HG_HEADS, HG_DIM, HG_DIM), F32),
                        pltpu.VMEM((HG_HEADS, HG_CHUNK, HG_DIM), F32)],
        compiler_params=_params(2),
        name="hgrn",
    )(*([hg5] * HG_STREAMS), hconst, ltri)


def _merge_kernel(x_ref, ya_ref, yh_ref, g_ref, wg_ref, wpa_ref, wph_ref, wo_ref, o_ref):
    x = x_ref[...]
    hb = _rmsnorm_rows(x, g_ref[...]).astype(BF16)
    gates = jnp.dot(hb, wg_ref[...], preferred_element_type=F32)
    pa = jnp.dot(ya_ref[...], wpa_ref[...], preferred_element_type=F32)
    ph = jnp.dot(yh_ref[...], wph_ref[...], preferred_element_type=F32)
    merged = _sigmoid(gates[:, :D_MODEL]) * pa + _sigmoid(gates[:, D_MODEL:]) * ph
    o_ref[...] = x + jnp.dot(merged.astype(BF16), wo_ref[...], preferred_element_type=F32)


def _merge(x2, ya, yh, gain, wg, wpa, wph, wo, tm):
    nt = x2.shape[0]
    row = lambda w: pl.BlockSpec((tm, w), lambda i: (i, 0))
    return pl.pallas_call(
        _merge_kernel,
        grid=(nt // tm,),
        in_specs=[row(D_MODEL), row(ATTN_WIDTH), row(HG_WIDTH), _resident(gain.shape), _resident(wg.shape),
                  _resident(wpa.shape), _resident(wph.shape), _resident(wo.shape)],
        out_specs=row(D_MODEL),
        out_shape=jax.ShapeDtypeStruct((nt, D_MODEL), F32),
        compiler_params=_params(1),
        name="merge",
    )(x2, ya, yh, gain, wg, wpa, wph, wo)


def _ffn_kernel(x_ref, g_ref, wi_ref, wo_ref, o_ref):
    x = x_ref[...]
    hb = _rmsnorm_rows(x, g_ref[...]).astype(BF16)
    gu = jnp.dot(hb, wi_ref[...], preferred_element_type=F32)
    g = gu[:, :D_FF]
    act = g * _sigmoid(g) * gu[:, D_FF:]
    o_ref[...] = x + jnp.dot(act.astype(BF16), wo_ref[...], preferred_element_type=F32)


def _ffn(x2, gain, wi, wo, tm):
    nt = x2.shape[0]
    row = pl.BlockSpec((tm, D_MODEL), lambda i: (i, 0))
    return pl.pallas_call(
        _ffn_kernel,
        grid=(nt // tm,),
        in_specs=[row, _resident(gain.shape), _resident(wi.shape), _resident(wo.shape)],
        out_specs=row,
        out_shape=jax.ShapeDtypeStruct((nt, D_MODEL), F32),
        compiler_params=_params(1),
        name="ffn",
    )(x2, gain, wi, wo)


def _split_hi_lo(a):
    hi = a.astype(BF16)
    lo = (a - hi.astype(F32)).astype(BF16)
    return hi, lo


def _layer(x2, b, t, cos1, sin1, consts, w):
    hsum, hexp, ltri_attn, ltri_hgrn = consts
    qt, k, vt4, iq3t, ik3, iwt, hg5 = _mix_in(x2, w["norm_mix"], w["wa"], *w["wi"], w["wh"], w["qn"], w["kn"],
                                              cos1, sin1, hsum, hexp, w["hconst"], tm=512)
    y_attn = _dsa(w["logit_bound"], iq3t, ik3, iwt, qt, k, vt4, ltri_attn, b, t, topk=min(MAX_TOPK, t // 4))
    y_hgrn = _hgrn(hg5, w["hconst"], ltri_hgrn, b, t)
    x2 = _merge(x2, y_attn, y_hgrn, w["norm_mix"], w["wg"], w["wpa"], w["wph"], w["wout"], tm=512)
    return _ffn(x2, w["norm_ffn"], w["wffn_in"], w["wffn_out"], tm=256)


def _constants():
    head_of_lane = np.arange(ATTN_WIDTH) // HEAD_DIM
    hsum = (head_of_lane[:, None] == np.arange(LANES)[None, :]).astype(np.float32)
    hexp = hsum.T.copy()
    r = np.arange(KEY_BLOCK)
    ltri_attn = (r[None, :] < r[:, None]).astype(np.float32)
    r = np.arange(HG_CHUNK)
    ltri_hgrn = (r[None, :] <= r[:, None]).astype(np.float32)
    return jnp.asarray(hsum, BF16), jnp.asarray(hexp, BF16), jnp.asarray(ltri_attn, BF16), jnp.asarray(ltri_hgrn)


def kernel(x, positions, w_in, w_proj_attn, w_proj_hgrn, w_out, norm_mix, norm_ffn, q_norm, k_norm, hgrn_norm,
           hgrn_lower_bound, w_ffn_in, w_ffn_out):
    b, t, d = x.shape
    depth = w_in.shape[0]
    nt = b * t

    inv = ROPE_THETA ** (-jnp.arange(0, HEAD_DIM, 2, dtype=F32) / HEAD_DIM)
    ang = positions.astype(F32)[..., None] * inv
    cos, sin = jnp.cos(ang), jnp.sin(ang)
    cos1 = jnp.concatenate([cos, cos, cos, cos], axis=-1).reshape(nt, LANES)
    sin1 = jnp.concatenate([-sin, sin, -sin, sin], axis=-1).reshape(nt, LANES)

    lb_all = jnp.cumsum(jax.nn.softmax(hgrn_lower_bound.astype(F32), axis=0), axis=0)
    lb_all = lb_all - lb_all[:1]

    consts = _constants()
    widths = (ATTN_WIDTH, ATTN_WIDTH, ATTN_WIDTH, IDX_HEADS * IDX_DIM, IDX_DIM, IDX_HEADS,
              HG_WIDTH, HG_WIDTH, HG_WIDTH, HG_WIDTH, D_MODEL, D_MODEL)
    off = np.concatenate([[0], np.cumsum(widths)])
    idx_cols = off[6] - off[3]

    x2 = x.reshape(nt, d)
    for l in range(depth):
        wl = w_in[l]
        lb = lb_all[l]
        hconst = jnp.zeros((8, HG_WIDTH), F32)
        hconst = hconst.at[0].set(jnp.log(lb)).at[1].set(jnp.log1p(-lb)).at[2].set(1.0 - lb)
        hconst = hconst.at[3].set(jnp.tile(hgrn_norm[l], HG_HEADS))
        w = {
            "norm_mix": norm_mix[l].reshape(1, d),
            "norm_ffn": norm_ffn[l].reshape(1, d),
            "wa": wl[:, off[0]:off[3]].astype(BF16),
            "wi": _split_hi_lo(jnp.pad(wl[:, off[3]:off[6]], ((0, 0), (0, 3 * LANES - idx_cols)))),
            "wh": wl[:, off[6]:off[10]].astype(BF16),
            "wg": wl[:, off[10]:off[12]].astype(BF16),
            "qn": jnp.tile(q_norm[l], ATTN_HEADS).reshape(1, ATTN_WIDTH),
            "kn": jnp.tile(k_norm[l], ATTN_HEADS).reshape(1, ATTN_WIDTH),
            "logit_bound": (HEAD_DIM ** 0.5 * LOG2_E * jnp.max(jnp.abs(q_norm[l])) * jnp.max(jnp.abs(k_norm[l]))
                            ).reshape(1).astype(F32),
            "hconst": hconst,
            "wpa": w_proj_attn[l].astype(BF16),
            "wph": w_proj_hgrn[l].astype(BF16),
            "wout": w_out[l].astype(BF16),
            "wffn_in": w_ffn_in[l].astype(BF16),
            "wffn_out": w_ffn_out[l].astype(BF16),
        }
        x2 = _layer(x2, b, t, cos1, sin1, consts, w)
    return x2.reshape(b, t, d)
```

```python
import functools

import numpy as np
import jax
import jax.numpy as jnp
from jax import lax
from jax.experimental import pallas as pl
from jax.experimental.pallas import tpu as pltpu

D_MODEL = 1024
ATTN_HEADS = 8
HEAD_DIM = 64
ATTN_WIDTH = ATTN_HEADS * HEAD_DIM
IDX_HEADS = 4
IDX_DIM = 64
MAX_TOPK = 256
HG_HEADS = 4
HG_DIM = 128
HG_WIDTH = HG_HEADS * HG_DIM
HG_CHUNK = 64
HG_STREAMS = 5
HG_STEP_CHUNKS = 2
HG_SUB = 16
HG_FACT = 32
MAX_SAFE_LOGIT = 60.0
MAX_SAFE_EXPONENT = 80.0
D_FF = 2816
ROPE_THETA = 10000.0
EPS = 1e-6
LOG2_E = 1.4426950408889634

LANES = 128
VMEM_LIMIT_BYTES = 56 * 1024 * 1024

KEY_BLOCK = 256
VT_ROWS = 80
INT_MIN = -(2 ** 31)
HALF_MIN = -(2 ** 15)
NEG_BIG = -1e30

F32 = jnp.float32
BF16 = jnp.bfloat16


def _resident(shape):
    nd = len(shape)
    return pl.BlockSpec(shape, lambda *_: (0,) * nd, pipeline_mode=pl.Buffered(1))


def _params(n_axes):
    return pltpu.CompilerParams(dimension_semantics=("arbitrary",) * n_axes,
                                vmem_limit_bytes=VMEM_LIMIT_BYTES)


def _rmsnorm_rows(x, gain):
    return x * lax.rsqrt(jnp.mean(x * x, axis=-1, keepdims=True) + EPS) * gain


def _sigmoid(x):
    return 1.0 / (1.0 + jnp.exp(-x))


def _rope(x, cos, sin_signed):
    w = x.shape[-1]
    lane = lax.broadcasted_iota(jnp.int32, x.shape, 1)
    first_half = (lane & (HEAD_DIM - 1)) < HEAD_DIM // 2
    partner = jnp.where(first_half, pltpu.roll(x, w - HEAD_DIM // 2, 1), pltpu.roll(x, HEAD_DIM // 2, 1))
    return x * cos + partner * sin_signed


def _dot_hi_lo(a, b_hi, b_lo=None):
    a_hi = a.astype(BF16)
    a_lo = (a - a_hi.astype(F32)).astype(BF16)
    out = jnp.dot(a_hi, b_hi, preferred_element_type=F32) + jnp.dot(a_lo, b_hi, preferred_element_type=F32)
    if b_lo is not None:
        out = out + jnp.dot(a_hi, b_lo, preferred_element_type=F32)
    return out


def _dot_exact_lhs(a, b):
    b1 = b.astype(BF16)
    r1 = b - b1.astype(F32)
    b2 = r1.astype(BF16)
    b3 = (r1 - b2.astype(F32)).astype(BF16)
    return (jnp.dot(a, b1, preferred_element_type=F32) + jnp.dot(a, b2, preferred_element_type=F32)
            + jnp.dot(a, b3, preferred_element_type=F32))


def _mix_in_kernel(x_ref, g_ref, wa_ref, wi_ref, wil_ref, wh_ref, qn_ref, kn_ref, cos_ref, sin_ref, hsum_ref, hexp_ref,
                   hc_ref, qt_ref, k_ref, vt_ref, iq3_ref, ik3_ref, iw_ref, hg_ref):
    h = _rmsnorm_rows(x_ref[...], g_ref[...])
    hb = h.astype(BF16)
    cos1 = cos_ref[...]
    sin1 = sin_ref[...]
    cos4 = jnp.concatenate([cos1] * 4, axis=1)
    sin4 = jnp.concatenate([sin1] * 4, axis=1)

    def head_norm(a, gain):
        ss = _dot_hi_lo(a * a, hsum_ref[...])
        r = lax.rsqrt(ss * (1.0 / HEAD_DIM) + EPS)
        return a * _dot_hi_lo(r, hexp_ref[...]) * gain

    pa = jnp.dot(hb, wa_ref[...], preferred_element_type=F32)
    q = _rope(head_norm(pa[:, :ATTN_WIDTH], qn_ref[...]), cos4, sin4) * (HEAD_DIM ** -0.5 * LOG2_E)
    k = _rope(head_norm(pa[:, ATTN_WIDTH:2 * ATTN_WIDTH], kn_ref[...]), cos4, sin4)
    qt_ref[...] = q.T.astype(BF16)
    k_ref[...] = k.astype(BF16)
    v = pa[:, 2 * ATTN_WIDTH:]
    ones_pad = jnp.where(lax.broadcasted_iota(jnp.int32, (VT_ROWS - HEAD_DIM, KEY_BLOCK), 0) == 0, 1.0, 0.0)
    for s in range(v.shape[0] // KEY_BLOCK):
        vt = v[KEY_BLOCK * s:KEY_BLOCK * (s + 1)].T
        rows = []
        for hd in range(ATTN_HEADS):
            rows += [vt[HEAD_DIM * hd:HEAD_DIM * (hd + 1)], ones_pad]
        vt_ref[s] = jnp.concatenate(rows, axis=0).astype(BF16)

    pi = _dot_hi_lo(h, wi_ref[...], wil_ref[...])
    nq = IDX_HEADS * IDX_DIM
    iq = _rope(pi[:, :nq], cos4[:, :nq], sin4[:, :nq]) * (IDX_DIM ** -0.5)
    slab = pi[:, nq:]
    ikw_t = _rope(slab, cos1, sin1).T
    iw_ref[...] = slab.T[IDX_DIM:IDX_DIM + 8] * (IDX_HEADS ** -0.5)

    def hi_lo(a):
        hi = a.astype(BF16).astype(F32)
        return hi, (a - hi).astype(BF16).astype(F32)

    k_hi, k_lo = hi_lo(ikw_t[:IDX_DIM])
    ik3_ref[...] = jnp.concatenate([k_hi, k_lo, k_hi, jnp.zeros_like(k_hi)], axis=0).T.astype(BF16)
    iq_t = iq.T
    rows = []
    for hd in range(IDX_HEADS):
        q_hi, q_lo = hi_lo(iq_t[IDX_DIM * hd:IDX_DIM * (hd + 1)])
        rows += [q_hi, q_hi, q_lo, jnp.zeros_like(q_hi)]
    iq3_ref[...] = jnp.concatenate(rows, axis=0).astype(BF16)

    ph = jnp.dot(hb, wh_ref[...], preferred_element_type=F32)
    w = HG_WIDTH
    hq, hf, hg = ph[:, :w], ph[:, w:2 * w], ph[:, 3 * w:]
    log_lb, log1m_lb, one_m_lb = hc_ref[0:1, :], hc_ref[1:2, :], hc_ref[2:3, :]
    c = log1m_lb + jnp.minimum(hf, 0.0) - jnp.log1p(jnp.exp(-jnp.abs(hf)))
    hg_ref[:, :w] = hq * _sigmoid(hq)
    hg_ref[:, w:2 * w] = jnp.maximum(log_lb, c) + jnp.log1p(jnp.exp(-jnp.abs(log_lb - c)))
    hg_ref[:, 2 * w:3 * w] = one_m_lb * _sigmoid(-hf)
    hg_ref[:, 3 * w:4 * w] = ph[:, 2 * w:3 * w]
    hg_ref[:, 4 * w:] = hg * _sigmoid(hg)


def _mix_in(x2, gain, wa, wi, wil, wh, qn, kn, cos1, sin1, hsum, hexp, hconst, tm):
    nt = x2.shape[0]
    kb = KEY_BLOCK
    row = lambda w: pl.BlockSpec((tm, w), lambda i: (i, 0))
    col = lambda w: pl.BlockSpec((w, tm), lambda i: (0, i))
    return pl.pallas_call(
        _mix_in_kernel,
        grid=(nt // tm,),
        in_specs=[row(D_MODEL), _resident(gain.shape), _resident(wa.shape), _resident(wi.shape), _resident(wil.shape),
                  _resident(wh.shape),
                  _resident(qn.shape), _resident(kn.shape), row(LANES), row(LANES),
                  _resident(hsum.shape), _resident(hexp.shape), _resident(hconst.shape)],
        out_specs=[col(ATTN_WIDTH), row(ATTN_WIDTH),
                   pl.BlockSpec((tm // kb, ATTN_HEADS * VT_ROWS, kb), lambda i: (i, 0, 0)),
                   col(IDX_HEADS * 4 * IDX_DIM), row(4 * IDX_DIM), col(8), row(HG_STREAMS * HG_WIDTH)],
        out_shape=[jax.ShapeDtypeStruct((ATTN_WIDTH, nt), BF16), jax.ShapeDtypeStruct((nt, ATTN_WIDTH), BF16),
                   jax.ShapeDtypeStruct((nt // kb, ATTN_HEADS * VT_ROWS, kb), BF16),
                   jax.ShapeDtypeStruct((IDX_HEADS * 4 * IDX_DIM, nt), BF16),
                   jax.ShapeDtypeStruct((nt, 4 * IDX_DIM), BF16), jax.ShapeDtypeStruct((8, nt), F32),
                   jax.ShapeDtypeStruct((nt, HG_STREAMS * HG_WIDTH), F32)],
        compiler_params=_params(1),
        name="mix_in",
    )(x2, gain, wa, wi, wil, wh, qn, kn, cos1, sin1, hsum, hexp, hconst)


def _dsa_kernel(bound_ref, iq3_ref, iq3_next_ref, ik3_ref, iw_ref, iw_next_ref, qt_ref, k_ref, vt_ref, ltri_ref,
                o_ref, hi2_ref, lo2_ref, qz_ref, lg_ref, m_ref, alpha_ref, acc_ref, *, topk):
    kb = KEY_BLOCK
    j = pl.program_id(1)
    nk = j + 1
    int_min = jnp.int32(INT_MIN)
    slot = j % 2
    hi_ref, lo_ref = hi2_ref.at[slot], lo2_ref.at[slot]
    hi_next_ref, lo_next_ref = hi2_ref.at[1 - slot], lo2_ref.at[1 - slot]

    zeros = jnp.zeros((HEAD_DIM, kb), BF16)
    for h in range(ATTN_HEADS):
        qh = qt_ref[HEAD_DIM * h:HEAD_DIM * (h + 1), :]
        qz_ref[h] = jnp.concatenate([qh, zeros] if h % 2 == 0 else [zeros, qh], axis=0)

    def score_head(c, h, next_block):
        q3, w = (iq3_next_ref, iw_next_ref) if next_block else (iq3_ref, iw_ref)
        rel = jnp.dot(ik3_ref[c], q3[4 * IDX_DIM * h:4 * IDX_DIM * (h + 1), :], preferred_element_type=F32)
        return w[h:h + 1, :] * jnp.maximum(rel, 0.0)

    def store_keys(c, score, next_block, diagonal):
        bits = pltpu.bitcast(score, jnp.int32)
        key = jnp.where(bits < 0, int_min - bits, bits)
        if diagonal:
            kpos = lax.broadcasted_iota(jnp.int32, (kb, kb), 0)
            qpos = lax.broadcasted_iota(jnp.int32, (kb, kb), 1)
            key = jnp.where(kpos <= qpos, key, int_min)
        hi_dst, lo_dst = (hi_next_ref, lo_next_ref) if next_block else (hi_ref, lo_ref)
        hi_dst[c] = lax.shift_right_arithmetic(key, 16).astype(jnp.int16)
        lo_dst[c] = (key ^ 0x8000).astype(jnp.int16)

    store_keys(j, sum(score_head(j, h, False) for h in range(IDX_HEADS)), False, True)

    one_b, zero_b = jnp.ones((), BF16), jnp.zeros((), BF16)
    one_i, zero_i = jnp.ones((), jnp.int16), jnp.zeros((), jnp.int16)
    rows16 = 16

    hi_ref[nk] = jnp.full((kb, kb), HALF_MIN, jnp.int16)
    lo_ref[nk] = jnp.full((kb, kb), HALF_MIN, jnp.int16)

    def count(hit_fn):
        def fold(hit):
            parts = [hit[rows16 * r:rows16 * (r + 1)] for r in range(kb // rows16)]
            while len(parts) > 1:
                parts = [a + b for a, b in zip(parts[::2], parts[1::2])]
            return parts[0]

        def body(i, cnt):
            return cnt + (fold(hit_fn(2 * i)) + fold(hit_fn(2 * i + 1)))

        cnt = lax.fori_loop(0, (nk + 1) // 2, body, jnp.zeros((rows16, kb), jnp.int16))
        return jnp.sum(cnt.astype(jnp.int32), axis=0, keepdims=True).astype(F32)

    def bisect_half(ref, target):
        def bit_step(i, carry):
            thr_u, n_above = carry
            cand_u = thr_u | lax.shift_left(jnp.int32(1), 15 - i)
            cand = (cand_u + HALF_MIN).astype(jnp.int16)
            cnt = count(lambda c: jnp.where(ref[c] >= cand, one_i, zero_i))
            accepted = cnt >= target
            return jnp.where(accepted, cand_u, thr_u), jnp.where(accepted, n_above, cnt)
        thr_u, n_above = lax.fori_loop(0, 16, bit_step, (jnp.zeros((1, kb), jnp.int32), jnp.zeros((1, kb), F32)))
        return thr_u + HALF_MIN, n_above

    thr_hi32, n_gt_hi = bisect_half(hi_ref, float(topk))
    thr_hi = thr_hi32.astype(jnp.int16)

    def keep_matching_low(c, carry):
        lo_ref[c] = jnp.where(hi_ref[c] == thr_hi, lo_ref[c], jnp.int16(HALF_MIN))
        return carry

    lax.fori_loop(0, nk, keep_matching_low, 0)
    thr_lo32, n_gt_lo = bisect_half(lo_ref, topk - n_gt_hi)
    thr_lo = thr_lo32.astype(jnp.int16)
    n_gt = n_gt_hi + n_gt_lo
    is_marker = (thr_hi32 == HALF_MIN) & (thr_lo32 == HALF_MIN)
    need = jnp.where(is_marker, 0.0, topk - n_gt)

    def bias_chunk(c, ties_before):
        hi, lo = hi_ref[c], lo_ref[c]
        above = jnp.where(hi > thr_hi, one_b, jnp.where(lo > thr_lo, one_b, zero_b)).astype(F32)
        eq_b = jnp.where(hi == thr_hi, jnp.where(lo == thr_lo, one_b, zero_b), zero_b)
        eq = eq_b.astype(F32)
        rank = jnp.dot(ltri_ref[...], eq_b, preferred_element_type=F32) + ties_before
        chosen = above + jnp.where(rank < need, eq, 0.0)
        bias = jnp.where(chosen > 0.5, 0.0, NEG_BIG).astype(BF16)
        hi_ref[c] = pltpu.bitcast(bias, jnp.int16)
        return ties_before + jnp.sum(eq, axis=0, keepdims=True)


    def attend(bounded):
        acc_ref[...] = jnp.zeros(acc_ref.shape, F32)
        if not bounded:
            m_ref[...] = jnp.full(m_ref.shape, NEG_BIG, F32)

        def logits_head(c, h, bias):
            kk = k_ref[c, :, LANES * (h // 2):LANES * (h // 2 + 1)]
            lgb = jnp.dot(kk, qz_ref[h], preferred_element_type=F32) + bias
            lg_ref[h] = lgb
            if not bounded:
                m_old = m_ref[h]
                m_new = jnp.maximum(m_old, jnp.max(lgb, axis=0, keepdims=True))
                alpha_ref[h] = jnp.exp2(m_old - m_new)
                m_ref[h] = m_new

        def softmax_pv_head(c, h):
            p = jnp.exp2(lg_ref[h] if bounded else lg_ref[h] - m_ref[h])
            pv = jnp.dot(vt_ref[c, VT_ROWS * h:VT_ROWS * (h + 1), :], p.astype(BF16),
                         preferred_element_type=F32)
            acc_ref[h] = (acc_ref[h] if bounded else alpha_ref[h] * acc_ref[h]) + pv

        def chunk_bias(c):
            return pltpu.bitcast(hi_ref[c], BF16).astype(F32)

        def next_block_scores(c, h, partial):
            if h % (ATTN_HEADS // IDX_HEADS) != ATTN_HEADS // IDX_HEADS - 1:
                return partial
            term = score_head(c, h // (ATTN_HEADS // IDX_HEADS), True)
            return term if partial is None else partial + term

        ties = bias_chunk(0, jnp.zeros((1, kb), F32))
        bias0 = chunk_bias(0)
        partial = None
        for h in range(ATTN_HEADS):
            logits_head(0, h, bias0)
            partial = next_block_scores(0, h, partial)
        store_keys(0, partial, True, False)
        ties = bias_chunk(1, ties)

        def pipelined(c, ties_before):
            bias = chunk_bias(c)
            ties_after = bias_chunk(c + 1, ties_before)
            partial = None
            for h in range(ATTN_HEADS):
                softmax_pv_head(c - 1, h)
                logits_head(c, h, bias)
                partial = next_block_scores(c, h, partial)
            store_keys(c, partial, True, False)
            return ties_after

        lax.fori_loop(1, nk, pipelined, ties)
        for h in range(ATTN_HEADS):
            softmax_pv_head(nk - 1, h)

    logits_bounded = bound_ref[0] < MAX_SAFE_LOGIT

    @pl.when(logits_bounded)
    def _():
        attend(True)

    @pl.when(jnp.logical_not(logits_bounded))
    def _():
        attend(False)

    out_t = [acc_ref[h][:HEAD_DIM] / acc_ref[h][HEAD_DIM:HEAD_DIM + 1] for h in range(ATTN_HEADS)]
    o_ref[...] = jnp.concatenate(out_t, axis=0).T.astype(o_ref.dtype)


def _dsa(logit_bound, iq3t, ik3, iwt, qt, k, vt4, ltri, b, t, topk):
    kb = KEY_BLOCK
    nkc = t // kb
    chunked = lambda a: a.reshape(b * nkc, kb, a.shape[-1])
    per_batch = lambda shape: pl.BlockSpec((nkc,) + shape, lambda i, j: (i, 0, 0), pipeline_mode=pl.Buffered(1))
    q_cols = lambda rows: pl.BlockSpec((rows, kb), lambda i, j: (0, i * nkc + j))
    next_q_cols = lambda rows: pl.BlockSpec((rows, kb), lambda i, j: (0, i * nkc + jnp.minimum(j + 1, nkc - 1)))
    return pl.pallas_call(
        functools.partial(_dsa_kernel, topk=topk),
        grid=(b, nkc),
        in_specs=[
            pl.BlockSpec(memory_space=pltpu.SMEM),
            q_cols(IDX_HEADS * 4 * IDX_DIM),
            next_q_cols(IDX_HEADS * 4 * IDX_DIM),
            per_batch((kb, 4 * IDX_DIM)),
            q_cols(8),
            next_q_cols(8),
            q_cols(ATTN_WIDTH),
            per_batch((kb, ATTN_WIDTH)),
            per_batch((ATTN_HEADS * VT_ROWS, kb)),
            _resident(ltri.shape),
        ],
        out_specs=pl.BlockSpec((kb, ATTN_WIDTH), lambda i, j: (i * nkc + j, 0)),
        out_shape=jax.ShapeDtypeStruct((b * t, ATTN_WIDTH), BF16),
        scratch_shapes=[
            pltpu.VMEM((2, nkc + 1, kb, kb), jnp.int16),
            pltpu.VMEM((2, nkc + 1, kb, kb), jnp.int16),
            pltpu.VMEM((ATTN_HEADS, LANES, kb), BF16),
            pltpu.VMEM((ATTN_HEADS, kb, kb), F32),
            pltpu.VMEM((ATTN_HEADS, 1, kb), F32),
            pltpu.VMEM((ATTN_HEADS, 1, kb), F32),
            pltpu.VMEM((ATTN_HEADS, VT_ROWS, kb), F32),
        ],
        compiler_params=_params(2),
        name="dsa",
    )(logit_bound, iq3t, iq3t, chunked(ik3), iwt, iwt, qt, chunked(k), vt4, ltri)


def _hgrn_kernel(q_ref, lf_ref, k_ref, v_ref, gate_ref, hc_ref, lt_ref, o_ref, st_ref, intra_ref):
    @pl.when(pl.program_id(1) == 0)
    def _():
        st_ref[...] = jnp.zeros(st_ref.shape, F32)

    for ci in range(HG_STEP_CHUNKS):
        rows = slice(HG_CHUNK * ci, HG_CHUNK * (ci + 1))
        _hgrn_chunk(q_ref[rows, :], lf_ref[rows, :], k_ref[rows, :], v_ref.at[rows, :], gate_ref[rows, :],
                    hc_ref[3:4, :], lt_ref, o_ref.at[rows, :], st_ref, intra_ref)


def _hgrn_chunk(q_all, log_f, k_all, hi_ref, gate, out_gain, lt_ref, o_ref, st_ref, intra_ref):
    c_len, sub = HG_CHUNK, HG_SUB
    b_all = _dot_exact_lhs(lt_ref[...], log_f)
    b_last = b_all[c_len - 1:c_len, :]
    q_inter = q_all * jnp.exp(b_all)
    k_carry = k_all * jnp.exp(b_last - b_all)

    n_sub = c_len // sub
    trans_b = (((1,), (1,)), ((), ()))

    def block_start(b, i):
        return b[sub * i - 1:sub * i, :] if i > 0 else jnp.zeros_like(b[0:1, :])

    def intra_factored_all():
        fs = HG_FACT
        heads = [slice(HG_DIM * h, HG_DIM * (h + 1)) for h in range(HG_HEADS)]
        scores = []
        for hs in heads:
            b, q, k = b_all[:, hs], q_all[:, hs], k_all[:, hs]
            for i in range(c_len // fs):
                rs, upto = slice(fs * i, fs * (i + 1)), slice(0, fs * (i + 1))
                b_ref = b[fs * i - 1:fs * i, :] if i > 0 else jnp.zeros_like(b[0:1, :])
                q_hat = (q[rs] * jnp.exp(b[rs] - b_ref)).astype(BF16)
                k_hat = (k[upto] * jnp.exp(b_ref - b[upto])).astype(BF16)
                scores.append(lax.dot_general(q_hat, k_hat, trans_b, preferred_element_type=F32))
        for h, hs in enumerate(heads):
            vb = hi_ref[:, hs].astype(BF16)
            blocks = []
            for i in range(c_len // fs):
                a = scores[h * (c_len // fs) + i]
                t_idx = fs * i + lax.broadcasted_iota(jnp.int32, a.shape, 0)
                s_idx = lax.broadcasted_iota(jnp.int32, a.shape, 1)
                a = jnp.where(s_idx <= t_idx, a, 0.0).astype(BF16)
                blocks.append(jnp.dot(a, vb[0:fs * (i + 1)], preferred_element_type=F32))
            intra_ref[h] = jnp.concatenate(blocks, axis=0)

    def intra_direct(b, q, k, vb):
        row = lax.broadcasted_iota(jnp.int32, (sub, 1), 0)
        col = lax.broadcasted_iota(jnp.int32, (sub, sub), 1)
        blocks = []
        for i in range(n_sub):
            rs = slice(sub * i, sub * (i + 1))
            b_i, q_i, k_i = b[rs], q[rs], k[rs]
            diag = jnp.zeros((sub, sub), F32)
            for s in range(sub):
                decay = jnp.exp(jnp.where(row >= s, b_i - b_i[s:s + 1, :], -jnp.inf))
                a_col = jnp.sum(q_i * k_i[s:s + 1, :] * decay, axis=1, keepdims=True)
                diag = jnp.where(col == s, a_col, diag)
            o_i = jnp.dot(diag.astype(BF16), vb[rs], preferred_element_type=F32)
            if i > 0:
                prev = slice(0, sub * i)
                b_ref = block_start(b, i)
                q_hat = (q_i * jnp.exp(b_i - b_ref)).astype(BF16)
                k_hat = (k[prev] * jnp.exp(b_ref - b[prev])).astype(BF16)
                a_off = lax.dot_general(q_hat, k_hat, trans_b, preferred_element_type=F32)
                o_i = o_i + jnp.dot(a_off.astype(BF16), vb[prev], preferred_element_type=F32)
            blocks.append(o_i)
        return jnp.concatenate(blocks, axis=0)

    ends = [b_all[HG_FACT * (i + 1) - 1:HG_FACT * (i + 1), :] for i in range(c_len // HG_FACT)]
    drops = [(ends[i - 1] if i > 0 else 0.0) - ends[i] for i in range(len(ends))]
    factored_ok = jnp.max(functools.reduce(jnp.maximum, drops)) < MAX_SAFE_EXPONENT

    @pl.when(factored_ok)
    def _():
        intra_factored_all()

    @pl.when(jnp.logical_not(factored_ok))
    def _():
        for h in range(HG_HEADS):
            hs = slice(HG_DIM * h, HG_DIM * (h + 1))
            intra_ref[h] = intra_direct(b_all[:, hs], q_all[:, hs], k_all[:, hs], hi_ref[:, hs].astype(BF16))

    outs = []
    for h in range(HG_HEADS):
        hs = slice(HG_DIM * h, HG_DIM * (h + 1))
        v = hi_ref[:, hs]
        st = st_ref[h]
        inter = lax.dot_general(q_inter[:, hs].astype(BF16), st.astype(BF16), trans_b,
                                preferred_element_type=F32)
        o = inter + intra_ref[h]
        st_ref[h] = jnp.exp(b_last[:, hs]) * st + jnp.dot(v.T.astype(BF16), k_carry[:, hs].astype(BF16),
                                                         preferred_element_type=F32)
        outs.append(_rmsnorm_rows(o, out_gain[:, hs]) * gate[:, hs])
    o_ref[...] = jnp.concatenate(outs, axis=1).astype(o_ref.dtype)


def _hgrn(hg5, hconst, ltri, b, t):
    rows = HG_CHUNK * HG_STEP_CHUNKS
    ns = t // rows
    col = lambda jcol: pl.BlockSpec((rows, HG_WIDTH), lambda i, s, jcol=jcol: (i * ns + s, jcol))
    return pl.pallas_call(
        _hgrn_kernel,
        grid=(b, ns),
        in_specs=[col(jcol) for jcol in range(HG_STREAMS)] + [_resident(hconst.shape), _resident(ltri.shape)],
        out_specs=pl.BlockSpec((rows, HG_WIDTH), lambda i, s: (i * ns + s, 0)),
        out_shape=jax.ShapeDtypeStruct((b * t, HG_WIDTH), BF16),
        scratch_shapes=[pltpu.VMEM((HG_HEADS, HG_DIM, HG_DIM), F32),
                        pltpu.VMEM((HG_HEADS, HG_CHUNK, HG_DIM), F32)],
        compiler_params=_params(2),
        name="hgrn",
    )(*([hg5] * HG_STREAMS), hconst, ltri)


def _merge_kernel(x_ref, ya_ref, yh_ref, g_ref, wg_ref, wpa_ref, wph_ref, wo_ref, o_ref):
    x = x_ref[...]
    hb = _rmsnorm_rows(x, g_ref[...]).astype(BF16)
    gates = jnp.dot(hb, wg_ref[...], preferred_element_type=F32)
    pa = jnp.dot(ya_ref[...], wpa_ref[...], preferred_element_type=F32)
    ph = jnp.dot(yh_ref[...], wph_ref[...], preferred_element_type=F32)
    merged = _sigmoid(gates[:, :D_MODEL]) * pa + _sigmoid(gates[:, D_MODEL:]) * ph
    o_ref[...] = x + jnp.dot(merged.astype(BF16), wo_ref[...], preferred_element_type=F32)


def _merge(x2, ya, yh, gain, wg, wpa, wph, wo, tm):
    nt = x2.shape[0]
    row = lambda w: pl.BlockSpec((tm, w), lambda i: (i, 0))
    return pl.pallas_call(
        _merge_kernel,
        grid=(nt // tm,),
        in_specs=[row(D_MODEL), row(ATTN_WIDTH), row(HG_WIDTH), _resident(gain.shape), _resident(wg.shape),
                  _resident(wpa.shape), _resident(wph.shape), _resident(wo.shape)],
        out_specs=row(D_MODEL),
        out_shape=jax.ShapeDtypeStruct((nt, D_MODEL), F32),
        compiler_params=_params(1),
        name="merge",
    )(x2, ya, yh, gain, wg, wpa, wph, wo)


def _ffn_kernel(x_ref, g_ref, wi_ref, wo_ref, o_ref):
    x = x_ref[...]
    hb = _rmsnorm_rows(x, g_ref[...]).astype(BF16)
    gu = jnp.dot(hb, wi_ref[...], preferred_element_type=F32)
    g = gu[:, :D_FF]
    act = g * _sigmoid(g) * gu[:, D_FF:]
    o_ref[...] = x + jnp.dot(act.astype(BF16), wo_ref[...], preferred_element_type=F32)


def _ffn(x2, gain, wi, wo, tm):
    nt = x2.shape[0]
    row = pl.BlockSpec((tm, D_MODEL), lambda i: (i, 0))
    return pl.pallas_call(
        _ffn_kernel,
        grid=(nt // tm,),
        in_specs=[row, _resident(gain.shape), _resident(wi.shape), _resident(wo.shape)],
        out_specs=row,
        out_shape=jax.ShapeDtypeStruct((nt, D_MODEL), F32),
        compiler_params=_params(1),
        name="ffn",
    )(x2, gain, wi, wo)


def _split_hi_lo(a):
    hi = a.astype(BF16)
    lo = (a - hi.astype(F32)).astype(BF16)
    return hi, lo


def _layer(x2, b, t, cos1, sin1, consts, w):
    hsum, hexp, ltri_attn, ltri_hgrn = consts
    qt, k, vt4, iq3t, ik3, iwt, hg5 = _mix_in(x2, w["norm_mix"], w["wa"], *w["wi"], w["wh"], w["qn"], w["kn"],
                                              cos1, sin1, hsum, hexp, w["hconst"], tm=512)
    y_attn = _dsa(w["logit_bound"], iq3t, ik3, iwt, qt, k, vt4, ltri_attn, b, t, topk=min(MAX_TOPK, t // 4))
    y_hgrn = _hgrn(hg5, w["hconst"], ltri_hgrn, b, t)
    x2 = _merge(x2, y_attn, y_hgrn, w["norm_mix"], w["wg"], w["wpa"], w["wph"], w["wout"], tm=512)
    return _ffn(x2, w["norm_ffn"], w["wffn_in"], w["wffn_out"], tm=512)


def _constants():
    head_of_lane = np.arange(ATTN_WIDTH) // HEAD_DIM
    hsum = (head_of_lane[:, None] == np.arange(LANES)[None, :]).astype(np.float32)
    hexp = hsum.T.copy()
    r = np.arange(KEY_BLOCK)
    ltri_attn = (r[None, :] < r[:, None]).astype(np.float32)
    r = np.arange(HG_CHUNK)
    ltri_hgrn = (r[None, :] <= r[:, None]).astype(np.float32)
    return jnp.asarray(hsum, BF16), jnp.asarray(hexp, BF16), jnp.asarray(ltri_attn, BF16), jnp.asarray(ltri_hgrn, BF16)


def kernel(x, positions, w_in, w_proj_attn, w_proj_hgrn, w_out, norm_mix, norm_ffn, q_norm, k_norm, hgrn_norm,
           hgrn_lower_bound, w_ffn_in, w_ffn_out):
    b, t, d = x.shape
    depth = w_in.shape[0]
    nt = b * t

    inv = ROPE_THETA ** (-jnp.arange(0, HEAD_DIM, 2, dtype=F32) / HEAD_DIM)
    ang = positions.astype(F32)[..., None] * inv
    cos, sin = jnp.cos(ang), jnp.sin(ang)
    cos1 = jnp.concatenate([cos, cos, cos, cos], axis=-1).reshape(nt, LANES)
    sin1 = jnp.concatenate([-sin, sin, -sin, sin], axis=-1).reshape(nt, LANES)

    lb_all = jnp.cumsum(jax.nn.softmax(hgrn_lower_bound.astype(F32), axis=0), axis=0)
    lb_all = lb_all - lb_all[:1]

    consts = _constants()
    widths = (ATTN_WIDTH, ATTN_WIDTH, ATTN_WIDTH, IDX_HEADS * IDX_DIM, IDX_DIM, IDX_HEADS,
              HG_WIDTH, HG_WIDTH, HG_WIDTH, HG_WIDTH, D_MODEL, D_MODEL)
    off = np.concatenate([[0], np.cumsum(widths)])
    idx_cols = off[6] - off[3]

    x2 = x.reshape(nt, d)
    for l in range(depth):
        wl = w_in[l]
        lb = lb_all[l]
        hconst = jnp.zeros((8, HG_WIDTH), F32)
        hconst = hconst.at[0].set(jnp.log(lb)).at[1].set(jnp.log1p(-lb)).at[2].set(1.0 - lb)
        hconst = hconst.at[3].set(jnp.tile(hgrn_norm[l], HG_HEADS))
        w = {
            "norm_mix": norm_mix[l].reshape(1, d),
            "norm_ffn": norm_ffn[l].reshape(1, d),
            "wa": wl[:, off[0]:off[3]].astype(BF16),
            "wi": _split_hi_lo(jnp.pad(wl[:, off[3]:off[6]], ((0, 0), (0, 3 * LANES - idx_cols)))),
            "wh": wl[:, off[6]:off[10]].astype(BF16),
            "wg": wl[:, off[10]:off[12]].astype(BF16),
            "qn": jnp.tile(q_norm[l], ATTN_HEADS).reshape(1, ATTN_WIDTH),
            "kn": jnp.tile(k_norm[l], ATTN_HEADS).reshape(1, ATTN_WIDTH),
            "logit_bound": (HEAD_DIM ** 0.5 * LOG2_E * jnp.max(jnp.abs(q_norm[l])) * jnp.max(jnp.abs(k_norm[l]))
                            ).reshape(1).astype(F32),
            "hconst": hconst,
            "wpa": w_proj_attn[l].astype(BF16),
            "wph": w_proj_hgrn[l].astype(BF16),
            "wout": w_out[l].astype(BF16),
            "wffn_in": w_ffn_in[l].astype(BF16),
            "wffn_out": w_ffn_out[l].astype(BF16),
        }
        x2 = _layer(x2, b, t, cos1, sin1, consts, w)
    return x2.reshape(b, t, d)
```

```python
import functools

import numpy as np
import jax
import jax.numpy as jnp
from jax import lax
from jax.experimental import pallas as pl
from jax.experimental.pallas import tpu as pltpu

D_MODEL = 1024
ATTN_HEADS = 8
HEAD_DIM = 64
ATTN_WIDTH = ATTN_HEADS * HEAD_DIM
IDX_HEADS = 4
IDX_DIM = 64
MAX_TOPK = 256
HG_HEADS = 4
HG_DIM = 128
HG_WIDTH = HG_HEADS * HG_DIM
HG_CHUNK = 64
HG_STREAMS = 5
HG_STEP_CHUNKS = 4
HG_SUB = 16
HG_FACT = 32
MAX_SAFE_LOGIT = 60.0
MAX_SAFE_EXPONENT = 80.0
D_FF = 2816
ROPE_THETA = 10000.0
EPS = 1e-6
LOG2_E = 1.4426950408889634

LANES = 128
VMEM_LIMIT_BYTES = 56 * 1024 * 1024

KEY_BLOCK = 256
VT_ROWS = 80
INT_MIN = -(2 ** 31)
HALF_MIN = -(2 ** 15)
NEG_BIG = -1e30

F32 = jnp.float32
BF16 = jnp.bfloat16


def _resident(shape):
    nd = len(shape)
    return pl.BlockSpec(shape, lambda *_: (0,) * nd, pipeline_mode=pl.Buffered(1))


def _params(n_axes):
    return pltpu.CompilerParams(dimension_semantics=("arbitrary",) * n_axes,
                                vmem_limit_bytes=VMEM_LIMIT_BYTES)


def _rmsnorm_rows(x, gain):
    return x * lax.rsqrt(jnp.mean(x * x, axis=-1, keepdims=True) + EPS) * gain


def _sigmoid(x):
    return 1.0 / (1.0 + jnp.exp(-x))


def _rope(x, cos, sin_signed):
    w = x.shape[-1]
    lane = lax.broadcasted_iota(jnp.int32, x.shape, 1)
    first_half = (lane & (HEAD_DIM - 1)) < HEAD_DIM // 2
    partner = jnp.where(first_half, pltpu.roll(x, w - HEAD_DIM // 2, 1), pltpu.roll(x, HEAD_DIM // 2, 1))
    return x * cos + partner * sin_signed


def _dot_hi_lo(a, b_hi, b_lo=None):
    a_hi = a.astype(BF16)
    a_lo = (a - a_hi.astype(F32)).astype(BF16)
    out = jnp.dot(a_hi, b_hi, preferred_element_type=F32) + jnp.dot(a_lo, b_hi, preferred_element_type=F32)
    if b_lo is not None:
        out = out + jnp.dot(a_hi, b_lo, preferred_element_type=F32)
    return out


def _dot_exact_lhs(a, b):
    b1 = b.astype(BF16)
    r1 = b - b1.astype(F32)
    b2 = r1.astype(BF16)
    b3 = (r1 - b2.astype(F32)).astype(BF16)
    return (jnp.dot(a, b1, preferred_element_type=F32) + jnp.dot(a, b2, preferred_element_type=F32)
            + jnp.dot(a, b3, preferred_element_type=F32))


def _mix_in_kernel(x_ref, g_ref, wa_ref, wi_ref, wil_ref, wh_ref, qn_ref, kn_ref, cos_ref, sin_ref, hsum_ref, hexp_ref,
                   hc_ref, qt_ref, k_ref, vt_ref, iq3_ref, ik3_ref, iw_ref, hg_ref):
    h = _rmsnorm_rows(x_ref[...], g_ref[...])
    hb = h.astype(BF16)
    cos1 = cos_ref[...]
    sin1 = sin_ref[...]
    cos4 = jnp.concatenate([cos1] * 4, axis=1)
    sin4 = jnp.concatenate([sin1] * 4, axis=1)

    def head_norm(a, gain):
        ss = _dot_hi_lo(a * a, hsum_ref[...])
        r = lax.rsqrt(ss * (1.0 / HEAD_DIM) + EPS)
        return a * _dot_hi_lo(r, hexp_ref[...]) * gain

    pa = jnp.dot(hb, wa_ref[...], preferred_element_type=F32)
    q = _rope(head_norm(pa[:, :ATTN_WIDTH], qn_ref[...]), cos4, sin4) * (HEAD_DIM ** -0.5 * LOG2_E)
    k = _rope(head_norm(pa[:, ATTN_WIDTH:2 * ATTN_WIDTH], kn_ref[...]), cos4, sin4)
    qt_ref[...] = q.T.astype(BF16)
    k_ref[...] = k.astype(BF16)
    v = pa[:, 2 * ATTN_WIDTH:]
    ones_pad = jnp.where(lax.broadcasted_iota(jnp.int32, (VT_ROWS - HEAD_DIM, KEY_BLOCK), 0) == 0, 1.0, 0.0)
    for s in range(v.shape[0] // KEY_BLOCK):
        vt = v[KEY_BLOCK * s:KEY_BLOCK * (s + 1)].T
        rows = []
        for hd in range(ATTN_HEADS):
            rows += [vt[HEAD_DIM * hd:HEAD_DIM * (hd + 1)], ones_pad]
        vt_ref[s] = jnp.concatenate(rows, axis=0).astype(BF16)

    pi = _dot_hi_lo(h, wi_ref[...], wil_ref[...])
    nq = IDX_HEADS * IDX_DIM
    iq = _rope(pi[:, :nq], cos4[:, :nq], sin4[:, :nq]) * (IDX_DIM ** -0.5)
    slab = pi[:, nq:]
    ikw_t = _rope(slab, cos1, sin1).T
    iw_ref[...] = slab.T[IDX_DIM:IDX_DIM + 8] * (IDX_HEADS ** -0.5)

    def hi_lo(a):
        hi = a.astype(BF16).astype(F32)
        return hi, (a - hi).astype(BF16).astype(F32)

    k_hi, k_lo = hi_lo(ikw_t[:IDX_DIM])
    ik3_ref[...] = jnp.concatenate([k_hi, k_lo, k_hi, jnp.zeros_like(k_hi)], axis=0).T.astype(BF16)
    iq_t = iq.T
    rows = []
    for hd in range(IDX_HEADS):
        q_hi, q_lo = hi_lo(iq_t[IDX_DIM * hd:IDX_DIM * (hd + 1)])
        rows += [q_hi, q_hi, q_lo, jnp.zeros_like(q_hi)]
    iq3_ref[...] = jnp.concatenate(rows, axis=0).astype(BF16)

    ph = jnp.dot(hb, wh_ref[...], preferred_element_type=F32)
    w = HG_WIDTH
    hq, hf, hg = ph[:, :w], ph[:, w:2 * w], ph[:, 3 * w:]
    log_lb, log1m_lb, one_m_lb = hc_ref[0:1, :], hc_ref[1:2, :], hc_ref[2:3, :]
    c = log1m_lb + jnp.minimum(hf, 0.0) - jnp.log1p(jnp.exp(-jnp.abs(hf)))
    hg_ref[:, :w] = hq * _sigmoid(hq)
    hg_ref[:, w:2 * w] = jnp.maximum(log_lb, c) + jnp.log1p(jnp.exp(-jnp.abs(log_lb - c)))
    hg_ref[:, 2 * w:3 * w] = one_m_lb * _sigmoid(-hf)
    hg_ref[:, 3 * w:4 * w] = ph[:, 2 * w:3 * w]
    hg_ref[:, 4 * w:] = hg * _sigmoid(hg)


def _mix_in(x2, gain, wa, wi, wil, wh, qn, kn, cos1, sin1, hsum, hexp, hconst, tm):
    nt = x2.shape[0]
    kb = KEY_BLOCK
    row = lambda w: pl.BlockSpec((tm, w), lambda i: (i, 0))
    col = lambda w: pl.BlockSpec((w, tm), lambda i: (0, i))
    return pl.pallas_call(
        _mix_in_kernel,
        grid=(nt // tm,),
        in_specs=[row(D_MODEL), _resident(gain.shape), _resident(wa.shape), _resident(wi.shape), _resident(wil.shape),
                  _resident(wh.shape),
                  _resident(qn.shape), _resident(kn.shape), row(LANES), row(LANES),
                  _resident(hsum.shape), _resident(hexp.shape), _resident(hconst.shape)],
        out_specs=[col(ATTN_WIDTH), row(ATTN_WIDTH),
                   pl.BlockSpec((tm // kb, ATTN_HEADS * VT_ROWS, kb), lambda i: (i, 0, 0)),
                   col(IDX_HEADS * 4 * IDX_DIM), row(4 * IDX_DIM), col(8), row(HG_STREAMS * HG_WIDTH)],
        out_shape=[jax.ShapeDtypeStruct((ATTN_WIDTH, nt), BF16), jax.ShapeDtypeStruct((nt, ATTN_WIDTH), BF16),
                   jax.ShapeDtypeStruct((nt // kb, ATTN_HEADS * VT_ROWS, kb), BF16),
                   jax.ShapeDtypeStruct((IDX_HEADS * 4 * IDX_DIM, nt), BF16),
                   jax.ShapeDtypeStruct((nt, 4 * IDX_DIM), BF16), jax.ShapeDtypeStruct((8, nt), F32),
                   jax.ShapeDtypeStruct((nt, HG_STREAMS * HG_WIDTH), F32)],
        compiler_params=_params(1),
        name="mix_in",
    )(x2, gain, wa, wi, wil, wh, qn, kn, cos1, sin1, hsum, hexp, hconst)


def _dsa_kernel(bound_ref, iq3_ref, iq3_next_ref, ik3_ref, iw_ref, iw_next_ref, qt_ref, k_ref, vt_ref, ltri_ref,
                o_ref, hi2_ref, lo2_ref, qz_ref, lg_ref, m_ref, alpha_ref, acc_ref, *, topk):
    kb = KEY_BLOCK
    j = pl.program_id(1)
    nk = j + 1
    int_min = jnp.int32(INT_MIN)
    slot = j % 2
    hi_ref, lo_ref = hi2_ref.at[slot], lo2_ref.at[slot]
    hi_next_ref, lo_next_ref = hi2_ref.at[1 - slot], lo2_ref.at[1 - slot]

    zeros = jnp.zeros((HEAD_DIM, kb), BF16)
    for h in range(ATTN_HEADS):
        qh = qt_ref[HEAD_DIM * h:HEAD_DIM * (h + 1), :]
        qz_ref[h] = jnp.concatenate([qh, zeros] if h % 2 == 0 else [zeros, qh], axis=0)

    def score_head(c, h, next_block):
        q3, w = (iq3_next_ref, iw_next_ref) if next_block else (iq3_ref, iw_ref)
        rel = jnp.dot(ik3_ref[c], q3[4 * IDX_DIM * h:4 * IDX_DIM * (h + 1), :], preferred_element_type=F32)
        return w[h:h + 1, :] * jnp.maximum(rel, 0.0)

    def store_keys(c, score, next_block, diagonal):
        bits = pltpu.bitcast(score, jnp.int32)
        key = jnp.where(bits < 0, int_min - bits, bits)
        if diagonal:
            kpos = lax.broadcasted_iota(jnp.int32, (kb, kb), 0)
            qpos = lax.broadcasted_iota(jnp.int32, (kb, kb), 1)
            key = jnp.where(kpos <= qpos, key, int_min)
        hi_dst, lo_dst = (hi_next_ref, lo_next_ref) if next_block else (hi_ref, lo_ref)
        hi_dst[c] = lax.shift_right_arithmetic(key, 16).astype(jnp.int16)
        lo_dst[c] = (key ^ 0x8000).astype(jnp.int16)

    store_keys(j, sum(score_head(j, h, False) for h in range(IDX_HEADS)), False, True)

    one_b, zero_b = jnp.ones((), BF16), jnp.zeros((), BF16)
    one_i, zero_i = jnp.ones((), jnp.int16), jnp.zeros((), jnp.int16)
    rows16 = 16

    hi_ref[nk] = jnp.full((kb, kb), HALF_MIN, jnp.int16)
    lo_ref[nk] = jnp.full((kb, kb), HALF_MIN, jnp.int16)

    def count(hit_fn):
        def fold(hit):
            parts = [hit[rows16 * r:rows16 * (r + 1)] for r in range(kb // rows16)]
            while len(parts) > 1:
                parts = [a + b for a, b in zip(parts[::2], parts[1::2])]
            return parts[0]

        def body(i, cnt):
            return cnt + (fold(hit_fn(2 * i)) + fold(hit_fn(2 * i + 1)))

        cnt = lax.fori_loop(0, (nk + 1) // 2, body, jnp.zeros((rows16, kb), jnp.int16))
        return jnp.sum(cnt.astype(jnp.int32), axis=0, keepdims=True).astype(F32)

    def bisect_half(ref, target):
        def bit_step(i, carry):
            thr_u, n_above = carry
            cand_u = thr_u | lax.shift_left(jnp.int32(1), 15 - i)
            cand = (cand_u + HALF_MIN).astype(jnp.int16)
            cnt = count(lambda c: jnp.where(ref[c] >= cand, one_i, zero_i))
            accepted = cnt >= target
            return jnp.where(accepted, cand_u, thr_u), jnp.where(accepted, n_above, cnt)
        thr_u, n_above = lax.fori_loop(0, 16, bit_step, (jnp.zeros((1, kb), jnp.int32), jnp.zeros((1, kb), F32)))
        return thr_u + HALF_MIN, n_above

    thr_hi32, n_gt_hi = bisect_half(hi_ref, float(topk))
    thr_hi = thr_hi32.astype(jnp.int16)

    def keep_matching_low(c, carry):
        lo_ref[c] = jnp.where(hi_ref[c] == thr_hi, lo_ref[c], jnp.int16(HALF_MIN))
        return carry

    lax.fori_loop(0, nk, keep_matching_low, 0)
    thr_lo32, n_gt_lo = bisect_half(lo_ref, topk - n_gt_hi)
    thr_lo = thr_lo32.astype(jnp.int16)
    n_gt = n_gt_hi + n_gt_lo
    is_marker = (thr_hi32 == HALF_MIN) & (thr_lo32 == HALF_MIN)
    need = jnp.where(is_marker, 0.0, topk - n_gt)

    def bias_chunk(c, ties_before):
        hi, lo = hi_ref[c], lo_ref[c]
        above = jnp.where(hi > thr_hi, one_b, jnp.where(lo > thr_lo, one_b, zero_b)).astype(F32)
        eq_b = jnp.where(hi == thr_hi, jnp.where(lo == thr_lo, one_b, zero_b), zero_b)
        eq = eq_b.astype(F32)
        rank = jnp.dot(ltri_ref[...], eq_b, preferred_element_type=F32) + ties_before
        chosen = above + jnp.where(rank < need, eq, 0.0)
        bias = jnp.where(chosen > 0.5, 0.0, NEG_BIG).astype(BF16)
        hi_ref[c] = pltpu.bitcast(bias, jnp.int16)
        return ties_before + jnp.sum(eq, axis=0, keepdims=True)


    def attend(bounded):
        acc_ref[...] = jnp.zeros(acc_ref.shape, F32)
        if not bounded:
            m_ref[...] = jnp.full(m_ref.shape, NEG_BIG, F32)

        def logits_head(c, h, bias):
            kk = k_ref[c, :, LANES * (h // 2):LANES * (h // 2 + 1)]
            lgb = jnp.dot(kk, qz_ref[h], preferred_element_type=F32) + bias
            lg_ref[h] = lgb
            if not bounded:
                m_old = m_ref[h]
                m_new = jnp.maximum(m_old, jnp.max(lgb, axis=0, keepdims=True))
                alpha_ref[h] = jnp.exp2(m_old - m_new)
                m_ref[h] = m_new

        def softmax_pv_head(c, h):
            p = jnp.exp2(lg_ref[h] if bounded else lg_ref[h] - m_ref[h])
            pv = jnp.dot(vt_ref[c, VT_ROWS * h:VT_ROWS * (h + 1), :], p.astype(BF16),
                         preferred_element_type=F32)
            acc_ref[h] = (acc_ref[h] if bounded else alpha_ref[h] * acc_ref[h]) + pv

        def chunk_bias(c):
            return pltpu.bitcast(hi_ref[c], BF16).astype(F32)

        def next_block_scores(c, h, partial):
            if h % (ATTN_HEADS // IDX_HEADS) != ATTN_HEADS // IDX_HEADS - 1:
                return partial
            term = score_head(c, h // (ATTN_HEADS // IDX_HEADS), True)
            return term if partial is None else partial + term

        ties = bias_chunk(0, jnp.zeros((1, kb), F32))
        bias0 = chunk_bias(0)
        partial = None
        for h in range(ATTN_HEADS):
            logits_head(0, h, bias0)
            partial = next_block_scores(0, h, partial)
        store_keys(0, partial, True, False)
        ties = bias_chunk(1, ties)

        def pipelined(c, ties_before):
            bias = chunk_bias(c)
            ties_after = bias_chunk(c + 1, ties_before)
            partial = None
            for h in range(ATTN_HEADS):
                softmax_pv_head(c - 1, h)
                logits_head(c, h, bias)
                partial = next_block_scores(c, h, partial)
            store_keys(c, partial, True, False)
            return ties_after

        lax.fori_loop(1, nk, pipelined, ties)
        for h in range(ATTN_HEADS):
            softmax_pv_head(nk - 1, h)

    logits_bounded = bound_ref[0] < MAX_SAFE_LOGIT

    @pl.when(logits_bounded)
    def _():
        attend(True)

    @pl.when(jnp.logical_not(logits_bounded))
    def _():
        attend(False)

    out_t = [acc_ref[h][:HEAD_DIM] / acc_ref[h][HEAD_DIM:HEAD_DIM + 1] for h in range(ATTN_HEADS)]
    o_ref[...] = jnp.concatenate(out_t, axis=0).T.astype(o_ref.dtype)


def _dsa(logit_bound, iq3t, ik3, iwt, qt, k, vt4, ltri, b, t, topk):
    kb = KEY_BLOCK
    nkc = t // kb
    chunked = lambda a: a.reshape(b * nkc, kb, a.shape[-1])
    per_batch = lambda shape: pl.BlockSpec((nkc,) + shape, lambda i, j: (i, 0, 0), pipeline_mode=pl.Buffered(1))
    q_cols = lambda rows: pl.BlockSpec((rows, kb), lambda i, j: (0, i * nkc + j))
    next_q_cols = lambda rows: pl.BlockSpec((rows, kb), lambda i, j: (0, i * nkc + jnp.minimum(j + 1, nkc - 1)))
    return pl.pallas_call(
        functools.partial(_dsa_kernel, topk=topk),
        grid=(b, nkc),
        in_specs=[
            pl.BlockSpec(memory_space=pltpu.SMEM),
            q_cols(IDX_HEADS * 4 * IDX_DIM),
            next_q_cols(IDX_HEADS * 4 * IDX_DIM),
            per_batch((kb, 4 * IDX_DIM)),
            q_cols(8),
            next_q_cols(8),
            q_cols(ATTN_WIDTH),
            per_batch((kb, ATTN_WIDTH)),
            per_batch((ATTN_HEADS * VT_ROWS, kb)),
            _resident(ltri.shape),
        ],
        out_specs=pl.BlockSpec((kb, ATTN_WIDTH), lambda i, j: (i * nkc + j, 0)),
        out_shape=jax.ShapeDtypeStruct((b * t, ATTN_WIDTH), BF16),
        scratch_shapes=[
            pltpu.VMEM((2, nkc + 1, kb, kb), jnp.int16),
            pltpu.VMEM((2, nkc + 1, kb, kb), jnp.int16),
            pltpu.VMEM((ATTN_HEADS, LANES, kb), BF16),
            pltpu.VMEM((ATTN_HEADS, kb, kb), F32),
            pltpu.VMEM((ATTN_HEADS, 1, kb), F32),
            pltpu.VMEM((ATTN_HEADS, 1, kb), F32),
            pltpu.VMEM((ATTN_HEADS, VT_ROWS, kb), F32),
        ],
        compiler_params=_params(2),
        name="dsa",
    )(logit_bound, iq3t, iq3t, chunked(ik3), iwt, iwt, qt, chunked(k), vt4, ltri)


def _hgrn_kernel(q_ref, lf_ref, k_ref, v_ref, gate_ref, hc_ref, lt_ref, o_ref, st_ref):
    @pl.when(pl.program_id(0) == 0)
    def _():
        st_ref[...] = jnp.zeros(st_ref.shape, F32)

    units = [(bi, ci) for ci in range(HG_STEP_CHUNKS) for bi in range(q_ref.shape[0])]
    rows = lambda ci: slice(HG_CHUNK * ci, HG_CHUNK * (ci + 1))
    cums = {(bi, ci): _dot_exact_lhs(lt_ref[...], lf_ref[bi, rows(ci), :]) for bi, ci in units}

    drops = []
    for b_all in cums.values():
        ends = [b_all[HG_FACT * (i + 1) - 1:HG_FACT * (i + 1), :] for i in range(HG_CHUNK // HG_FACT)]
        drops += [(ends[i - 1] if i > 0 else 0.0) - ends[i] for i in range(len(ends))]
    factored_ok = jnp.max(functools.reduce(jnp.maximum, drops)) < MAX_SAFE_EXPONENT

    def run(factored):
        for bi, ci in units:
            _hgrn_chunk(q_ref[bi, rows(ci), :], cums[(bi, ci)], k_ref[bi, rows(ci), :], v_ref.at[bi, rows(ci), :],
                        gate_ref[bi, rows(ci), :], hc_ref[3:4, :], o_ref.at[bi, rows(ci), :], st_ref.at[bi], factored)

    @pl.when(factored_ok)
    def _():
        run(True)

    @pl.when(jnp.logical_not(factored_ok))
    def _():
        run(False)


def _hgrn_chunk(q_all, b_all, k_all, hi_ref, gate, out_gain, o_ref, st_ref, factored):
    c_len, sub = HG_CHUNK, HG_SUB
    b_last = b_all[c_len - 1:c_len, :]
    q_inter = q_all * jnp.exp(b_all)
    k_carry = k_all * jnp.exp(b_last - b_all)

    n_sub = c_len // sub
    trans_b = (((1,), (1,)), ((), ()))

    def block_start(b, i):
        return b[sub * i - 1:sub * i, :] if i > 0 else jnp.zeros_like(b[0:1, :])

    def intra_factored_all():
        fs = HG_FACT
        heads = [slice(HG_DIM * h, HG_DIM * (h + 1)) for h in range(HG_HEADS)]
        scores, intra = [], []
        for hs in heads:
            b, q, k = b_all[:, hs], q_all[:, hs], k_all[:, hs]
            for i in range(c_len // fs):
                rs, upto = slice(fs * i, fs * (i + 1)), slice(0, fs * (i + 1))
                b_ref = b[fs * i - 1:fs * i, :] if i > 0 else jnp.zeros_like(b[0:1, :])
                q_hat = (q[rs] * jnp.exp(b[rs] - b_ref)).astype(BF16)
                k_hat = (k[upto] * jnp.exp(b_ref - b[upto])).astype(BF16)
                scores.append(lax.dot_general(q_hat, k_hat, trans_b, preferred_element_type=F32))
        for h, hs in enumerate(heads):
            vb = hi_ref[:, hs].astype(BF16)
            blocks = []
            for i in range(c_len // fs):
                a = scores[h * (c_len // fs) + i]
                t_idx = fs * i + lax.broadcasted_iota(jnp.int32, a.shape, 0)
                s_idx = lax.broadcasted_iota(jnp.int32, a.shape, 1)
                a = jnp.where(s_idx <= t_idx, a, 0.0).astype(BF16)
                blocks.append(jnp.dot(a, vb[0:fs * (i + 1)], preferred_element_type=F32))
            intra.append(jnp.concatenate(blocks, axis=0))
        return intra

    def intra_direct(b, q, k, vb):
        row = lax.broadcasted_iota(jnp.int32, (sub, 1), 0)
        col = lax.broadcasted_iota(jnp.int32, (sub, sub), 1)
        blocks = []
        for i in range(n_sub):
            rs = slice(sub * i, sub * (i + 1))
            b_i, q_i, k_i = b[rs], q[rs], k[rs]
            diag = jnp.zeros((sub, sub), F32)
            for s in range(sub):
                decay = jnp.exp(jnp.where(row >= s, b_i - b_i[s:s + 1, :], -jnp.inf))
                a_col = jnp.sum(q_i * k_i[s:s + 1, :] * decay, axis=1, keepdims=True)
                diag = jnp.where(col == s, a_col, diag)
            o_i = jnp.dot(diag.astype(BF16), vb[rs], preferred_element_type=F32)
            if i > 0:
                prev = slice(0, sub * i)
                b_ref = block_start(b, i)
                q_hat = (q_i * jnp.exp(b_i - b_ref)).astype(BF16)
                k_hat = (k[prev] * jnp.exp(b_ref - b[prev])).astype(BF16)
                a_off = lax.dot_general(q_hat, k_hat, trans_b, preferred_element_type=F32)
                o_i = o_i + jnp.dot(a_off.astype(BF16), vb[prev], preferred_element_type=F32)
            blocks.append(o_i)
        return jnp.concatenate(blocks, axis=0)

    if factored:
        intra = intra_factored_all()
    else:
        intra = [intra_direct(b_all[:, hs], q_all[:, hs], k_all[:, hs], hi_ref[:, hs].astype(BF16))
                 for hs in (slice(HG_DIM * h, HG_DIM * (h + 1)) for h in range(HG_HEADS))]

    outs = []
    for h in range(HG_HEADS):
        hs = slice(HG_DIM * h, HG_DIM * (h + 1))
        v = hi_ref[:, hs]
        st = st_ref[h]
        inter = lax.dot_general(q_inter[:, hs].astype(BF16), st.astype(BF16), trans_b,
                                preferred_element_type=F32)
        o = inter + intra[h]
        st_ref[h] = jnp.exp(b_last[:, hs]) * st + jnp.dot(v.T.astype(BF16), k_carry[:, hs].astype(BF16),
                                                         preferred_element_type=F32)
        outs.append(_rmsnorm_rows(o, out_gain[:, hs]) * gate[:, hs])
    o_ref[...] = jnp.concatenate(outs, axis=1).astype(o_ref.dtype)


def _hgrn(hg5, hconst, ltri, b, t):
    rows = HG_CHUNK * HG_STEP_CHUNKS
    streams = hg5.reshape(b, t, HG_STREAMS * HG_WIDTH)
    col = lambda jcol: pl.BlockSpec((b, rows, HG_WIDTH), lambda s, jcol=jcol: (0, s, jcol))
    return pl.pallas_call(
        _hgrn_kernel,
        grid=(t // rows,),
        in_specs=[col(jcol) for jcol in range(HG_STREAMS)] + [_resident(hconst.shape), _resident(ltri.shape)],
        out_specs=pl.BlockSpec((b, rows, HG_WIDTH), lambda s: (0, s, 0)),
        out_shape=jax.ShapeDtypeStruct((b, t, HG_WIDTH), BF16),
        scratch_shapes=[pltpu.VMEM((b, HG_HEADS, HG_DIM, HG_DIM), F32)],
        compiler_params=_params(1),
        name="hgrn",
    )(*([streams] * HG_STREAMS), hconst, ltri).reshape(b * t, HG_WIDTH)


def _merge_kernel(x_ref, ya_ref, yh_ref, g_ref, wg_ref, wpa_ref, wph_ref, wo_ref, o_ref):
    x = x_ref[...]
    hb = _rmsnorm_rows(x, g_ref[...]).astype(BF16)
    gates = jnp.dot(hb, wg_ref[...], preferred_element_type=F32)
    pa = jnp.dot(ya_ref[...], wpa_ref[...], preferred_element_type=F32)
    ph = jnp.dot(yh_ref[...], wph_ref[...], preferred_element_type=F32)
    merged = _sigmoid(gates[:, :D_MODEL]) * pa + _sigmoid(gates[:, D_MODEL:]) * ph
    o_ref[...] = x + jnp.dot(merged.astype(BF16), wo_ref[...], preferred_element_type=F32)


def _merge(x2, ya, yh, gain, wg, wpa, wph, wo, tm):
    nt = x2.shape[0]
    row = lambda w: pl.BlockSpec((tm, w), lambda i: (i, 0))
    return pl.pallas_call(
        _merge_kernel,
        grid=(nt // tm,),
        in_specs=[row(D_MODEL), row(ATTN_WIDTH), row(HG_WIDTH), _resident(gain.shape), _resident(wg.shape),
                  _resident(wpa.shape), _resident(wph.shape), _resident(wo.shape)],
        out_specs=row(D_MODEL),
        out_shape=jax.ShapeDtypeStruct((nt, D_MODEL), F32),
        compiler_params=_params(1),
        name="merge",
    )(x2, ya, yh, gain, wg, wpa, wph, wo)


def _ffn_kernel(x_ref, g_ref, wi_ref, wo_ref, o_ref):
    x = x_ref[...]
    hb = _rmsnorm_rows(x, g_ref[...]).astype(BF16)
    gu = jnp.dot(hb, wi_ref[...], preferred_element_type=F32)
    g = gu[:, :D_FF]
    act = g * _sigmoid(g) * gu[:, D_FF:]
    o_ref[...] = x + jnp.dot(act.astype(BF16), wo_ref[...], preferred_element_type=F32)


def _ffn(x2, gain, wi, wo, tm):
    nt = x2.shape[0]
    row = pl.BlockSpec((tm, D_MODEL), lambda i: (i, 0))
    return pl.pallas_call(
        _ffn_kernel,
        grid=(nt // tm,),
        in_specs=[row, _resident(gain.shape), _resident(wi.shape), _resident(wo.shape)],
        out_specs=row,
        out_shape=jax.ShapeDtypeStruct((nt, D_MODEL), F32),
        compiler_params=_params(1),
        name="ffn",
    )(x2, gain, wi, wo)


def _split_hi_lo(a):
    hi = a.astype(BF16)
    lo = (a - hi.astype(F32)).astype(BF16)
    return hi, lo


def _layer(x2, b, t, cos1, sin1, consts, w):
    hsum, hexp, ltri_attn, ltri_hgrn = consts
    qt, k, vt4, iq3t, ik3, iwt, hg5 = _mix_in(x2, w["norm_mix"], w["wa"], *w["wi"], w["wh"], w["qn"], w["kn"],
                                              cos1, sin1, hsum, hexp, w["hconst"], tm=512)
    y_attn = _dsa(w["logit_bound"], iq3t, ik3, iwt, qt, k, vt4, ltri_attn, b, t, topk=min(MAX_TOPK, t // 4))
    y_hgrn = _hgrn(hg5, w["hconst"], ltri_hgrn, b, t)
    x2 = _merge(x2, y_attn, y_hgrn, w["norm_mix"], w["wg"], w["wpa"], w["wph"], w["wout"], tm=512)
    return _ffn(x2, w["norm_ffn"], w["wffn_in"], w["wffn_out"], tm=512)


def _constants():
    head_of_lane = np.arange(ATTN_WIDTH) // HEAD_DIM
    hsum = (head_of_lane[:, None] == np.arange(LANES)[None, :]).astype(np.float32)
    hexp = hsum.T.copy()
    r = np.arange(KEY_BLOCK)
    ltri_attn = (r[None, :] < r[:, None]).astype(np.float32)
    r = np.arange(HG_CHUNK)
    ltri_hgrn = (r[None, :] <= r[:, None]).astype(np.float32)
    return jnp.asarray(hsum, BF16), jnp.asarray(hexp, BF16), jnp.asarray(ltri_attn, BF16), jnp.asarray(ltri_hgrn, BF16)


def kernel(x, positions, w_in, w_proj_attn, w_proj_hgrn, w_out, norm_mix, norm_ffn, q_norm, k_norm, hgrn_norm,
           hgrn_lower_bound, w_ffn_in, w_ffn_out):
    b, t, d = x.shape
    depth = w_in.shape[0]
    nt = b * t

    inv = ROPE_THETA ** (-jnp.arange(0, HEAD_DIM, 2, dtype=F32) / HEAD_DIM)
    ang = positions.astype(F32)[..., None] * inv
    cos, sin = jnp.cos(ang), jnp.sin(ang)
    cos1 = jnp.concatenate([cos, cos, cos, cos], axis=-1).reshape(nt, LANES)
    sin1 = jnp.concatenate([-sin, sin, -sin, sin], axis=-1).reshape(nt, LANES)

    lb_all = jnp.cumsum(jax.nn.softmax(hgrn_lower_bound.astype(F32), axis=0), axis=0)
    lb_all = lb_all - lb_all[:1]

    consts = _constants()
    widths = (ATTN_WIDTH, ATTN_WIDTH, ATTN_WIDTH, IDX_HEADS * IDX_DIM, IDX_DIM, IDX_HEADS,
              HG_WIDTH, HG_WIDTH, HG_WIDTH, HG_WIDTH, D_MODEL, D_MODEL)
    off = np.concatenate([[0], np.cumsum(widths)])
    idx_cols = off[6] - off[3]

    x2 = x.reshape(nt, d)
    for l in range(depth):
        wl = w_in[l]
        lb = lb_all[l]
        hconst = jnp.zeros((8, HG_WIDTH), F32)
        hconst = hconst.at[0].set(jnp.log(lb)).at[1].set(jnp.log1p(-lb)).at[2].set(1.0 - lb)
        hconst = hconst.at[3].set(jnp.tile(hgrn_norm[l], HG_HEADS))
        w = {
            "norm_mix": norm_mix[l].reshape(1, d),
            "norm_ffn": norm_ffn[l].reshape(1, d),
            "wa": wl[:, off[0]:off[3]].astype(BF16),
            "wi": _split_hi_lo(jnp.pad(wl[:, off[3]:off[6]], ((0, 0), (0, 3 * LANES - idx_cols)))),
            "wh": wl[:, off[6]:off[10]].astype(BF16),
            "wg": wl[:, off[10]:off[12]].astype(BF16),
            "qn": jnp.tile(q_norm[l], ATTN_HEADS).reshape(1, ATTN_WIDTH),
            "kn": jnp.tile(k_norm[l], ATTN_HEADS).reshape(1, ATTN_WIDTH),
            "logit_bound": (HEAD_DIM ** 0.5 * LOG2_E * jnp.max(jnp.abs(q_norm[l])) * jnp.max(jnp.abs(k_norm[l]))
                            ).reshape(1).astype(F32),
            "hconst": hconst,
            "wpa": w_proj_attn[l].astype(BF16),
            "wph": w_proj_hgrn[l].astype(BF16),
            "wout": w_out[l].astype(BF16),
            "wffn_in": w_ffn_in[l].astype(BF16),
            "wffn_out": w_ffn_out[l].astype(BF16),
        }
        x2 = _layer(x2, b, t, cos1, sin1, consts, w)
    return x2.reshape(b, t, d)
```

```python
import functools

import numpy as np
import jax
import jax.numpy as jnp
from jax import lax
from jax.experimental import pallas as pl
from jax.experimental.pallas import tpu as pltpu

D_MODEL = 1024
ATTN_HEADS = 8
HEAD_DIM = 64
ATTN_WIDTH = ATTN_HEADS * HEAD_DIM
IDX_HEADS = 4
IDX_DIM = 64
MAX_TOPK = 256
HG_HEADS = 4
HG_DIM = 128
HG_WIDTH = HG_HEADS * HG_DIM
HG_CHUNK = 64
HG_STREAMS = 5
HG_STEP_CHUNKS = 4
HG_SUB = 16
HG_FACT = 32
MAX_SAFE_LOGIT = 60.0
MAX_SAFE_EXPONENT = 80.0
D_FF = 2816
ROPE_THETA = 10000.0
EPS = 1e-6
LOG2_E = 1.4426950408889634

LANES = 128
SUBLANES = 8
ROW_TILE = 512
VMEM_LIMIT_BYTES = 56 * 1024 * 1024

KEY_BLOCK = 256
VT_ROWS = 80
INT_MIN = -(2 ** 31)
HALF_MIN = -(2 ** 15)
NEG_BIG = -1e30

F32 = jnp.float32
BF16 = jnp.bfloat16


def _resident(shape):
    nd = len(shape)
    return pl.BlockSpec(shape, lambda *_: (0,) * nd, pipeline_mode=pl.Buffered(1))


def _params(n_axes):
    return pltpu.CompilerParams(dimension_semantics=("arbitrary",) * n_axes,
                                vmem_limit_bytes=VMEM_LIMIT_BYTES)


def _rmsnorm_rows(x, gain):
    return x * lax.rsqrt(jnp.mean(x * x, axis=-1, keepdims=True) + EPS) * gain


def _sigmoid(x):
    return 1.0 / (1.0 + jnp.exp(-x))


def _rope(x, cos, sin_signed):
    w = x.shape[-1]
    lane = lax.broadcasted_iota(jnp.int32, x.shape, 1)
    first_half = (lane & (HEAD_DIM - 1)) < HEAD_DIM // 2
    partner = jnp.where(first_half, pltpu.roll(x, w - HEAD_DIM // 2, 1), pltpu.roll(x, HEAD_DIM // 2, 1))
    return x * cos + partner * sin_signed


def _dot_hi_lo(a, b_hi, b_lo=None):
    a_hi = a.astype(BF16)
    a_lo = (a - a_hi.astype(F32)).astype(BF16)
    out = jnp.dot(a_hi, b_hi, preferred_element_type=F32) + jnp.dot(a_lo, b_hi, preferred_element_type=F32)
    if b_lo is not None:
        out = out + jnp.dot(a_hi, b_lo, preferred_element_type=F32)
    return out


def _dot_exact_lhs(a, b):
    b1 = b.astype(BF16)
    r1 = b - b1.astype(F32)
    b2 = r1.astype(BF16)
    b3 = (r1 - b2.astype(F32)).astype(BF16)
    return (jnp.dot(a, b1, preferred_element_type=F32) + jnp.dot(a, b2, preferred_element_type=F32)
            + jnp.dot(a, b3, preferred_element_type=F32))


def _mix_in_kernel(x_ref, g_ref, wa_ref, wi_ref, wil_ref, wh_ref, qn_ref, kn_ref, cos_ref, sin_ref, hsum_ref, hexp_ref,
                   hc_ref, qt_ref, k_ref, vt_ref, iq3_ref, ik3_ref, iw_ref, hg_ref):
    h = _rmsnorm_rows(x_ref[...], g_ref[...])
    hb = h.astype(BF16)
    cos1 = cos_ref[...]
    sin1 = sin_ref[...]
    cos4 = jnp.concatenate([cos1] * 4, axis=1)
    sin4 = jnp.concatenate([sin1] * 4, axis=1)

    def head_norm(a, gain):
        ss = _dot_hi_lo(a * a, hsum_ref[...])
        r = lax.rsqrt(ss * (1.0 / HEAD_DIM) + EPS)
        return a * _dot_hi_lo(r, hexp_ref[...]) * gain

    pa = jnp.dot(hb, wa_ref[...], preferred_element_type=F32)
    q = _rope(head_norm(pa[:, :ATTN_WIDTH], qn_ref[...]), cos4, sin4) * (HEAD_DIM ** -0.5 * LOG2_E)
    k = _rope(head_norm(pa[:, ATTN_WIDTH:2 * ATTN_WIDTH], kn_ref[...]), cos4, sin4)
    qt_ref[...] = q.T.astype(BF16)
    k_ref[...] = k.astype(BF16)
    v = pa[:, 2 * ATTN_WIDTH:]
    ones_pad = jnp.where(lax.broadcasted_iota(jnp.int32, (VT_ROWS - HEAD_DIM, KEY_BLOCK), 0) == 0, 1.0, 0.0)
    for s in range(v.shape[0] // KEY_BLOCK):
        vt = v[KEY_BLOCK * s:KEY_BLOCK * (s + 1)].T
        rows = []
        for hd in range(ATTN_HEADS):
            rows += [vt[HEAD_DIM * hd:HEAD_DIM * (hd + 1)], ones_pad]
        vt_ref[s] = jnp.concatenate(rows, axis=0).astype(BF16)

    pi = _dot_hi_lo(h, wi_ref[...], wil_ref[...])
    nq = IDX_HEADS * IDX_DIM
    iq = _rope(pi[:, :nq], cos4[:, :nq], sin4[:, :nq]) * (IDX_DIM ** -0.5)
    slab = pi[:, nq:]
    ikw_t = _rope(slab, cos1, sin1).T
    iw_ref[...] = slab.T[IDX_DIM:IDX_DIM + SUBLANES] * (IDX_HEADS ** -0.5)

    def hi_lo(a):
        hi = a.astype(BF16).astype(F32)
        return hi, (a - hi).astype(BF16).astype(F32)

    k_hi, k_lo = hi_lo(ikw_t[:IDX_DIM])
    ik3_ref[...] = jnp.concatenate([k_hi, k_lo, k_hi, jnp.zeros_like(k_hi)], axis=0).T.astype(BF16)
    iq_t = iq.T
    rows = []
    for hd in range(IDX_HEADS):
        q_hi, q_lo = hi_lo(iq_t[IDX_DIM * hd:IDX_DIM * (hd + 1)])
        rows += [q_hi, q_hi, q_lo, jnp.zeros_like(q_hi)]
    iq3_ref[...] = jnp.concatenate(rows, axis=0).astype(BF16)

    ph = jnp.dot(hb, wh_ref[...], preferred_element_type=F32)
    w = HG_WIDTH
    hq, hf, hg = ph[:, :w], ph[:, w:2 * w], ph[:, 3 * w:]
    log_lb, log1m_lb, one_m_lb = hc_ref[0:1, :], hc_ref[1:2, :], hc_ref[2:3, :]
    c = log1m_lb + jnp.minimum(hf, 0.0) - jnp.log1p(jnp.exp(-jnp.abs(hf)))
    hg_ref[:, :w] = hq * _sigmoid(hq)
    hg_ref[:, w:2 * w] = jnp.maximum(log_lb, c) + jnp.log1p(jnp.exp(-jnp.abs(log_lb - c)))
    hg_ref[:, 2 * w:3 * w] = one_m_lb * _sigmoid(-hf)
    hg_ref[:, 3 * w:4 * w] = ph[:, 2 * w:3 * w]
    hg_ref[:, 4 * w:] = hg * _sigmoid(hg)


def _mix_in(x2, gain, wa, wi, wil, wh, qn, kn, cos1, sin1, hsum, hexp, hconst, tm):
    nt = x2.shape[0]
    kb = KEY_BLOCK
    row = lambda w: pl.BlockSpec((tm, w), lambda i: (i, 0))
    col = lambda w: pl.BlockSpec((w, tm), lambda i: (0, i))
    return pl.pallas_call(
        _mix_in_kernel,
        grid=(nt // tm,),
        in_specs=[row(D_MODEL), _resident(gain.shape), _resident(wa.shape), _resident(wi.shape), _resident(wil.shape),
                  _resident(wh.shape),
                  _resident(qn.shape), _resident(kn.shape), row(LANES), row(LANES),
                  _resident(hsum.shape), _resident(hexp.shape), _resident(hconst.shape)],
        out_specs=[col(ATTN_WIDTH), row(ATTN_WIDTH),
                   pl.BlockSpec((tm // kb, ATTN_HEADS * VT_ROWS, kb), lambda i: (i, 0, 0)),
                   col(IDX_HEADS * 4 * IDX_DIM), row(4 * IDX_DIM), col(SUBLANES), row(HG_STREAMS * HG_WIDTH)],
        out_shape=[jax.ShapeDtypeStruct((ATTN_WIDTH, nt), BF16), jax.ShapeDtypeStruct((nt, ATTN_WIDTH), BF16),
                   jax.ShapeDtypeStruct((nt // kb, ATTN_HEADS * VT_ROWS, kb), BF16),
                   jax.ShapeDtypeStruct((IDX_HEADS * 4 * IDX_DIM, nt), BF16),
                   jax.ShapeDtypeStruct((nt, 4 * IDX_DIM), BF16), jax.ShapeDtypeStruct((SUBLANES, nt), F32),
                   jax.ShapeDtypeStruct((nt, HG_STREAMS * HG_WIDTH), F32)],
        compiler_params=_params(1),
        name="mix_in",
    )(x2, gain, wa, wi, wil, wh, qn, kn, cos1, sin1, hsum, hexp, hconst)


def _dsa_kernel(bound_ref, iq3_ref, iq3_next_ref, ik3_ref, iw_ref, iw_next_ref, qt_ref, k_ref, vt_ref, ltri_ref,
                o_ref, hi2_ref, lo2_ref, qz_ref, lg_ref, m_ref, alpha_ref, acc_ref, *, topk):
    kb = KEY_BLOCK
    j = pl.program_id(1)
    nk = j + 1
    int_min = jnp.int32(INT_MIN)
    slot = j % 2
    hi_ref, lo_ref = hi2_ref.at[slot], lo2_ref.at[slot]
    hi_next_ref, lo_next_ref = hi2_ref.at[1 - slot], lo2_ref.at[1 - slot]

    zeros = jnp.zeros((HEAD_DIM, kb), BF16)
    for h in range(ATTN_HEADS):
        qh = qt_ref[HEAD_DIM * h:HEAD_DIM * (h + 1), :]
        qz_ref[h] = jnp.concatenate([qh, zeros] if h % 2 == 0 else [zeros, qh], axis=0)

    def score_head(c, h, next_block):
        q3, w = (iq3_next_ref, iw_next_ref) if next_block else (iq3_ref, iw_ref)
        rel = jnp.dot(ik3_ref[c], q3[4 * IDX_DIM * h:4 * IDX_DIM * (h + 1), :], preferred_element_type=F32)
        return w[h:h + 1, :] * jnp.maximum(rel, 0.0)

    def store_keys(c, score, next_block, diagonal):
        bits = pltpu.bitcast(score, jnp.int32)
        key = jnp.where(bits < 0, int_min - bits, bits)
        if diagonal:
            kpos = lax.broadcasted_iota(jnp.int32, (kb, kb), 0)
            qpos = lax.broadcasted_iota(jnp.int32, (kb, kb), 1)
            key = jnp.where(kpos <= qpos, key, int_min)
        hi_dst, lo_dst = (hi_next_ref, lo_next_ref) if next_block else (hi_ref, lo_ref)
        hi_dst[c] = lax.shift_right_arithmetic(key, 16).astype(jnp.int16)
        lo_dst[c] = (key ^ 0x8000).astype(jnp.int16)

    store_keys(j, sum(score_head(j, h, False) for h in range(IDX_HEADS)), False, True)

    one_b, zero_b = jnp.ones((), BF16), jnp.zeros((), BF16)
    one_i, zero_i = jnp.ones((), jnp.int16), jnp.zeros((), jnp.int16)
    rows16 = 16

    hi_ref[nk] = jnp.full((kb, kb), HALF_MIN, jnp.int16)
    lo_ref[nk] = jnp.full((kb, kb), HALF_MIN, jnp.int16)

    def count(hit_fn):
        def fold(hit):
            parts = [hit[rows16 * r:rows16 * (r + 1)] for r in range(kb // rows16)]
            while len(parts) > 1:
                parts = [a + b for a, b in zip(parts[::2], parts[1::2])]
            return parts[0]

        def body(i, cnt):
            return cnt + (fold(hit_fn(2 * i)) + fold(hit_fn(2 * i + 1)))

        cnt = lax.fori_loop(0, (nk + 1) // 2, body, jnp.zeros((rows16, kb), jnp.int16))
        return jnp.sum(cnt.astype(jnp.int32), axis=0, keepdims=True).astype(F32)

    def bisect_half(ref, target):
        def bit_step(i, carry):
            thr_u, n_above = carry
            cand_u = thr_u | lax.shift_left(jnp.int32(1), 15 - i)
            cand = (cand_u + HALF_MIN).astype(jnp.int16)
            cnt = count(lambda c: jnp.where(ref[c] >= cand, one_i, zero_i))
            accepted = cnt >= target
            return jnp.where(accepted, cand_u, thr_u), jnp.where(accepted, n_above, cnt)
        thr_u, n_above = lax.fori_loop(0, 16, bit_step, (jnp.zeros((1, kb), jnp.int32), jnp.zeros((1, kb), F32)))
        return thr_u + HALF_MIN, n_above

    thr_hi32, n_gt_hi = bisect_half(hi_ref, float(topk))
    thr_hi = thr_hi32.astype(jnp.int16)

    def keep_matching_low(c, carry):
        lo_ref[c] = jnp.where(hi_ref[c] == thr_hi, lo_ref[c], jnp.int16(HALF_MIN))
        return carry

    lax.fori_loop(0, nk, keep_matching_low, 0)
    thr_lo32, n_gt_lo = bisect_half(lo_ref, topk - n_gt_hi)
    thr_lo = thr_lo32.astype(jnp.int16)
    n_gt = n_gt_hi + n_gt_lo
    is_marker = (thr_hi32 == HALF_MIN) & (thr_lo32 == HALF_MIN)
    need = jnp.where(is_marker, 0.0, topk - n_gt)

    def bias_chunk(c, ties_before):
        hi, lo = hi_ref[c], lo_ref[c]
        above = jnp.where(hi > thr_hi, one_b, jnp.where(lo > thr_lo, one_b, zero_b)).astype(F32)
        eq_b = jnp.where(hi == thr_hi, jnp.where(lo == thr_lo, one_b, zero_b), zero_b)
        eq = eq_b.astype(F32)
        rank = jnp.dot(ltri_ref[...], eq_b, preferred_element_type=F32) + ties_before
        chosen = above + jnp.where(rank < need, eq, 0.0)
        bias = jnp.where(chosen > 0.5, 0.0, NEG_BIG).astype(BF16)
        hi_ref[c] = pltpu.bitcast(bias, jnp.int16)
        return ties_before + jnp.sum(eq, axis=0, keepdims=True)


    def attend(bounded):
        acc_ref[...] = jnp.zeros(acc_ref.shape, F32)
        if not bounded:
            m_ref[...] = jnp.full(m_ref.shape, NEG_BIG, F32)

        def logits_head(c, h, bias):
            kk = k_ref[c, :, LANES * (h // 2):LANES * (h // 2 + 1)]
            lgb = jnp.dot(kk, qz_ref[h], preferred_element_type=F32) + bias
            lg_ref[h] = lgb
            if not bounded:
                m_old = m_ref[h]
                m_new = jnp.maximum(m_old, jnp.max(lgb, axis=0, keepdims=True))
                alpha_ref[h] = jnp.exp2(m_old - m_new)
                m_ref[h] = m_new

        def softmax_pv_head(c, h):
            p = jnp.exp2(lg_ref[h] if bounded else lg_ref[h] - m_ref[h])
            pv = jnp.dot(vt_ref[c, VT_ROWS * h:VT_ROWS * (h + 1), :], p.astype(BF16),
                         preferred_element_type=F32)
            acc_ref[h] = (acc_ref[h] if bounded else alpha_ref[h] * acc_ref[h]) + pv

        def chunk_bias(c):
            return pltpu.bitcast(hi_ref[c], BF16).astype(F32)

        def next_block_scores(c, h, partial):
            if h % (ATTN_HEADS // IDX_HEADS) != ATTN_HEADS // IDX_HEADS - 1:
                return partial
            term = score_head(c, h // (ATTN_HEADS // IDX_HEADS), True)
            return term if partial is None else partial + term

        ties = bias_chunk(0, jnp.zeros((1, kb), F32))
        bias0 = chunk_bias(0)
        partial = None
        for h in range(ATTN_HEADS):
            logits_head(0, h, bias0)
            partial = next_block_scores(0, h, partial)
        store_keys(0, partial, True, False)
        ties = bias_chunk(1, ties)

        def pipelined(c, ties_before):
            bias = chunk_bias(c)
            ties_after = bias_chunk(c + 1, ties_before)
            partial = None
            for h in range(ATTN_HEADS):
                softmax_pv_head(c - 1, h)
                logits_head(c, h, bias)
                partial = next_block_scores(c, h, partial)
            store_keys(c, partial, True, False)
            return ties_after

        lax.fori_loop(1, nk, pipelined, ties)
        for h in range(ATTN_HEADS):
            softmax_pv_head(nk - 1, h)

    logits_bounded = bound_ref[0] < MAX_SAFE_LOGIT

    @pl.when(logits_bounded)
    def _():
        attend(True)

    @pl.when(jnp.logical_not(logits_bounded))
    def _():
        attend(False)

    out_t = [acc_ref[h][:HEAD_DIM] / acc_ref[h][HEAD_DIM:HEAD_DIM + 1] for h in range(ATTN_HEADS)]
    o_ref[...] = jnp.concatenate(out_t, axis=0).T.astype(o_ref.dtype)


def _dsa(logit_bound, iq3t, ik3, iwt, qt, k, vt4, ltri, b, t, topk):
    kb = KEY_BLOCK
    nkc = t // kb
    chunked = lambda a: a.reshape(b * nkc, kb, a.shape[-1])
    per_batch = lambda shape: pl.BlockSpec((nkc,) + shape, lambda i, j: (i, 0, 0), pipeline_mode=pl.Buffered(1))
    q_cols = lambda rows: pl.BlockSpec((rows, kb), lambda i, j: (0, i * nkc + j))
    next_q_cols = lambda rows: pl.BlockSpec((rows, kb), lambda i, j: (0, i * nkc + jnp.minimum(j + 1, nkc - 1)))
    return pl.pallas_call(
        functools.partial(_dsa_kernel, topk=topk),
        grid=(b, nkc),
        in_specs=[
            pl.BlockSpec(memory_space=pltpu.SMEM),
            q_cols(IDX_HEADS * 4 * IDX_DIM),
            next_q_cols(IDX_HEADS * 4 * IDX_DIM),
            per_batch((kb, 4 * IDX_DIM)),
            q_cols(SUBLANES),
            next_q_cols(SUBLANES),
            q_cols(ATTN_WIDTH),
            per_batch((kb, ATTN_WIDTH)),
            per_batch((ATTN_HEADS * VT_ROWS, kb)),
            _resident(ltri.shape),
        ],
        out_specs=pl.BlockSpec((kb, ATTN_WIDTH), lambda i, j: (i * nkc + j, 0)),
        out_shape=jax.ShapeDtypeStruct((b * t, ATTN_WIDTH), BF16),
        scratch_shapes=[
            pltpu.VMEM((2, nkc + 1, kb, kb), jnp.int16),
            pltpu.VMEM((2, nkc + 1, kb, kb), jnp.int16),
            pltpu.VMEM((ATTN_HEADS, LANES, kb), BF16),
            pltpu.VMEM((ATTN_HEADS, kb, kb), F32),
            pltpu.VMEM((ATTN_HEADS, 1, kb), F32),
            pltpu.VMEM((ATTN_HEADS, 1, kb), F32),
            pltpu.VMEM((ATTN_HEADS, VT_ROWS, kb), F32),
        ],
        compiler_params=_params(2),
        name="dsa",
    )(logit_bound, iq3t, iq3t, chunked(ik3), iwt, iwt, qt, chunked(k), vt4, ltri)


def _hgrn_kernel(q_ref, lf_ref, k_ref, v_ref, gate_ref, hc_ref, lt_ref, o_ref, st_ref):
    @pl.when(pl.program_id(0) == 0)
    def _():
        st_ref[...] = jnp.zeros(st_ref.shape, F32)

    units = [(bi, ci) for ci in range(HG_STEP_CHUNKS) for bi in range(q_ref.shape[0])]
    rows = lambda ci: slice(HG_CHUNK * ci, HG_CHUNK * (ci + 1))
    cums = {(bi, ci): _dot_exact_lhs(lt_ref[...], lf_ref[bi, rows(ci), :]) for bi, ci in units}

    drops = []
    for b_all in cums.values():
        ends = [b_all[HG_FACT * (i + 1) - 1:HG_FACT * (i + 1), :] for i in range(HG_CHUNK // HG_FACT)]
        drops += [(ends[i - 1] if i > 0 else 0.0) - ends[i] for i in range(len(ends))]
    factored_ok = jnp.max(functools.reduce(jnp.maximum, drops)) < MAX_SAFE_EXPONENT

    def run(factored):
        for bi, ci in units:
            _hgrn_chunk(q_ref[bi, rows(ci), :], cums[(bi, ci)], k_ref[bi, rows(ci), :], v_ref.at[bi, rows(ci), :],
                        gate_ref[bi, rows(ci), :], hc_ref[3:4, :], o_ref.at[bi, rows(ci), :], st_ref.at[bi], factored)

    @pl.when(factored_ok)
    def _():
        run(True)

    @pl.when(jnp.logical_not(factored_ok))
    def _():
        run(False)


def _hgrn_chunk(q_all, b_all, k_all, hi_ref, gate, out_gain, o_ref, st_ref, factored):
    c_len, sub = HG_CHUNK, HG_SUB
    b_last = b_all[c_len - 1:c_len, :]
    q_inter = q_all * jnp.exp(b_all)
    k_carry = k_all * jnp.exp(b_last - b_all)

    n_sub = c_len // sub
    trans_b = (((1,), (1,)), ((), ()))

    def block_start(b, i):
        return b[sub * i - 1:sub * i, :] if i > 0 else jnp.zeros_like(b[0:1, :])

    def intra_factored_all():
        fs = HG_FACT
        heads = [slice(HG_DIM * h, HG_DIM * (h + 1)) for h in range(HG_HEADS)]
        scores, intra = [], []
        for hs in heads:
            b, q, k = b_all[:, hs], q_all[:, hs], k_all[:, hs]
            for i in range(c_len // fs):
                rs, upto = slice(fs * i, fs * (i + 1)), slice(0, fs * (i + 1))
                b_ref = b[fs * i - 1:fs * i, :] if i > 0 else jnp.zeros_like(b[0:1, :])
                q_hat = (q[rs] * jnp.exp(b[rs] - b_ref)).astype(BF16)
                k_hat = (k[upto] * jnp.exp(b_ref - b[upto])).astype(BF16)
                scores.append(lax.dot_general(q_hat, k_hat, trans_b, preferred_element_type=F32))
        for h, hs in enumerate(heads):
            vb = hi_ref[:, hs].astype(BF16)
            blocks = []
            for i in range(c_len // fs):
                a = scores[h * (c_len // fs) + i]
                t_idx = fs * i + lax.broadcasted_iota(jnp.int32, a.shape, 0)
                s_idx = lax.broadcasted_iota(jnp.int32, a.shape, 1)
                a = jnp.where(s_idx <= t_idx, a, 0.0).astype(BF16)
                blocks.append(jnp.dot(a, vb[0:fs * (i + 1)], preferred_element_type=F32))
            intra.append(jnp.concatenate(blocks, axis=0))
        return intra

    def intra_direct(b, q, k, vb):
        row = lax.broadcasted_iota(jnp.int32, (sub, 1), 0)
        col = lax.broadcasted_iota(jnp.int32, (sub, sub), 1)
        blocks = []
        for i in range(n_sub):
            rs = slice(sub * i, sub * (i + 1))
            b_i, q_i, k_i = b[rs], q[rs], k[rs]
            diag = jnp.zeros((sub, sub), F32)
            for s in range(sub):
                decay = jnp.exp(jnp.where(row >= s, b_i - b_i[s:s + 1, :], -jnp.inf))
                a_col = jnp.sum(q_i * k_i[s:s + 1, :] * decay, axis=1, keepdims=True)
                diag = jnp.where(col == s, a_col, diag)
            o_i = jnp.dot(diag.astype(BF16), vb[rs], preferred_element_type=F32)
            if i > 0:
                prev = slice(0, sub * i)
                b_ref = block_start(b, i)
                q_hat = (q_i * jnp.exp(b_i - b_ref)).astype(BF16)
                k_hat = (k[prev] * jnp.exp(b_ref - b[prev])).astype(BF16)
                a_off = lax.dot_general(q_hat, k_hat, trans_b, preferred_element_type=F32)
                o_i = o_i + jnp.dot(a_off.astype(BF16), vb[prev], preferred_element_type=F32)
            blocks.append(o_i)
        return jnp.concatenate(blocks, axis=0)

    if factored:
        intra = intra_factored_all()
    else:
        intra = [intra_direct(b_all[:, hs], q_all[:, hs], k_all[:, hs], hi_ref[:, hs].astype(BF16))
                 for hs in (slice(HG_DIM * h, HG_DIM * (h + 1)) for h in range(HG_HEADS))]

    outs = []
    for h in range(HG_HEADS):
        hs = slice(HG_DIM * h, HG_DIM * (h + 1))
        v = hi_ref[:, hs]
        st = st_ref[h]
        inter = lax.dot_general(q_inter[:, hs].astype(BF16), st.astype(BF16), trans_b,
                                preferred_element_type=F32)
        o = inter + intra[h]
        st_ref[h] = jnp.exp(b_last[:, hs]) * st + jnp.dot(v.T.astype(BF16), k_carry[:, hs].astype(BF16),
                                                         preferred_element_type=F32)
        outs.append(_rmsnorm_rows(o, out_gain[:, hs]) * gate[:, hs])
    o_ref[...] = jnp.concatenate(outs, axis=1).astype(o_ref.dtype)


def _hgrn(hg5, hconst, ltri, b, t):
    rows = HG_CHUNK * HG_STEP_CHUNKS
    streams = hg5.reshape(b, t, HG_STREAMS * HG_WIDTH)
    col = lambda jcol: pl.BlockSpec((b, rows, HG_WIDTH), lambda s, jcol=jcol: (0, s, jcol))
    return pl.pallas_call(
        _hgrn_kernel,
        grid=(t // rows,),
        in_specs=[col(jcol) for jcol in range(HG_STREAMS)] + [_resident(hconst.shape), _resident(ltri.shape)],
        out_specs=pl.BlockSpec((b, rows, HG_WIDTH), lambda s: (0, s, 0)),
        out_shape=jax.ShapeDtypeStruct((b, t, HG_WIDTH), BF16),
        scratch_shapes=[pltpu.VMEM((b, HG_HEADS, HG_DIM, HG_DIM), F32)],
        compiler_params=_params(1),
        name="hgrn",
    )(*([streams] * HG_STREAMS), hconst, ltri).reshape(b * t, HG_WIDTH)


def _merge_kernel(x_ref, ya_ref, yh_ref, g_ref, wg_ref, wpa_ref, wph_ref, wo_ref, o_ref):
    x = x_ref[...]
    hb = _rmsnorm_rows(x, g_ref[...]).astype(BF16)
    gates = jnp.dot(hb, wg_ref[...], preferred_element_type=F32)
    pa = jnp.dot(ya_ref[...], wpa_ref[...], preferred_element_type=F32)
    ph = jnp.dot(yh_ref[...], wph_ref[...], preferred_element_type=F32)
    merged = _sigmoid(gates[:, :D_MODEL]) * pa + _sigmoid(gates[:, D_MODEL:]) * ph
    o_ref[...] = x + jnp.dot(merged.astype(BF16), wo_ref[...], preferred_element_type=F32)


def _merge(x2, ya, yh, gain, wg, wpa, wph, wo, tm):
    nt = x2.shape[0]
    row = lambda w: pl.BlockSpec((tm, w), lambda i: (i, 0))
    return pl.pallas_call(
        _merge_kernel,
        grid=(nt // tm,),
        in_specs=[row(D_MODEL), row(ATTN_WIDTH), row(HG_WIDTH), _resident(gain.shape), _resident(wg.shape),
                  _resident(wpa.shape), _resident(wph.shape), _resident(wo.shape)],
        out_specs=row(D_MODEL),
        out_shape=jax.ShapeDtypeStruct((nt, D_MODEL), F32),
        compiler_params=_params(1),
        name="merge",
    )(x2, ya, yh, gain, wg, wpa, wph, wo)


def _ffn_kernel(x_ref, g_ref, wi_ref, wo_ref, o_ref):
    x = x_ref[...]
    hb = _rmsnorm_rows(x, g_ref[...]).astype(BF16)
    gu = jnp.dot(hb, wi_ref[...], preferred_element_type=F32)
    g = gu[:, :D_FF]
    act = g * _sigmoid(g) * gu[:, D_FF:]
    o_ref[...] = x + jnp.dot(act.astype(BF16), wo_ref[...], preferred_element_type=F32)


def _ffn(x2, gain, wi, wo, tm):
    nt = x2.shape[0]
    row = pl.BlockSpec((tm, D_MODEL), lambda i: (i, 0))
    return pl.pallas_call(
        _ffn_kernel,
        grid=(nt // tm,),
        in_specs=[row, _resident(gain.shape), _resident(wi.shape), _resident(wo.shape)],
        out_specs=row,
        out_shape=jax.ShapeDtypeStruct((nt, D_MODEL), F32),
        compiler_params=_params(1),
        name="ffn",
    )(x2, gain, wi, wo)


def _split_hi_lo(a):
    hi = a.astype(BF16)
    lo = (a - hi.astype(F32)).astype(BF16)
    return hi, lo


def _layer(x2, b, t, cos1, sin1, consts, w):
    hsum, hexp, ltri_attn, ltri_hgrn = consts
    qt, k, vt4, iq3t, ik3, iwt, hg5 = _mix_in(x2, w["norm_mix"], w["wa"], *w["wi"], w["wh"], w["qn"], w["kn"],
                                              cos1, sin1, hsum, hexp, w["hconst"], tm=ROW_TILE)
    y_attn = _dsa(w["logit_bound"], iq3t, ik3, iwt, qt, k, vt4, ltri_attn, b, t, topk=min(MAX_TOPK, t // 4))
    y_hgrn = _hgrn(hg5, w["hconst"], ltri_hgrn, b, t)
    x2 = _merge(x2, y_attn, y_hgrn, w["norm_mix"], w["wg"], w["wpa"], w["wph"], w["wout"], tm=ROW_TILE)
    return _ffn(x2, w["norm_ffn"], w["wffn_in"], w["wffn_out"], tm=ROW_TILE)


def _constants():
    head_of_lane = np.arange(ATTN_WIDTH) // HEAD_DIM
    hsum = (head_of_lane[:, None] == np.arange(LANES)[None, :]).astype(np.float32)
    hexp = hsum.T.copy()
    r = np.arange(KEY_BLOCK)
    ltri_attn = (r[None, :] < r[:, None]).astype(np.float32)
    r = np.arange(HG_CHUNK)
    ltri_hgrn = (r[None, :] <= r[:, None]).astype(np.float32)
    return jnp.asarray(hsum, BF16), jnp.asarray(hexp, BF16), jnp.asarray(ltri_attn, BF16), jnp.asarray(ltri_hgrn, BF16)


def kernel(x, positions, w_in, w_proj_attn, w_proj_hgrn, w_out, norm_mix, norm_ffn, q_norm, k_norm, hgrn_norm,
           hgrn_lower_bound, w_ffn_in, w_ffn_out):
    b, t, d = x.shape
    depth = w_in.shape[0]
    nt = b * t

    inv = ROPE_THETA ** (-jnp.arange(0, HEAD_DIM, 2, dtype=F32) / HEAD_DIM)
    ang = positions.astype(F32)[..., None] * inv
    cos, sin = jnp.cos(ang), jnp.sin(ang)
    cos1 = jnp.concatenate([cos, cos, cos, cos], axis=-1).reshape(nt, LANES)
    sin1 = jnp.concatenate([-sin, sin, -sin, sin], axis=-1).reshape(nt, LANES)

    lb_all = jnp.cumsum(jax.nn.softmax(hgrn_lower_bound.astype(F32), axis=0), axis=0)
    lb_all = lb_all - lb_all[:1]

    consts = _constants()
    widths = (ATTN_WIDTH, ATTN_WIDTH, ATTN_WIDTH, IDX_HEADS * IDX_DIM, IDX_DIM, IDX_HEADS,
              HG_WIDTH, HG_WIDTH, HG_WIDTH, HG_WIDTH, D_MODEL, D_MODEL)
    off = np.concatenate([[0], np.cumsum(widths)])
    idx_cols = off[6] - off[3]

    x2 = x.reshape(nt, d)
    w_in_bf16 = w_in.astype(BF16)
    for l in range(depth):
        wl = w_in[l]
        wl_bf16 = w_in_bf16[l]
        lb = lb_all[l]
        hconst = jnp.zeros((SUBLANES, HG_WIDTH), F32)
        hconst = hconst.at[0].set(jnp.log(lb)).at[1].set(jnp.log1p(-lb)).at[2].set(1.0 - lb)
        hconst = hconst.at[3].set(jnp.tile(hgrn_norm[l], HG_HEADS))
        w = {
            "norm_mix": norm_mix[l].reshape(1, d),
            "norm_ffn": norm_ffn[l].reshape(1, d),
            "wa": wl_bf16[:, off[0]:off[3]],
            "wi": _split_hi_lo(jnp.pad(wl[:, off[3]:off[6]], ((0, 0), (0, 3 * LANES - idx_cols)))),
            "wh": wl_bf16[:, off[6]:off[10]],
            "wg": wl_bf16[:, off[10]:off[12]],
            "qn": jnp.tile(q_norm[l], ATTN_HEADS).reshape(1, ATTN_WIDTH),
            "kn": jnp.tile(k_norm[l], ATTN_HEADS).reshape(1, ATTN_WIDTH),
            "logit_bound": (HEAD_DIM ** 0.5 * LOG2_E * jnp.max(jnp.abs(q_norm[l])) * jnp.max(jnp.abs(k_norm[l]))
                            ).reshape(1).astype(F32),
            "hconst": hconst,
            "wpa": w_proj_attn[l].astype(BF16),
            "wph": w_proj_hgrn[l].astype(BF16),
            "wout": w_out[l].astype(BF16),
            "wffn_in": w_ffn_in[l].astype(BF16),
            "wffn_out": w_ffn_out[l].astype(BF16),
        }
        x2 = _layer(x2, b, t, cos1, sin1, consts, w)
    return x2.reshape(b, t, d)
```

```python
import functools

import numpy as np
import jax
import jax.numpy as jnp
from jax import lax
from jax.experimental import pallas as pl
from jax.experimental.pallas import tpu as pltpu

D_MODEL = 1024
ATTN_HEADS = 8
HEAD_DIM = 64
ATTN_WIDTH = ATTN_HEADS * HEAD_DIM
IDX_HEADS = 4
IDX_DIM = 64
MAX_TOPK = 256
HG_HEADS = 4
HG_DIM = 128
HG_WIDTH = HG_HEADS * HG_DIM
HG_CHUNK = 64
HG_STREAMS = 5
HG_STEP_CHUNKS = 4
HG_SUB = 16
HG_FACT = 32
MAX_SAFE_LOGIT = 60.0
MAX_SAFE_EXPONENT = 80.0
D_FF = 2816
ROPE_THETA = 10000.0
EPS = 1e-6
LOG2_E = 1.4426950408889634

LANES = 128
SUBLANES = 8
ROW_TILE = 512
VMEM_LIMIT_BYTES = 56 * 1024 * 1024

KEY_BLOCK = 256
VT_ROWS = 80
INT_MIN = -(2 ** 31)
HALF_MIN = -(2 ** 15)
NEG_BIG = -1e30

F32 = jnp.float32
BF16 = jnp.bfloat16


def _resident(shape):
    nd = len(shape)
    return pl.BlockSpec(shape, lambda *_: (0,) * nd, pipeline_mode=pl.Buffered(1))


def _params(n_axes):
    return pltpu.CompilerParams(dimension_semantics=("arbitrary",) * n_axes,
                                vmem_limit_bytes=VMEM_LIMIT_BYTES)


def _rmsnorm_rows(x, gain):
    return x * lax.rsqrt(jnp.mean(x * x, axis=-1, keepdims=True) + EPS) * gain


def _sigmoid(x):
    return 1.0 / (1.0 + jnp.exp(-x))


def _rope(x, cos, sin_signed):
    w = x.shape[-1]
    lane = lax.broadcasted_iota(jnp.int32, x.shape, 1)
    first_half = (lane & (HEAD_DIM - 1)) < HEAD_DIM // 2
    partner = jnp.where(first_half, pltpu.roll(x, w - HEAD_DIM // 2, 1), pltpu.roll(x, HEAD_DIM // 2, 1))
    return x * cos + partner * sin_signed


def _dot_hi_lo(a, b_hi, b_lo=None):
    a_hi = a.astype(BF16)
    a_lo = (a - a_hi.astype(F32)).astype(BF16)
    out = jnp.dot(a_hi, b_hi, preferred_element_type=F32) + jnp.dot(a_lo, b_hi, preferred_element_type=F32)
    if b_lo is not None:
        out = out + jnp.dot(a_hi, b_lo, preferred_element_type=F32)
    return out


def _dot_exact_lhs(a, b):
    b1 = b.astype(BF16)
    r1 = b - b1.astype(F32)
    b2 = r1.astype(BF16)
    b3 = (r1 - b2.astype(F32)).astype(BF16)
    return (jnp.dot(a, b1, preferred_element_type=F32) + jnp.dot(a, b2, preferred_element_type=F32)
            + jnp.dot(a, b3, preferred_element_type=F32))


def _mix_in_kernel(x_ref, g_ref, wa_ref, wi_ref, wil_ref, wh_ref, qn_ref, kn_ref, cos_ref, sin_ref, hsum_ref, hexp_ref,
                   hc_ref, qt_ref, k_ref, vt_ref, iq3_ref, ik3_ref, iw_ref, hg_ref):
    h = _rmsnorm_rows(x_ref[...], g_ref[...])
    hb = h.astype(BF16)
    cos1 = cos_ref[...]
    sin1 = sin_ref[...]
    cos4 = jnp.concatenate([cos1] * 4, axis=1)
    sin4 = jnp.concatenate([sin1] * 4, axis=1)

    def head_norm(a, gain):
        ss = _dot_hi_lo(a * a, hsum_ref[...])
        r = lax.rsqrt(ss * (1.0 / HEAD_DIM) + EPS)
        return a * _dot_hi_lo(r, hexp_ref[...]) * gain

    pa = jnp.dot(hb, wa_ref[...], preferred_element_type=F32)
    q = _rope(head_norm(pa[:, :ATTN_WIDTH], qn_ref[...]), cos4, sin4) * (HEAD_DIM ** -0.5 * LOG2_E)
    k = _rope(head_norm(pa[:, ATTN_WIDTH:2 * ATTN_WIDTH], kn_ref[...]), cos4, sin4)
    qt_ref[...] = q.T.astype(BF16)
    k_ref[...] = k.astype(BF16)
    v = pa[:, 2 * ATTN_WIDTH:]
    ones_pad = jnp.where(lax.broadcasted_iota(jnp.int32, (VT_ROWS - HEAD_DIM, KEY_BLOCK), 0) == 0, 1.0, 0.0)
    for s in range(v.shape[0] // KEY_BLOCK):
        vt = v[KEY_BLOCK * s:KEY_BLOCK * (s + 1)].T
        rows = []
        for hd in range(ATTN_HEADS):
            rows += [vt[HEAD_DIM * hd:HEAD_DIM * (hd + 1)], ones_pad]
        vt_ref[s] = jnp.concatenate(rows, axis=0).astype(BF16)

    pi = _dot_hi_lo(h, wi_ref[...], wil_ref[...])
    nq = IDX_HEADS * IDX_DIM
    iq = _rope(pi[:, :nq], cos4[:, :nq], sin4[:, :nq]) * (IDX_DIM ** -0.5)
    slab = pi[:, nq:]
    ikw_t = _rope(slab, cos1, sin1).T
    iw_ref[...] = slab.T[IDX_DIM:IDX_DIM + SUBLANES] * (IDX_HEADS ** -0.5)

    def hi_lo(a):
        hi = a.astype(BF16).astype(F32)
        return hi, (a - hi).astype(BF16).astype(F32)

    k_hi, k_lo = hi_lo(ikw_t[:IDX_DIM])
    ik3_ref[...] = jnp.concatenate([k_hi, k_lo, k_hi, jnp.zeros_like(k_hi)], axis=0).T.astype(BF16)
    iq_t = iq.T
    rows = []
    for hd in range(IDX_HEADS):
        q_hi, q_lo = hi_lo(iq_t[IDX_DIM * hd:IDX_DIM * (hd + 1)])
        rows += [q_hi, q_hi, q_lo, jnp.zeros_like(q_hi)]
    iq3_ref[...] = jnp.concatenate(rows, axis=0).astype(BF16)

    ph = jnp.dot(hb, wh_ref[...], preferred_element_type=F32)
    w = HG_WIDTH
    hq, hf, hg = ph[:, :w], ph[:, w:2 * w], ph[:, 3 * w:]
    log_lb, log1m_lb, one_m_lb = hc_ref[0:1, :], hc_ref[1:2, :], hc_ref[2:3, :]
    c = log1m_lb + jnp.minimum(hf, 0.0) - jnp.log1p(jnp.exp(-jnp.abs(hf)))
    hg_ref[:, :w] = hq * _sigmoid(hq)
    hg_ref[:, w:2 * w] = jnp.maximum(log_lb, c) + jnp.log1p(jnp.exp(-jnp.abs(log_lb - c)))
    hg_ref[:, 2 * w:3 * w] = one_m_lb * _sigmoid(-hf)
    hg_ref[:, 3 * w:4 * w] = ph[:, 2 * w:3 * w]
    hg_ref[:, 4 * w:] = hg * _sigmoid(hg)


def _mix_in(x2, gain, wa, wi, wil, wh, qn, kn, cos1, sin1, hsum, hexp, hconst, tm):
    nt = x2.shape[0]
    kb = KEY_BLOCK
    row = lambda w: pl.BlockSpec((tm, w), lambda i: (i, 0))
    col = lambda w: pl.BlockSpec((w, tm), lambda i: (0, i))
    return pl.pallas_call(
        _mix_in_kernel,
        grid=(nt // tm,),
        in_specs=[row(D_MODEL), _resident(gain.shape), _resident(wa.shape), _resident(wi.shape), _resident(wil.shape),
                  _resident(wh.shape),
                  _resident(qn.shape), _resident(kn.shape), row(LANES), row(LANES),
                  _resident(hsum.shape), _resident(hexp.shape), _resident(hconst.shape)],
        out_specs=[col(ATTN_WIDTH), row(ATTN_WIDTH),
                   pl.BlockSpec((tm // kb, ATTN_HEADS * VT_ROWS, kb), lambda i: (i, 0, 0)),
                   col(IDX_HEADS * 4 * IDX_DIM), row(4 * IDX_DIM), col(SUBLANES), row(HG_STREAMS * HG_WIDTH)],
        out_shape=[jax.ShapeDtypeStruct((ATTN_WIDTH, nt), BF16), jax.ShapeDtypeStruct((nt, ATTN_WIDTH), BF16),
                   jax.ShapeDtypeStruct((nt // kb, ATTN_HEADS * VT_ROWS, kb), BF16),
                   jax.ShapeDtypeStruct((IDX_HEADS * 4 * IDX_DIM, nt), BF16),
                   jax.ShapeDtypeStruct((nt, 4 * IDX_DIM), BF16), jax.ShapeDtypeStruct((SUBLANES, nt), F32),
                   jax.ShapeDtypeStruct((nt, HG_STREAMS * HG_WIDTH), F32)],
        compiler_params=_params(1),
        name="mix_in",
    )(x2, gain, wa, wi, wil, wh, qn, kn, cos1, sin1, hsum, hexp, hconst)


def _dsa_kernel(bound_ref, iq3_ref, iq3_next_ref, ik3_ref, iw_ref, iw_next_ref, qt_ref, k_ref, vt_ref, ltri_ref,
                o_ref, hi2_ref, lo2_ref, qz_ref, lg_ref, m_ref, alpha_ref, acc_ref, *, topk):
    kb = KEY_BLOCK
    j = pl.program_id(1)
    nk = j + 1
    int_min = jnp.int32(INT_MIN)
    slot = j % 2
    hi_ref, lo_ref = hi2_ref.at[slot], lo2_ref.at[slot]
    hi_next_ref, lo_next_ref = hi2_ref.at[1 - slot], lo2_ref.at[1 - slot]

    zeros = jnp.zeros((HEAD_DIM, kb), BF16)
    for h in range(ATTN_HEADS):
        qh = qt_ref[HEAD_DIM * h:HEAD_DIM * (h + 1), :]
        qz_ref[h] = jnp.concatenate([qh, zeros] if h % 2 == 0 else [zeros, qh], axis=0)

    def score_head(c, h, next_block):
        q3, w = (iq3_next_ref, iw_next_ref) if next_block else (iq3_ref, iw_ref)
        rel = jnp.dot(ik3_ref[c], q3[4 * IDX_DIM * h:4 * IDX_DIM * (h + 1), :], preferred_element_type=F32)
        return w[h:h + 1, :] * jnp.maximum(rel, 0.0)

    def store_keys(c, score, next_block, diagonal):
        bits = pltpu.bitcast(score, jnp.int32)
        key = jnp.where(bits < 0, int_min - bits, bits)
        if diagonal:
            kpos = lax.broadcasted_iota(jnp.int32, (kb, kb), 0)
            qpos = lax.broadcasted_iota(jnp.int32, (kb, kb), 1)
            key = jnp.where(kpos <= qpos, key, int_min)
        hi_dst, lo_dst = (hi_next_ref, lo_next_ref) if next_block else (hi_ref, lo_ref)
        hi_dst[c] = lax.shift_right_arithmetic(key, 16).astype(jnp.int16)
        lo_dst[c] = (key ^ 0x8000).astype(jnp.int16)

    store_keys(j, sum(score_head(j, h, False) for h in range(IDX_HEADS)), False, True)

    one_b, zero_b = jnp.ones((), BF16), jnp.zeros((), BF16)
    one_i, zero_i = jnp.ones((), jnp.int16), jnp.zeros((), jnp.int16)
    rows16 = 16

    hi_ref[nk] = jnp.full((kb, kb), HALF_MIN, jnp.int16)
    lo_ref[nk] = jnp.full((kb, kb), HALF_MIN, jnp.int16)

    def count(hit_fn):
        def fold(hit):
            parts = [hit[rows16 * r:rows16 * (r + 1)] for r in range(kb // rows16)]
            while len(parts) > 1:
                parts = [a + b for a, b in zip(parts[::2], parts[1::2])]
            return parts[0]

        def body(i, cnt):
            return cnt + (fold(hit_fn(2 * i)) + fold(hit_fn(2 * i + 1)))

        cnt = lax.fori_loop(0, (nk + 1) // 2, body, jnp.zeros((rows16, kb), jnp.int16))
        return jnp.sum(cnt.astype(jnp.int32), axis=0, keepdims=True).astype(F32)

    def bisect_half(ref, target):
        def bit_step(i, carry):
            thr_u, n_above = carry
            cand_u = thr_u | lax.shift_left(jnp.int32(1), 15 - i)
            cand = (cand_u + HALF_MIN).astype(jnp.int16)
            cnt = count(lambda c: jnp.where(ref[c] >= cand, one_i, zero_i))
            accepted = cnt >= target
            return jnp.where(accepted, cand_u, thr_u), jnp.where(accepted, n_above, cnt)
        thr_u, n_above = lax.fori_loop(0, 16, bit_step, (jnp.zeros((1, kb), jnp.int32), jnp.zeros((1, kb), F32)))
        return thr_u + HALF_MIN, n_above

    thr_hi32, n_gt_hi = bisect_half(hi_ref, float(topk))
    thr_hi = thr_hi32.astype(jnp.int16)

    def keep_matching_low(c, carry):
        lo_ref[c] = jnp.where(hi_ref[c] == thr_hi, lo_ref[c], jnp.int16(HALF_MIN))
        return carry

    lax.fori_loop(0, nk, keep_matching_low, 0)
    thr_lo32, n_gt_lo = bisect_half(lo_ref, topk - n_gt_hi)
    thr_lo = thr_lo32.astype(jnp.int16)
    n_gt = n_gt_hi + n_gt_lo
    is_marker = (thr_hi32 == HALF_MIN) & (thr_lo32 == HALF_MIN)
    need = jnp.where(is_marker, 0.0, topk - n_gt)

    def bias_chunk(c, ties_before):
        hi, lo = hi_ref[c], lo_ref[c]
        above = jnp.where(hi > thr_hi, one_b, jnp.where(lo > thr_lo, one_b, zero_b)).astype(F32)
        eq_b = jnp.where(hi == thr_hi, jnp.where(lo == thr_lo, one_b, zero_b), zero_b)
        eq = eq_b.astype(F32)
        rank = jnp.dot(ltri_ref[...], eq_b, preferred_element_type=F32) + ties_before
        chosen = above + jnp.where(rank < need, eq, 0.0)
        bias = jnp.where(chosen > 0.5, 0.0, NEG_BIG).astype(BF16)
        hi_ref[c] = pltpu.bitcast(bias, jnp.int16)
        return ties_before + jnp.sum(eq, axis=0, keepdims=True)


    def attend(bounded):
        acc_ref[...] = jnp.zeros(acc_ref.shape, F32)
        if not bounded:
            m_ref[...] = jnp.full(m_ref.shape, NEG_BIG, F32)

        def logits_head(c, h, bias):
            kk = k_ref[c, :, LANES * (h // 2):LANES * (h // 2 + 1)]
            lgb = jnp.dot(kk, qz_ref[h], preferred_element_type=F32) + bias
            lg_ref[h] = lgb
            if not bounded:
                m_old = m_ref[h]
                m_new = jnp.maximum(m_old, jnp.max(lgb, axis=0, keepdims=True))
                alpha_ref[h] = jnp.exp2(m_old - m_new)
                m_ref[h] = m_new

        def softmax_pv_head(c, h):
            p = jnp.exp2(lg_ref[h] if bounded else lg_ref[h] - m_ref[h])
            pv = jnp.dot(vt_ref[c, VT_ROWS * h:VT_ROWS * (h + 1), :], p.astype(BF16),
                         preferred_element_type=F32)
            acc_ref[h] = (acc_ref[h] if bounded else alpha_ref[h] * acc_ref[h]) + pv

        def chunk_bias(c):
            return pltpu.bitcast(hi_ref[c], BF16).astype(F32)

        def next_block_scores(c, h, partial):
            if h % (ATTN_HEADS // IDX_HEADS) != ATTN_HEADS // IDX_HEADS - 1:
                return partial
            term = score_head(c, h // (ATTN_HEADS // IDX_HEADS), True)
            return term if partial is None else partial + term

        ties = bias_chunk(0, jnp.zeros((1, kb), F32))
        bias0 = chunk_bias(0)
        partial = None
        for h in range(ATTN_HEADS):
            logits_head(0, h, bias0)
            partial = next_block_scores(0, h, partial)
        store_keys(0, partial, True, False)
        ties = bias_chunk(1, ties)

        def pipelined(c, ties_before):
            bias = chunk_bias(c)
            ties_after = bias_chunk(c + 1, ties_before)
            partial = None
            for h in range(ATTN_HEADS):
                softmax_pv_head(c - 1, h)
                logits_head(c, h, bias)
                partial = next_block_scores(c, h, partial)
            store_keys(c, partial, True, False)
            return ties_after

        lax.fori_loop(1, nk, pipelined, ties)
        for h in range(ATTN_HEADS):
            softmax_pv_head(nk - 1, h)

    logits_bounded = bound_ref[0] < MAX_SAFE_LOGIT

    @pl.when(logits_bounded)
    def _():
        attend(True)

    @pl.when(jnp.logical_not(logits_bounded))
    def _():
        attend(False)

    out_t = [acc_ref[h][:HEAD_DIM] / acc_ref[h][HEAD_DIM:HEAD_DIM + 1] for h in range(ATTN_HEADS)]
    o_ref[...] = jnp.concatenate(out_t, axis=0).T.astype(o_ref.dtype)


def _dsa(logit_bound, iq3t, ik3, iwt, qt, k, vt4, ltri, b, t, topk):
    kb = KEY_BLOCK
    nkc = t // kb
    chunked = lambda a: a.reshape(b * nkc, kb, a.shape[-1])
    per_batch = lambda shape: pl.BlockSpec((nkc,) + shape, lambda i, j: (i, 0, 0), pipeline_mode=pl.Buffered(1))
    q_cols = lambda rows: pl.BlockSpec((rows, kb), lambda i, j: (0, i * nkc + j))
    next_q_cols = lambda rows: pl.BlockSpec((rows, kb), lambda i, j: (0, i * nkc + jnp.minimum(j + 1, nkc - 1)))
    return pl.pallas_call(
        functools.partial(_dsa_kernel, topk=topk),
        grid=(b, nkc),
        in_specs=[
            pl.BlockSpec(memory_space=pltpu.SMEM),
            q_cols(IDX_HEADS * 4 * IDX_DIM),
            next_q_cols(IDX_HEADS * 4 * IDX_DIM),
            per_batch((kb, 4 * IDX_DIM)),
            q_cols(SUBLANES),
            next_q_cols(SUBLANES),
            q_cols(ATTN_WIDTH),
            per_batch((kb, ATTN_WIDTH)),
            per_batch((ATTN_HEADS * VT_ROWS, kb)),
            _resident(ltri.shape),
        ],
        out_specs=pl.BlockSpec((kb, ATTN_WIDTH), lambda i, j: (i * nkc + j, 0)),
        out_shape=jax.ShapeDtypeStruct((b * t, ATTN_WIDTH), BF16),
        scratch_shapes=[
            pltpu.VMEM((2, nkc + 1, kb, kb), jnp.int16),
            pltpu.VMEM((2, nkc + 1, kb, kb), jnp.int16),
            pltpu.VMEM((ATTN_HEADS, LANES, kb), BF16),
            pltpu.VMEM((ATTN_HEADS, kb, kb), F32),
            pltpu.VMEM((ATTN_HEADS, 1, kb), F32),
            pltpu.VMEM((ATTN_HEADS, 1, kb), F32),
            pltpu.VMEM((ATTN_HEADS, VT_ROWS, kb), F32),
        ],
        compiler_params=_params(2),
        name="dsa",
    )(logit_bound, iq3t, iq3t, chunked(ik3), iwt, iwt, qt, chunked(k), vt4, ltri)


def _hgrn_kernel(q_ref, lf_ref, k_ref, v_ref, gate_ref, hc_ref, lt_ref, o_ref, st_ref):
    @pl.when(pl.program_id(0) == 0)
    def _():
        st_ref[...] = jnp.zeros(st_ref.shape, F32)

    units = [(bi, ci) for ci in range(HG_STEP_CHUNKS) for bi in range(q_ref.shape[0])]
    rows = lambda ci: slice(HG_CHUNK * ci, HG_CHUNK * (ci + 1))
    cums = {(bi, ci): _dot_exact_lhs(lt_ref[...], lf_ref[bi, rows(ci), :]) for bi, ci in units}

    drops = []
    for b_all in cums.values():
        ends = [b_all[HG_FACT * (i + 1) - 1:HG_FACT * (i + 1), :] for i in range(HG_CHUNK // HG_FACT)]
        drops += [(ends[i - 1] if i > 0 else 0.0) - ends[i] for i in range(len(ends))]
    factored_ok = jnp.max(functools.reduce(jnp.maximum, drops)) < MAX_SAFE_EXPONENT

    def run(factored):
        for bi, ci in units:
            _hgrn_chunk(q_ref[bi, rows(ci), :], cums[(bi, ci)], k_ref[bi, rows(ci), :], v_ref.at[bi, rows(ci), :],
                        gate_ref[bi, rows(ci), :], hc_ref[3:4, :], o_ref.at[bi, rows(ci), :], st_ref.at[bi], factored)

    @pl.when(factored_ok)
    def _():
        run(True)

    @pl.when(jnp.logical_not(factored_ok))
    def _():
        run(False)


def _hgrn_chunk(q_all, b_all, k_all, hi_ref, gate, out_gain, o_ref, st_ref, factored):
    c_len, sub = HG_CHUNK, HG_SUB
    b_last = b_all[c_len - 1:c_len, :]
    q_inter = q_all * jnp.exp(b_all)
    k_carry = k_all * jnp.exp(b_last - b_all)

    n_sub = c_len // sub
    trans_b = (((1,), (1,)), ((), ()))

    def block_start(b, i):
        return b[sub * i - 1:sub * i, :] if i > 0 else jnp.zeros_like(b[0:1, :])

    def intra_factored_all():
        fs = HG_FACT
        heads = [slice(HG_DIM * h, HG_DIM * (h + 1)) for h in range(HG_HEADS)]
        scores, intra = [], []
        for hs in heads:
            b, q, k = b_all[:, hs], q_all[:, hs], k_all[:, hs]
            for i in range(c_len // fs):
                rs, upto = slice(fs * i, fs * (i + 1)), slice(0, fs * (i + 1))
                b_ref = b[fs * i - 1:fs * i, :] if i > 0 else jnp.zeros_like(b[0:1, :])
                q_hat = (q[rs] * jnp.exp(b[rs] - b_ref)).astype(BF16)
                k_hat = (k[upto] * jnp.exp(b_ref - b[upto])).astype(BF16)
                scores.append(lax.dot_general(q_hat, k_hat, trans_b, preferred_element_type=F32))
        for h, hs in enumerate(heads):
            vb = hi_ref[:, hs].astype(BF16)
            blocks = []
            for i in range(c_len // fs):
                a = scores[h * (c_len // fs) + i]
                t_idx = fs * i + lax.broadcasted_iota(jnp.int32, a.shape, 0)
                s_idx = lax.broadcasted_iota(jnp.int32, a.shape, 1)
                a = jnp.where(s_idx <= t_idx, a, 0.0).astype(BF16)
                blocks.append(jnp.dot(a, vb[0:fs * (i + 1)], preferred_element_type=F32))
            intra.append(jnp.concatenate(blocks, axis=0))
        return intra

    def intra_direct(b, q, k, vb):
        row = lax.broadcasted_iota(jnp.int32, (sub, 1), 0)
        col = lax.broadcasted_iota(jnp.int32, (sub, sub), 1)
        blocks = []
        for i in range(n_sub):
            rs = slice(sub * i, sub * (i + 1))
            b_i, q_i, k_i = b[rs], q[rs], k[rs]
            diag = jnp.zeros((sub, sub), F32)
            for s in range(sub):
                decay = jnp.exp(jnp.where(row >= s, b_i - b_i[s:s + 1, :], -jnp.inf))
                a_col = jnp.sum(q_i * k_i[s:s + 1, :] * decay, axis=1, keepdims=True)
                diag = jnp.where(col == s, a_col, diag)
            o_i = jnp.dot(diag.astype(BF16), vb[rs], preferred_element_type=F32)
            if i > 0:
                prev = slice(0, sub * i)
                b_ref = block_start(b, i)
                q_hat = (q_i * jnp.exp(b_i - b_ref)).astype(BF16)
                k_hat = (k[prev] * jnp.exp(b_ref - b[prev])).astype(BF16)
                a_off = lax.dot_general(q_hat, k_hat, trans_b, preferred_element_type=F32)
                o_i = o_i + jnp.dot(a_off.astype(BF16), vb[prev], preferred_element_type=F32)
            blocks.append(o_i)
        return jnp.concatenate(blocks, axis=0)

    if factored:
        intra = intra_factored_all()
    else:
        intra = [intra_direct(b_all[:, hs], q_all[:, hs], k_all[:, hs], hi_ref[:, hs].astype(BF16))
                 for hs in (slice(HG_DIM * h, HG_DIM * (h + 1)) for h in range(HG_HEADS))]

    outs = []
    for h in range(HG_HEADS):
        hs = slice(HG_DIM * h, HG_DIM * (h + 1))
        v = hi_ref[:, hs]
        st = st_ref[h]
        inter = lax.dot_general(q_inter[:, hs].astype(BF16), st.astype(BF16), trans_b,
                                preferred_element_type=F32)
        o = inter + intra[h]
        st_ref[h] = jnp.exp(b_last[:, hs]) * st + jnp.dot(v.T.astype(BF16), k_carry[:, hs].astype(BF16),
                                                         preferred_element_type=F32)
        outs.append(_rmsnorm_rows(o, out_gain[:, hs]) * gate[:, hs])
    o_ref[...] = jnp.concatenate(outs, axis=1).astype(o_ref.dtype)


def _hgrn(hg5, hconst, ltri, b, t):
    rows = HG_CHUNK * HG_STEP_CHUNKS
    streams = hg5.reshape(b, t, HG_STREAMS * HG_WIDTH)
    col = lambda jcol: pl.BlockSpec((b, rows, HG_WIDTH), lambda s, jcol=jcol: (0, s, jcol))
    return pl.pallas_call(
        _hgrn_kernel,
        grid=(t // rows,),
        in_specs=[col(jcol) for jcol in range(HG_STREAMS)] + [_resident(hconst.shape), _resident(ltri.shape)],
        out_specs=pl.BlockSpec((b, rows, HG_WIDTH), lambda s: (0, s, 0)),
        out_shape=jax.ShapeDtypeStruct((b, t, HG_WIDTH), BF16),
        scratch_shapes=[pltpu.VMEM((b, HG_HEADS, HG_DIM, HG_DIM), F32)],
        compiler_params=_params(1),
        name="hgrn",
    )(*([streams] * HG_STREAMS), hconst, ltri).reshape(b * t, HG_WIDTH)


def _merge_kernel(x_ref, ya_ref, yh_ref, g_ref, wg_ref, wpa_ref, wph_ref, wo_ref, o_ref):
    x = x_ref[...]
    hb = _rmsnorm_rows(x, g_ref[...]).astype(BF16)
    gates = jnp.dot(hb, wg_ref[...], preferred_element_type=F32)
    pa = jnp.dot(ya_ref[...], wpa_ref[...], preferred_element_type=F32)
    ph = jnp.dot(yh_ref[...], wph_ref[...], preferred_element_type=F32)
    merged = _sigmoid(gates[:, :D_MODEL]) * pa + _sigmoid(gates[:, D_MODEL:]) * ph
    o_ref[...] = x + jnp.dot(merged.astype(BF16), wo_ref[...], preferred_element_type=F32)


def _merge(x2, ya, yh, gain, wg, wpa, wph, wo, tm):
    nt = x2.shape[0]
    row = lambda w: pl.BlockSpec((tm, w), lambda i: (i, 0))
    return pl.pallas_call(
        _merge_kernel,
        grid=(nt // tm,),
        in_specs=[row(D_MODEL), row(ATTN_WIDTH), row(HG_WIDTH), _resident(gain.shape), _resident(wg.shape),
                  _resident(wpa.shape), _resident(wph.shape), _resident(wo.shape)],
        out_specs=row(D_MODEL),
        out_shape=jax.ShapeDtypeStruct((nt, D_MODEL), F32),
        compiler_params=_params(1),
        name="merge",
    )(x2, ya, yh, gain, wg, wpa, wph, wo)


def _ffn_kernel(x_ref, g_ref, wi_ref, wo_ref, o_ref):
    x = x_ref[...]
    hb = _rmsnorm_rows(x, g_ref[...]).astype(BF16)
    gu = jnp.dot(hb, wi_ref[...], preferred_element_type=F32)
    g = gu[:, :D_FF]
    act = g * _sigmoid(g) * gu[:, D_FF:]
    o_ref[...] = x + jnp.dot(act.astype(BF16), wo_ref[...], preferred_element_type=F32)


def _ffn(x2, gain, wi, wo, tm):
    nt = x2.shape[0]
    row = pl.BlockSpec((tm, D_MODEL), lambda i: (i, 0))
    return pl.pallas_call(
        _ffn_kernel,
        grid=(nt // tm,),
        in_specs=[row, _resident(gain.shape), _resident(wi.shape), _resident(wo.shape)],
        out_specs=row,
        out_shape=jax.ShapeDtypeStruct((nt, D_MODEL), F32),
        compiler_params=_params(1),
        name="ffn",
    )(x2, gain, wi, wo)


def _layer(x2, b, t, cos1, sin1, consts, w):
    hsum, hexp, ltri_attn, ltri_hgrn = consts
    qt, k, vt4, iq3t, ik3, iwt, hg5 = _mix_in(x2, w["norm_mix"], w["wa"], *w["wi"], w["wh"], w["qn"], w["kn"],
                                              cos1, sin1, hsum, hexp, w["hconst"], tm=ROW_TILE)
    y_attn = _dsa(w["logit_bound"], iq3t, ik3, iwt, qt, k, vt4, ltri_attn, b, t, topk=min(MAX_TOPK, t // 4))
    y_hgrn = _hgrn(hg5, w["hconst"], ltri_hgrn, b, t)
    x2 = _merge(x2, y_attn, y_hgrn, w["norm_mix"], w["wg"], w["wpa"], w["wph"], w["wout"], tm=ROW_TILE)
    return _ffn(x2, w["norm_ffn"], w["wffn_in"], w["wffn_out"], tm=ROW_TILE)


IN_WIDTHS = (ATTN_WIDTH, ATTN_WIDTH, ATTN_WIDTH, IDX_HEADS * IDX_DIM, IDX_DIM, IDX_HEADS,
             HG_WIDTH, HG_WIDTH, HG_WIDTH, HG_WIDTH, D_MODEL, D_MODEL)
IN_OFF = tuple(int(v) for v in np.concatenate([[0], np.cumsum(IN_WIDTHS)]))
IDX_COLS = IN_OFF[6] - IN_OFF[3]
IDX_COLS_PADDED = 3 * LANES
PACK_ROWS = 256


def _pack_w_in_kernel(w_ref, wa_ref, wi_hi_ref, wi_lo_ref, wh_ref, wg_ref):
    w = w_ref[0]
    wa_ref[0] = w[:, IN_OFF[0]:IN_OFF[3]].astype(BF16)
    wi = w[:, IN_OFF[3]:IN_OFF[3] + IDX_COLS_PADDED]
    wi = jnp.where(lax.broadcasted_iota(jnp.int32, wi.shape, 1) < IDX_COLS, wi, 0.0)
    wi_hi = wi.astype(BF16)
    wi_hi_ref[0] = wi_hi
    wi_lo_ref[0] = (wi - wi_hi.astype(F32)).astype(BF16)
    wh_ref[0] = w[:, IN_OFF[6]:IN_OFF[10]].astype(BF16)
    wg_ref[0] = w[:, IN_OFF[10]:IN_OFF[12]].astype(BF16)


def _pack_w_in(w_in):
    depth, d, n = w_in.shape
    cols = (IN_OFF[3] - IN_OFF[0], IDX_COLS_PADDED, IDX_COLS_PADDED, IN_OFF[10] - IN_OFF[6], IN_OFF[12] - IN_OFF[10])
    return pl.pallas_call(
        _pack_w_in_kernel,
        grid=(depth, d // PACK_ROWS),
        in_specs=[pl.BlockSpec((1, PACK_ROWS, n), lambda l, r: (l, r, 0))],
        out_specs=[pl.BlockSpec((1, PACK_ROWS, c), lambda l, r: (l, r, 0)) for c in cols],
        out_shape=[jax.ShapeDtypeStruct((depth, d, c), BF16) for c in cols],
        compiler_params=_params(2),
        name="pack_w_in",
    )(w_in)


def _constants():
    head_of_lane = np.arange(ATTN_WIDTH) // HEAD_DIM
    hsum = (head_of_lane[:, None] == np.arange(LANES)[None, :]).astype(np.float32)
    hexp = hsum.T.copy()
    r = np.arange(KEY_BLOCK)
    ltri_attn = (r[None, :] < r[:, None]).astype(np.float32)
    r = np.arange(HG_CHUNK)
    ltri_hgrn = (r[None, :] <= r[:, None]).astype(np.float32)
    return jnp.asarray(hsum, BF16), jnp.asarray(hexp, BF16), jnp.asarray(ltri_attn, BF16), jnp.asarray(ltri_hgrn, BF16)


def kernel(x, positions, w_in, w_proj_attn, w_proj_hgrn, w_out, norm_mix, norm_ffn, q_norm, k_norm, hgrn_norm,
           hgrn_lower_bound, w_ffn_in, w_ffn_out):
    b, t, d = x.shape
    depth = w_in.shape[0]
    nt = b * t

    inv = ROPE_THETA ** (-jnp.arange(0, HEAD_DIM, 2, dtype=F32) / HEAD_DIM)
    ang = positions.astype(F32)[..., None] * inv
    cos, sin = jnp.cos(ang), jnp.sin(ang)
    cos1 = jnp.concatenate([cos, cos, cos, cos], axis=-1).reshape(nt, LANES)
    sin1 = jnp.concatenate([-sin, sin, -sin, sin], axis=-1).reshape(nt, LANES)

    lb_all = jnp.cumsum(jax.nn.softmax(hgrn_lower_bound.astype(F32), axis=0), axis=0)
    lb_all = lb_all - lb_all[:1]

    consts = _constants()
    wa_all, wi_hi_all, wi_lo_all, wh_all, wg_all = _pack_w_in(w_in)

    x2 = x.reshape(nt, d)
    for l in range(depth):
        lb = lb_all[l]
        hconst = jnp.zeros((SUBLANES, HG_WIDTH), F32)
        hconst = hconst.at[0].set(jnp.log(lb)).at[1].set(jnp.log1p(-lb)).at[2].set(1.0 - lb)
        hconst = hconst.at[3].set(jnp.tile(hgrn_norm[l], HG_HEADS))
        w = {
            "norm_mix": norm_mix[l].reshape(1, d),
            "norm_ffn": norm_ffn[l].reshape(1, d),
            "wa": wa_all[l],
            "wi": (wi_hi_all[l], wi_lo_all[l]),
            "wh": wh_all[l],
            "wg": wg_all[l],
            "qn": jnp.tile(q_norm[l], ATTN_HEADS).reshape(1, ATTN_WIDTH),
            "kn": jnp.tile(k_norm[l], ATTN_HEADS).reshape(1, ATTN_WIDTH),
            "logit_bound": (HEAD_DIM ** 0.5 * LOG2_E * jnp.max(jnp.abs(q_norm[l])) * jnp.max(jnp.abs(k_norm[l]))
                            ).reshape(1).astype(F32),
            "hconst": hconst,
            "wpa": w_proj_attn[l].astype(BF16),
            "wph": w_proj_hgrn[l].astype(BF16),
            "wout": w_out[l].astype(BF16),
            "wffn_in": w_ffn_in[l].astype(BF16),
            "wffn_out": w_ffn_out[l].astype(BF16),
        }
        x2 = _layer(x2, b, t, cos1, sin1, consts, w)
    return x2.reshape(b, t, d)
```

```python
import functools

import numpy as np
import jax
import jax.numpy as jnp
from jax import lax
from jax.experimental import pallas as pl
from jax.experimental.pallas import tpu as pltpu

D_MODEL = 1024
ATTN_HEADS = 8
HEAD_DIM = 64
ATTN_WIDTH = ATTN_HEADS * HEAD_DIM
IDX_HEADS = 4
IDX_DIM = 64
MAX_TOPK = 256
HG_HEADS = 4
HG_DIM = 128
HG_WIDTH = HG_HEADS * HG_DIM
HG_CHUNK = 64
HG_STREAMS = 5
HG_STEP_CHUNKS = 4
HG_SUB = 16
HG_FACT = 32
MAX_SAFE_LOGIT = 60.0
MAX_SAFE_EXPONENT = 80.0
D_FF = 2816
ROPE_THETA = 10000.0
EPS = 1e-6
LOG2_E = 1.4426950408889634

LANES = 128
SUBLANES = 8
ROW_TILE = 512
VMEM_LIMIT_BYTES = 56 * 1024 * 1024

KEY_BLOCK = 256
VT_ROWS = 80
INT_MIN = -(2 ** 31)
HALF_MIN = -(2 ** 15)
NEG_BIG = -1e30

F32 = jnp.float32
BF16 = jnp.bfloat16


def _resident(shape):
    nd = len(shape)
    return pl.BlockSpec(shape, lambda *_: (0,) * nd, pipeline_mode=pl.Buffered(1))


def _layer_of(stacked, l):
    return pl.BlockSpec((None,) + stacked.shape[1:], lambda *_: (l, 0, 0), pipeline_mode=pl.Buffered(1))


def _params(n_axes):
    return pltpu.CompilerParams(dimension_semantics=("arbitrary",) * n_axes,
                                vmem_limit_bytes=VMEM_LIMIT_BYTES)


def _rmsnorm_rows(x, gain):
    return x * lax.rsqrt(jnp.mean(x * x, axis=-1, keepdims=True) + EPS) * gain


def _sigmoid(x):
    return 1.0 / (1.0 + jnp.exp(-x))


def _rope(x, cos, sin_signed):
    w = x.shape[-1]
    lane = lax.broadcasted_iota(jnp.int32, x.shape, 1)
    first_half = (lane & (HEAD_DIM - 1)) < HEAD_DIM // 2
    partner = jnp.where(first_half, pltpu.roll(x, w - HEAD_DIM // 2, 1), pltpu.roll(x, HEAD_DIM // 2, 1))
    return x * cos + partner * sin_signed


def _dot_hi_lo(a, b_hi, b_lo=None):
    a_hi = a.astype(BF16)
    a_lo = (a - a_hi.astype(F32)).astype(BF16)
    out = jnp.dot(a_hi, b_hi, preferred_element_type=F32) + jnp.dot(a_lo, b_hi, preferred_element_type=F32)
    if b_lo is not None:
        out = out + jnp.dot(a_hi, b_lo, preferred_element_type=F32)
    return out


def _dot_exact_lhs(a, b):
    b1 = b.astype(BF16)
    r1 = b - b1.astype(F32)
    b2 = r1.astype(BF16)
    b3 = (r1 - b2.astype(F32)).astype(BF16)
    return (jnp.dot(a, b1, preferred_element_type=F32) + jnp.dot(a, b2, preferred_element_type=F32)
            + jnp.dot(a, b3, preferred_element_type=F32))


def _mix_in_kernel(x_ref, g_ref, wa_ref, wi_ref, wil_ref, wh_ref, qn_ref, kn_ref, cos_ref, sin_ref, hsum_ref, hexp_ref,
                   hc_ref, qt_ref, k_ref, vt_ref, iq3_ref, ik3_ref, iw_ref, hg_ref):
    h = _rmsnorm_rows(x_ref[...], g_ref[...])
    hb = h.astype(BF16)
    cos1 = cos_ref[...]
    sin1 = sin_ref[...]
    cos4 = jnp.concatenate([cos1] * 4, axis=1)
    sin4 = jnp.concatenate([sin1] * 4, axis=1)

    def head_norm(a, gain):
        ss = _dot_hi_lo(a * a, hsum_ref[...])
        r = lax.rsqrt(ss * (1.0 / HEAD_DIM) + EPS)
        return a * _dot_hi_lo(r, hexp_ref[...]) * gain

    pa = jnp.dot(hb, wa_ref[...], preferred_element_type=F32)
    q = _rope(head_norm(pa[:, :ATTN_WIDTH], qn_ref[...]), cos4, sin4) * (HEAD_DIM ** -0.5 * LOG2_E)
    k = _rope(head_norm(pa[:, ATTN_WIDTH:2 * ATTN_WIDTH], kn_ref[...]), cos4, sin4)
    qt_ref[...] = q.T.astype(BF16)
    k_ref[...] = k.astype(BF16)
    v = pa[:, 2 * ATTN_WIDTH:]
    ones_pad = jnp.where(lax.broadcasted_iota(jnp.int32, (VT_ROWS - HEAD_DIM, KEY_BLOCK), 0) == 0, 1.0, 0.0)
    for s in range(v.shape[0] // KEY_BLOCK):
        vt = v[KEY_BLOCK * s:KEY_BLOCK * (s + 1)].T
        rows = []
        for hd in range(ATTN_HEADS):
            rows += [vt[HEAD_DIM * hd:HEAD_DIM * (hd + 1)], ones_pad]
        vt_ref[s] = jnp.concatenate(rows, axis=0).astype(BF16)

    pi = _dot_hi_lo(h, wi_ref[...], wil_ref[...])
    nq = IDX_HEADS * IDX_DIM
    iq = _rope(pi[:, :nq], cos4[:, :nq], sin4[:, :nq]) * (IDX_DIM ** -0.5)
    slab = pi[:, nq:]
    ikw_t = _rope(slab, cos1, sin1).T
    iw_ref[...] = slab.T[IDX_DIM:IDX_DIM + SUBLANES] * (IDX_HEADS ** -0.5)

    def hi_lo(a):
        hi = a.astype(BF16).astype(F32)
        return hi, (a - hi).astype(BF16).astype(F32)

    k_hi, k_lo = hi_lo(ikw_t[:IDX_DIM])
    ik3_ref[...] = jnp.concatenate([k_hi, k_lo, k_hi, jnp.zeros_like(k_hi)], axis=0).T.astype(BF16)
    iq_t = iq.T
    rows = []
    for hd in range(IDX_HEADS):
        q_hi, q_lo = hi_lo(iq_t[IDX_DIM * hd:IDX_DIM * (hd + 1)])
        rows += [q_hi, q_hi, q_lo, jnp.zeros_like(q_hi)]
    iq3_ref[...] = jnp.concatenate(rows, axis=0).astype(BF16)

    ph = jnp.dot(hb, wh_ref[...], preferred_element_type=F32)
    w = HG_WIDTH
    hq, hf, hg = ph[:, :w], ph[:, w:2 * w], ph[:, 3 * w:]
    log_lb, log1m_lb, one_m_lb = hc_ref[0:1, :], hc_ref[1:2, :], hc_ref[2:3, :]
    c = log1m_lb + jnp.minimum(hf, 0.0) - jnp.log1p(jnp.exp(-jnp.abs(hf)))
    hg_ref[:, :w] = hq * _sigmoid(hq)
    hg_ref[:, w:2 * w] = jnp.maximum(log_lb, c) + jnp.log1p(jnp.exp(-jnp.abs(log_lb - c)))
    hg_ref[:, 2 * w:3 * w] = one_m_lb * _sigmoid(-hf)
    hg_ref[:, 3 * w:4 * w] = ph[:, 2 * w:3 * w]
    hg_ref[:, 4 * w:] = hg * _sigmoid(hg)


def _mix_in(x2, gain, wa, wi, wil, wh, qn, kn, cos1, sin1, hsum, hexp, hconst, l, tm):
    nt = x2.shape[0]
    kb = KEY_BLOCK
    row = lambda w: pl.BlockSpec((tm, w), lambda i: (i, 0))
    col = lambda w: pl.BlockSpec((w, tm), lambda i: (0, i))
    return pl.pallas_call(
        _mix_in_kernel,
        grid=(nt // tm,),
        in_specs=[row(D_MODEL), _resident(gain.shape), _layer_of(wa, l), _layer_of(wi, l), _layer_of(wil, l),
                  _layer_of(wh, l),
                  _resident(qn.shape), _resident(kn.shape), row(LANES), row(LANES),
                  _resident(hsum.shape), _resident(hexp.shape), _resident(hconst.shape)],
        out_specs=[col(ATTN_WIDTH), row(ATTN_WIDTH),
                   pl.BlockSpec((tm // kb, ATTN_HEADS * VT_ROWS, kb), lambda i: (i, 0, 0)),
                   col(IDX_HEADS * 4 * IDX_DIM), row(4 * IDX_DIM), col(SUBLANES), row(HG_STREAMS * HG_WIDTH)],
        out_shape=[jax.ShapeDtypeStruct((ATTN_WIDTH, nt), BF16), jax.ShapeDtypeStruct((nt, ATTN_WIDTH), BF16),
                   jax.ShapeDtypeStruct((nt // kb, ATTN_HEADS * VT_ROWS, kb), BF16),
                   jax.ShapeDtypeStruct((IDX_HEADS * 4 * IDX_DIM, nt), BF16),
                   jax.ShapeDtypeStruct((nt, 4 * IDX_DIM), BF16), jax.ShapeDtypeStruct((SUBLANES, nt), F32),
                   jax.ShapeDtypeStruct((nt, HG_STREAMS * HG_WIDTH), F32)],
        compiler_params=_params(1),
        name="mix_in",
    )(x2, gain, wa, wi, wil, wh, qn, kn, cos1, sin1, hsum, hexp, hconst)


def _dsa_kernel(bound_ref, iq3_ref, iq3_next_ref, ik3_ref, iw_ref, iw_next_ref, qt_ref, k_ref, vt_ref, ltri_ref,
                o_ref, hi2_ref, lo2_ref, qz_ref, lg_ref, m_ref, alpha_ref, acc_ref, *, topk):
    kb = KEY_BLOCK
    j = pl.program_id(1)
    nk = j + 1
    int_min = jnp.int32(INT_MIN)
    slot = j % 2
    hi_ref, lo_ref = hi2_ref.at[slot], lo2_ref.at[slot]
    hi_next_ref, lo_next_ref = hi2_ref.at[1 - slot], lo2_ref.at[1 - slot]

    zeros = jnp.zeros((HEAD_DIM, kb), BF16)
    for h in range(ATTN_HEADS):
        qh = qt_ref[HEAD_DIM * h:HEAD_DIM * (h + 1), :]
        qz_ref[h] = jnp.concatenate([qh, zeros] if h % 2 == 0 else [zeros, qh], axis=0)

    def score_head(c, h, next_block):
        q3, w = (iq3_next_ref, iw_next_ref) if next_block else (iq3_ref, iw_ref)
        rel = jnp.dot(ik3_ref[c], q3[4 * IDX_DIM * h:4 * IDX_DIM * (h + 1), :], preferred_element_type=F32)
        return w[h:h + 1, :] * jnp.maximum(rel, 0.0)

    def store_keys(c, score, next_block, diagonal):
        bits = pltpu.bitcast(score, jnp.int32)
        key = jnp.where(bits < 0, int_min - bits, bits)
        if diagonal:
            kpos = lax.broadcasted_iota(jnp.int32, (kb, kb), 0)
            qpos = lax.broadcasted_iota(jnp.int32, (kb, kb), 1)
            key = jnp.where(kpos <= qpos, key, int_min)
        hi_dst, lo_dst = (hi_next_ref, lo_next_ref) if next_block else (hi_ref, lo_ref)
        hi_dst[c] = lax.shift_right_arithmetic(key, 16).astype(jnp.int16)
        lo_dst[c] = (key ^ 0x8000).astype(jnp.int16)

    store_keys(j, sum(score_head(j, h, False) for h in range(IDX_HEADS)), False, True)

    one_b, zero_b = jnp.ones((), BF16), jnp.zeros((), BF16)
    one_i, zero_i = jnp.ones((), jnp.int16), jnp.zeros((), jnp.int16)
    rows16 = 16

    hi_ref[nk] = jnp.full((kb, kb), HALF_MIN, jnp.int16)
    lo_ref[nk] = jnp.full((kb, kb), HALF_MIN, jnp.int16)

    def count(hit_fn):
        def fold(hit):
            parts = [hit[rows16 * r:rows16 * (r + 1)] for r in range(kb // rows16)]
            while len(parts) > 1:
                parts = [a + b for a, b in zip(parts[::2], parts[1::2])]
            return parts[0]

        def body(i, cnt):
            return cnt + (fold(hit_fn(2 * i)) + fold(hit_fn(2 * i + 1)))

        cnt = lax.fori_loop(0, (nk + 1) // 2, body, jnp.zeros((rows16, kb), jnp.int16))
        return jnp.sum(cnt.astype(jnp.int32), axis=0, keepdims=True).astype(F32)

    def bisect_half(ref, target):
        def bit_step(i, carry):
            thr_u, n_above = carry
            cand_u = thr_u | lax.shift_left(jnp.int32(1), 15 - i)
            cand = (cand_u + HALF_MIN).astype(jnp.int16)
            cnt = count(lambda c: jnp.where(ref[c] >= cand, one_i, zero_i))
            accepted = cnt >= target
            return jnp.where(accepted, cand_u, thr_u), jnp.where(accepted, n_above, cnt)
        thr_u, n_above = lax.fori_loop(0, 16, bit_step, (jnp.zeros((1, kb), jnp.int32), jnp.zeros((1, kb), F32)))
        return thr_u + HALF_MIN, n_above

    thr_hi32, n_gt_hi = bisect_half(hi_ref, float(topk))
    thr_hi = thr_hi32.astype(jnp.int16)

    def keep_matching_low(c, carry):
        lo_ref[c] = jnp.where(hi_ref[c] == thr_hi, lo_ref[c], jnp.int16(HALF_MIN))
        return carry

    lax.fori_loop(0, nk, keep_matching_low, 0)
    thr_lo32, n_gt_lo = bisect_half(lo_ref, topk - n_gt_hi)
    thr_lo = thr_lo32.astype(jnp.int16)
    n_gt = n_gt_hi + n_gt_lo
    is_marker = (thr_hi32 == HALF_MIN) & (thr_lo32 == HALF_MIN)
    need = jnp.where(is_marker, 0.0, topk - n_gt)

    def bias_chunk(c, ties_before):
        hi, lo = hi_ref[c], lo_ref[c]
        above = jnp.where(hi > thr_hi, one_b, jnp.where(lo > thr_lo, one_b, zero_b)).astype(F32)
        eq_b = jnp.where(hi == thr_hi, jnp.where(lo == thr_lo, one_b, zero_b), zero_b)
        eq = eq_b.astype(F32)
        rank = jnp.dot(ltri_ref[...], eq_b, preferred_element_type=F32) + ties_before
        chosen = above + jnp.where(rank < need, eq, 0.0)
        bias = jnp.where(chosen > 0.5, 0.0, NEG_BIG).astype(BF16)
        hi_ref[c] = pltpu.bitcast(bias, jnp.int16)
        return ties_before + jnp.sum(eq, axis=0, keepdims=True)


    def attend(bounded):
        acc_ref[...] = jnp.zeros(acc_ref.shape, F32)
        if not bounded:
            m_ref[...] = jnp.full(m_ref.shape, NEG_BIG, F32)

        def logits_head(c, h, bias):
            kk = k_ref[c, :, LANES * (h // 2):LANES * (h // 2 + 1)]
            lgb = jnp.dot(kk, qz_ref[h], preferred_element_type=F32) + bias
            lg_ref[h] = lgb
            if not bounded:
                m_old = m_ref[h]
                m_new = jnp.maximum(m_old, jnp.max(lgb, axis=0, keepdims=True))
                alpha_ref[h] = jnp.exp2(m_old - m_new)
                m_ref[h] = m_new

        def softmax_pv_head(c, h):
            p = jnp.exp2(lg_ref[h] if bounded else lg_ref[h] - m_ref[h])
            pv = jnp.dot(vt_ref[c, VT_ROWS * h:VT_ROWS * (h + 1), :], p.astype(BF16),
                         preferred_element_type=F32)
            acc_ref[h] = (acc_ref[h] if bounded else alpha_ref[h] * acc_ref[h]) + pv

        def chunk_bias(c):
            return pltpu.bitcast(hi_ref[c], BF16).astype(F32)

        def next_block_scores(c, h, partial):
            if h % (ATTN_HEADS // IDX_HEADS) != ATTN_HEADS // IDX_HEADS - 1:
                return partial
            term = score_head(c, h // (ATTN_HEADS // IDX_HEADS), True)
            return term if partial is None else partial + term

        ties = bias_chunk(0, jnp.zeros((1, kb), F32))
        bias0 = chunk_bias(0)
        partial = None
        for h in range(ATTN_HEADS):
            logits_head(0, h, bias0)
            partial = next_block_scores(0, h, partial)
        store_keys(0, partial, True, False)
        ties = bias_chunk(1, ties)

        def pipelined(c, ties_before):
            bias = chunk_bias(c)
            ties_after = bias_chunk(c + 1, ties_before)
            partial = None
            for h in range(ATTN_HEADS):
                softmax_pv_head(c - 1, h)
                logits_head(c, h, bias)
                partial = next_block_scores(c, h, partial)
            store_keys(c, partial, True, False)
            return ties_after

        lax.fori_loop(1, nk, pipelined, ties)
        for h in range(ATTN_HEADS):
            softmax_pv_head(nk - 1, h)

    logits_bounded = bound_ref[0] < MAX_SAFE_LOGIT

    @pl.when(logits_bounded)
    def _():
        attend(True)

    @pl.when(jnp.logical_not(logits_bounded))
    def _():
        attend(False)

    out_t = [acc_ref[h][:HEAD_DIM] / acc_ref[h][HEAD_DIM:HEAD_DIM + 1] for h in range(ATTN_HEADS)]
    o_ref[...] = jnp.concatenate(out_t, axis=0).T.astype(o_ref.dtype)


def _dsa(logit_bound, iq3t, ik3, iwt, qt, k, vt4, ltri, b, t, topk):
    kb = KEY_BLOCK
    nkc = t // kb
    chunked = lambda a: a.reshape(b * nkc, kb, a.shape[-1])
    per_batch = lambda shape: pl.BlockSpec((nkc,) + shape, lambda i, j: (i, 0, 0), pipeline_mode=pl.Buffered(1))
    q_cols = lambda rows: pl.BlockSpec((rows, kb), lambda i, j: (0, i * nkc + j))
    next_q_cols = lambda rows: pl.BlockSpec((rows, kb), lambda i, j: (0, i * nkc + jnp.minimum(j + 1, nkc - 1)))
    return pl.pallas_call(
        functools.partial(_dsa_kernel, topk=topk),
        grid=(b, nkc),
        in_specs=[
            pl.BlockSpec(memory_space=pltpu.SMEM),
            q_cols(IDX_HEADS * 4 * IDX_DIM),
            next_q_cols(IDX_HEADS * 4 * IDX_DIM),
            per_batch((kb, 4 * IDX_DIM)),
            q_cols(SUBLANES),
            next_q_cols(SUBLANES),
            q_cols(ATTN_WIDTH),
            per_batch((kb, ATTN_WIDTH)),
            per_batch((ATTN_HEADS * VT_ROWS, kb)),
            _resident(ltri.shape),
        ],
        out_specs=pl.BlockSpec((kb, ATTN_WIDTH), lambda i, j: (i * nkc + j, 0)),
        out_shape=jax.ShapeDtypeStruct((b * t, ATTN_WIDTH), BF16),
        scratch_shapes=[
            pltpu.VMEM((2, nkc + 1, kb, kb), jnp.int16),
            pltpu.VMEM((2, nkc + 1, kb, kb), jnp.int16),
            pltpu.VMEM((ATTN_HEADS, LANES, kb), BF16),
            pltpu.VMEM((ATTN_HEADS, kb, kb), F32),
            pltpu.VMEM((ATTN_HEADS, 1, kb), F32),
            pltpu.VMEM((ATTN_HEADS, 1, kb), F32),
            pltpu.VMEM((ATTN_HEADS, VT_ROWS, kb), F32),
        ],
        compiler_params=_params(2),
        name="dsa",
    )(logit_bound, iq3t, iq3t, chunked(ik3), iwt, iwt, qt, chunked(k), vt4, ltri)


def _hgrn_kernel(q_ref, lf_ref, k_ref, v_ref, gate_ref, hc_ref, lt_ref, o_ref, st_ref):
    @pl.when(pl.program_id(0) == 0)
    def _():
        st_ref[...] = jnp.zeros(st_ref.shape, F32)

    units = [(bi, ci) for ci in range(HG_STEP_CHUNKS) for bi in range(q_ref.shape[0])]
    rows = lambda ci: slice(HG_CHUNK * ci, HG_CHUNK * (ci + 1))
    cums = {(bi, ci): _dot_exact_lhs(lt_ref[...], lf_ref[bi, rows(ci), :]) for bi, ci in units}

    drops = []
    for b_all in cums.values():
        ends = [b_all[HG_FACT * (i + 1) - 1:HG_FACT * (i + 1), :] for i in range(HG_CHUNK // HG_FACT)]
        drops += [(ends[i - 1] if i > 0 else 0.0) - ends[i] for i in range(len(ends))]
    factored_ok = jnp.max(functools.reduce(jnp.maximum, drops)) < MAX_SAFE_EXPONENT

    def run(factored):
        for bi, ci in units:
            _hgrn_chunk(q_ref[bi, rows(ci), :], cums[(bi, ci)], k_ref[bi, rows(ci), :], v_ref.at[bi, rows(ci), :],
                        gate_ref[bi, rows(ci), :], hc_ref[3:4, :], o_ref.at[bi, rows(ci), :], st_ref.at[bi], factored)

    @pl.when(factored_ok)
    def _():
        run(True)

    @pl.when(jnp.logical_not(factored_ok))
    def _():
        run(False)


def _hgrn_chunk(q_all, b_all, k_all, hi_ref, gate, out_gain, o_ref, st_ref, factored):
    c_len, sub = HG_CHUNK, HG_SUB
    b_last = b_all[c_len - 1:c_len, :]
    q_inter = q_all * jnp.exp(b_all)
    k_carry = k_all * jnp.exp(b_last - b_all)

    n_sub = c_len // sub
    trans_b = (((1,), (1,)), ((), ()))

    def block_start(b, i):
        return b[sub * i - 1:sub * i, :] if i > 0 else jnp.zeros_like(b[0:1, :])

    def intra_factored_all():
        fs = HG_FACT
        heads = [slice(HG_DIM * h, HG_DIM * (h + 1)) for h in range(HG_HEADS)]
        scores, intra = [], []
        for hs in heads:
            b, q, k = b_all[:, hs], q_all[:, hs], k_all[:, hs]
            for i in range(c_len // fs):
                rs, upto = slice(fs * i, fs * (i + 1)), slice(0, fs * (i + 1))
                b_ref = b[fs * i - 1:fs * i, :] if i > 0 else jnp.zeros_like(b[0:1, :])
                q_hat = (q[rs] * jnp.exp(b[rs] - b_ref)).astype(BF16)
                k_hat = (k[upto] * jnp.exp(b_ref - b[upto])).astype(BF16)
                scores.append(lax.dot_general(q_hat, k_hat, trans_b, preferred_element_type=F32))
        for h, hs in enumerate(heads):
            vb = hi_ref[:, hs].astype(BF16)
            blocks = []
            for i in range(c_len // fs):
                a = scores[h * (c_len // fs) + i]
                t_idx = fs * i + lax.broadcasted_iota(jnp.int32, a.shape, 0)
                s_idx = lax.broadcasted_iota(jnp.int32, a.shape, 1)
                a = jnp.where(s_idx <= t_idx, a, 0.0).astype(BF16)
                blocks.append(jnp.dot(a, vb[0:fs * (i + 1)], preferred_element_type=F32))
            intra.append(jnp.concatenate(blocks, axis=0))
        return intra

    def intra_direct(b, q, k, vb):
        row = lax.broadcasted_iota(jnp.int32, (sub, 1), 0)
        col = lax.broadcasted_iota(jnp.int32, (sub, sub), 1)
        blocks = []
        for i in range(n_sub):
            rs = slice(sub * i, sub * (i + 1))
            b_i, q_i, k_i = b[rs], q[rs], k[rs]
            diag = jnp.zeros((sub, sub), F32)
            for s in range(sub):
                decay = jnp.exp(jnp.where(row >= s, b_i - b_i[s:s + 1, :], -jnp.inf))
                a_col = jnp.sum(q_i * k_i[s:s + 1, :] * decay, axis=1, keepdims=True)
                diag = jnp.where(col == s, a_col, diag)
            o_i = jnp.dot(diag.astype(BF16), vb[rs], preferred_element_type=F32)
            if i > 0:
                prev = slice(0, sub * i)
                b_ref = block_start(b, i)
                q_hat = (q_i * jnp.exp(b_i - b_ref)).astype(BF16)
                k_hat = (k[prev] * jnp.exp(b_ref - b[prev])).astype(BF16)
                a_off = lax.dot_general(q_hat, k_hat, trans_b, preferred_element_type=F32)
                o_i = o_i + jnp.dot(a_off.astype(BF16), vb[prev], preferred_element_type=F32)
            blocks.append(o_i)
        return jnp.concatenate(blocks, axis=0)

    if factored:
        intra = intra_factored_all()
    else:
        intra = [intra_direct(b_all[:, hs], q_all[:, hs], k_all[:, hs], hi_ref[:, hs].astype(BF16))
                 for hs in (slice(HG_DIM * h, HG_DIM * (h + 1)) for h in range(HG_HEADS))]

    outs = []
    for h in range(HG_HEADS):
        hs = slice(HG_DIM * h, HG_DIM * (h + 1))
        v = hi_ref[:, hs]
        st = st_ref[h]
        inter = lax.dot_general(q_inter[:, hs].astype(BF16), st.astype(BF16), trans_b,
                                preferred_element_type=F32)
        o = inter + intra[h]
        st_ref[h] = jnp.exp(b_last[:, hs]) * st + jnp.dot(v.T.astype(BF16), k_carry[:, hs].astype(BF16),
                                                         preferred_element_type=F32)
        outs.append(_rmsnorm_rows(o, out_gain[:, hs]) * gate[:, hs])
    o_ref[...] = jnp.concatenate(outs, axis=1).astype(o_ref.dtype)


def _hgrn(hg5, hconst, ltri, b, t):
    rows = HG_CHUNK * HG_STEP_CHUNKS
    streams = hg5.reshape(b, t, HG_STREAMS * HG_WIDTH)
    col = lambda jcol: pl.BlockSpec((b, rows, HG_WIDTH), lambda s, jcol=jcol: (0, s, jcol))
    return pl.pallas_call(
        _hgrn_kernel,
        grid=(t // rows,),
        in_specs=[col(jcol) for jcol in range(HG_STREAMS)] + [_resident(hconst.shape), _resident(ltri.shape)],
        out_specs=pl.BlockSpec((b, rows, HG_WIDTH), lambda s: (0, s, 0)),
        out_shape=jax.ShapeDtypeStruct((b, t, HG_WIDTH), BF16),
        scratch_shapes=[pltpu.VMEM((b, HG_HEADS, HG_DIM, HG_DIM), F32)],
        compiler_params=_params(1),
        name="hgrn",
    )(*([streams] * HG_STREAMS), hconst, ltri).reshape(b * t, HG_WIDTH)


def _merge_kernel(x_ref, ya_ref, yh_ref, g_ref, wg_ref, wpa_ref, wph_ref, wo_ref, o_ref):
    x = x_ref[...]
    hb = _rmsnorm_rows(x, g_ref[...]).astype(BF16)
    gates = jnp.dot(hb, wg_ref[...], preferred_element_type=F32)
    pa = jnp.dot(ya_ref[...], wpa_ref[...], preferred_element_type=F32)
    ph = jnp.dot(yh_ref[...], wph_ref[...], preferred_element_type=F32)
    merged = _sigmoid(gates[:, :D_MODEL]) * pa + _sigmoid(gates[:, D_MODEL:]) * ph
    o_ref[...] = x + jnp.dot(merged.astype(BF16), wo_ref[...], preferred_element_type=F32)


def _merge(x2, ya, yh, gain, wg, wpa, wph, wo, l, tm):
    nt = x2.shape[0]
    row = lambda w: pl.BlockSpec((tm, w), lambda i: (i, 0))
    return pl.pallas_call(
        _merge_kernel,
        grid=(nt // tm,),
        in_specs=[row(D_MODEL), row(ATTN_WIDTH), row(HG_WIDTH), _resident(gain.shape), _layer_of(wg, l),
                  _layer_of(wpa, l), _layer_of(wph, l), _layer_of(wo, l)],
        out_specs=row(D_MODEL),
        out_shape=jax.ShapeDtypeStruct((nt, D_MODEL), F32),
        compiler_params=_params(1),
        name="merge",
    )(x2, ya, yh, gain, wg, wpa, wph, wo)


def _ffn_kernel(x_ref, g_ref, wi_ref, wo_ref, o_ref):
    x = x_ref[...]
    hb = _rmsnorm_rows(x, g_ref[...]).astype(BF16)
    gu = jnp.dot(hb, wi_ref[...], preferred_element_type=F32)
    g = gu[:, :D_FF]
    act = g * _sigmoid(g) * gu[:, D_FF:]
    o_ref[...] = x + jnp.dot(act.astype(BF16), wo_ref[...], preferred_element_type=F32)


def _ffn(x2, gain, wi, wo, l, tm):
    nt = x2.shape[0]
    row = pl.BlockSpec((tm, D_MODEL), lambda i: (i, 0))
    return pl.pallas_call(
        _ffn_kernel,
        grid=(nt // tm,),
        in_specs=[row, _resident(gain.shape), _layer_of(wi, l), _layer_of(wo, l)],
        out_specs=row,
        out_shape=jax.ShapeDtypeStruct((nt, D_MODEL), F32),
        compiler_params=_params(1),
        name="ffn",
    )(x2, gain, wi, wo)


def _layer(x2, b, t, cos1, sin1, consts, w, l):
    hsum, hexp, ltri_attn, ltri_hgrn = consts
    qt, k, vt4, iq3t, ik3, iwt, hg5 = _mix_in(x2, w["norm_mix"], w["wa"], *w["wi"], w["wh"], w["qn"], w["kn"],
                                              cos1, sin1, hsum, hexp, w["hconst"], l, tm=ROW_TILE)
    y_attn = _dsa(w["logit_bound"], iq3t, ik3, iwt, qt, k, vt4, ltri_attn, b, t, topk=min(MAX_TOPK, t // 4))
    y_hgrn = _hgrn(hg5, w["hconst"], ltri_hgrn, b, t)
    x2 = _merge(x2, y_attn, y_hgrn, w["norm_mix"], w["wg"], w["wpa"], w["wph"], w["wout"], l, tm=ROW_TILE)
    return _ffn(x2, w["norm_ffn"], w["wffn_in"], w["wffn_out"], l, tm=ROW_TILE)


IN_WIDTHS = (ATTN_WIDTH, ATTN_WIDTH, ATTN_WIDTH, IDX_HEADS * IDX_DIM, IDX_DIM, IDX_HEADS,
             HG_WIDTH, HG_WIDTH, HG_WIDTH, HG_WIDTH, D_MODEL, D_MODEL)
IN_OFF = tuple(int(v) for v in np.concatenate([[0], np.cumsum(IN_WIDTHS)]))
IDX_COLS = IN_OFF[6] - IN_OFF[3]
IDX_COLS_PADDED = 3 * LANES
PACK_ROWS = 256


def _pack_w_in_kernel(w_ref, wa_ref, wi_hi_ref, wi_lo_ref, wh_ref, wg_ref):
    w = w_ref[0]
    wa_ref[0] = w[:, IN_OFF[0]:IN_OFF[3]].astype(BF16)
    wi = w[:, IN_OFF[3]:IN_OFF[3] + IDX_COLS_PADDED]
    wi = jnp.where(lax.broadcasted_iota(jnp.int32, wi.shape, 1) < IDX_COLS, wi, 0.0)
    wi_hi = wi.astype(BF16)
    wi_hi_ref[0] = wi_hi
    wi_lo_ref[0] = (wi - wi_hi.astype(F32)).astype(BF16)
    wh_ref[0] = w[:, IN_OFF[6]:IN_OFF[10]].astype(BF16)
    wg_ref[0] = w[:, IN_OFF[10]:IN_OFF[12]].astype(BF16)


def _pack_w_in(w_in):
    depth, d, n = w_in.shape
    cols = (IN_OFF[3] - IN_OFF[0], IDX_COLS_PADDED, IDX_COLS_PADDED, IN_OFF[10] - IN_OFF[6], IN_OFF[12] - IN_OFF[10])
    return pl.pallas_call(
        _pack_w_in_kernel,
        grid=(depth, d // PACK_ROWS),
        in_specs=[pl.BlockSpec((1, PACK_ROWS, n), lambda l, r: (l, r, 0))],
        out_specs=[pl.BlockSpec((1, PACK_ROWS, c), lambda l, r: (l, r, 0)) for c in cols],
        out_shape=[jax.ShapeDtypeStruct((depth, d, c), BF16) for c in cols],
        compiler_params=_params(2),
        name="pack_w_in",
    )(w_in)


def _constants():
    head_of_lane = np.arange(ATTN_WIDTH) // HEAD_DIM
    hsum = (head_of_lane[:, None] == np.arange(LANES)[None, :]).astype(np.float32)
    hexp = hsum.T.copy()
    r = np.arange(KEY_BLOCK)
    ltri_attn = (r[None, :] < r[:, None]).astype(np.float32)
    r = np.arange(HG_CHUNK)
    ltri_hgrn = (r[None, :] <= r[:, None]).astype(np.float32)
    return jnp.asarray(hsum, BF16), jnp.asarray(hexp, BF16), jnp.asarray(ltri_attn, BF16), jnp.asarray(ltri_hgrn, BF16)


def kernel(x, positions, w_in, w_proj_attn, w_proj_hgrn, w_out, norm_mix, norm_ffn, q_norm, k_norm, hgrn_norm,
           hgrn_lower_bound, w_ffn_in, w_ffn_out):
    b, t, d = x.shape
    depth = w_in.shape[0]
    nt = b * t

    inv = ROPE_THETA ** (-jnp.arange(0, HEAD_DIM, 2, dtype=F32) / HEAD_DIM)
    ang = positions.astype(F32)[..., None] * inv
    cos, sin = jnp.cos(ang), jnp.sin(ang)
    cos1 = jnp.concatenate([cos, cos, cos, cos], axis=-1).reshape(nt, LANES)
    sin1 = jnp.concatenate([-sin, sin, -sin, sin], axis=-1).reshape(nt, LANES)

    lb_all = jnp.cumsum(jax.nn.softmax(hgrn_lower_bound.astype(F32), axis=0), axis=0)
    lb_all = lb_all - lb_all[:1]

    consts = _constants()
    wa_all, wi_hi_all, wi_lo_all, wh_all, wg_all = _pack_w_in(w_in)
    wpa_all, wph_all, wout_all = w_proj_attn.astype(BF16), w_proj_hgrn.astype(BF16), w_out.astype(BF16)
    wffn_in_all, wffn_out_all = w_ffn_in.astype(BF16), w_ffn_out.astype(BF16)

    x2 = x.reshape(nt, d)
    for l in range(depth):
        lb = lb_all[l]
        hconst = jnp.zeros((SUBLANES, HG_WIDTH), F32)
        hconst = hconst.at[0].set(jnp.log(lb)).at[1].set(jnp.log1p(-lb)).at[2].set(1.0 - lb)
        hconst = hconst.at[3].set(jnp.tile(hgrn_norm[l], HG_HEADS))
        w = {
            "norm_mix": norm_mix[l].reshape(1, d),
            "norm_ffn": norm_ffn[l].reshape(1, d),
            "wa": wa_all,
            "wi": (wi_hi_all, wi_lo_all),
            "wh": wh_all,
            "wg": wg_all,
            "qn": jnp.tile(q_norm[l], ATTN_HEADS).reshape(1, ATTN_WIDTH),
            "kn": jnp.tile(k_norm[l], ATTN_HEADS).reshape(1, ATTN_WIDTH),
            "logit_bound": (HEAD_DIM ** 0.5 * LOG2_E * jnp.max(jnp.abs(q_norm[l])) * jnp.max(jnp.abs(k_norm[l]))
                            ).reshape(1).astype(F32),
            "hconst": hconst,
            "wpa": wpa_all,
            "wph": wph_all,
            "wout": wout_all,
            "wffn_in": wffn_in_all,
            "wffn_out": wffn_out_all,
        }
        x2 = _layer(x2, b, t, cos1, sin1, consts, w, l)
    return x2.reshape(b, t, d)
```

```python
import functools

import numpy as np
import jax
import jax.numpy as jnp
from jax import lax
from jax.experimental import pallas as pl
from jax.experimental.pallas import tpu as pltpu

D_MODEL = 1024
ATTN_HEADS = 8
HEAD_DIM = 64
ATTN_WIDTH = ATTN_HEADS * HEAD_DIM
IDX_HEADS = 4
IDX_DIM = 64
MAX_TOPK = 256
HG_HEADS = 4
HG_DIM = 128
HG_WIDTH = HG_HEADS * HG_DIM
HG_CHUNK = 64
HG_STREAMS = 5
HG_STEP_CHUNKS = 4
HG_SUB = 16
HG_FACT = 32
MAX_SAFE_LOGIT = 60.0
MAX_SAFE_EXPONENT = 80.0
D_FF = 2816
ROPE_THETA = 10000.0
EPS = 1e-6
LOG2_E = 1.4426950408889634

LANES = 128
SUBLANES = 8
ROW_TILE = 512
VMEM_LIMIT_BYTES = 56 * 1024 * 1024

KEY_BLOCK = 256
VT_ROWS = 80
INT_MIN = -(2 ** 31)
HALF_MIN = -(2 ** 15)
NEG_BIG = -1e30

F32 = jnp.float32
BF16 = jnp.bfloat16


def _resident(shape):
    nd = len(shape)
    return pl.BlockSpec(shape, lambda *_: (0,) * nd, pipeline_mode=pl.Buffered(1))


def _layer_of(stacked, l):
    return pl.BlockSpec((None,) + stacked.shape[1:], lambda *_: (l, 0, 0), pipeline_mode=pl.Buffered(1))


def _params(n_axes):
    return pltpu.CompilerParams(dimension_semantics=("arbitrary",) * n_axes,
                                vmem_limit_bytes=VMEM_LIMIT_BYTES)


def _rmsnorm_rows(x, gain):
    return x * lax.rsqrt(jnp.mean(x * x, axis=-1, keepdims=True) + EPS) * gain


def _sigmoid(x):
    return 1.0 / (1.0 + jnp.exp(-x))


def _rope(x, cos, sin_signed):
    w = x.shape[-1]
    lane = lax.broadcasted_iota(jnp.int32, x.shape, 1)
    first_half = (lane & (HEAD_DIM - 1)) < HEAD_DIM // 2
    partner = jnp.where(first_half, pltpu.roll(x, w - HEAD_DIM // 2, 1), pltpu.roll(x, HEAD_DIM // 2, 1))
    return x * cos + partner * sin_signed


_TRANS_B = (((1,), (1,)), ((), ()))


def _dot_hi_lo(a, b_hi, b_lo=None, trans_b=False):
    dims = _TRANS_B if trans_b else (((1,), (0,)), ((), ()))
    dot = lambda x, y: lax.dot_general(x, y, dims, preferred_element_type=F32)
    a_hi = a.astype(BF16)
    a_lo = (a - a_hi.astype(F32)).astype(BF16)
    out = dot(a_hi, b_hi) + dot(a_lo, b_hi)
    if b_lo is not None:
        out = out + dot(a_hi, b_lo)
    return out


def _dot_exact_lhs(a, b):
    b1 = b.astype(BF16)
    r1 = b - b1.astype(F32)
    b2 = r1.astype(BF16)
    b3 = (r1 - b2.astype(F32)).astype(BF16)
    return (jnp.dot(a, b1, preferred_element_type=F32) + jnp.dot(a, b2, preferred_element_type=F32)
            + jnp.dot(a, b3, preferred_element_type=F32))


def _mix_in_kernel(x_ref, g_ref, wa_ref, wi_ref, wil_ref, wh_ref, qn_ref, kn_ref, cos_ref, sin_ref, hsum_ref, hexp_ref,
                   hc_ref, qt_ref, k_ref, vt_ref, iq3_ref, ik3_ref, iw_ref, hg_ref):
    h = _rmsnorm_rows(x_ref[...], g_ref[...])
    hb = h.astype(BF16)
    cos1 = cos_ref[...]
    sin1 = sin_ref[...]
    cos4 = jnp.concatenate([cos1] * 4, axis=1)
    sin4 = jnp.concatenate([sin1] * 4, axis=1)

    def head_norm(a, gain):
        ss = _dot_hi_lo(a * a, hsum_ref[...])
        r = lax.rsqrt(ss * (1.0 / HEAD_DIM) + EPS)
        return a * _dot_hi_lo(r, hexp_ref[...]) * gain

    pa = lax.dot_general(hb, wa_ref[...], _TRANS_B, preferred_element_type=F32)
    q = _rope(head_norm(pa[:, :ATTN_WIDTH], qn_ref[...]), cos4, sin4) * (HEAD_DIM ** -0.5 * LOG2_E)
    k = _rope(head_norm(pa[:, ATTN_WIDTH:2 * ATTN_WIDTH], kn_ref[...]), cos4, sin4)
    qt_ref[...] = q.T.astype(BF16)
    k_ref[...] = k.astype(BF16)
    v = pa[:, 2 * ATTN_WIDTH:]
    ones_pad = jnp.where(lax.broadcasted_iota(jnp.int32, (VT_ROWS - HEAD_DIM, KEY_BLOCK), 0) == 0, 1.0, 0.0)
    for s in range(v.shape[0] // KEY_BLOCK):
        vt = v[KEY_BLOCK * s:KEY_BLOCK * (s + 1)].T
        rows = []
        for hd in range(ATTN_HEADS):
            rows += [vt[HEAD_DIM * hd:HEAD_DIM * (hd + 1)], ones_pad]
        vt_ref[s] = jnp.concatenate(rows, axis=0).astype(BF16)

    pi = _dot_hi_lo(h, wi_ref[...], wil_ref[...], trans_b=True)
    nq = IDX_HEADS * IDX_DIM
    iq = _rope(pi[:, :nq], cos4[:, :nq], sin4[:, :nq]) * (IDX_DIM ** -0.5)
    slab = pi[:, nq:]
    ikw_t = _rope(slab, cos1, sin1).T
    iw_ref[...] = slab.T[IDX_DIM:IDX_DIM + SUBLANES] * (IDX_HEADS ** -0.5)

    def hi_lo(a):
        hi = a.astype(BF16).astype(F32)
        return hi, (a - hi).astype(BF16).astype(F32)

    k_hi, k_lo = hi_lo(ikw_t[:IDX_DIM])
    ik3_ref[...] = jnp.concatenate([k_hi, k_lo, k_hi, jnp.zeros_like(k_hi)], axis=0).T.astype(BF16)
    iq_t = iq.T
    rows = []
    for hd in range(IDX_HEADS):
        q_hi, q_lo = hi_lo(iq_t[IDX_DIM * hd:IDX_DIM * (hd + 1)])
        rows += [q_hi, q_hi, q_lo, jnp.zeros_like(q_hi)]
    iq3_ref[...] = jnp.concatenate(rows, axis=0).astype(BF16)

    ph = lax.dot_general(hb, wh_ref[...], _TRANS_B, preferred_element_type=F32)
    w = HG_WIDTH
    hq, hf, hg = ph[:, :w], ph[:, w:2 * w], ph[:, 3 * w:]
    log_lb, log1m_lb, one_m_lb = hc_ref[0:1, :], hc_ref[1:2, :], hc_ref[2:3, :]
    c = log1m_lb + jnp.minimum(hf, 0.0) - jnp.log1p(jnp.exp(-jnp.abs(hf)))
    hg_ref[:, :w] = hq * _sigmoid(hq)
    hg_ref[:, w:2 * w] = jnp.maximum(log_lb, c) + jnp.log1p(jnp.exp(-jnp.abs(log_lb - c)))
    hg_ref[:, 2 * w:3 * w] = one_m_lb * _sigmoid(-hf)
    hg_ref[:, 3 * w:4 * w] = ph[:, 2 * w:3 * w]
    hg_ref[:, 4 * w:] = hg * _sigmoid(hg)


def _mix_in(x2, gain, wa, wi, wil, wh, qn, kn, cos1, sin1, hsum, hexp, hconst, l, tm):
    nt = x2.shape[0]
    kb = KEY_BLOCK
    row = lambda w: pl.BlockSpec((tm, w), lambda i: (i, 0))
    col = lambda w: pl.BlockSpec((w, tm), lambda i: (0, i))
    return pl.pallas_call(
        _mix_in_kernel,
        grid=(nt // tm,),
        in_specs=[row(D_MODEL), _resident(gain.shape), _layer_of(wa, l), _layer_of(wi, l), _layer_of(wil, l),
                  _layer_of(wh, l),
                  _resident(qn.shape), _resident(kn.shape), row(LANES), row(LANES),
                  _resident(hsum.shape), _resident(hexp.shape), _resident(hconst.shape)],
        out_specs=[col(ATTN_WIDTH), row(ATTN_WIDTH),
                   pl.BlockSpec((tm // kb, ATTN_HEADS * VT_ROWS, kb), lambda i: (i, 0, 0)),
                   col(IDX_HEADS * 4 * IDX_DIM), row(4 * IDX_DIM), col(SUBLANES), row(HG_STREAMS * HG_WIDTH)],
        out_shape=[jax.ShapeDtypeStruct((ATTN_WIDTH, nt), BF16), jax.ShapeDtypeStruct((nt, ATTN_WIDTH), BF16),
                   jax.ShapeDtypeStruct((nt // kb, ATTN_HEADS * VT_ROWS, kb), BF16),
                   jax.ShapeDtypeStruct((IDX_HEADS * 4 * IDX_DIM, nt), BF16),
                   jax.ShapeDtypeStruct((nt, 4 * IDX_DIM), BF16), jax.ShapeDtypeStruct((SUBLANES, nt), F32),
                   jax.ShapeDtypeStruct((nt, HG_STREAMS * HG_WIDTH), F32)],
        compiler_params=_params(1),
        name="mix_in",
    )(x2, gain, wa, wi, wil, wh, qn, kn, cos1, sin1, hsum, hexp, hconst)


def _dsa_kernel(bound_ref, iq3_ref, iq3_next_ref, ik3_ref, iw_ref, iw_next_ref, qt_ref, k_ref, vt_ref, ltri_ref,
                o_ref, hi2_ref, lo2_ref, qz_ref, lg_ref, m_ref, alpha_ref, acc_ref, *, topk):
    kb = KEY_BLOCK
    j = pl.program_id(1)
    nk = j + 1
    int_min = jnp.int32(INT_MIN)
    slot = j % 2
    hi_ref, lo_ref = hi2_ref.at[slot], lo2_ref.at[slot]
    hi_next_ref, lo_next_ref = hi2_ref.at[1 - slot], lo2_ref.at[1 - slot]

    zeros = jnp.zeros((HEAD_DIM, kb), BF16)
    for h in range(ATTN_HEADS):
        qh = qt_ref[HEAD_DIM * h:HEAD_DIM * (h + 1), :]
        qz_ref[h] = jnp.concatenate([qh, zeros] if h % 2 == 0 else [zeros, qh], axis=0)

    def score_head(c, h, next_block):
        q3, w = (iq3_next_ref, iw_next_ref) if next_block else (iq3_ref, iw_ref)
        rel = jnp.dot(ik3_ref[c], q3[4 * IDX_DIM * h:4 * IDX_DIM * (h + 1), :], preferred_element_type=F32)
        return w[h:h + 1, :] * jnp.maximum(rel, 0.0)

    def store_keys(c, score, next_block, diagonal):
        bits = pltpu.bitcast(score, jnp.int32)
        key = jnp.where(bits < 0, int_min - bits, bits)
        if diagonal:
            kpos = lax.broadcasted_iota(jnp.int32, (kb, kb), 0)
            qpos = lax.broadcasted_iota(jnp.int32, (kb, kb), 1)
            key = jnp.where(kpos <= qpos, key, int_min)
        hi_dst, lo_dst = (hi_next_ref, lo_next_ref) if next_block else (hi_ref, lo_ref)
        hi_dst[c] = lax.shift_right_arithmetic(key, 16).astype(jnp.int16)
        lo_dst[c] = (key ^ 0x8000).astype(jnp.int16)

    store_keys(j, sum(score_head(j, h, False) for h in range(IDX_HEADS)), False, True)

    one_b, zero_b = jnp.ones((), BF16), jnp.zeros((), BF16)
    one_i, zero_i = jnp.ones((), jnp.int16), jnp.zeros((), jnp.int16)
    rows16 = 16

    hi_ref[nk] = jnp.full((kb, kb), HALF_MIN, jnp.int16)
    lo_ref[nk] = jnp.full((kb, kb), HALF_MIN, jnp.int16)

    def count(hit_fn):
        def fold(hit):
            parts = [hit[rows16 * r:rows16 * (r + 1)] for r in range(kb // rows16)]
            while len(parts) > 1:
                parts = [a + b for a, b in zip(parts[::2], parts[1::2])]
            return parts[0]

        def body(i, cnt):
            return cnt + (fold(hit_fn(2 * i)) + fold(hit_fn(2 * i + 1)))

        cnt = lax.fori_loop(0, (nk + 1) // 2, body, jnp.zeros((rows16, kb), jnp.int16))
        return jnp.sum(cnt.astype(jnp.int32), axis=0, keepdims=True).astype(F32)

    def bisect_half(ref, target):
        def bit_step(i, carry):
            thr_u, n_above = carry
            cand_u = thr_u | lax.shift_left(jnp.int32(1), 15 - i)
            cand = (cand_u + HALF_MIN).astype(jnp.int16)
            cnt = count(lambda c: jnp.where(ref[c] >= cand, one_i, zero_i))
            accepted = cnt >= target
            return jnp.where(accepted, cand_u, thr_u), jnp.where(accepted, n_above, cnt)
        thr_u, n_above = lax.fori_loop(0, 16, bit_step, (jnp.zeros((1, kb), jnp.int32), jnp.zeros((1, kb), F32)))
        return thr_u + HALF_MIN, n_above

    thr_hi32, n_gt_hi = bisect_half(hi_ref, float(topk))
    thr_hi = thr_hi32.astype(jnp.int16)

    def keep_matching_low(c, carry):
        lo_ref[c] = jnp.where(hi_ref[c] == thr_hi, lo_ref[c], jnp.int16(HALF_MIN))
        return carry

    lax.fori_loop(0, nk, keep_matching_low, 0)
    thr_lo32, n_gt_lo = bisect_half(lo_ref, topk - n_gt_hi)
    thr_lo = thr_lo32.astype(jnp.int16)
    n_gt = n_gt_hi + n_gt_lo
    is_marker = (thr_hi32 == HALF_MIN) & (thr_lo32 == HALF_MIN)
    need = jnp.where(is_marker, 0.0, topk - n_gt)

    def bias_chunk(c, ties_before):
        hi, lo = hi_ref[c], lo_ref[c]
        above = jnp.where(hi > thr_hi, one_b, jnp.where(lo > thr_lo, one_b, zero_b)).astype(F32)
        eq_b = jnp.where(hi == thr_hi, jnp.where(lo == thr_lo, one_b, zero_b), zero_b)
        eq = eq_b.astype(F32)
        rank = jnp.dot(ltri_ref[...], eq_b, preferred_element_type=F32) + ties_before
        chosen = above + jnp.where(rank < need, eq, 0.0)
        bias = jnp.where(chosen > 0.5, 0.0, NEG_BIG).astype(BF16)
        hi_ref[c] = pltpu.bitcast(bias, jnp.int16)
        return ties_before + jnp.sum(eq, axis=0, keepdims=True)


    def attend(bounded):
        acc_ref[...] = jnp.zeros(acc_ref.shape, F32)
        if not bounded:
            m_ref[...] = jnp.full(m_ref.shape, NEG_BIG, F32)

        def logits_head(c, h, bias):
            kk = k_ref[c, :, LANES * (h // 2):LANES * (h // 2 + 1)]
            lgb = jnp.dot(kk, qz_ref[h], preferred_element_type=F32) + bias
            lg_ref[h] = lgb
            if not bounded:
                m_old = m_ref[h]
                m_new = jnp.maximum(m_old, jnp.max(lgb, axis=0, keepdims=True))
                alpha_ref[h] = jnp.exp2(m_old - m_new)
                m_ref[h] = m_new

        def softmax_pv_head(c, h):
            p = jnp.exp2(lg_ref[h] if bounded else lg_ref[h] - m_ref[h])
            pv = jnp.dot(vt_ref[c, VT_ROWS * h:VT_ROWS * (h + 1), :], p.astype(BF16),
                         preferred_element_type=F32)
            acc_ref[h] = (acc_ref[h] if bounded else alpha_ref[h] * acc_ref[h]) + pv

        def chunk_bias(c):
            return pltpu.bitcast(hi_ref[c], BF16).astype(F32)

        def next_block_scores(c, h, partial):
            if h % (ATTN_HEADS // IDX_HEADS) != ATTN_HEADS // IDX_HEADS - 1:
                return partial
            term = score_head(c, h // (ATTN_HEADS // IDX_HEADS), True)
            return term if partial is None else partial + term

        ties = bias_chunk(0, jnp.zeros((1, kb), F32))
        bias0 = chunk_bias(0)
        partial = None
        for h in range(ATTN_HEADS):
            logits_head(0, h, bias0)
            partial = next_block_scores(0, h, partial)
        store_keys(0, partial, True, False)
        ties = bias_chunk(1, ties)

        def pipelined(c, ties_before):
            bias = chunk_bias(c)
            ties_after = bias_chunk(c + 1, ties_before)
            partial = None
            for h in range(ATTN_HEADS):
                softmax_pv_head(c - 1, h)
                logits_head(c, h, bias)
                partial = next_block_scores(c, h, partial)
            store_keys(c, partial, True, False)
            return ties_after

        lax.fori_loop(1, nk, pipelined, ties)
        for h in range(ATTN_HEADS):
            softmax_pv_head(nk - 1, h)

    logits_bounded = bound_ref[0] < MAX_SAFE_LOGIT

    @pl.when(logits_bounded)
    def _():
        attend(True)

    @pl.when(jnp.logical_not(logits_bounded))
    def _():
        attend(False)

    out_t = [acc_ref[h][:HEAD_DIM] / acc_ref[h][HEAD_DIM:HEAD_DIM + 1] for h in range(ATTN_HEADS)]
    o_ref[...] = jnp.concatenate(out_t, axis=0).T.astype(o_ref.dtype)


def _dsa(logit_bound, iq3t, ik3, iwt, qt, k, vt4, ltri, b, t, topk):
    kb = KEY_BLOCK
    nkc = t // kb
    chunked = lambda a: a.reshape(b * nkc, kb, a.shape[-1])
    per_batch = lambda shape: pl.BlockSpec((nkc,) + shape, lambda i, j: (i, 0, 0), pipeline_mode=pl.Buffered(1))
    q_cols = lambda rows: pl.BlockSpec((rows, kb), lambda i, j: (0, i * nkc + j))
    next_q_cols = lambda rows: pl.BlockSpec((rows, kb), lambda i, j: (0, i * nkc + jnp.minimum(j + 1, nkc - 1)))
    return pl.pallas_call(
        functools.partial(_dsa_kernel, topk=topk),
        grid=(b, nkc),
        in_specs=[
            pl.BlockSpec(memory_space=pltpu.SMEM),
            q_cols(IDX_HEADS * 4 * IDX_DIM),
            next_q_cols(IDX_HEADS * 4 * IDX_DIM),
            per_batch((kb, 4 * IDX_DIM)),
            q_cols(SUBLANES),
            next_q_cols(SUBLANES),
            q_cols(ATTN_WIDTH),
            per_batch((kb, ATTN_WIDTH)),
            per_batch((ATTN_HEADS * VT_ROWS, kb)),
            _resident(ltri.shape),
        ],
        out_specs=pl.BlockSpec((kb, ATTN_WIDTH), lambda i, j: (i * nkc + j, 0)),
        out_shape=jax.ShapeDtypeStruct((b * t, ATTN_WIDTH), BF16),
        scratch_shapes=[
            pltpu.VMEM((2, nkc + 1, kb, kb), jnp.int16),
            pltpu.VMEM((2, nkc + 1, kb, kb), jnp.int16),
            pltpu.VMEM((ATTN_HEADS, LANES, kb), BF16),
            pltpu.VMEM((ATTN_HEADS, kb, kb), F32),
            pltpu.VMEM((ATTN_HEADS, 1, kb), F32),
            pltpu.VMEM((ATTN_HEADS, 1, kb), F32),
            pltpu.VMEM((ATTN_HEADS, VT_ROWS, kb), F32),
        ],
        compiler_params=_params(2),
        name="dsa",
    )(logit_bound, iq3t, iq3t, chunked(ik3), iwt, iwt, qt, chunked(k), vt4, ltri)


def _hgrn_kernel(q_ref, lf_ref, k_ref, v_ref, gate_ref, hc_ref, lt_ref, o_ref, st_ref):
    @pl.when(pl.program_id(0) == 0)
    def _():
        st_ref[...] = jnp.zeros(st_ref.shape, F32)

    units = [(bi, ci) for ci in range(HG_STEP_CHUNKS) for bi in range(q_ref.shape[0])]
    rows = lambda ci: slice(HG_CHUNK * ci, HG_CHUNK * (ci + 1))
    cums = {(bi, ci): _dot_exact_lhs(lt_ref[...], lf_ref[bi, rows(ci), :]) for bi, ci in units}

    drops = []
    for b_all in cums.values():
        ends = [b_all[HG_FACT * (i + 1) - 1:HG_FACT * (i + 1), :] for i in range(HG_CHUNK // HG_FACT)]
        drops += [(ends[i - 1] if i > 0 else 0.0) - ends[i] for i in range(len(ends))]
    factored_ok = jnp.max(functools.reduce(jnp.maximum, drops)) < MAX_SAFE_EXPONENT

    def run(factored):
        for bi, ci in units:
            _hgrn_chunk(q_ref[bi, rows(ci), :], cums[(bi, ci)], k_ref[bi, rows(ci), :], v_ref.at[bi, rows(ci), :],
                        gate_ref[bi, rows(ci), :], hc_ref[3:4, :], o_ref.at[bi, rows(ci), :], st_ref.at[bi], factored)

    @pl.when(factored_ok)
    def _():
        run(True)

    @pl.when(jnp.logical_not(factored_ok))
    def _():
        run(False)


def _hgrn_chunk(q_all, b_all, k_all, hi_ref, gate, out_gain, o_ref, st_ref, factored):
    c_len, sub = HG_CHUNK, HG_SUB
    b_last = b_all[c_len - 1:c_len, :]
    q_inter = q_all * jnp.exp(b_all)
    k_carry = k_all * jnp.exp(b_last - b_all)

    n_sub = c_len // sub
    trans_b = (((1,), (1,)), ((), ()))

    def block_start(b, i):
        return b[sub * i - 1:sub * i, :] if i > 0 else jnp.zeros_like(b[0:1, :])

    def intra_factored_all():
        fs = HG_FACT
        heads = [slice(HG_DIM * h, HG_DIM * (h + 1)) for h in range(HG_HEADS)]
        scores, intra = [], []
        for hs in heads:
            b, q, k = b_all[:, hs], q_all[:, hs], k_all[:, hs]
            for i in range(c_len // fs):
                rs, upto = slice(fs * i, fs * (i + 1)), slice(0, fs * (i + 1))
                b_ref = b[fs * i - 1:fs * i, :] if i > 0 else jnp.zeros_like(b[0:1, :])
                q_hat = (q[rs] * jnp.exp(b[rs] - b_ref)).astype(BF16)
                k_hat = (k[upto] * jnp.exp(b_ref - b[upto])).astype(BF16)
                scores.append(lax.dot_general(q_hat, k_hat, trans_b, preferred_element_type=F32))
        for h, hs in enumerate(heads):
            vb = hi_ref[:, hs].astype(BF16)
            blocks = []
            for i in range(c_len // fs):
                a = scores[h * (c_len // fs) + i]
                t_idx = fs * i + lax.broadcasted_iota(jnp.int32, a.shape, 0)
                s_idx = lax.broadcasted_iota(jnp.int32, a.shape, 1)
                a = jnp.where(s_idx <= t_idx, a, 0.0).astype(BF16)
                blocks.append(jnp.dot(a, vb[0:fs * (i + 1)], preferred_element_type=F32))
            intra.append(jnp.concatenate(blocks, axis=0))
        return intra

    def intra_direct(b, q, k, vb):
        row = lax.broadcasted_iota(jnp.int32, (sub, 1), 0)
        col = lax.broadcasted_iota(jnp.int32, (sub, sub), 1)
        blocks = []
        for i in range(n_sub):
            rs = slice(sub * i, sub * (i + 1))
            b_i, q_i, k_i = b[rs], q[rs], k[rs]
            diag = jnp.zeros((sub, sub), F32)
            for s in range(sub):
                decay = jnp.exp(jnp.where(row >= s, b_i - b_i[s:s + 1, :], -jnp.inf))
                a_col = jnp.sum(q_i * k_i[s:s + 1, :] * decay, axis=1, keepdims=True)
                diag = jnp.where(col == s, a_col, diag)
            o_i = jnp.dot(diag.astype(BF16), vb[rs], preferred_element_type=F32)
            if i > 0:
                prev = slice(0, sub * i)
                b_ref = block_start(b, i)
                q_hat = (q_i * jnp.exp(b_i - b_ref)).astype(BF16)
                k_hat = (k[prev] * jnp.exp(b_ref - b[prev])).astype(BF16)
                a_off = lax.dot_general(q_hat, k_hat, trans_b, preferred_element_type=F32)
                o_i = o_i + jnp.dot(a_off.astype(BF16), vb[prev], preferred_element_type=F32)
            blocks.append(o_i)
        return jnp.concatenate(blocks, axis=0)

    if factored:
        intra = intra_factored_all()
    else:
        intra = [intra_direct(b_all[:, hs], q_all[:, hs], k_all[:, hs], hi_ref[:, hs].astype(BF16))
                 for hs in (slice(HG_DIM * h, HG_DIM * (h + 1)) for h in range(HG_HEADS))]

    outs = []
    for h in range(HG_HEADS):
        hs = slice(HG_DIM * h, HG_DIM * (h + 1))
        v = hi_ref[:, hs]
        st = st_ref[h]
        inter = lax.dot_general(q_inter[:, hs].astype(BF16), st.astype(BF16), trans_b,
                                preferred_element_type=F32)
        o = inter + intra[h]
        st_ref[h] = jnp.exp(b_last[:, hs]) * st + jnp.dot(v.T.astype(BF16), k_carry[:, hs].astype(BF16),
                                                         preferred_element_type=F32)
        outs.append(_rmsnorm_rows(o, out_gain[:, hs]) * gate[:, hs])
    o_ref[...] = jnp.concatenate(outs, axis=1).astype(o_ref.dtype)


def _hgrn(hg5, hconst, ltri, b, t):
    rows = HG_CHUNK * HG_STEP_CHUNKS
    streams = hg5.reshape(b, t, HG_STREAMS * HG_WIDTH)
    col = lambda jcol: pl.BlockSpec((b, rows, HG_WIDTH), lambda s, jcol=jcol: (0, s, jcol))
    return pl.pallas_call(
        _hgrn_kernel,
        grid=(t // rows,),
        in_specs=[col(jcol) for jcol in range(HG_STREAMS)] + [_resident(hconst.shape), _resident(ltri.shape)],
        out_specs=pl.BlockSpec((b, rows, HG_WIDTH), lambda s: (0, s, 0)),
        out_shape=jax.ShapeDtypeStruct((b, t, HG_WIDTH), BF16),
        scratch_shapes=[pltpu.VMEM((b, HG_HEADS, HG_DIM, HG_DIM), F32)],
        compiler_params=_params(1),
        name="hgrn",
    )(*([streams] * HG_STREAMS), hconst, ltri).reshape(b * t, HG_WIDTH)


def _merge_kernel(x_ref, ya_ref, yh_ref, g_ref, wg_ref, wpa_ref, wph_ref, wo_ref, o_ref):
    x = x_ref[...]
    hb = _rmsnorm_rows(x, g_ref[...]).astype(BF16)
    gates = lax.dot_general(hb, wg_ref[...], _TRANS_B, preferred_element_type=F32)
    pa = jnp.dot(ya_ref[...], wpa_ref[...], preferred_element_type=F32)
    ph = jnp.dot(yh_ref[...], wph_ref[...], preferred_element_type=F32)
    merged = _sigmoid(gates[:, :D_MODEL]) * pa + _sigmoid(gates[:, D_MODEL:]) * ph
    o_ref[...] = x + jnp.dot(merged.astype(BF16), wo_ref[...], preferred_element_type=F32)


def _merge(x2, ya, yh, gain, wg, wpa, wph, wo, l, tm):
    nt = x2.shape[0]
    row = lambda w: pl.BlockSpec((tm, w), lambda i: (i, 0))
    return pl.pallas_call(
        _merge_kernel,
        grid=(nt // tm,),
        in_specs=[row(D_MODEL), row(ATTN_WIDTH), row(HG_WIDTH), _resident(gain.shape), _layer_of(wg, l),
                  _layer_of(wpa, l), _layer_of(wph, l), _layer_of(wo, l)],
        out_specs=row(D_MODEL),
        out_shape=jax.ShapeDtypeStruct((nt, D_MODEL), F32),
        compiler_params=_params(1),
        name="merge",
    )(x2, ya, yh, gain, wg, wpa, wph, wo)


def _ffn_kernel(x_ref, g_ref, wi_ref, wo_ref, o_ref):
    x = x_ref[...]
    hb = _rmsnorm_rows(x, g_ref[...]).astype(BF16)
    gu = jnp.dot(hb, wi_ref[...], preferred_element_type=F32)
    g = gu[:, :D_FF]
    act = g * _sigmoid(g) * gu[:, D_FF:]
    o_ref[...] = x + jnp.dot(act.astype(BF16), wo_ref[...], preferred_element_type=F32)


def _ffn(x2, gain, wi, wo, l, tm):
    nt = x2.shape[0]
    row = pl.BlockSpec((tm, D_MODEL), lambda i: (i, 0))
    return pl.pallas_call(
        _ffn_kernel,
        grid=(nt // tm,),
        in_specs=[row, _resident(gain.shape), _layer_of(wi, l), _layer_of(wo, l)],
        out_specs=row,
        out_shape=jax.ShapeDtypeStruct((nt, D_MODEL), F32),
        compiler_params=_params(1),
        name="ffn",
    )(x2, gain, wi, wo)


def _layer(x2, b, t, cos1, sin1, consts, w, l):
    hsum, hexp, ltri_attn, ltri_hgrn = consts
    qt, k, vt4, iq3t, ik3, iwt, hg5 = _mix_in(x2, w["norm_mix"], w["wa"], *w["wi"], w["wh"], w["qn"], w["kn"],
                                              cos1, sin1, hsum, hexp, w["hconst"], l, tm=ROW_TILE)
    y_attn = _dsa(w["logit_bound"], iq3t, ik3, iwt, qt, k, vt4, ltri_attn, b, t, topk=min(MAX_TOPK, t // 4))
    y_hgrn = _hgrn(hg5, w["hconst"], ltri_hgrn, b, t)
    x2 = _merge(x2, y_attn, y_hgrn, w["norm_mix"], w["wg"], w["wpa"], w["wph"], w["wout"], l, tm=ROW_TILE)
    return _ffn(x2, w["norm_ffn"], w["wffn_in"], w["wffn_out"], l, tm=ROW_TILE)


IN_WIDTHS = (ATTN_WIDTH, ATTN_WIDTH, ATTN_WIDTH, IDX_HEADS * IDX_DIM, IDX_DIM, IDX_HEADS,
             HG_WIDTH, HG_WIDTH, HG_WIDTH, HG_WIDTH, D_MODEL, D_MODEL)
IN_OFF = tuple(int(v) for v in np.concatenate([[0], np.cumsum(IN_WIDTHS)]))
IDX_COLS = IN_OFF[6] - IN_OFF[3]
IDX_COLS_PADDED = 3 * LANES
PACK_COLS = 256


def _pack_group_kernel(v_ref, *o_refs, depth, lane_tiles, real_cols):
    cb = o_refs[0].shape[1]
    stride = lane_tiles * depth
    for l in range(depth):
        wt = jnp.concatenate([v_ref[pl.ds(l + depth * j, cb, stride=stride), :] for j in range(lane_tiles)], axis=1)
        if real_cols < cb:
            wt = jnp.where(lax.broadcasted_iota(jnp.int32, wt.shape, 0) < real_cols, wt, 0.0)
        hi = wt.astype(BF16)
        o_refs[0][l] = hi
        if len(o_refs) > 1:
            o_refs[1][l] = (wt - hi.astype(F32)).astype(BF16)


def _pack_group(view, depth, d, first_col, n_cols, cols_per_step, real_cols=None, hi_lo=False):
    lane_tiles = d // LANES
    rows_per_col = lane_tiles * depth
    n_out = 2 if hi_lo else 1
    return pl.pallas_call(
        functools.partial(_pack_group_kernel, depth=depth, lane_tiles=lane_tiles,
                          real_cols=cols_per_step if real_cols is None else real_cols),
        grid=(n_cols // cols_per_step,),
        in_specs=[pl.BlockSpec((pl.Element(cols_per_step * rows_per_col), pl.Element(LANES)),
                               lambda i: ((first_col + i * cols_per_step) * rows_per_col, 0))],
        out_specs=[pl.BlockSpec((depth, cols_per_step, d), lambda i: (0, i, 0))] * n_out,
        out_shape=[jax.ShapeDtypeStruct((depth, n_cols, d), BF16)] * n_out,
        compiler_params=_params(1),
        name="pack_w_in",
    )(view)


def _pack_w_in(w_in):
    depth, d, n = w_in.shape
    lane_tiles = d // LANES
    view = w_in.transpose(2, 0, 1).reshape(n, depth, lane_tiles, LANES).transpose(0, 2, 1, 3).reshape(
        n * lane_tiles * depth, LANES)
    group = functools.partial(_pack_group, view, depth, d)
    wa, = group(IN_OFF[0], IN_OFF[3] - IN_OFF[0], PACK_COLS)
    wi_hi, wi_lo = group(IN_OFF[3], IDX_COLS_PADDED, IDX_COLS_PADDED, real_cols=IDX_COLS, hi_lo=True)
    wh, = group(IN_OFF[6], IN_OFF[10] - IN_OFF[6], PACK_COLS)
    wg, = group(IN_OFF[10], IN_OFF[12] - IN_OFF[10], PACK_COLS)
    return wa, wi_hi, wi_lo, wh, wg


def _constants():
    head_of_lane = np.arange(ATTN_WIDTH) // HEAD_DIM
    hsum = (head_of_lane[:, None] == np.arange(LANES)[None, :]).astype(np.float32)
    hexp = hsum.T.copy()
    r = np.arange(KEY_BLOCK)
    ltri_attn = (r[None, :] < r[:, None]).astype(np.float32)
    r = np.arange(HG_CHUNK)
    ltri_hgrn = (r[None, :] <= r[:, None]).astype(np.float32)
    return jnp.asarray(hsum, BF16), jnp.asarray(hexp, BF16), jnp.asarray(ltri_attn, BF16), jnp.asarray(ltri_hgrn, BF16)


def kernel(x, positions, w_in, w_proj_attn, w_proj_hgrn, w_out, norm_mix, norm_ffn, q_norm, k_norm, hgrn_norm,
           hgrn_lower_bound, w_ffn_in, w_ffn_out):
    b, t, d = x.shape
    depth = w_in.shape[0]
    nt = b * t

    inv = ROPE_THETA ** (-jnp.arange(0, HEAD_DIM, 2, dtype=F32) / HEAD_DIM)
    ang = positions.astype(F32)[..., None] * inv
    cos, sin = jnp.cos(ang), jnp.sin(ang)
    cos1 = jnp.concatenate([cos, cos, cos, cos], axis=-1).reshape(nt, LANES)
    sin1 = jnp.concatenate([-sin, sin, -sin, sin], axis=-1).reshape(nt, LANES)

    lb_all = jnp.cumsum(jax.nn.softmax(hgrn_lower_bound.astype(F32), axis=0), axis=0)
    lb_all = lb_all - lb_all[:1]

    consts = _constants()
    wa_all, wi_hi_all, wi_lo_all, wh_all, wg_all = _pack_w_in(w_in)
    wpa_all, wph_all, wout_all = w_proj_attn.astype(BF16), w_proj_hgrn.astype(BF16), w_out.astype(BF16)
    wffn_in_all, wffn_out_all = w_ffn_in.astype(BF16), w_ffn_out.astype(BF16)

    x2 = x.reshape(nt, d)
    for l in range(depth):
        lb = lb_all[l]
        hconst = jnp.zeros((SUBLANES, HG_WIDTH), F32)
        hconst = hconst.at[0].set(jnp.log(lb)).at[1].set(jnp.log1p(-lb)).at[2].set(1.0 - lb)
        hconst = hconst.at[3].set(jnp.tile(hgrn_norm[l], HG_HEADS))
        w = {
            "norm_mix": norm_mix[l].reshape(1, d),
            "norm_ffn": norm_ffn[l].reshape(1, d),
            "wa": wa_all,
            "wi": (wi_hi_all, wi_lo_all),
            "wh": wh_all,
            "wg": wg_all,
            "qn": jnp.tile(q_norm[l], ATTN_HEADS).reshape(1, ATTN_WIDTH),
            "kn": jnp.tile(k_norm[l], ATTN_HEADS).reshape(1, ATTN_WIDTH),
            "logit_bound": (HEAD_DIM ** 0.5 * LOG2_E * jnp.max(jnp.abs(q_norm[l])) * jnp.max(jnp.abs(k_norm[l]))
                            ).reshape(1).astype(F32),
            "hconst": hconst,
            "wpa": wpa_all,
            "wph": wph_all,
            "wout": wout_all,
            "wffn_in": wffn_in_all,
            "wffn_out": wffn_out_all,
        }
        x2 = _layer(x2, b, t, cos1, sin1, consts, w, l)
    return x2.reshape(b, t, d)
```

```python
import functools

import numpy as np
import jax
import jax.numpy as jnp
from jax import lax
from jax.experimental import pallas as pl
from jax.experimental.pallas import tpu as pltpu

D_MODEL = 1024
ATTN_HEADS = 8
HEAD_DIM = 64
ATTN_WIDTH = ATTN_HEADS * HEAD_DIM
IDX_HEADS = 4
IDX_DIM = 64
MAX_TOPK = 256
HG_HEADS = 4
HG_DIM = 128
HG_WIDTH = HG_HEADS * HG_DIM
HG_CHUNK = 64
HG_STREAMS = 5
HG_STEP_CHUNKS = 8
HG_SUB = 16
HG_FACT = 32
MAX_SAFE_LOGIT = 60.0
MAX_SAFE_EXPONENT = 80.0
D_FF = 2816
ROPE_THETA = 10000.0
EPS = 1e-6
LOG2_E = 1.4426950408889634

LANES = 128
SUBLANES = 8
ROW_TILE = 512
VMEM_LIMIT_BYTES = 56 * 1024 * 1024

KEY_BLOCK = 256
VT_ROWS = 80
INT_MIN = -(2 ** 31)
HALF_MIN = -(2 ** 15)
NEG_BIG = -1e30

F32 = jnp.float32
BF16 = jnp.bfloat16


def _resident(shape):
    nd = len(shape)
    return pl.BlockSpec(shape, lambda *_: (0,) * nd, pipeline_mode=pl.Buffered(1))


def _layer_of(stacked, l):
    return pl.BlockSpec((None,) + stacked.shape[1:], lambda *_: (l, 0, 0), pipeline_mode=pl.Buffered(1))


def _params(n_axes):
    return pltpu.CompilerParams(dimension_semantics=("arbitrary",) * n_axes,
                                vmem_limit_bytes=VMEM_LIMIT_BYTES)


def _rmsnorm_rows(x, gain):
    return x * lax.rsqrt(jnp.mean(x * x, axis=-1, keepdims=True) + EPS) * gain


def _sigmoid(x):
    return 1.0 / (1.0 + jnp.exp(-x))


def _rope(x, cos, sin_signed):
    w = x.shape[-1]
    lane = lax.broadcasted_iota(jnp.int32, x.shape, 1)
    first_half = (lane & (HEAD_DIM - 1)) < HEAD_DIM // 2
    partner = jnp.where(first_half, pltpu.roll(x, w - HEAD_DIM // 2, 1), pltpu.roll(x, HEAD_DIM // 2, 1))
    return x * cos + partner * sin_signed


_TRANS_B = (((1,), (1,)), ((), ()))


def _dot_hi_lo(a, b_hi, b_lo=None, trans_b=False):
    dims = _TRANS_B if trans_b else (((1,), (0,)), ((), ()))
    dot = lambda x, y: lax.dot_general(x, y, dims, preferred_element_type=F32)
    a_hi = a.astype(BF16)
    a_lo = (a - a_hi.astype(F32)).astype(BF16)
    out = dot(a_hi, b_hi) + dot(a_lo, b_hi)
    if b_lo is not None:
        out = out + dot(a_hi, b_lo)
    return out


def _dot_exact_lhs(a, b):
    b1 = b.astype(BF16)
    r1 = b - b1.astype(F32)
    b2 = r1.astype(BF16)
    b3 = (r1 - b2.astype(F32)).astype(BF16)
    return (jnp.dot(a, b1, preferred_element_type=F32) + jnp.dot(a, b2, preferred_element_type=F32)
            + jnp.dot(a, b3, preferred_element_type=F32))


def _mix_in_kernel(x_ref, g_ref, wa_ref, wi_ref, wil_ref, wh_ref, qn_ref, kn_ref, cos_ref, sin_ref, hsum_ref, hexp_ref,
                   hc_ref, qt_ref, k_ref, vt_ref, iq3_ref, ik3_ref, iw_ref, hg_ref):
    h = _rmsnorm_rows(x_ref[...], g_ref[...])
    hb = h.astype(BF16)
    cos1 = cos_ref[...]
    sin1 = sin_ref[...]
    cos4 = jnp.concatenate([cos1] * 4, axis=1)
    sin4 = jnp.concatenate([sin1] * 4, axis=1)

    def head_norm(a, gain):
        ss = _dot_hi_lo(a * a, hsum_ref[...])
        r = lax.rsqrt(ss * (1.0 / HEAD_DIM) + EPS)
        return a * _dot_hi_lo(r, hexp_ref[...]) * gain

    pa = lax.dot_general(hb, wa_ref[...], _TRANS_B, preferred_element_type=F32)
    q = _rope(head_norm(pa[:, :ATTN_WIDTH], qn_ref[...]), cos4, sin4) * (HEAD_DIM ** -0.5 * LOG2_E)
    k = _rope(head_norm(pa[:, ATTN_WIDTH:2 * ATTN_WIDTH], kn_ref[...]), cos4, sin4)
    qt_ref[...] = q.T.astype(BF16)
    k_ref[...] = k.astype(BF16)
    v = pa[:, 2 * ATTN_WIDTH:]
    ones_pad = jnp.where(lax.broadcasted_iota(jnp.int32, (VT_ROWS - HEAD_DIM, KEY_BLOCK), 0) == 0, 1.0, 0.0)
    for s in range(v.shape[0] // KEY_BLOCK):
        vt = v[KEY_BLOCK * s:KEY_BLOCK * (s + 1)].T
        rows = []
        for hd in range(ATTN_HEADS):
            rows += [vt[HEAD_DIM * hd:HEAD_DIM * (hd + 1)], ones_pad]
        vt_ref[s] = jnp.concatenate(rows, axis=0).astype(BF16)

    pi = _dot_hi_lo(h, wi_ref[...], wil_ref[...], trans_b=True)
    nq = IDX_HEADS * IDX_DIM
    iq = _rope(pi[:, :nq], cos4[:, :nq], sin4[:, :nq]) * (IDX_DIM ** -0.5)
    slab = pi[:, nq:]
    ikw_t = _rope(slab, cos1, sin1).T
    iw_ref[...] = slab.T[IDX_DIM:IDX_DIM + SUBLANES] * (IDX_HEADS ** -0.5)

    def hi_lo(a):
        hi = a.astype(BF16).astype(F32)
        return hi, (a - hi).astype(BF16).astype(F32)

    k_hi, k_lo = hi_lo(ikw_t[:IDX_DIM])
    ik3_ref[...] = jnp.concatenate([k_hi, k_lo, k_hi, jnp.zeros_like(k_hi)], axis=0).T.astype(BF16)
    iq_t = iq.T
    rows = []
    for hd in range(IDX_HEADS):
        q_hi, q_lo = hi_lo(iq_t[IDX_DIM * hd:IDX_DIM * (hd + 1)])
        rows += [q_hi, q_hi, q_lo, jnp.zeros_like(q_hi)]
    iq3_ref[...] = jnp.concatenate(rows, axis=0).astype(BF16)

    ph = lax.dot_general(hb, wh_ref[...], _TRANS_B, preferred_element_type=F32)
    w = HG_WIDTH
    hq, hf, hg = ph[:, :w], ph[:, w:2 * w], ph[:, 3 * w:]
    log_lb, log1m_lb, one_m_lb = hc_ref[0:1, :], hc_ref[1:2, :], hc_ref[2:3, :]
    c = log1m_lb + jnp.minimum(hf, 0.0) - jnp.log1p(jnp.exp(-jnp.abs(hf)))
    hg_ref[:, :w] = hq * _sigmoid(hq)
    hg_ref[:, w:2 * w] = jnp.maximum(log_lb, c) + jnp.log1p(jnp.exp(-jnp.abs(log_lb - c)))
    hg_ref[:, 2 * w:3 * w] = one_m_lb * _sigmoid(-hf)
    hg_ref[:, 3 * w:4 * w] = ph[:, 2 * w:3 * w]
    hg_ref[:, 4 * w:] = hg * _sigmoid(hg)


def _mix_in(x2, gain, wa, wi, wil, wh, qn, kn, cos1, sin1, hsum, hexp, hconst, l, tm):
    nt = x2.shape[0]
    kb = KEY_BLOCK
    row = lambda w: pl.BlockSpec((tm, w), lambda i: (i, 0))
    col = lambda w: pl.BlockSpec((w, tm), lambda i: (0, i))
    return pl.pallas_call(
        _mix_in_kernel,
        grid=(nt // tm,),
        in_specs=[row(D_MODEL), _resident(gain.shape), _layer_of(wa, l), _layer_of(wi, l), _layer_of(wil, l),
                  _layer_of(wh, l),
                  _resident(qn.shape), _resident(kn.shape), row(LANES), row(LANES),
                  _resident(hsum.shape), _resident(hexp.shape), _resident(hconst.shape)],
        out_specs=[col(ATTN_WIDTH), row(ATTN_WIDTH),
                   pl.BlockSpec((tm // kb, ATTN_HEADS * VT_ROWS, kb), lambda i: (i, 0, 0)),
                   col(IDX_HEADS * 4 * IDX_DIM), row(4 * IDX_DIM), col(SUBLANES), row(HG_STREAMS * HG_WIDTH)],
        out_shape=[jax.ShapeDtypeStruct((ATTN_WIDTH, nt), BF16), jax.ShapeDtypeStruct((nt, ATTN_WIDTH), BF16),
                   jax.ShapeDtypeStruct((nt // kb, ATTN_HEADS * VT_ROWS, kb), BF16),
                   jax.ShapeDtypeStruct((IDX_HEADS * 4 * IDX_DIM, nt), BF16),
                   jax.ShapeDtypeStruct((nt, 4 * IDX_DIM), BF16), jax.ShapeDtypeStruct((SUBLANES, nt), F32),
                   jax.ShapeDtypeStruct((nt, HG_STREAMS * HG_WIDTH), F32)],
        compiler_params=_params(1),
        name="mix_in",
    )(x2, gain, wa, wi, wil, wh, qn, kn, cos1, sin1, hsum, hexp, hconst)


def _dsa_kernel(bound_ref, iq3_ref, iq3_next_ref, ik3_ref, iw_ref, iw_next_ref, qt_ref, k_ref, vt_ref, ltri_ref,
                o_ref, hi2_ref, lo2_ref, qz_ref, lg_ref, m_ref, alpha_ref, acc_ref, *, topk):
    kb = KEY_BLOCK
    j = pl.program_id(1)
    nk = j + 1
    int_min = jnp.int32(INT_MIN)
    slot = j % 2
    hi_ref, lo_ref = hi2_ref.at[slot], lo2_ref.at[slot]
    hi_next_ref, lo_next_ref = hi2_ref.at[1 - slot], lo2_ref.at[1 - slot]

    zeros = jnp.zeros((HEAD_DIM, kb), BF16)
    for h in range(ATTN_HEADS):
        qh = qt_ref[HEAD_DIM * h:HEAD_DIM * (h + 1), :]
        qz_ref[h] = jnp.concatenate([qh, zeros] if h % 2 == 0 else [zeros, qh], axis=0)

    def score_head(c, h, next_block):
        q3, w = (iq3_next_ref, iw_next_ref) if next_block else (iq3_ref, iw_ref)
        rel = jnp.dot(ik3_ref[c], q3[4 * IDX_DIM * h:4 * IDX_DIM * (h + 1), :], preferred_element_type=F32)
        return w[h:h + 1, :] * jnp.maximum(rel, 0.0)

    def store_keys(c, score, next_block, diagonal):
        bits = pltpu.bitcast(score, jnp.int32)
        key = jnp.where(bits < 0, int_min - bits, bits)
        if diagonal:
            kpos = lax.broadcasted_iota(jnp.int32, (kb, kb), 0)
            qpos = lax.broadcasted_iota(jnp.int32, (kb, kb), 1)
            key = jnp.where(kpos <= qpos, key, int_min)
        hi_dst, lo_dst = (hi_next_ref, lo_next_ref) if next_block else (hi_ref, lo_ref)
        hi_dst[c] = lax.shift_right_arithmetic(key, 16).astype(jnp.int16)
        lo_dst[c] = (key ^ 0x8000).astype(jnp.int16)

    store_keys(j, sum(score_head(j, h, False) for h in range(IDX_HEADS)), False, True)

    one_b, zero_b = jnp.ones((), BF16), jnp.zeros((), BF16)
    one_i, zero_i = jnp.ones((), jnp.int16), jnp.zeros((), jnp.int16)
    rows16 = 16

    hi_ref[nk] = jnp.full((kb, kb), HALF_MIN, jnp.int16)
    lo_ref[nk] = jnp.full((kb, kb), HALF_MIN, jnp.int16)

    def count(hit_fn):
        def fold(hit):
            parts = [hit[rows16 * r:rows16 * (r + 1)] for r in range(kb // rows16)]
            while len(parts) > 1:
                parts = [a + b for a, b in zip(parts[::2], parts[1::2])]
            return parts[0]

        def four_chunks(i, cnt):
            return cnt + ((fold(hit_fn(4 * i)) + fold(hit_fn(4 * i + 1)))
                          + (fold(hit_fn(4 * i + 2)) + fold(hit_fn(4 * i + 3))))

        def two_chunks(i, cnt):
            return cnt + (fold(hit_fn(2 * i)) + fold(hit_fn(2 * i + 1)))

        cnt = lax.fori_loop(0, nk // 4, four_chunks, jnp.zeros((rows16, kb), jnp.int16))
        cnt = lax.fori_loop(2 * (nk // 4), (nk + 1) // 2, two_chunks, cnt)
        return jnp.sum(cnt.astype(jnp.int32), axis=0, keepdims=True).astype(F32)

    def bisect_half(ref, target):
        def bit_step(i, carry):
            thr_u, n_above = carry
            cand_u = thr_u | lax.shift_left(jnp.int32(1), 15 - i)
            cand = (cand_u + HALF_MIN).astype(jnp.int16)
            cnt = count(lambda c: jnp.where(ref[c] >= cand, one_i, zero_i))
            accepted = cnt >= target
            return jnp.where(accepted, cand_u, thr_u), jnp.where(accepted, n_above, cnt)
        thr_u, n_above = lax.fori_loop(0, 16, bit_step, (jnp.zeros((1, kb), jnp.int32), jnp.zeros((1, kb), F32)))
        return thr_u + HALF_MIN, n_above

    thr_hi32, n_gt_hi = bisect_half(hi_ref, float(topk))
    thr_hi = thr_hi32.astype(jnp.int16)

    def keep_matching_low(c, carry):
        lo_ref[c] = jnp.where(hi_ref[c] == thr_hi, lo_ref[c], jnp.int16(HALF_MIN))
        return carry

    lax.fori_loop(0, nk, keep_matching_low, 0)
    thr_lo32, n_gt_lo = bisect_half(lo_ref, topk - n_gt_hi)
    thr_lo = thr_lo32.astype(jnp.int16)
    n_gt = n_gt_hi + n_gt_lo
    is_marker = (thr_hi32 == HALF_MIN) & (thr_lo32 == HALF_MIN)
    need = jnp.where(is_marker, 0.0, topk - n_gt)

    def bias_chunk(c, ties_before):
        hi, lo = hi_ref[c], lo_ref[c]
        above = jnp.where(hi > thr_hi, one_b, jnp.where(lo > thr_lo, one_b, zero_b)).astype(F32)
        eq_b = jnp.where(hi == thr_hi, jnp.where(lo == thr_lo, one_b, zero_b), zero_b)
        eq = eq_b.astype(F32)
        rank = jnp.dot(ltri_ref[...], eq_b, preferred_element_type=F32) + ties_before
        chosen = above + jnp.where(rank < need, eq, 0.0)
        bias = jnp.where(chosen > 0.5, 0.0, NEG_BIG).astype(BF16)
        hi_ref[c] = pltpu.bitcast(bias, jnp.int16)
        return ties_before + jnp.sum(eq, axis=0, keepdims=True)


    def attend(bounded):
        acc_ref[...] = jnp.zeros(acc_ref.shape, F32)
        if not bounded:
            m_ref[...] = jnp.full(m_ref.shape, NEG_BIG, F32)

        def logits_head(c, h, bias):
            kk = k_ref[c, :, LANES * (h // 2):LANES * (h // 2 + 1)]
            lgb = jnp.dot(kk, qz_ref[h], preferred_element_type=F32) + bias
            lg_ref[h] = lgb
            if not bounded:
                m_old = m_ref[h]
                m_new = jnp.maximum(m_old, jnp.max(lgb, axis=0, keepdims=True))
                alpha_ref[h] = jnp.exp2(m_old - m_new)
                m_ref[h] = m_new

        def softmax_pv_head(c, h):
            p = jnp.exp2(lg_ref[h] if bounded else lg_ref[h] - m_ref[h])
            pv = jnp.dot(vt_ref[c, VT_ROWS * h:VT_ROWS * (h + 1), :], p.astype(BF16),
                         preferred_element_type=F32)
            acc_ref[h] = (acc_ref[h] if bounded else alpha_ref[h] * acc_ref[h]) + pv

        def chunk_bias(c):
            return pltpu.bitcast(hi_ref[c], BF16).astype(F32)

        def next_block_scores(c, h, partial):
            if h % (ATTN_HEADS // IDX_HEADS) != ATTN_HEADS // IDX_HEADS - 1:
                return partial
            term = score_head(c, h // (ATTN_HEADS // IDX_HEADS), True)
            return term if partial is None else partial + term

        ties = bias_chunk(0, jnp.zeros((1, kb), F32))
        bias0 = chunk_bias(0)
        partial = None
        for h in range(ATTN_HEADS):
            logits_head(0, h, bias0)
            partial = next_block_scores(0, h, partial)
        store_keys(0, partial, True, False)
        ties = bias_chunk(1, ties)

        def pipelined(c, ties_before):
            bias = chunk_bias(c)
            ties_after = bias_chunk(c + 1, ties_before)
            partial = None
            for h in range(ATTN_HEADS):
                softmax_pv_head(c - 1, h)
                logits_head(c, h, bias)
                partial = next_block_scores(c, h, partial)
            store_keys(c, partial, True, False)
            return ties_after

        lax.fori_loop(1, nk, pipelined, ties)
        for h in range(ATTN_HEADS):
            softmax_pv_head(nk - 1, h)

    logits_bounded = bound_ref[0] < MAX_SAFE_LOGIT

    @pl.when(logits_bounded)
    def _():
        attend(True)

    @pl.when(jnp.logical_not(logits_bounded))
    def _():
        attend(False)

    out_t = [acc_ref[h][:HEAD_DIM] / acc_ref[h][HEAD_DIM:HEAD_DIM + 1] for h in range(ATTN_HEADS)]
    o_ref[...] = jnp.concatenate(out_t, axis=0).T.astype(o_ref.dtype)


def _dsa(logit_bound, iq3t, ik3, iwt, qt, k, vt4, ltri, b, t, topk):
    kb = KEY_BLOCK
    nkc = t // kb
    chunked = lambda a: a.reshape(b * nkc, kb, a.shape[-1])
    per_batch = lambda shape: pl.BlockSpec((nkc,) + shape, lambda i, j: (i, 0, 0), pipeline_mode=pl.Buffered(1))
    q_cols = lambda rows: pl.BlockSpec((rows, kb), lambda i, j: (0, i * nkc + j))
    next_q_cols = lambda rows: pl.BlockSpec((rows, kb), lambda i, j: (0, i * nkc + jnp.minimum(j + 1, nkc - 1)))
    return pl.pallas_call(
        functools.partial(_dsa_kernel, topk=topk),
        grid=(b, nkc),
        in_specs=[
            pl.BlockSpec(memory_space=pltpu.SMEM),
            q_cols(IDX_HEADS * 4 * IDX_DIM),
            next_q_cols(IDX_HEADS * 4 * IDX_DIM),
            per_batch((kb, 4 * IDX_DIM)),
            q_cols(SUBLANES),
            next_q_cols(SUBLANES),
            q_cols(ATTN_WIDTH),
            per_batch((kb, ATTN_WIDTH)),
            per_batch((ATTN_HEADS * VT_ROWS, kb)),
            _resident(ltri.shape),
        ],
        out_specs=pl.BlockSpec((kb, ATTN_WIDTH), lambda i, j: (i * nkc + j, 0)),
        out_shape=jax.ShapeDtypeStruct((b * t, ATTN_WIDTH), BF16),
        scratch_shapes=[
            pltpu.VMEM((2, nkc + 1, kb, kb), jnp.int16),
            pltpu.VMEM((2, nkc + 1, kb, kb), jnp.int16),
            pltpu.VMEM((ATTN_HEADS, LANES, kb), BF16),
            pltpu.VMEM((ATTN_HEADS, kb, kb), F32),
            pltpu.VMEM((ATTN_HEADS, 1, kb), F32),
            pltpu.VMEM((ATTN_HEADS, 1, kb), F32),
            pltpu.VMEM((ATTN_HEADS, VT_ROWS, kb), F32),
        ],
        compiler_params=_params(2),
        name="dsa",
    )(logit_bound, iq3t, iq3t, chunked(ik3), iwt, iwt, qt, chunked(k), vt4, ltri)


def _hgrn_kernel(q_ref, lf_ref, k_ref, v_ref, gate_ref, hc_ref, lt_ref, o_ref, st_ref):
    @pl.when(pl.program_id(0) == 0)
    def _():
        st_ref[...] = jnp.zeros(st_ref.shape, F32)

    units = [(bi, ci) for ci in range(HG_STEP_CHUNKS) for bi in range(q_ref.shape[0])]
    rows = lambda ci: slice(HG_CHUNK * ci, HG_CHUNK * (ci + 1))
    cums = {(bi, ci): _dot_exact_lhs(lt_ref[...], lf_ref[bi, rows(ci), :]) for bi, ci in units}

    drops = []
    for b_all in cums.values():
        ends = [b_all[HG_FACT * (i + 1) - 1:HG_FACT * (i + 1), :] for i in range(HG_CHUNK // HG_FACT)]
        drops += [(ends[i - 1] if i > 0 else 0.0) - ends[i] for i in range(len(ends))]
    factored_ok = jnp.max(functools.reduce(jnp.maximum, drops)) < MAX_SAFE_EXPONENT

    def run(factored):
        for bi, ci in units:
            _hgrn_chunk(q_ref[bi, rows(ci), :], cums[(bi, ci)], k_ref[bi, rows(ci), :], v_ref.at[bi, rows(ci), :],
                        gate_ref[bi, rows(ci), :], hc_ref[3:4, :], o_ref.at[bi, rows(ci), :], st_ref.at[bi], factored)

    @pl.when(factored_ok)
    def _():
        run(True)

    @pl.when(jnp.logical_not(factored_ok))
    def _():
        run(False)


def _hgrn_chunk(q_all, b_all, k_all, hi_ref, gate, out_gain, o_ref, st_ref, factored):
    c_len, sub = HG_CHUNK, HG_SUB
    b_last = b_all[c_len - 1:c_len, :]
    q_inter = q_all * jnp.exp(b_all)
    k_carry = k_all * jnp.exp(b_last - b_all)

    n_sub = c_len // sub
    trans_b = (((1,), (1,)), ((), ()))

    def block_start(b, i):
        return b[sub * i - 1:sub * i, :] if i > 0 else jnp.zeros_like(b[0:1, :])

    def intra_factored_all():
        fs = HG_FACT
        heads = [slice(HG_DIM * h, HG_DIM * (h + 1)) for h in range(HG_HEADS)]
        scores, intra = [], []
        for hs in heads:
            b, q, k = b_all[:, hs], q_all[:, hs], k_all[:, hs]
            for i in range(c_len // fs):
                rs, upto = slice(fs * i, fs * (i + 1)), slice(0, fs * (i + 1))
                b_ref = b[fs * i - 1:fs * i, :] if i > 0 else jnp.zeros_like(b[0:1, :])
                q_hat = (q[rs] * jnp.exp(b[rs] - b_ref)).astype(BF16)
                k_hat = (k[upto] * jnp.exp(b_ref - b[upto])).astype(BF16)
                scores.append(lax.dot_general(q_hat, k_hat, trans_b, preferred_element_type=F32))
        for h, hs in enumerate(heads):
            vb = hi_ref[:, hs].astype(BF16)
            blocks = []
            for i in range(c_len // fs):
                a = scores[h * (c_len // fs) + i]
                t_idx = fs * i + lax.broadcasted_iota(jnp.int32, a.shape, 0)
                s_idx = lax.broadcasted_iota(jnp.int32, a.shape, 1)
                a = jnp.where(s_idx <= t_idx, a, 0.0).astype(BF16)
                blocks.append(jnp.dot(a, vb[0:fs * (i + 1)], preferred_element_type=F32))
            intra.append(jnp.concatenate(blocks, axis=0))
        return intra

    def intra_direct(b, q, k, vb):
        row = lax.broadcasted_iota(jnp.int32, (sub, 1), 0)
        col = lax.broadcasted_iota(jnp.int32, (sub, sub), 1)
        blocks = []
        for i in range(n_sub):
            rs = slice(sub * i, sub * (i + 1))
            b_i, q_i, k_i = b[rs], q[rs], k[rs]
            diag = jnp.zeros((sub, sub), F32)
            for s in range(sub):
                decay = jnp.exp(jnp.where(row >= s, b_i - b_i[s:s + 1, :], -jnp.inf))
                a_col = jnp.sum(q_i * k_i[s:s + 1, :] * decay, axis=1, keepdims=True)
                diag = jnp.where(col == s, a_col, diag)
            o_i = jnp.dot(diag.astype(BF16), vb[rs], preferred_element_type=F32)
            if i > 0:
                prev = slice(0, sub * i)
                b_ref = block_start(b, i)
                q_hat = (q_i * jnp.exp(b_i - b_ref)).astype(BF16)
                k_hat = (k[prev] * jnp.exp(b_ref - b[prev])).astype(BF16)
                a_off = lax.dot_general(q_hat, k_hat, trans_b, preferred_element_type=F32)
                o_i = o_i + jnp.dot(a_off.astype(BF16), vb[prev], preferred_element_type=F32)
            blocks.append(o_i)
        return jnp.concatenate(blocks, axis=0)

    if factored:
        intra = intra_factored_all()
    else:
        intra = [intra_direct(b_all[:, hs], q_all[:, hs], k_all[:, hs], hi_ref[:, hs].astype(BF16))
                 for hs in (slice(HG_DIM * h, HG_DIM * (h + 1)) for h in range(HG_HEADS))]

    outs = []
    for h in range(HG_HEADS):
        hs = slice(HG_DIM * h, HG_DIM * (h + 1))
        v = hi_ref[:, hs]
        st = st_ref[h]
        inter = lax.dot_general(q_inter[:, hs].astype(BF16), st.astype(BF16), trans_b,
                                preferred_element_type=F32)
        o = inter + intra[h]
        st_ref[h] = jnp.exp(b_last[:, hs]) * st + jnp.dot(v.T.astype(BF16), k_carry[:, hs].astype(BF16),
                                                         preferred_element_type=F32)
        outs.append(_rmsnorm_rows(o, out_gain[:, hs]) * gate[:, hs])
    o_ref[...] = jnp.concatenate(outs, axis=1).astype(o_ref.dtype)


def _hgrn(hg5, hconst, ltri, b, t):
    rows = HG_CHUNK * HG_STEP_CHUNKS
    streams = hg5.reshape(b, t, HG_STREAMS * HG_WIDTH)
    col = lambda jcol: pl.BlockSpec((b, rows, HG_WIDTH), lambda s, jcol=jcol: (0, s, jcol))
    return pl.pallas_call(
        _hgrn_kernel,
        grid=(t // rows,),
        in_specs=[col(jcol) for jcol in range(HG_STREAMS)] + [_resident(hconst.shape), _resident(ltri.shape)],
        out_specs=pl.BlockSpec((b, rows, HG_WIDTH), lambda s: (0, s, 0)),
        out_shape=jax.ShapeDtypeStruct((b, t, HG_WIDTH), BF16),
        scratch_shapes=[pltpu.VMEM((b, HG_HEADS, HG_DIM, HG_DIM), F32)],
        compiler_params=_params(1),
        name="hgrn",
    )(*([streams] * HG_STREAMS), hconst, ltri).reshape(b * t, HG_WIDTH)


def _merge_kernel(x_ref, ya_ref, yh_ref, g_ref, wg_ref, wpa_ref, wph_ref, wo_ref, o_ref):
    x = x_ref[...]
    hb = _rmsnorm_rows(x, g_ref[...]).astype(BF16)
    gates = lax.dot_general(hb, wg_ref[...], _TRANS_B, preferred_element_type=F32)
    pa = jnp.dot(ya_ref[...], wpa_ref[...], preferred_element_type=F32)
    ph = jnp.dot(yh_ref[...], wph_ref[...], preferred_element_type=F32)
    merged = _sigmoid(gates[:, :D_MODEL]) * pa + _sigmoid(gates[:, D_MODEL:]) * ph
    o_ref[...] = x + jnp.dot(merged.astype(BF16), wo_ref[...], preferred_element_type=F32)


def _merge(x2, ya, yh, gain, wg, wpa, wph, wo, l, tm):
    nt = x2.shape[0]
    row = lambda w: pl.BlockSpec((tm, w), lambda i: (i, 0))
    return pl.pallas_call(
        _merge_kernel,
        grid=(nt // tm,),
        in_specs=[row(D_MODEL), row(ATTN_WIDTH), row(HG_WIDTH), _resident(gain.shape), _layer_of(wg, l),
                  _layer_of(wpa, l), _layer_of(wph, l), _layer_of(wo, l)],
        out_specs=row(D_MODEL),
        out_shape=jax.ShapeDtypeStruct((nt, D_MODEL), F32),
        compiler_params=_params(1),
        name="merge",
    )(x2, ya, yh, gain, wg, wpa, wph, wo)


def _ffn_kernel(x_ref, g_ref, wi_ref, wo_ref, o_ref):
    x = x_ref[...]
    hb = _rmsnorm_rows(x, g_ref[...]).astype(BF16)
    gu = jnp.dot(hb, wi_ref[...], preferred_element_type=F32)
    g = gu[:, :D_FF]
    act = g * _sigmoid(g) * gu[:, D_FF:]
    o_ref[...] = x + jnp.dot(act.astype(BF16), wo_ref[...], preferred_element_type=F32)


def _ffn(x2, gain, wi, wo, l, tm):
    nt = x2.shape[0]
    row = pl.BlockSpec((tm, D_MODEL), lambda i: (i, 0))
    return pl.pallas_call(
        _ffn_kernel,
        grid=(nt // tm,),
        in_specs=[row, _resident(gain.shape), _layer_of(wi, l), _layer_of(wo, l)],
        out_specs=row,
        out_shape=jax.ShapeDtypeStruct((nt, D_MODEL), F32),
        compiler_params=_params(1),
        name="ffn",
    )(x2, gain, wi, wo)


def _layer(x2, b, t, cos1, sin1, consts, w, l):
    hsum, hexp, ltri_attn, ltri_hgrn = consts
    qt, k, vt4, iq3t, ik3, iwt, hg5 = _mix_in(x2, w["norm_mix"], w["wa"], *w["wi"], w["wh"], w["qn"], w["kn"],
                                              cos1, sin1, hsum, hexp, w["hconst"], l, tm=ROW_TILE)
    y_attn = _dsa(w["logit_bound"], iq3t, ik3, iwt, qt, k, vt4, ltri_attn, b, t, topk=min(MAX_TOPK, t // 4))
    y_hgrn = _hgrn(hg5, w["hconst"], ltri_hgrn, b, t)
    x2 = _merge(x2, y_attn, y_hgrn, w["norm_mix"], w["wg"], w["wpa"], w["wph"], w["wout"], l, tm=ROW_TILE)
    return _ffn(x2, w["norm_ffn"], w["wffn_in"], w["wffn_out"], l, tm=ROW_TILE)


IN_WIDTHS = (ATTN_WIDTH, ATTN_WIDTH, ATTN_WIDTH, IDX_HEADS * IDX_DIM, IDX_DIM, IDX_HEADS,
             HG_WIDTH, HG_WIDTH, HG_WIDTH, HG_WIDTH, D_MODEL, D_MODEL)
IN_OFF = tuple(int(v) for v in np.concatenate([[0], np.cumsum(IN_WIDTHS)]))
IDX_COLS = IN_OFF[6] - IN_OFF[3]
IDX_COLS_PADDED = 3 * LANES
PACK_COLS = 256


def _pack_group_kernel(v_ref, *o_refs, depth, lane_tiles, real_cols):
    cb = o_refs[0].shape[1]
    stride = lane_tiles * depth
    for l in range(depth):
        wt = jnp.concatenate([v_ref[pl.ds(l + depth * j, cb, stride=stride), :] for j in range(lane_tiles)], axis=1)
        if real_cols < cb:
            wt = jnp.where(lax.broadcasted_iota(jnp.int32, wt.shape, 0) < real_cols, wt, 0.0)
        hi = wt.astype(BF16)
        o_refs[0][l] = hi
        if len(o_refs) > 1:
            o_refs[1][l] = (wt - hi.astype(F32)).astype(BF16)


def _pack_group(view, depth, d, first_col, n_cols, cols_per_step, real_cols=None, hi_lo=False):
    lane_tiles = d // LANES
    rows_per_col = lane_tiles * depth
    n_out = 2 if hi_lo else 1
    return pl.pallas_call(
        functools.partial(_pack_group_kernel, depth=depth, lane_tiles=lane_tiles,
                          real_cols=cols_per_step if real_cols is None else real_cols),
        grid=(n_cols // cols_per_step,),
        in_specs=[pl.BlockSpec((pl.Element(cols_per_step * rows_per_col), pl.Element(LANES)),
                               lambda i: ((first_col + i * cols_per_step) * rows_per_col, 0))],
        out_specs=[pl.BlockSpec((depth, cols_per_step, d), lambda i: (0, i, 0))] * n_out,
        out_shape=[jax.ShapeDtypeStruct((depth, n_cols, d), BF16)] * n_out,
        compiler_params=_params(1),
        name="pack_w_in",
    )(view)


def _pack_w_in(w_in):
    depth, d, n = w_in.shape
    lane_tiles = d // LANES
    view = w_in.transpose(2, 0, 1).reshape(n, depth, lane_tiles, LANES).transpose(0, 2, 1, 3).reshape(
        n * lane_tiles * depth, LANES)
    group = functools.partial(_pack_group, view, depth, d)
    wa, = group(IN_OFF[0], IN_OFF[3] - IN_OFF[0], PACK_COLS)
    wi_hi, wi_lo = group(IN_OFF[3], IDX_COLS_PADDED, IDX_COLS_PADDED, real_cols=IDX_COLS, hi_lo=True)
    wh, = group(IN_OFF[6], IN_OFF[10] - IN_OFF[6], PACK_COLS)
    wg, = group(IN_OFF[10], IN_OFF[12] - IN_OFF[10], PACK_COLS)
    return wa, wi_hi, wi_lo, wh, wg


def _constants():
    head_of_lane = np.arange(ATTN_WIDTH) // HEAD_DIM
    hsum = (head_of_lane[:, None] == np.arange(LANES)[None, :]).astype(np.float32)
    hexp = hsum.T.copy()
    r = np.arange(KEY_BLOCK)
    ltri_attn = (r[None, :] < r[:, None]).astype(np.float32)
    r = np.arange(HG_CHUNK)
    ltri_hgrn = (r[None, :] <= r[:, None]).astype(np.float32)
    return jnp.asarray(hsum, BF16), jnp.asarray(hexp, BF16), jnp.asarray(ltri_attn, BF16), jnp.asarray(ltri_hgrn, BF16)


def kernel(x, positions, w_in, w_proj_attn, w_proj_hgrn, w_out, norm_mix, norm_ffn, q_norm, k_norm, hgrn_norm,
           hgrn_lower_bound, w_ffn_in, w_ffn_out):
    b, t, d = x.shape
    depth = w_in.shape[0]
    nt = b * t

    inv = ROPE_THETA ** (-jnp.arange(0, HEAD_DIM, 2, dtype=F32) / HEAD_DIM)
    ang = positions.astype(F32)[..., None] * inv
    cos, sin = jnp.cos(ang), jnp.sin(ang)
    cos1 = jnp.concatenate([cos, cos, cos, cos], axis=-1).reshape(nt, LANES)
    sin1 = jnp.concatenate([-sin, sin, -sin, sin], axis=-1).reshape(nt, LANES)

    lb_all = jnp.cumsum(jax.nn.softmax(hgrn_lower_bound.astype(F32), axis=0), axis=0)
    lb_all = lb_all - lb_all[:1]

    consts = _constants()
    wa_all, wi_hi_all, wi_lo_all, wh_all, wg_all = _pack_w_in(w_in)
    wpa_all, wph_all, wout_all = w_proj_attn.astype(BF16), w_proj_hgrn.astype(BF16), w_out.astype(BF16)
    wffn_in_all, wffn_out_all = w_ffn_in.astype(BF16), w_ffn_out.astype(BF16)

    x2 = x.reshape(nt, d)
    for l in range(depth):
        lb = lb_all[l]
        hconst = jnp.zeros((SUBLANES, HG_WIDTH), F32)
        hconst = hconst.at[0].set(jnp.log(lb)).at[1].set(jnp.log1p(-lb)).at[2].set(1.0 - lb)
        hconst = hconst.at[3].set(jnp.tile(hgrn_norm[l], HG_HEADS))
        w = {
            "norm_mix": norm_mix[l].reshape(1, d),
            "norm_ffn": norm_ffn[l].reshape(1, d),
            "wa": wa_all,
            "wi": (wi_hi_all, wi_lo_all),
            "wh": wh_all,
            "wg": wg_all,
            "qn": jnp.tile(q_norm[l], ATTN_HEADS).reshape(1, ATTN_WIDTH),
            "kn": jnp.tile(k_norm[l], ATTN_HEADS).reshape(1, ATTN_WIDTH),
            "logit_bound": (HEAD_DIM ** 0.5 * LOG2_E * jnp.max(jnp.abs(q_norm[l])) * jnp.max(jnp.abs(k_norm[l]))
                            ).reshape(1).astype(F32),
            "hconst": hconst,
            "wpa": wpa_all,
            "wph": wph_all,
            "wout": wout_all,
            "wffn_in": wffn_in_all,
            "wffn_out": wffn_out_all,
        }
        x2 = _layer(x2, b, t, cos1, sin1, consts, w, l)
    return x2.reshape(b, t, d)
```

```python
import functools

import numpy as np
import jax
import jax.numpy as jnp
from jax import lax
from jax.experimental import pallas as pl
from jax.experimental.pallas import tpu as pltpu

D_MODEL = 1024
ATTN_HEADS = 8
HEAD_DIM = 64
ATTN_WIDTH = ATTN_HEADS * HEAD_DIM
IDX_HEADS = 4
IDX_DIM = 64
MAX_TOPK = 256
HG_HEADS = 4
HG_DIM = 128
HG_WIDTH = HG_HEADS * HG_DIM
HG_CHUNK = 64
HG_STREAMS = 5
HG_STEP_CHUNKS = 8
HG_SUB = 16
HG_FACT = 32
MAX_SAFE_LOGIT = 60.0
MAX_SAFE_EXPONENT = 80.0
D_FF = 2816
ROPE_THETA = 10000.0
EPS = 1e-6
LOG2_E = 1.4426950408889634

LANES = 128
SUBLANES = 8
ROW_TILE = 512
VMEM_LIMIT_BYTES = 56 * 1024 * 1024

KEY_BLOCK = 256
VT_ROWS = 80
INT_MIN = -(2 ** 31)
HALF_MIN = -(2 ** 15)
NEG_BIG = -1e30

F32 = jnp.float32
BF16 = jnp.bfloat16


def _resident(shape):
    nd = len(shape)
    return pl.BlockSpec(shape, lambda *_: (0,) * nd, pipeline_mode=pl.Buffered(1))


def _layer_of(stacked, l):
    return pl.BlockSpec((None,) + stacked.shape[1:], lambda *_: (l, 0, 0), pipeline_mode=pl.Buffered(1))


def _params(n_axes):
    return pltpu.CompilerParams(dimension_semantics=("arbitrary",) * n_axes,
                                vmem_limit_bytes=VMEM_LIMIT_BYTES)


def _rmsnorm_rows(x, gain):
    return x * lax.rsqrt(jnp.mean(x * x, axis=-1, keepdims=True) + EPS) * gain


def _sigmoid(x):
    return 1.0 / (1.0 + jnp.exp(-x))


def _rope(x, cos, sin_signed):
    w = x.shape[-1]
    lane = lax.broadcasted_iota(jnp.int32, x.shape, 1)
    first_half = (lane & (HEAD_DIM - 1)) < HEAD_DIM // 2
    partner = jnp.where(first_half, pltpu.roll(x, w - HEAD_DIM // 2, 1), pltpu.roll(x, HEAD_DIM // 2, 1))
    return x * cos + partner * sin_signed


_TRANS_B = (((1,), (1,)), ((), ()))


def _dot_hi_lo(a, b_hi, b_lo=None, trans_b=False):
    dims = _TRANS_B if trans_b else (((1,), (0,)), ((), ()))
    dot = lambda x, y: lax.dot_general(x, y, dims, preferred_element_type=F32)
    a_hi = a.astype(BF16)
    a_lo = (a - a_hi.astype(F32)).astype(BF16)
    out = dot(a_hi, b_hi) + dot(a_lo, b_hi)
    if b_lo is not None:
        out = out + dot(a_hi, b_lo)
    return out


def _dot_exact_lhs(a, b):
    b1 = b.astype(BF16)
    r1 = b - b1.astype(F32)
    b2 = r1.astype(BF16)
    b3 = (r1 - b2.astype(F32)).astype(BF16)
    return (jnp.dot(a, b1, preferred_element_type=F32) + jnp.dot(a, b2, preferred_element_type=F32)
            + jnp.dot(a, b3, preferred_element_type=F32))


def _mix_in_kernel(x_ref, g_ref, wa_ref, wi_ref, wil_ref, wh_ref, qn_ref, kn_ref, cos_ref, sin_ref, hsum_ref, hexp_ref,
                   hc_ref, qt_ref, k_ref, vt_ref, iq3_ref, ik3_ref, iw_ref, hg_ref):
    h = _rmsnorm_rows(x_ref[...], g_ref[...])
    hb = h.astype(BF16)
    cos1 = cos_ref[...]
    sin1 = sin_ref[...]
    cos4 = jnp.concatenate([cos1] * 4, axis=1)
    sin4 = jnp.concatenate([sin1] * 4, axis=1)

    def head_norm(a, gain):
        ss = _dot_hi_lo(a * a, hsum_ref[...])
        r = lax.rsqrt(ss * (1.0 / HEAD_DIM) + EPS)
        return a * _dot_hi_lo(r, hexp_ref[...]) * gain

    pa = lax.dot_general(hb, wa_ref[...], _TRANS_B, preferred_element_type=F32)
    q = _rope(head_norm(pa[:, :ATTN_WIDTH], qn_ref[...]), cos4, sin4) * (HEAD_DIM ** -0.5 * LOG2_E)
    k = _rope(head_norm(pa[:, ATTN_WIDTH:2 * ATTN_WIDTH], kn_ref[...]), cos4, sin4)
    qt_ref[...] = q.T.astype(BF16)
    k_ref[...] = k.astype(BF16)
    v = pa[:, 2 * ATTN_WIDTH:]
    ones_pad = jnp.where(lax.broadcasted_iota(jnp.int32, (VT_ROWS - HEAD_DIM, KEY_BLOCK), 0) == 0, 1.0, 0.0)
    for s in range(v.shape[0] // KEY_BLOCK):
        vt = v[KEY_BLOCK * s:KEY_BLOCK * (s + 1)].T
        rows = []
        for hd in range(ATTN_HEADS):
            rows += [vt[HEAD_DIM * hd:HEAD_DIM * (hd + 1)], ones_pad]
        vt_ref[s] = jnp.concatenate(rows, axis=0).astype(BF16)

    pi = _dot_hi_lo(h, wi_ref[...], wil_ref[...], trans_b=True)
    nq = IDX_HEADS * IDX_DIM
    iq = _rope(pi[:, :nq], cos4[:, :nq], sin4[:, :nq]) * (IDX_DIM ** -0.5)
    slab = pi[:, nq:]
    ikw_t = _rope(slab, cos1, sin1).T
    iw_ref[...] = slab.T[IDX_DIM:IDX_DIM + SUBLANES] * (IDX_HEADS ** -0.5)

    def hi_lo(a):
        hi = a.astype(BF16).astype(F32)
        return hi, (a - hi).astype(BF16).astype(F32)

    k_hi, k_lo = hi_lo(ikw_t[:IDX_DIM])
    ik3_ref[...] = jnp.concatenate([k_hi, k_lo, k_hi, jnp.zeros_like(k_hi)], axis=0).T.astype(BF16)
    iq_t = iq.T
    rows = []
    for hd in range(IDX_HEADS):
        q_hi, q_lo = hi_lo(iq_t[IDX_DIM * hd:IDX_DIM * (hd + 1)])
        rows += [q_hi, q_hi, q_lo, jnp.zeros_like(q_hi)]
    iq3_ref[...] = jnp.concatenate(rows, axis=0).astype(BF16)

    ph = lax.dot_general(hb, wh_ref[...], _TRANS_B, preferred_element_type=F32)
    w = HG_WIDTH
    hq, hf, hg = ph[:, :w], ph[:, w:2 * w], ph[:, 3 * w:]
    log_lb, log1m_lb, one_m_lb = hc_ref[0:1, :], hc_ref[1:2, :], hc_ref[2:3, :]
    c = log1m_lb + jnp.minimum(hf, 0.0) - jnp.log1p(jnp.exp(-jnp.abs(hf)))
    hg_ref[:, :w] = hq * _sigmoid(hq)
    hg_ref[:, w:2 * w] = jnp.maximum(log_lb, c) + jnp.log1p(jnp.exp(-jnp.abs(log_lb - c)))
    hg_ref[:, 2 * w:3 * w] = one_m_lb * _sigmoid(-hf)
    hg_ref[:, 3 * w:4 * w] = ph[:, 2 * w:3 * w]
    hg_ref[:, 4 * w:] = hg * _sigmoid(hg)


def _mix_in(x2, gain, wa, wi, wil, wh, qn, kn, cos1, sin1, hsum, hexp, hconst, l, tm):
    nt = x2.shape[0]
    kb = KEY_BLOCK
    row = lambda w: pl.BlockSpec((tm, w), lambda i: (i, 0))
    col = lambda w: pl.BlockSpec((w, tm), lambda i: (0, i))
    return pl.pallas_call(
        _mix_in_kernel,
        grid=(nt // tm,),
        in_specs=[row(D_MODEL), _resident(gain.shape), _layer_of(wa, l), _layer_of(wi, l), _layer_of(wil, l),
                  _layer_of(wh, l),
                  _resident(qn.shape), _resident(kn.shape), row(LANES), row(LANES),
                  _resident(hsum.shape), _resident(hexp.shape), _resident(hconst.shape)],
        out_specs=[col(ATTN_WIDTH), row(ATTN_WIDTH),
                   pl.BlockSpec((tm // kb, ATTN_HEADS * VT_ROWS, kb), lambda i: (i, 0, 0)),
                   col(IDX_HEADS * 4 * IDX_DIM), row(4 * IDX_DIM), col(SUBLANES), row(HG_STREAMS * HG_WIDTH)],
        out_shape=[jax.ShapeDtypeStruct((ATTN_WIDTH, nt), BF16), jax.ShapeDtypeStruct((nt, ATTN_WIDTH), BF16),
                   jax.ShapeDtypeStruct((nt // kb, ATTN_HEADS * VT_ROWS, kb), BF16),
                   jax.ShapeDtypeStruct((IDX_HEADS * 4 * IDX_DIM, nt), BF16),
                   jax.ShapeDtypeStruct((nt, 4 * IDX_DIM), BF16), jax.ShapeDtypeStruct((SUBLANES, nt), F32),
                   jax.ShapeDtypeStruct((nt, HG_STREAMS * HG_WIDTH), F32)],
        compiler_params=_params(1),
        name="mix_in",
    )(x2, gain, wa, wi, wil, wh, qn, kn, cos1, sin1, hsum, hexp, hconst)


def _dsa_kernel(bound_ref, iq3_ref, iq3_next_ref, ik3_ref, iw_ref, iw_next_ref, qt_ref, k_ref, vt_ref, ltri_ref,
                o_ref, hi2_ref, lo2_ref, qz_ref, lg_ref, m_ref, alpha_ref, acc_ref, *, topk, nkc):
    kb = KEY_BLOCK
    j = pl.program_id(1)
    nk = j + 1
    int_min = jnp.int32(INT_MIN)
    slot = j % 2
    hi_ref, lo_ref = hi2_ref.at[slot], lo2_ref.at[slot]
    hi_next_ref, lo_next_ref = hi2_ref.at[1 - slot], lo2_ref.at[1 - slot]

    zeros = jnp.zeros((HEAD_DIM, kb), BF16)
    for h in range(ATTN_HEADS):
        qh = qt_ref[HEAD_DIM * h:HEAD_DIM * (h + 1), :]
        qz_ref[h] = jnp.concatenate([qh, zeros] if h % 2 == 0 else [zeros, qh], axis=0)

    def score_head(c, h, next_block):
        q3, w = (iq3_next_ref, iw_next_ref) if next_block else (iq3_ref, iw_ref)
        rel = jnp.dot(ik3_ref[c], q3[4 * IDX_DIM * h:4 * IDX_DIM * (h + 1), :], preferred_element_type=F32)
        return w[h:h + 1, :] * jnp.maximum(rel, 0.0)

    def store_keys(c, score, next_block, diagonal):
        bits = pltpu.bitcast(score, jnp.int32)
        key = jnp.where(bits < 0, int_min - bits, bits)
        if diagonal:
            kpos = lax.broadcasted_iota(jnp.int32, (kb, kb), 0)
            qpos = lax.broadcasted_iota(jnp.int32, (kb, kb), 1)
            key = jnp.where(kpos <= qpos, key, int_min)
        hi_dst, lo_dst = (hi_next_ref, lo_next_ref) if next_block else (hi_ref, lo_ref)
        hi_dst[c] = lax.shift_right_arithmetic(key, 16).astype(jnp.int16)
        lo_dst[c] = (key ^ 0x8000).astype(jnp.int16)

    @pl.when(j == 0)
    def _():
        store_keys(0, sum(score_head(0, h, False) for h in range(IDX_HEADS)), False, True)

    one_b, zero_b = jnp.ones((), BF16), jnp.zeros((), BF16)
    one_i, zero_i = jnp.ones((), jnp.int16), jnp.zeros((), jnp.int16)
    rows16 = 16

    hi_ref[nk] = jnp.full((kb, kb), HALF_MIN, jnp.int16)
    lo_ref[nk] = jnp.full((kb, kb), HALF_MIN, jnp.int16)

    def count(hit_fn):
        def fold(hit):
            parts = [hit[rows16 * r:rows16 * (r + 1)] for r in range(kb // rows16)]
            while len(parts) > 1:
                parts = [a + b for a, b in zip(parts[::2], parts[1::2])]
            return parts[0]

        def four_chunks(i, cnt):
            return cnt + ((fold(hit_fn(4 * i)) + fold(hit_fn(4 * i + 1)))
                          + (fold(hit_fn(4 * i + 2)) + fold(hit_fn(4 * i + 3))))

        def two_chunks(i, cnt):
            return cnt + (fold(hit_fn(2 * i)) + fold(hit_fn(2 * i + 1)))

        cnt = lax.fori_loop(0, nk // 4, four_chunks, jnp.zeros((rows16, kb), jnp.int16))
        cnt = lax.fori_loop(2 * (nk // 4), (nk + 1) // 2, two_chunks, cnt)
        return jnp.sum(cnt.astype(jnp.int32), axis=0, keepdims=True).astype(F32)

    def bisect_half(ref, target):
        def bit_step(i, carry):
            thr_u, n_above = carry
            cand_u = thr_u | lax.shift_left(jnp.int32(1), 15 - i)
            cand = (cand_u + HALF_MIN).astype(jnp.int16)
            cnt = count(lambda c: jnp.where(ref[c] >= cand, one_i, zero_i))
            accepted = cnt >= target
            return jnp.where(accepted, cand_u, thr_u), jnp.where(accepted, n_above, cnt)
        thr_u, n_above = lax.fori_loop(0, 16, bit_step, (jnp.zeros((1, kb), jnp.int32), jnp.zeros((1, kb), F32)))
        return thr_u + HALF_MIN, n_above

    thr_hi32, n_gt_hi = bisect_half(hi_ref, float(topk))
    thr_hi = thr_hi32.astype(jnp.int16)

    def keep_matching_low(c, carry):
        lo_ref[c] = jnp.where(hi_ref[c] == thr_hi, lo_ref[c], jnp.int16(HALF_MIN))
        return carry

    lax.fori_loop(0, nk, keep_matching_low, 0)
    thr_lo32, n_gt_lo = bisect_half(lo_ref, topk - n_gt_hi)
    thr_lo = thr_lo32.astype(jnp.int16)
    n_gt = n_gt_hi + n_gt_lo
    is_marker = (thr_hi32 == HALF_MIN) & (thr_lo32 == HALF_MIN)
    need = jnp.where(is_marker, 0.0, topk - n_gt)

    def bias_chunk(c, ties_before):
        hi, lo = hi_ref[c], lo_ref[c]
        above = jnp.where(hi > thr_hi, one_b, jnp.where(lo > thr_lo, one_b, zero_b)).astype(F32)
        eq_b = jnp.where(hi == thr_hi, jnp.where(lo == thr_lo, one_b, zero_b), zero_b)
        eq = eq_b.astype(F32)
        rank = jnp.dot(ltri_ref[...], eq_b, preferred_element_type=F32) + ties_before
        chosen = above + jnp.where(rank < need, eq, 0.0)
        bias = jnp.where(chosen > 0.5, 0.0, NEG_BIG).astype(BF16)
        hi_ref[c] = pltpu.bitcast(bias, jnp.int16)
        return ties_before + jnp.sum(eq, axis=0, keepdims=True)


    def attend(bounded):
        acc_ref[...] = jnp.zeros(acc_ref.shape, F32)
        if not bounded:
            m_ref[...] = jnp.full(m_ref.shape, NEG_BIG, F32)

        def logits_head(c, h, bias):
            kk = k_ref[c, :, LANES * (h // 2):LANES * (h // 2 + 1)]
            lgb = jnp.dot(kk, qz_ref[h], preferred_element_type=F32) + bias
            lg_ref[h] = lgb
            if not bounded:
                m_old = m_ref[h]
                m_new = jnp.maximum(m_old, jnp.max(lgb, axis=0, keepdims=True))
                alpha_ref[h] = jnp.exp2(m_old - m_new)
                m_ref[h] = m_new

        def softmax_pv_head(c, h):
            p = jnp.exp2(lg_ref[h] if bounded else lg_ref[h] - m_ref[h])
            pv = jnp.dot(vt_ref[c, VT_ROWS * h:VT_ROWS * (h + 1), :], p.astype(BF16),
                         preferred_element_type=F32)
            acc_ref[h] = (acc_ref[h] if bounded else alpha_ref[h] * acc_ref[h]) + pv

        def chunk_bias(c):
            return pltpu.bitcast(hi_ref[c], BF16).astype(F32)

        def next_block_scores(c, h, partial):
            if h % (ATTN_HEADS // IDX_HEADS) != ATTN_HEADS // IDX_HEADS - 1:
                return partial
            term = score_head(c, h // (ATTN_HEADS // IDX_HEADS), True)
            return term if partial is None else partial + term

        ties = bias_chunk(0, jnp.zeros((1, kb), F32))
        bias0 = chunk_bias(0)
        partial = None
        for h in range(ATTN_HEADS):
            logits_head(0, h, bias0)
            partial = next_block_scores(0, h, partial)
        store_keys(0, partial, True, False)
        ties = bias_chunk(1, ties)

        def pipelined(c, ties_before):
            bias = chunk_bias(c)
            ties_after = bias_chunk(c + 1, ties_before)
            partial = None
            for h in range(ATTN_HEADS):
                softmax_pv_head(c - 1, h)
                logits_head(c, h, bias)
                partial = next_block_scores(c, h, partial)
            store_keys(c, partial, True, False)
            return ties_after

        lax.fori_loop(1, nk, pipelined, ties)
        c_diag = jnp.minimum(nk, nkc - 1)
        partial = None
        for h in range(ATTN_HEADS):
            softmax_pv_head(nk - 1, h)
            partial = next_block_scores(c_diag, h, partial)
        store_keys(c_diag, partial, True, True)

    logits_bounded = bound_ref[0] < MAX_SAFE_LOGIT

    @pl.when(logits_bounded)
    def _():
        attend(True)

    @pl.when(jnp.logical_not(logits_bounded))
    def _():
        attend(False)

    out_t = [acc_ref[h][:HEAD_DIM] / acc_ref[h][HEAD_DIM:HEAD_DIM + 1] for h in range(ATTN_HEADS)]
    o_ref[...] = jnp.concatenate(out_t, axis=0).T.astype(o_ref.dtype)


def _dsa(logit_bound, iq3t, ik3, iwt, qt, k, vt4, ltri, b, t, topk):
    kb = KEY_BLOCK
    nkc = t // kb
    chunked = lambda a: a.reshape(b * nkc, kb, a.shape[-1])
    per_batch = lambda shape: pl.BlockSpec((nkc,) + shape, lambda i, j: (i, 0, 0), pipeline_mode=pl.Buffered(1))
    q_cols = lambda rows: pl.BlockSpec((rows, kb), lambda i, j: (0, i * nkc + j))
    next_q_cols = lambda rows: pl.BlockSpec((rows, kb), lambda i, j: (0, i * nkc + jnp.minimum(j + 1, nkc - 1)))
    return pl.pallas_call(
        functools.partial(_dsa_kernel, topk=topk, nkc=nkc),
        grid=(b, nkc),
        in_specs=[
            pl.BlockSpec(memory_space=pltpu.SMEM),
            q_cols(IDX_HEADS * 4 * IDX_DIM),
            next_q_cols(IDX_HEADS * 4 * IDX_DIM),
            per_batch((kb, 4 * IDX_DIM)),
            q_cols(SUBLANES),
            next_q_cols(SUBLANES),
            q_cols(ATTN_WIDTH),
            per_batch((kb, ATTN_WIDTH)),
            per_batch((ATTN_HEADS * VT_ROWS, kb)),
            _resident(ltri.shape),
        ],
        out_specs=pl.BlockSpec((kb, ATTN_WIDTH), lambda i, j: (i * nkc + j, 0)),
        out_shape=jax.ShapeDtypeStruct((b * t, ATTN_WIDTH), BF16),
        scratch_shapes=[
            pltpu.VMEM((2, nkc + 1, kb, kb), jnp.int16),
            pltpu.VMEM((2, nkc + 1, kb, kb), jnp.int16),
            pltpu.VMEM((ATTN_HEADS, LANES, kb), BF16),
            pltpu.VMEM((ATTN_HEADS, kb, kb), F32),
            pltpu.VMEM((ATTN_HEADS, 1, kb), F32),
            pltpu.VMEM((ATTN_HEADS, 1, kb), F32),
            pltpu.VMEM((ATTN_HEADS, VT_ROWS, kb), F32),
        ],
        compiler_params=_params(2),
        name="dsa",
    )(logit_bound, iq3t, iq3t, chunked(ik3), iwt, iwt, qt, chunked(k), vt4, ltri)


def _hgrn_kernel(q_ref, lf_ref, k_ref, v_ref, gate_ref, hc_ref, lt_ref, o_ref, st_ref):
    @pl.when(pl.program_id(0) == 0)
    def _():
        st_ref[...] = jnp.zeros(st_ref.shape, F32)

    units = [(bi, ci) for ci in range(HG_STEP_CHUNKS) for bi in range(q_ref.shape[0])]
    rows = lambda ci: slice(HG_CHUNK * ci, HG_CHUNK * (ci + 1))
    cums = {(bi, ci): _dot_exact_lhs(lt_ref[...], lf_ref[bi, rows(ci), :]) for bi, ci in units}

    drops = []
    for b_all in cums.values():
        ends = [b_all[HG_FACT * (i + 1) - 1:HG_FACT * (i + 1), :] for i in range(HG_CHUNK // HG_FACT)]
        drops += [(ends[i - 1] if i > 0 else 0.0) - ends[i] for i in range(len(ends))]
    factored_ok = jnp.max(functools.reduce(jnp.maximum, drops)) < MAX_SAFE_EXPONENT

    def run(factored):
        for bi, ci in units:
            _hgrn_chunk(q_ref[bi, rows(ci), :], cums[(bi, ci)], k_ref[bi, rows(ci), :], v_ref.at[bi, rows(ci), :],
                        gate_ref[bi, rows(ci), :], hc_ref[3:4, :], o_ref.at[bi, rows(ci), :], st_ref.at[bi], factored)

    @pl.when(factored_ok)
    def _():
        run(True)

    @pl.when(jnp.logical_not(factored_ok))
    def _():
        run(False)


def _hgrn_chunk(q_all, b_all, k_all, hi_ref, gate, out_gain, o_ref, st_ref, factored):
    c_len, sub = HG_CHUNK, HG_SUB
    b_last = b_all[c_len - 1:c_len, :]
    q_inter = q_all * jnp.exp(b_all)
    k_carry = k_all * jnp.exp(b_last - b_all)

    n_sub = c_len // sub
    trans_b = (((1,), (1,)), ((), ()))

    def block_start(b, i):
        return b[sub * i - 1:sub * i, :] if i > 0 else jnp.zeros_like(b[0:1, :])

    def intra_factored_all():
        fs = HG_FACT
        heads = [slice(HG_DIM * h, HG_DIM * (h + 1)) for h in range(HG_HEADS)]
        scores, intra = [], []
        for hs in heads:
            b, q, k = b_all[:, hs], q_all[:, hs], k_all[:, hs]
            for i in range(c_len // fs):
                rs, upto = slice(fs * i, fs * (i + 1)), slice(0, fs * (i + 1))
                b_ref = b[fs * i - 1:fs * i, :] if i > 0 else jnp.zeros_like(b[0:1, :])
                q_hat = (q[rs] * jnp.exp(b[rs] - b_ref)).astype(BF16)
                k_hat = (k[upto] * jnp.exp(b_ref - b[upto])).astype(BF16)
                scores.append(lax.dot_general(q_hat, k_hat, trans_b, preferred_element_type=F32))
        for h, hs in enumerate(heads):
            vb = hi_ref[:, hs].astype(BF16)
            blocks = []
            for i in range(c_len // fs):
                a = scores[h * (c_len // fs) + i]
                t_idx = fs * i + lax.broadcasted_iota(jnp.int32, a.shape, 0)
                s_idx = lax.broadcasted_iota(jnp.int32, a.shape, 1)
                a = jnp.where(s_idx <= t_idx, a, 0.0).astype(BF16)
                blocks.append(jnp.dot(a, vb[0:fs * (i + 1)], preferred_element_type=F32))
            intra.append(jnp.concatenate(blocks, axis=0))
        return intra

    def intra_direct(b, q, k, vb):
        row = lax.broadcasted_iota(jnp.int32, (sub, 1), 0)
        col = lax.broadcasted_iota(jnp.int32, (sub, sub), 1)
        blocks = []
        for i in range(n_sub):
            rs = slice(sub * i, sub * (i + 1))
            b_i, q_i, k_i = b[rs], q[rs], k[rs]
            diag = jnp.zeros((sub, sub), F32)
            for s in range(sub):
                decay = jnp.exp(jnp.where(row >= s, b_i - b_i[s:s + 1, :], -jnp.inf))
                a_col = jnp.sum(q_i * k_i[s:s + 1, :] * decay, axis=1, keepdims=True)
                diag = jnp.where(col == s, a_col, diag)
            o_i = jnp.dot(diag.astype(BF16), vb[rs], preferred_element_type=F32)
            if i > 0:
                prev = slice(0, sub * i)
                b_ref = block_start(b, i)
                q_hat = (q_i * jnp.exp(b_i - b_ref)).astype(BF16)
                k_hat = (k[prev] * jnp.exp(b_ref - b[prev])).astype(BF16)
                a_off = lax.dot_general(q_hat, k_hat, trans_b, preferred_element_type=F32)
                o_i = o_i + jnp.dot(a_off.astype(BF16), vb[prev], preferred_element_type=F32)
            blocks.append(o_i)
        return jnp.concatenate(blocks, axis=0)

    if factored:
        intra = intra_factored_all()
    else:
        intra = [intra_direct(b_all[:, hs], q_all[:, hs], k_all[:, hs], hi_ref[:, hs].astype(BF16))
                 for hs in (slice(HG_DIM * h, HG_DIM * (h + 1)) for h in range(HG_HEADS))]

    outs = []
    for h in range(HG_HEADS):
        hs = slice(HG_DIM * h, HG_DIM * (h + 1))
        v = hi_ref[:, hs]
        st = st_ref[h]
        inter = lax.dot_general(q_inter[:, hs].astype(BF16), st.astype(BF16), trans_b,
                                preferred_element_type=F32)
        o = inter + intra[h]
        st_ref[h] = jnp.exp(b_last[:, hs]) * st + jnp.dot(v.T.astype(BF16), k_carry[:, hs].astype(BF16),
                                                         preferred_element_type=F32)
        outs.append(_rmsnorm_rows(o, out_gain[:, hs]) * gate[:, hs])
    o_ref[...] = jnp.concatenate(outs, axis=1).astype(o_ref.dtype)


def _hgrn(hg5, hconst, ltri, b, t):
    rows = HG_CHUNK * HG_STEP_CHUNKS
    streams = hg5.reshape(b, t, HG_STREAMS * HG_WIDTH)
    col = lambda jcol: pl.BlockSpec((b, rows, HG_WIDTH), lambda s, jcol=jcol: (0, s, jcol))
    return pl.pallas_call(
        _hgrn_kernel,
        grid=(t // rows,),
        in_specs=[col(jcol) for jcol in range(HG_STREAMS)] + [_resident(hconst.shape), _resident(ltri.shape)],
        out_specs=pl.BlockSpec((b, rows, HG_WIDTH), lambda s: (0, s, 0)),
        out_shape=jax.ShapeDtypeStruct((b, t, HG_WIDTH), BF16),
        scratch_shapes=[pltpu.VMEM((b, HG_HEADS, HG_DIM, HG_DIM), F32)],
        compiler_params=_params(1),
        name="hgrn",
    )(*([streams] * HG_STREAMS), hconst, ltri).reshape(b * t, HG_WIDTH)


def _merge_kernel(x_ref, ya_ref, yh_ref, g_ref, wg_ref, wpa_ref, wph_ref, wo_ref, o_ref):
    x = x_ref[...]
    hb = _rmsnorm_rows(x, g_ref[...]).astype(BF16)
    gates = lax.dot_general(hb, wg_ref[...], _TRANS_B, preferred_element_type=F32)
    pa = jnp.dot(ya_ref[...], wpa_ref[...], preferred_element_type=F32)
    ph = jnp.dot(yh_ref[...], wph_ref[...], preferred_element_type=F32)
    merged = _sigmoid(gates[:, :D_MODEL]) * pa + _sigmoid(gates[:, D_MODEL:]) * ph
    o_ref[...] = x + jnp.dot(merged.astype(BF16), wo_ref[...], preferred_element_type=F32)


def _merge(x2, ya, yh, gain, wg, wpa, wph, wo, l, tm):
    nt = x2.shape[0]
    row = lambda w: pl.BlockSpec((tm, w), lambda i: (i, 0))
    return pl.pallas_call(
        _merge_kernel,
        grid=(nt // tm,),
        in_specs=[row(D_MODEL), row(ATTN_WIDTH), row(HG_WIDTH), _resident(gain.shape), _layer_of(wg, l),
                  _layer_of(wpa, l), _layer_of(wph, l), _layer_of(wo, l)],
        out_specs=row(D_MODEL),
        out_shape=jax.ShapeDtypeStruct((nt, D_MODEL), F32),
        compiler_params=_params(1),
        name="merge",
    )(x2, ya, yh, gain, wg, wpa, wph, wo)


def _ffn_kernel(x_ref, g_ref, wi_ref, wo_ref, o_ref):
    x = x_ref[...]
    hb = _rmsnorm_rows(x, g_ref[...]).astype(BF16)
    gu = jnp.dot(hb, wi_ref[...], preferred_element_type=F32)
    g = gu[:, :D_FF]
    act = g * _sigmoid(g) * gu[:, D_FF:]
    o_ref[...] = x + jnp.dot(act.astype(BF16), wo_ref[...], preferred_element_type=F32)


def _ffn(x2, gain, wi, wo, l, tm):
    nt = x2.shape[0]
    row = pl.BlockSpec((tm, D_MODEL), lambda i: (i, 0))
    return pl.pallas_call(
        _ffn_kernel,
        grid=(nt // tm,),
        in_specs=[row, _resident(gain.shape), _layer_of(wi, l), _layer_of(wo, l)],
        out_specs=row,
        out_shape=jax.ShapeDtypeStruct((nt, D_MODEL), F32),
        compiler_params=_params(1),
        name="ffn",
    )(x2, gain, wi, wo)


def _layer(x2, b, t, cos1, sin1, consts, w, l):
    hsum, hexp, ltri_attn, ltri_hgrn = consts
    qt, k, vt4, iq3t, ik3, iwt, hg5 = _mix_in(x2, w["norm_mix"], w["wa"], *w["wi"], w["wh"], w["qn"], w["kn"],
                                              cos1, sin1, hsum, hexp, w["hconst"], l, tm=ROW_TILE)
    y_attn = _dsa(w["logit_bound"], iq3t, ik3, iwt, qt, k, vt4, ltri_attn, b, t, topk=min(MAX_TOPK, t // 4))
    y_hgrn = _hgrn(hg5, w["hconst"], ltri_hgrn, b, t)
    x2 = _merge(x2, y_attn, y_hgrn, w["norm_mix"], w["wg"], w["wpa"], w["wph"], w["wout"], l, tm=ROW_TILE)
    return _ffn(x2, w["norm_ffn"], w["wffn_in"], w["wffn_out"], l, tm=ROW_TILE)


IN_WIDTHS = (ATTN_WIDTH, ATTN_WIDTH, ATTN_WIDTH, IDX_HEADS * IDX_DIM, IDX_DIM, IDX_HEADS,
             HG_WIDTH, HG_WIDTH, HG_WIDTH, HG_WIDTH, D_MODEL, D_MODEL)
IN_OFF = tuple(int(v) for v in np.concatenate([[0], np.cumsum(IN_WIDTHS)]))
IDX_COLS = IN_OFF[6] - IN_OFF[3]
IDX_COLS_PADDED = 3 * LANES
PACK_COLS = 256


def _pack_group_kernel(v_ref, *o_refs, depth, lane_tiles, real_cols):
    cb = o_refs[0].shape[1]
    stride = lane_tiles * depth
    for l in range(depth):
        wt = jnp.concatenate([v_ref[pl.ds(l + depth * j, cb, stride=stride), :] for j in range(lane_tiles)], axis=1)
        if real_cols < cb:
            wt = jnp.where(lax.broadcasted_iota(jnp.int32, wt.shape, 0) < real_cols, wt, 0.0)
        hi = wt.astype(BF16)
        o_refs[0][l] = hi
        if len(o_refs) > 1:
            o_refs[1][l] = (wt - hi.astype(F32)).astype(BF16)


def _pack_group(view, depth, d, first_col, n_cols, cols_per_step, real_cols=None, hi_lo=False):
    lane_tiles = d // LANES
    rows_per_col = lane_tiles * depth
    n_out = 2 if hi_lo else 1
    return pl.pallas_call(
        functools.partial(_pack_group_kernel, depth=depth, lane_tiles=lane_tiles,
                          real_cols=cols_per_step if real_cols is None else real_cols),
        grid=(n_cols // cols_per_step,),
        in_specs=[pl.BlockSpec((pl.Element(cols_per_step * rows_per_col), pl.Element(LANES)),
                               lambda i: ((first_col + i * cols_per_step) * rows_per_col, 0))],
        out_specs=[pl.BlockSpec((depth, cols_per_step, d), lambda i: (0, i, 0))] * n_out,
        out_shape=[jax.ShapeDtypeStruct((depth, n_cols, d), BF16)] * n_out,
        compiler_params=_params(1),
        name="pack_w_in",
    )(view)


def _pack_w_in(w_in):
    depth, d, n = w_in.shape
    lane_tiles = d // LANES
    view = w_in.transpose(2, 0, 1).reshape(n, depth, lane_tiles, LANES).transpose(0, 2, 1, 3).reshape(
        n * lane_tiles * depth, LANES)
    group = functools.partial(_pack_group, view, depth, d)
    wa, = group(IN_OFF[0], IN_OFF[3] - IN_OFF[0], PACK_COLS)
    wi_hi, wi_lo = group(IN_OFF[3], IDX_COLS_PADDED, IDX_COLS_PADDED, real_cols=IDX_COLS, hi_lo=True)
    wh, = group(IN_OFF[6], IN_OFF[10] - IN_OFF[6], PACK_COLS)
    wg, = group(IN_OFF[10], IN_OFF[12] - IN_OFF[10], PACK_COLS)
    return wa, wi_hi, wi_lo, wh, wg


def _constants():
    head_of_lane = np.arange(ATTN_WIDTH) // HEAD_DIM
    hsum = (head_of_lane[:, None] == np.arange(LANES)[None, :]).astype(np.float32)
    hexp = hsum.T.copy()
    r = np.arange(KEY_BLOCK)
    ltri_attn = (r[None, :] < r[:, None]).astype(np.float32)
    r = np.arange(HG_CHUNK)
    ltri_hgrn = (r[None, :] <= r[:, None]).astype(np.float32)
    return jnp.asarray(hsum, BF16), jnp.asarray(hexp, BF16), jnp.asarray(ltri_attn, BF16), jnp.asarray(ltri_hgrn, BF16)


def kernel(x, positions, w_in, w_proj_attn, w_proj_hgrn, w_out, norm_mix, norm_ffn, q_norm, k_norm, hgrn_norm,
           hgrn_lower_bound, w_ffn_in, w_ffn_out):
    b, t, d = x.shape
    depth = w_in.shape[0]
    nt = b * t

    inv = ROPE_THETA ** (-jnp.arange(0, HEAD_DIM, 2, dtype=F32) / HEAD_DIM)
    ang = positions.astype(F32)[..., None] * inv
    cos, sin = jnp.cos(ang), jnp.sin(ang)
    cos1 = jnp.concatenate([cos, cos, cos, cos], axis=-1).reshape(nt, LANES)
    sin1 = jnp.concatenate([-sin, sin, -sin, sin], axis=-1).reshape(nt, LANES)

    lb_all = jnp.cumsum(jax.nn.softmax(hgrn_lower_bound.astype(F32), axis=0), axis=0)
    lb_all = lb_all - lb_all[:1]

    consts = _constants()
    wa_all, wi_hi_all, wi_lo_all, wh_all, wg_all = _pack_w_in(w_in)
    wpa_all, wph_all, wout_all = w_proj_attn.astype(BF16), w_proj_hgrn.astype(BF16), w_out.astype(BF16)
    wffn_in_all, wffn_out_all = w_ffn_in.astype(BF16), w_ffn_out.astype(BF16)

    x2 = x.reshape(nt, d)
    for l in range(depth):
        lb = lb_all[l]
        hconst = jnp.zeros((SUBLANES, HG_WIDTH), F32)
        hconst = hconst.at[0].set(jnp.log(lb)).at[1].set(jnp.log1p(-lb)).at[2].set(1.0 - lb)
        hconst = hconst.at[3].set(jnp.tile(hgrn_norm[l], HG_HEADS))
        w = {
            "norm_mix": norm_mix[l].reshape(1, d),
            "norm_ffn": norm_ffn[l].reshape(1, d),
            "wa": wa_all,
            "wi": (wi_hi_all, wi_lo_all),
            "wh": wh_all,
            "wg": wg_all,
            "qn": jnp.tile(q_norm[l], ATTN_HEADS).reshape(1, ATTN_WIDTH),
            "kn": jnp.tile(k_norm[l], ATTN_HEADS).reshape(1, ATTN_WIDTH),
            "logit_bound": (HEAD_DIM ** 0.5 * LOG2_E * jnp.max(jnp.abs(q_norm[l])) * jnp.max(jnp.abs(k_norm[l]))
                            ).reshape(1).astype(F32),
            "hconst": hconst,
            "wpa": wpa_all,
            "wph": wph_all,
            "wout": wout_all,
            "wffn_in": wffn_in_all,
            "wffn_out": wffn_out_all,
        }
        x2 = _layer(x2, b, t, cos1, sin1, consts, w, l)
    return x2.reshape(b, t, d)
```

```python
import functools

import numpy as np
import jax
import jax.numpy as jnp
from jax import lax
from jax.experimental import pallas as pl
from jax.experimental.pallas import tpu as pltpu

D_MODEL = 1024
ATTN_HEADS = 8
HEAD_DIM = 64
ATTN_WIDTH = ATTN_HEADS * HEAD_DIM
IDX_HEADS = 4
IDX_DIM = 64
MAX_TOPK = 256
HG_HEADS = 4
HG_DIM = 128
HG_WIDTH = HG_HEADS * HG_DIM
HG_CHUNK = 64
HG_STREAMS = 5
HG_STEP_CHUNKS = 8
HG_SUB = 16
HG_FACT = 32
MAX_SAFE_LOGIT = 60.0
MAX_SAFE_EXPONENT = 80.0
D_FF = 2816
ROPE_THETA = 10000.0
EPS = 1e-6
LOG2_E = 1.4426950408889634

LANES = 128
SUBLANES = 8
ROW_TILE = 512
VMEM_LIMIT_BYTES = 56 * 1024 * 1024

KEY_BLOCK = 256
VT_ROWS = 80
INT_MIN = -(2 ** 31)
HALF_MIN = -(2 ** 15)
NEG_BIG = -1e30

F32 = jnp.float32
BF16 = jnp.bfloat16


def _resident(shape):
    nd = len(shape)
    return pl.BlockSpec(shape, lambda *_: (0,) * nd, pipeline_mode=pl.Buffered(1))


def _layer_of(stacked, l):
    return pl.BlockSpec((None,) + stacked.shape[1:], lambda *_: (l, 0, 0), pipeline_mode=pl.Buffered(1))


def _params(n_axes):
    return pltpu.CompilerParams(dimension_semantics=("arbitrary",) * n_axes,
                                vmem_limit_bytes=VMEM_LIMIT_BYTES)


def _rmsnorm_rows(x, gain):
    return x * lax.rsqrt(jnp.mean(x * x, axis=-1, keepdims=True) + EPS) * gain


def _sigmoid(x):
    return 1.0 / (1.0 + jnp.exp(-x))


def _rope(x, cos, sin_signed):
    w = x.shape[-1]
    lane = lax.broadcasted_iota(jnp.int32, x.shape, 1)
    first_half = (lane & (HEAD_DIM - 1)) < HEAD_DIM // 2
    partner = jnp.where(first_half, pltpu.roll(x, w - HEAD_DIM // 2, 1), pltpu.roll(x, HEAD_DIM // 2, 1))
    return x * cos + partner * sin_signed


_TRANS_B = (((1,), (1,)), ((), ()))


def _dot_hi_lo(a, b_hi, b_lo=None, trans_b=False):
    dims = _TRANS_B if trans_b else (((1,), (0,)), ((), ()))
    dot = lambda x, y: lax.dot_general(x, y, dims, preferred_element_type=F32)
    a_hi = a.astype(BF16)
    a_lo = (a - a_hi.astype(F32)).astype(BF16)
    out = dot(a_hi, b_hi) + dot(a_lo, b_hi)
    if b_lo is not None:
        out = out + dot(a_hi, b_lo)
    return out


def _dot_exact_lhs(a, b):
    b1 = b.astype(BF16)
    r1 = b - b1.astype(F32)
    b2 = r1.astype(BF16)
    b3 = (r1 - b2.astype(F32)).astype(BF16)
    return (jnp.dot(a, b1, preferred_element_type=F32) + jnp.dot(a, b2, preferred_element_type=F32)
            + jnp.dot(a, b3, preferred_element_type=F32))


def _mix_in_kernel(x_ref, g_ref, wa_ref, wi_ref, wil_ref, wh_ref, qn_ref, kn_ref, cos_ref, sin_ref, hsum_ref, hexp_ref,
                   hc_ref, qt_ref, k_ref, vt_ref, iq3_ref, ik3_ref, iw_ref, hg_ref):
    h = _rmsnorm_rows(x_ref[...], g_ref[...])
    hb = h.astype(BF16)
    cos1 = cos_ref[...]
    sin1 = sin_ref[...]
    cos4 = jnp.concatenate([cos1] * 4, axis=1)
    sin4 = jnp.concatenate([sin1] * 4, axis=1)

    def head_norm(a, gain):
        ss = _dot_hi_lo(a * a, hsum_ref[...])
        r = lax.rsqrt(ss * (1.0 / HEAD_DIM) + EPS)
        return a * _dot_hi_lo(r, hexp_ref[...]) * gain

    pa = lax.dot_general(hb, wa_ref[...], _TRANS_B, preferred_element_type=F32)
    q = _rope(head_norm(pa[:, :ATTN_WIDTH], qn_ref[...]), cos4, sin4) * (HEAD_DIM ** -0.5 * LOG2_E)
    k = _rope(head_norm(pa[:, ATTN_WIDTH:2 * ATTN_WIDTH], kn_ref[...]), cos4, sin4)
    qt_ref[...] = q.T.astype(BF16)
    k_ref[...] = k.astype(BF16)
    v = pa[:, 2 * ATTN_WIDTH:]
    ones_pad = jnp.where(lax.broadcasted_iota(jnp.int32, (VT_ROWS - HEAD_DIM, KEY_BLOCK), 0) == 0, 1.0, 0.0)
    for s in range(v.shape[0] // KEY_BLOCK):
        vt = v[KEY_BLOCK * s:KEY_BLOCK * (s + 1)].T
        rows = []
        for hd in range(ATTN_HEADS):
            rows += [vt[HEAD_DIM * hd:HEAD_DIM * (hd + 1)], ones_pad]
        vt_ref[s] = jnp.concatenate(rows, axis=0).astype(BF16)

    pi = _dot_hi_lo(h, wi_ref[...], wil_ref[...], trans_b=True)
    nq = IDX_HEADS * IDX_DIM
    iq = _rope(pi[:, :nq], cos4[:, :nq], sin4[:, :nq]) * (IDX_DIM ** -0.5)
    slab = pi[:, nq:]
    ikw_t = _rope(slab, cos1, sin1).T
    iw_ref[...] = slab.T[IDX_DIM:IDX_DIM + SUBLANES] * (IDX_HEADS ** -0.5)

    def hi_lo(a):
        hi = a.astype(BF16).astype(F32)
        return hi, (a - hi).astype(BF16).astype(F32)

    k_hi, k_lo = hi_lo(ikw_t[:IDX_DIM])
    ik3_ref[...] = jnp.concatenate([k_hi, k_lo, k_hi, jnp.zeros_like(k_hi)], axis=0).T.astype(BF16)
    iq_t = iq.T
    rows = []
    for hd in range(IDX_HEADS):
        q_hi, q_lo = hi_lo(iq_t[IDX_DIM * hd:IDX_DIM * (hd + 1)])
        rows += [q_hi, q_hi, q_lo, jnp.zeros_like(q_hi)]
    iq3_ref[...] = jnp.concatenate(rows, axis=0).astype(BF16)

    ph = lax.dot_general(hb, wh_ref[...], _TRANS_B, preferred_element_type=F32)
    w = HG_WIDTH
    hq, hf, hg = ph[:, :w], ph[:, w:2 * w], ph[:, 3 * w:]
    log_lb, log1m_lb, one_m_lb = hc_ref[0:1, :], hc_ref[1:2, :], hc_ref[2:3, :]
    c = log1m_lb + jnp.minimum(hf, 0.0) - jnp.log1p(jnp.exp(-jnp.abs(hf)))
    hg_ref[:, :w] = hq * _sigmoid(hq)
    hg_ref[:, w:2 * w] = jnp.maximum(log_lb, c) + jnp.log1p(jnp.exp(-jnp.abs(log_lb - c)))
    hg_ref[:, 2 * w:3 * w] = one_m_lb * _sigmoid(-hf)
    hg_ref[:, 3 * w:4 * w] = ph[:, 2 * w:3 * w]
    hg_ref[:, 4 * w:] = hg * _sigmoid(hg)


def _mix_in(x2, gain, wa, wi, wil, wh, qn, kn, cos1, sin1, hsum, hexp, hconst, l, tm):
    nt = x2.shape[0]
    kb = KEY_BLOCK
    row = lambda w: pl.BlockSpec((tm, w), lambda i: (i, 0))
    col = lambda w: pl.BlockSpec((w, tm), lambda i: (0, i))
    return pl.pallas_call(
        _mix_in_kernel,
        grid=(nt // tm,),
        in_specs=[row(D_MODEL), _resident(gain.shape), _layer_of(wa, l), _layer_of(wi, l), _layer_of(wil, l),
                  _layer_of(wh, l),
                  _resident(qn.shape), _resident(kn.shape), row(LANES), row(LANES),
                  _resident(hsum.shape), _resident(hexp.shape), _resident(hconst.shape)],
        out_specs=[col(ATTN_WIDTH), row(ATTN_WIDTH),
                   pl.BlockSpec((tm // kb, ATTN_HEADS * VT_ROWS, kb), lambda i: (i, 0, 0)),
                   col(IDX_HEADS * 4 * IDX_DIM), row(4 * IDX_DIM), col(SUBLANES), row(HG_STREAMS * HG_WIDTH)],
        out_shape=[jax.ShapeDtypeStruct((ATTN_WIDTH, nt), BF16), jax.ShapeDtypeStruct((nt, ATTN_WIDTH), BF16),
                   jax.ShapeDtypeStruct((nt // kb, ATTN_HEADS * VT_ROWS, kb), BF16),
                   jax.ShapeDtypeStruct((IDX_HEADS * 4 * IDX_DIM, nt), BF16),
                   jax.ShapeDtypeStruct((nt, 4 * IDX_DIM), BF16), jax.ShapeDtypeStruct((SUBLANES, nt), F32),
                   jax.ShapeDtypeStruct((nt, HG_STREAMS * HG_WIDTH), F32)],
        compiler_params=_params(1),
        name="mix_in",
    )(x2, gain, wa, wi, wil, wh, qn, kn, cos1, sin1, hsum, hexp, hconst)


def _dsa_kernel(bound_ref, iq3_ref, iq3_next_ref, ik3_ref, iw_ref, iw_next_ref, qt_ref, k_ref, vt_ref, ltri_ref,
                o_ref, hi2_ref, lo2_ref, qz_ref, lg_ref, m_ref, alpha_ref, acc_ref, *, topk, nkc):
    kb = KEY_BLOCK
    j = pl.program_id(1)
    nk = j + 1
    int_min = jnp.int32(INT_MIN)
    slot = j % 2
    hi_ref, lo_ref = hi2_ref.at[slot], lo2_ref.at[slot]
    hi_next_ref, lo_next_ref = hi2_ref.at[1 - slot], lo2_ref.at[1 - slot]

    zeros = jnp.zeros((HEAD_DIM, kb), BF16)
    for h in range(ATTN_HEADS):
        qh = qt_ref[HEAD_DIM * h:HEAD_DIM * (h + 1), :]
        qz_ref[h] = jnp.concatenate([qh, zeros] if h % 2 == 0 else [zeros, qh], axis=0)

    def score_head(c, h, next_block):
        q3, w = (iq3_next_ref, iw_next_ref) if next_block else (iq3_ref, iw_ref)
        rel = jnp.dot(ik3_ref[c], q3[4 * IDX_DIM * h:4 * IDX_DIM * (h + 1), :], preferred_element_type=F32)
        return w[h:h + 1, :] * jnp.maximum(rel, 0.0)

    def store_keys(c, score, next_block, diagonal):
        bits = pltpu.bitcast(score, jnp.int32)
        key = jnp.where(bits < 0, int_min - bits, bits)
        if diagonal:
            kpos = lax.broadcasted_iota(jnp.int32, (kb, kb), 0)
            qpos = lax.broadcasted_iota(jnp.int32, (kb, kb), 1)
            key = jnp.where(kpos <= qpos, key, int_min)
        hi_dst, lo_dst = (hi_next_ref, lo_next_ref) if next_block else (hi_ref, lo_ref)
        hi_dst[c] = lax.shift_right_arithmetic(key, 16).astype(jnp.int16)
        lo_dst[c] = (key ^ 0x8000).astype(jnp.int16)

    @pl.when(j == 0)
    def _():
        store_keys(0, sum(score_head(0, h, False) for h in range(IDX_HEADS)), False, True)

    one_b, zero_b = jnp.ones((), BF16), jnp.zeros((), BF16)
    one_i, zero_i = jnp.ones((), jnp.int16), jnp.zeros((), jnp.int16)
    rows16 = 16

    hi_ref[nk] = jnp.full((kb, kb), HALF_MIN, jnp.int16)
    lo_ref[nk] = jnp.full((kb, kb), HALF_MIN, jnp.int16)

    def count(hit_fn):
        def fold(hit):
            parts = [hit[rows16 * r:rows16 * (r + 1)] for r in range(kb // rows16)]
            while len(parts) > 1:
                parts = [a + b for a, b in zip(parts[::2], parts[1::2])]
            return parts[0]

        def four_chunks(i, cnt):
            return cnt + ((fold(hit_fn(4 * i)) + fold(hit_fn(4 * i + 1)))
                          + (fold(hit_fn(4 * i + 2)) + fold(hit_fn(4 * i + 3))))

        def two_chunks(i, cnt):
            return cnt + (fold(hit_fn(2 * i)) + fold(hit_fn(2 * i + 1)))

        cnt = lax.fori_loop(0, nk // 4, four_chunks, jnp.zeros((rows16, kb), jnp.int16))
        cnt = lax.fori_loop(2 * (nk // 4), (nk + 1) // 2, two_chunks, cnt)
        return jnp.sum(cnt.astype(jnp.int32), axis=0, keepdims=True).astype(F32)

    def bisect_half(ref, target, first_pass_hits=None):
        def plain_hits(cand):
            return lambda c: jnp.where(ref[c] >= cand, one_i, zero_i)

        def bit_step(i, carry, hits_of=plain_hits):
            thr_u, n_above = carry
            cand_u = thr_u | lax.shift_left(jnp.int32(1), 15 - i)
            cand = (cand_u + HALF_MIN).astype(jnp.int16)
            cnt = count(hits_of(cand))
            accepted = cnt >= target
            return jnp.where(accepted, cand_u, thr_u), jnp.where(accepted, n_above, cnt)

        carry, first = (jnp.zeros((1, kb), jnp.int32), jnp.zeros((1, kb), F32)), 0
        if first_pass_hits is not None:
            carry, first = bit_step(0, carry, first_pass_hits), 1
        thr_u, n_above = lax.fori_loop(first, 16, bit_step, carry)
        return thr_u + HALF_MIN, n_above

    thr_hi32, n_gt_hi = bisect_half(hi_ref, float(topk))
    thr_hi = thr_hi32.astype(jnp.int16)

    def keep_matching_low_and_hit(cand):
        def hit(c):
            matching = jnp.where(hi_ref[c] == thr_hi, lo_ref[c], jnp.int16(HALF_MIN))
            lo_ref[c] = matching
            return jnp.where(matching >= cand, one_i, zero_i)
        return hit

    thr_lo32, n_gt_lo = bisect_half(lo_ref, topk - n_gt_hi, keep_matching_low_and_hit)
    thr_lo = thr_lo32.astype(jnp.int16)
    n_gt = n_gt_hi + n_gt_lo
    is_marker = (thr_hi32 == HALF_MIN) & (thr_lo32 == HALF_MIN)
    need = jnp.where(is_marker, 0.0, topk - n_gt)

    def bias_chunk(c, ties_before):
        hi, lo = hi_ref[c], lo_ref[c]
        above = jnp.where(hi > thr_hi, one_b, jnp.where(lo > thr_lo, one_b, zero_b)).astype(F32)
        eq_b = jnp.where(hi == thr_hi, jnp.where(lo == thr_lo, one_b, zero_b), zero_b)
        eq = eq_b.astype(F32)
        rank = jnp.dot(ltri_ref[...], eq_b, preferred_element_type=F32) + ties_before
        chosen = above + jnp.where(rank < need, eq, 0.0)
        bias = jnp.where(chosen > 0.5, 0.0, NEG_BIG).astype(BF16)
        hi_ref[c] = pltpu.bitcast(bias, jnp.int16)
        return ties_before + jnp.sum(eq, axis=0, keepdims=True)


    def attend(bounded):
        acc_ref[...] = jnp.zeros(acc_ref.shape, F32)
        if not bounded:
            m_ref[...] = jnp.full(m_ref.shape, NEG_BIG, F32)

        def logits_head(c, h, bias):
            kk = k_ref[c, :, LANES * (h // 2):LANES * (h // 2 + 1)]
            lgb = jnp.dot(kk, qz_ref[h], preferred_element_type=F32) + bias
            lg_ref[h] = lgb
            if not bounded:
                m_old = m_ref[h]
                m_new = jnp.maximum(m_old, jnp.max(lgb, axis=0, keepdims=True))
                alpha_ref[h] = jnp.exp2(m_old - m_new)
                m_ref[h] = m_new

        def softmax_pv_head(c, h):
            p = jnp.exp2(lg_ref[h] if bounded else lg_ref[h] - m_ref[h])
            pv = jnp.dot(vt_ref[c, VT_ROWS * h:VT_ROWS * (h + 1), :], p.astype(BF16),
                         preferred_element_type=F32)
            acc_ref[h] = (acc_ref[h] if bounded else alpha_ref[h] * acc_ref[h]) + pv

        def chunk_bias(c):
            return pltpu.bitcast(hi_ref[c], BF16).astype(F32)

        def next_block_scores(c, h, partial):
            if h % (ATTN_HEADS // IDX_HEADS) != ATTN_HEADS // IDX_HEADS - 1:
                return partial
            term = score_head(c, h // (ATTN_HEADS // IDX_HEADS), True)
            return term if partial is None else partial + term

        ties = bias_chunk(0, jnp.zeros((1, kb), F32))
        bias0 = chunk_bias(0)
        partial = None
        for h in range(ATTN_HEADS):
            logits_head(0, h, bias0)
            partial = next_block_scores(0, h, partial)
        store_keys(0, partial, True, False)
        ties = bias_chunk(1, ties)

        def pipelined(c, ties_before):
            bias = chunk_bias(c)
            ties_after = bias_chunk(c + 1, ties_before)
            partial = None
            for h in range(ATTN_HEADS):
                softmax_pv_head(c - 1, h)
                logits_head(c, h, bias)
                partial = next_block_scores(c, h, partial)
            store_keys(c, partial, True, False)
            return ties_after

        lax.fori_loop(1, nk, pipelined, ties)
        c_diag = jnp.minimum(nk, nkc - 1)
        partial = None
        for h in range(ATTN_HEADS):
            softmax_pv_head(nk - 1, h)
            partial = next_block_scores(c_diag, h, partial)
        store_keys(c_diag, partial, True, True)

    logits_bounded = bound_ref[0] < MAX_SAFE_LOGIT

    @pl.when(logits_bounded)
    def _():
        attend(True)

    @pl.when(jnp.logical_not(logits_bounded))
    def _():
        attend(False)

    out_t = [acc_ref[h][:HEAD_DIM] / acc_ref[h][HEAD_DIM:HEAD_DIM + 1] for h in range(ATTN_HEADS)]
    o_ref[...] = jnp.concatenate(out_t, axis=0).T.astype(o_ref.dtype)


def _dsa(logit_bound, iq3t, ik3, iwt, qt, k, vt4, ltri, b, t, topk):
    kb = KEY_BLOCK
    nkc = t // kb
    chunked = lambda a: a.reshape(b * nkc, kb, a.shape[-1])
    per_batch = lambda shape: pl.BlockSpec((nkc,) + shape, lambda i, j: (i, 0, 0), pipeline_mode=pl.Buffered(1))
    q_cols = lambda rows: pl.BlockSpec((rows, kb), lambda i, j: (0, i * nkc + j))
    next_q_cols = lambda rows: pl.BlockSpec((rows, kb), lambda i, j: (0, i * nkc + jnp.minimum(j + 1, nkc - 1)))
    return pl.pallas_call(
        functools.partial(_dsa_kernel, topk=topk, nkc=nkc),
        grid=(b, nkc),
        in_specs=[
            pl.BlockSpec(memory_space=pltpu.SMEM),
            q_cols(IDX_HEADS * 4 * IDX_DIM),
            next_q_cols(IDX_HEADS * 4 * IDX_DIM),
            per_batch((kb, 4 * IDX_DIM)),
            q_cols(SUBLANES),
            next_q_cols(SUBLANES),
            q_cols(ATTN_WIDTH),
            per_batch((kb, ATTN_WIDTH)),
            per_batch((ATTN_HEADS * VT_ROWS, kb)),
            _resident(ltri.shape),
        ],
        out_specs=pl.BlockSpec((kb, ATTN_WIDTH), lambda i, j: (i * nkc + j, 0)),
        out_shape=jax.ShapeDtypeStruct((b * t, ATTN_WIDTH), BF16),
        scratch_shapes=[
            pltpu.VMEM((2, nkc + 1, kb, kb), jnp.int16),
            pltpu.VMEM((2, nkc + 1, kb, kb), jnp.int16),
            pltpu.VMEM((ATTN_HEADS, LANES, kb), BF16),
            pltpu.VMEM((ATTN_HEADS, kb, kb), F32),
            pltpu.VMEM((ATTN_HEADS, 1, kb), F32),
            pltpu.VMEM((ATTN_HEADS, 1, kb), F32),
            pltpu.VMEM((ATTN_HEADS, VT_ROWS, kb), F32),
        ],
        compiler_params=_params(2),
        name="dsa",
    )(logit_bound, iq3t, iq3t, chunked(ik3), iwt, iwt, qt, chunked(k), vt4, ltri)


def _hgrn_kernel(q_ref, lf_ref, k_ref, v_ref, gate_ref, hc_ref, lt_ref, o_ref, st_ref):
    @pl.when(pl.program_id(0) == 0)
    def _():
        st_ref[...] = jnp.zeros(st_ref.shape, F32)

    units = [(bi, ci) for ci in range(HG_STEP_CHUNKS) for bi in range(q_ref.shape[0])]
    rows = lambda ci: slice(HG_CHUNK * ci, HG_CHUNK * (ci + 1))
    cums = {(bi, ci): _dot_exact_lhs(lt_ref[...], lf_ref[bi, rows(ci), :]) for bi, ci in units}

    drops = []
    for b_all in cums.values():
        ends = [b_all[HG_FACT * (i + 1) - 1:HG_FACT * (i + 1), :] for i in range(HG_CHUNK // HG_FACT)]
        drops += [(ends[i - 1] if i > 0 else 0.0) - ends[i] for i in range(len(ends))]
    factored_ok = jnp.max(functools.reduce(jnp.maximum, drops)) < MAX_SAFE_EXPONENT

    def run(factored):
        for bi, ci in units:
            _hgrn_chunk(q_ref[bi, rows(ci), :], cums[(bi, ci)], k_ref[bi, rows(ci), :], v_ref.at[bi, rows(ci), :],
                        gate_ref[bi, rows(ci), :], hc_ref[3:4, :], o_ref.at[bi, rows(ci), :], st_ref.at[bi], factored)

    @pl.when(factored_ok)
    def _():
        run(True)

    @pl.when(jnp.logical_not(factored_ok))
    def _():
        run(False)


def _hgrn_chunk(q_all, b_all, k_all, hi_ref, gate, out_gain, o_ref, st_ref, factored):
    c_len, sub = HG_CHUNK, HG_SUB
    b_last = b_all[c_len - 1:c_len, :]
    q_inter = q_all * jnp.exp(b_all)
    k_carry = k_all * jnp.exp(b_last - b_all)

    n_sub = c_len // sub
    trans_b = (((1,), (1,)), ((), ()))

    def block_start(b, i):
        return b[sub * i - 1:sub * i, :] if i > 0 else jnp.zeros_like(b[0:1, :])

    def intra_factored_all():
        fs = HG_FACT
        heads = [slice(HG_DIM * h, HG_DIM * (h + 1)) for h in range(HG_HEADS)]
        scores, intra = [], []
        for hs in heads:
            b, q, k = b_all[:, hs], q_all[:, hs], k_all[:, hs]
            for i in range(c_len // fs):
                rs, upto = slice(fs * i, fs * (i + 1)), slice(0, fs * (i + 1))
                b_ref = b[fs * i - 1:fs * i, :] if i > 0 else jnp.zeros_like(b[0:1, :])
                q_hat = (q[rs] * jnp.exp(b[rs] - b_ref)).astype(BF16)
                k_hat = (k[upto] * jnp.exp(b_ref - b[upto])).astype(BF16)
                scores.append(lax.dot_general(q_hat, k_hat, trans_b, preferred_element_type=F32))
        for h, hs in enumerate(heads):
            vb = hi_ref[:, hs].astype(BF16)
            blocks = []
            for i in range(c_len // fs):
                a = scores[h * (c_len // fs) + i]
                t_idx = fs * i + lax.broadcasted_iota(jnp.int32, a.shape, 0)
                s_idx = lax.broadcasted_iota(jnp.int32, a.shape, 1)
                a = jnp.where(s_idx <= t_idx, a, 0.0).astype(BF16)
                blocks.append(jnp.dot(a, vb[0:fs * (i + 1)], preferred_element_type=F32))
            intra.append(jnp.concatenate(blocks, axis=0))
        return intra

    def intra_direct(b, q, k, vb):
        row = lax.broadcasted_iota(jnp.int32, (sub, 1), 0)
        col = lax.broadcasted_iota(jnp.int32, (sub, sub), 1)
        blocks = []
        for i in range(n_sub):
            rs = slice(sub * i, sub * (i + 1))
            b_i, q_i, k_i = b[rs], q[rs], k[rs]
            diag = jnp.zeros((sub, sub), F32)
            for s in range(sub):
                decay = jnp.exp(jnp.where(row >= s, b_i - b_i[s:s + 1, :], -jnp.inf))
                a_col = jnp.sum(q_i * k_i[s:s + 1, :] * decay, axis=1, keepdims=True)
                diag = jnp.where(col == s, a_col, diag)
            o_i = jnp.dot(diag.astype(BF16), vb[rs], preferred_element_type=F32)
            if i > 0:
                prev = slice(0, sub * i)
                b_ref = block_start(b, i)
                q_hat = (q_i * jnp.exp(b_i - b_ref)).astype(BF16)
                k_hat = (k[prev] * jnp.exp(b_ref - b[prev])).astype(BF16)
                a_off = lax.dot_general(q_hat, k_hat, trans_b, preferred_element_type=F32)
                o_i = o_i + jnp.dot(a_off.astype(BF16), vb[prev], preferred_element_type=F32)
            blocks.append(o_i)
        return jnp.concatenate(blocks, axis=0)

    if factored:
        intra = intra_factored_all()
    else:
        intra = [intra_direct(b_all[:, hs], q_all[:, hs], k_all[:, hs], hi_ref[:, hs].astype(BF16))
                 for hs in (slice(HG_DIM * h, HG_DIM * (h + 1)) for h in range(HG_HEADS))]

    outs = []
    for h in range(HG_HEADS):
        hs = slice(HG_DIM * h, HG_DIM * (h + 1))
        v = hi_ref[:, hs]
        st = st_ref[h]
        inter = lax.dot_general(q_inter[:, hs].astype(BF16), st.astype(BF16), trans_b,
                                preferred_element_type=F32)
        o = inter + intra[h]
        st_ref[h] = jnp.exp(b_last[:, hs]) * st + jnp.dot(v.T.astype(BF16), k_carry[:, hs].astype(BF16),
                                                         preferred_element_type=F32)
        outs.append(_rmsnorm_rows(o, out_gain[:, hs]) * gate[:, hs])
    o_ref[...] = jnp.concatenate(outs, axis=1).astype(o_ref.dtype)


def _hgrn(hg5, hconst, ltri, b, t):
    rows = HG_CHUNK * HG_STEP_CHUNKS
    streams = hg5.reshape(b, t, HG_STREAMS * HG_WIDTH)
    col = lambda jcol: pl.BlockSpec((b, rows, HG_WIDTH), lambda s, jcol=jcol: (0, s, jcol))
    return pl.pallas_call(
        _hgrn_kernel,
        grid=(t // rows,),
        in_specs=[col(jcol) for jcol in range(HG_STREAMS)] + [_resident(hconst.shape), _resident(ltri.shape)],
        out_specs=pl.BlockSpec((b, rows, HG_WIDTH), lambda s: (0, s, 0)),
        out_shape=jax.ShapeDtypeStruct((b, t, HG_WIDTH), BF16),
        scratch_shapes=[pltpu.VMEM((b, HG_HEADS, HG_DIM, HG_DIM), F32)],
        compiler_params=_params(1),
        name="hgrn",
    )(*([streams] * HG_STREAMS), hconst, ltri).reshape(b * t, HG_WIDTH)


def _merge_kernel(x_ref, ya_ref, yh_ref, g_ref, wg_ref, wpa_ref, wph_ref, wo_ref, o_ref):
    x = x_ref[...]
    hb = _rmsnorm_rows(x, g_ref[...]).astype(BF16)
    gates = lax.dot_general(hb, wg_ref[...], _TRANS_B, preferred_element_type=F32)
    pa = jnp.dot(ya_ref[...], wpa_ref[...], preferred_element_type=F32)
    ph = jnp.dot(yh_ref[...], wph_ref[...], preferred_element_type=F32)
    merged = _sigmoid(gates[:, :D_MODEL]) * pa + _sigmoid(gates[:, D_MODEL:]) * ph
    o_ref[...] = x + jnp.dot(merged.astype(BF16), wo_ref[...], preferred_element_type=F32)


def _merge(x2, ya, yh, gain, wg, wpa, wph, wo, l, tm):
    nt = x2.shape[0]
    row = lambda w: pl.BlockSpec((tm, w), lambda i: (i, 0))
    return pl.pallas_call(
        _merge_kernel,
        grid=(nt // tm,),
        in_specs=[row(D_MODEL), row(ATTN_WIDTH), row(HG_WIDTH), _resident(gain.shape), _layer_of(wg, l),
                  _layer_of(wpa, l), _layer_of(wph, l), _layer_of(wo, l)],
        out_specs=row(D_MODEL),
        out_shape=jax.ShapeDtypeStruct((nt, D_MODEL), F32),
        compiler_params=_params(1),
        name="merge",
    )(x2, ya, yh, gain, wg, wpa, wph, wo)


def _ffn_kernel(x_ref, g_ref, wi_ref, wo_ref, o_ref):
    x = x_ref[...]
    hb = _rmsnorm_rows(x, g_ref[...]).astype(BF16)
    gu = jnp.dot(hb, wi_ref[...], preferred_element_type=F32)
    g = gu[:, :D_FF]
    act = g * _sigmoid(g) * gu[:, D_FF:]
    o_ref[...] = x + jnp.dot(act.astype(BF16), wo_ref[...], preferred_element_type=F32)


def _ffn(x2, gain, wi, wo, l, tm):
    nt = x2.shape[0]
    row = pl.BlockSpec((tm, D_MODEL), lambda i: (i, 0))
    return pl.pallas_call(
        _ffn_kernel,
        grid=(nt // tm,),
        in_specs=[row, _resident(gain.shape), _layer_of(wi, l), _layer_of(wo, l)],
        out_specs=row,
        out_shape=jax.ShapeDtypeStruct((nt, D_MODEL), F32),
        compiler_params=_params(1),
        name="ffn",
    )(x2, gain, wi, wo)


def _layer(x2, b, t, cos1, sin1, consts, w, l):
    hsum, hexp, ltri_attn, ltri_hgrn = consts
    qt, k, vt4, iq3t, ik3, iwt, hg5 = _mix_in(x2, w["norm_mix"], w["wa"], *w["wi"], w["wh"], w["qn"], w["kn"],
                                              cos1, sin1, hsum, hexp, w["hconst"], l, tm=ROW_TILE)
    y_attn = _dsa(w["logit_bound"], iq3t, ik3, iwt, qt, k, vt4, ltri_attn, b, t, topk=min(MAX_TOPK, t // 4))
    y_hgrn = _hgrn(hg5, w["hconst"], ltri_hgrn, b, t)
    x2 = _merge(x2, y_attn, y_hgrn, w["norm_mix"], w["wg"], w["wpa"], w["wph"], w["wout"], l, tm=ROW_TILE)
    return _ffn(x2, w["norm_ffn"], w["wffn_in"], w["wffn_out"], l, tm=ROW_TILE)


IN_WIDTHS = (ATTN_WIDTH, ATTN_WIDTH, ATTN_WIDTH, IDX_HEADS * IDX_DIM, IDX_DIM, IDX_HEADS,
             HG_WIDTH, HG_WIDTH, HG_WIDTH, HG_WIDTH, D_MODEL, D_MODEL)
IN_OFF = tuple(int(v) for v in np.concatenate([[0], np.cumsum(IN_WIDTHS)]))
IDX_COLS = IN_OFF[6] - IN_OFF[3]
IDX_COLS_PADDED = 3 * LANES
PACK_COLS = 256


def _pack_group_kernel(v_ref, *o_refs, depth, lane_tiles, real_cols):
    cb = o_refs[0].shape[1]
    stride = lane_tiles * depth
    for l in range(depth):
        wt = jnp.concatenate([v_ref[pl.ds(l + depth * j, cb, stride=stride), :] for j in range(lane_tiles)], axis=1)
        if real_cols < cb:
            wt = jnp.where(lax.broadcasted_iota(jnp.int32, wt.shape, 0) < real_cols, wt, 0.0)
        hi = wt.astype(BF16)
        o_refs[0][l] = hi
        if len(o_refs) > 1:
            o_refs[1][l] = (wt - hi.astype(F32)).astype(BF16)


def _pack_group(view, depth, d, first_col, n_cols, cols_per_step, real_cols=None, hi_lo=False):
    lane_tiles = d // LANES
    rows_per_col = lane_tiles * depth
    n_out = 2 if hi_lo else 1
    return pl.pallas_call(
        functools.partial(_pack_group_kernel, depth=depth, lane_tiles=lane_tiles,
                          real_cols=cols_per_step if real_cols is None else real_cols),
        grid=(n_cols // cols_per_step,),
        in_specs=[pl.BlockSpec((pl.Element(cols_per_step * rows_per_col), pl.Element(LANES)),
                               lambda i: ((first_col + i * cols_per_step) * rows_per_col, 0))],
        out_specs=[pl.BlockSpec((depth, cols_per_step, d), lambda i: (0, i, 0))] * n_out,
        out_shape=[jax.ShapeDtypeStruct((depth, n_cols, d), BF16)] * n_out,
        compiler_params=_params(1),
        name="pack_w_in",
    )(view)


def _pack_w_in(w_in):
    depth, d, n = w_in.shape
    lane_tiles = d // LANES
    view = w_in.transpose(2, 0, 1).reshape(n, depth, lane_tiles, LANES).transpose(0, 2, 1, 3).reshape(
        n * lane_tiles * depth, LANES)
    group = functools.partial(_pack_group, view, depth, d)
    wa, = group(IN_OFF[0], IN_OFF[3] - IN_OFF[0], PACK_COLS)
    wi_hi, wi_lo = group(IN_OFF[3], IDX_COLS_PADDED, IDX_COLS_PADDED, real_cols=IDX_COLS, hi_lo=True)
    wh, = group(IN_OFF[6], IN_OFF[10] - IN_OFF[6], PACK_COLS)
    wg, = group(IN_OFF[10], IN_OFF[12] - IN_OFF[10], PACK_COLS)
    return wa, wi_hi, wi_lo, wh, wg


def _constants():
    head_of_lane = np.arange(ATTN_WIDTH) // HEAD_DIM
    hsum = (head_of_lane[:, None] == np.arange(LANES)[None, :]).astype(np.float32)
    hexp = hsum.T.copy()
    r = np.arange(KEY_BLOCK)
    ltri_attn = (r[None, :] < r[:, None]).astype(np.float32)
    r = np.arange(HG_CHUNK)
    ltri_hgrn = (r[None, :] <= r[:, None]).astype(np.float32)
    return jnp.asarray(hsum, BF16), jnp.asarray(hexp, BF16), jnp.asarray(ltri_attn, BF16), jnp.asarray(ltri_hgrn, BF16)


def kernel(x, positions, w_in, w_proj_attn, w_proj_hgrn, w_out, norm_mix, norm_ffn, q_norm, k_norm, hgrn_norm,
           hgrn_lower_bound, w_ffn_in, w_ffn_out):
    b, t, d = x.shape
    depth = w_in.shape[0]
    nt = b * t

    inv = ROPE_THETA ** (-jnp.arange(0, HEAD_DIM, 2, dtype=F32) / HEAD_DIM)
    ang = positions.astype(F32)[..., None] * inv
    cos, sin = jnp.cos(ang), jnp.sin(ang)
    cos1 = jnp.concatenate([cos, cos, cos, cos], axis=-1).reshape(nt, LANES)
    sin1 = jnp.concatenate([-sin, sin, -sin, sin], axis=-1).reshape(nt, LANES)

    lb_all = jnp.cumsum(jax.nn.softmax(hgrn_lower_bound.astype(F32), axis=0), axis=0)
    lb_all = lb_all - lb_all[:1]

    consts = _constants()
    wa_all, wi_hi_all, wi_lo_all, wh_all, wg_all = _pack_w_in(w_in)
    wpa_all, wph_all, wout_all = w_proj_attn.astype(BF16), w_proj_hgrn.astype(BF16), w_out.astype(BF16)
    wffn_in_all, wffn_out_all = w_ffn_in.astype(BF16), w_ffn_out.astype(BF16)

    x2 = x.reshape(nt, d)
    for l in range(depth):
        lb = lb_all[l]
        hconst = jnp.zeros((SUBLANES, HG_WIDTH), F32)
        hconst = hconst.at[0].set(jnp.log(lb)).at[1].set(jnp.log1p(-lb)).at[2].set(1.0 - lb)
        hconst = hconst.at[3].set(jnp.tile(hgrn_norm[l], HG_HEADS))
        w = {
            "norm_mix": norm_mix[l].reshape(1, d),
            "norm_ffn": norm_ffn[l].reshape(1, d),
            "wa": wa_all,
            "wi": (wi_hi_all, wi_lo_all),
            "wh": wh_all,
            "wg": wg_all,
            "qn": jnp.tile(q_norm[l], ATTN_HEADS).reshape(1, ATTN_WIDTH),
            "kn": jnp.tile(k_norm[l], ATTN_HEADS).reshape(1, ATTN_WIDTH),
            "logit_bound": (HEAD_DIM ** 0.5 * LOG2_E * jnp.max(jnp.abs(q_norm[l])) * jnp.max(jnp.abs(k_norm[l]))
                            ).reshape(1).astype(F32),
            "hconst": hconst,
            "wpa": wpa_all,
            "wph": wph_all,
            "wout": wout_all,
            "wffn_in": wffn_in_all,
            "wffn_out": wffn_out_all,
        }
        x2 = _layer(x2, b, t, cos1, sin1, consts, w, l)
    return x2.reshape(b, t, d)
```

```python
import functools

import numpy as np
import jax
import jax.numpy as jnp
from jax import lax
from jax.experimental import pallas as pl
from jax.experimental.pallas import tpu as pltpu

D_MODEL = 1024
ATTN_HEADS = 8
HEAD_DIM = 64
ATTN_WIDTH = ATTN_HEADS * HEAD_DIM
IDX_HEADS = 4
IDX_DIM = 64
MAX_TOPK = 256
HG_HEADS = 4
HG_DIM = 128
HG_WIDTH = HG_HEADS * HG_DIM
HG_CHUNK = 64
HG_STREAMS = 5
HG_STEP_CHUNKS = 8
HG_SUB = 16
HG_FACT = 32
MAX_SAFE_LOGIT = 60.0
MAX_SAFE_EXPONENT = 80.0
D_FF = 2816
ROPE_THETA = 10000.0
EPS = 1e-6
LOG2_E = 1.4426950408889634

LANES = 128
SUBLANES = 8
ROW_TILE = 512
VMEM_LIMIT_BYTES = 56 * 1024 * 1024

KEY_BLOCK = 256
VT_ROWS = 80
INT_MIN = -(2 ** 31)
HALF_MIN = -(2 ** 15)
NEG_BIG = -1e30

F32 = jnp.float32
BF16 = jnp.bfloat16


def _resident(shape):
    nd = len(shape)
    return pl.BlockSpec(shape, lambda *_: (0,) * nd, pipeline_mode=pl.Buffered(1))


def _layer_of(stacked, l):
    return pl.BlockSpec((None,) + stacked.shape[1:], lambda *_: (l, 0, 0), pipeline_mode=pl.Buffered(1))


def _params(n_axes):
    return pltpu.CompilerParams(dimension_semantics=("arbitrary",) * n_axes,
                                vmem_limit_bytes=VMEM_LIMIT_BYTES)


def _rmsnorm_rows(x, gain):
    return x * lax.rsqrt(jnp.mean(x * x, axis=-1, keepdims=True) + EPS) * gain


def _sigmoid(x):
    return 1.0 / (1.0 + jnp.exp(-x))


def _rope(x, cos, sin_signed):
    w = x.shape[-1]
    lane = lax.broadcasted_iota(jnp.int32, x.shape, 1)
    first_half = (lane & (HEAD_DIM - 1)) < HEAD_DIM // 2
    partner = jnp.where(first_half, pltpu.roll(x, w - HEAD_DIM // 2, 1), pltpu.roll(x, HEAD_DIM // 2, 1))
    return x * cos + partner * sin_signed


_TRANS_B = (((1,), (1,)), ((), ()))


def _dot_hi_lo(a, b_hi, b_lo=None, trans_b=False):
    dims = _TRANS_B if trans_b else (((1,), (0,)), ((), ()))
    dot = lambda x, y: lax.dot_general(x, y, dims, preferred_element_type=F32)
    a_hi = a.astype(BF16)
    a_lo = (a - a_hi.astype(F32)).astype(BF16)
    out = dot(a_hi, b_hi) + dot(a_lo, b_hi)
    if b_lo is not None:
        out = out + dot(a_hi, b_lo)
    return out


def _dot_exact_lhs(a, b):
    b1 = b.astype(BF16)
    r1 = b - b1.astype(F32)
    b2 = r1.astype(BF16)
    b3 = (r1 - b2.astype(F32)).astype(BF16)
    return (jnp.dot(a, b1, preferred_element_type=F32) + jnp.dot(a, b2, preferred_element_type=F32)
            + jnp.dot(a, b3, preferred_element_type=F32))


def _mix_in_kernel(x_ref, g_ref, wa_ref, wi_ref, wil_ref, wh_ref, qn_ref, kn_ref, cos_ref, sin_ref, hsum_ref, hexp_ref,
                   hc_ref, qt_ref, k_ref, vt_ref, iq3_ref, ik3_ref, iw_ref, hg_ref):
    h = _rmsnorm_rows(x_ref[...], g_ref[...])
    hb = h.astype(BF16)
    cos1 = cos_ref[...]
    sin1 = sin_ref[...]
    cos4 = jnp.concatenate([cos1] * 4, axis=1)
    sin4 = jnp.concatenate([sin1] * 4, axis=1)

    def head_norm(a, gain):
        ss = _dot_hi_lo(a * a, hsum_ref[...])
        r = lax.rsqrt(ss * (1.0 / HEAD_DIM) + EPS)
        return a * _dot_hi_lo(r, hexp_ref[...]) * gain

    pa = lax.dot_general(hb, wa_ref[...], _TRANS_B, preferred_element_type=F32)
    q = _rope(head_norm(pa[:, :ATTN_WIDTH], qn_ref[...]), cos4, sin4) * (HEAD_DIM ** -0.5 * LOG2_E)
    k = _rope(head_norm(pa[:, ATTN_WIDTH:2 * ATTN_WIDTH], kn_ref[...]), cos4, sin4)
    qt_ref[...] = q.T.astype(BF16)
    k_ref[...] = k.astype(BF16)
    v = pa[:, 2 * ATTN_WIDTH:]
    ones_pad = jnp.where(lax.broadcasted_iota(jnp.int32, (VT_ROWS - HEAD_DIM, KEY_BLOCK), 0) == 0, 1.0, 0.0)
    for s in range(v.shape[0] // KEY_BLOCK):
        vt = v[KEY_BLOCK * s:KEY_BLOCK * (s + 1)].T
        rows = []
        for hd in range(ATTN_HEADS):
            rows += [vt[HEAD_DIM * hd:HEAD_DIM * (hd + 1)], ones_pad]
        vt_ref[s] = jnp.concatenate(rows, axis=0).astype(BF16)

    pi = _dot_hi_lo(h, wi_ref[...], wil_ref[...], trans_b=True)
    nq = IDX_HEADS * IDX_DIM
    iq = _rope(pi[:, :nq], cos4[:, :nq], sin4[:, :nq]) * (IDX_DIM ** -0.5)
    slab = pi[:, nq:]
    ikw_t = _rope(slab, cos1, sin1).T
    iw_ref[...] = slab.T[IDX_DIM:IDX_DIM + SUBLANES] * (IDX_HEADS ** -0.5)

    def hi_lo(a):
        hi = a.astype(BF16).astype(F32)
        return hi, (a - hi).astype(BF16).astype(F32)

    k_hi, k_lo = hi_lo(ikw_t[:IDX_DIM])
    ik3_ref[...] = jnp.concatenate([k_hi, k_lo, k_hi, jnp.zeros_like(k_hi)], axis=0).T.astype(BF16)
    iq_t = iq.T
    rows = []
    for hd in range(IDX_HEADS):
        q_hi, q_lo = hi_lo(iq_t[IDX_DIM * hd:IDX_DIM * (hd + 1)])
        rows += [q_hi, q_hi, q_lo, jnp.zeros_like(q_hi)]
    iq3_ref[...] = jnp.concatenate(rows, axis=0).astype(BF16)

    ph = lax.dot_general(hb, wh_ref[...], _TRANS_B, preferred_element_type=F32)
    w = HG_WIDTH
    hq, hf, hg = ph[:, :w], ph[:, w:2 * w], ph[:, 3 * w:]
    log_lb, log1m_lb, one_m_lb = hc_ref[0:1, :], hc_ref[1:2, :], hc_ref[2:3, :]
    c = log1m_lb + jnp.minimum(hf, 0.0) - jnp.log1p(jnp.exp(-jnp.abs(hf)))
    hg_ref[:, :w] = hq * _sigmoid(hq)
    hg_ref[:, w:2 * w] = jnp.maximum(log_lb, c) + jnp.log1p(jnp.exp(-jnp.abs(log_lb - c)))
    hg_ref[:, 2 * w:3 * w] = one_m_lb * _sigmoid(-hf)
    hg_ref[:, 3 * w:4 * w] = ph[:, 2 * w:3 * w]
    hg_ref[:, 4 * w:] = hg * _sigmoid(hg)


def _mix_in(x2, gain, wa, wi, wil, wh, qn, kn, cos1, sin1, hsum, hexp, hconst, l, tm):
    nt = x2.shape[0]
    kb = KEY_BLOCK
    row = lambda w: pl.BlockSpec((tm, w), lambda i: (i, 0))
    col = lambda w: pl.BlockSpec((w, tm), lambda i: (0, i))
    return pl.pallas_call(
        _mix_in_kernel,
        grid=(nt // tm,),
        in_specs=[row(D_MODEL), _resident(gain.shape), _layer_of(wa, l), _layer_of(wi, l), _layer_of(wil, l),
                  _layer_of(wh, l),
                  _resident(qn.shape), _resident(kn.shape), row(LANES), row(LANES),
                  _resident(hsum.shape), _resident(hexp.shape), _resident(hconst.shape)],
        out_specs=[col(ATTN_WIDTH), row(ATTN_WIDTH),
                   pl.BlockSpec((tm // kb, ATTN_HEADS * VT_ROWS, kb), lambda i: (i, 0, 0)),
                   col(IDX_HEADS * 4 * IDX_DIM), row(4 * IDX_DIM), col(SUBLANES), row(HG_STREAMS * HG_WIDTH)],
        out_shape=[jax.ShapeDtypeStruct((ATTN_WIDTH, nt), BF16), jax.ShapeDtypeStruct((nt, ATTN_WIDTH), BF16),
                   jax.ShapeDtypeStruct((nt // kb, ATTN_HEADS * VT_ROWS, kb), BF16),
                   jax.ShapeDtypeStruct((IDX_HEADS * 4 * IDX_DIM, nt), BF16),
                   jax.ShapeDtypeStruct((nt, 4 * IDX_DIM), BF16), jax.ShapeDtypeStruct((SUBLANES, nt), F32),
                   jax.ShapeDtypeStruct((nt, HG_STREAMS * HG_WIDTH), F32)],
        compiler_params=_params(1),
        name="mix_in",
    )(x2, gain, wa, wi, wil, wh, qn, kn, cos1, sin1, hsum, hexp, hconst)


def _dsa_kernel(bound_ref, iq3_ref, iq3_next_ref, ik3_ref, iw_ref, iw_next_ref, qt_ref, k_ref, vt_ref, ltri_ref,
                o_ref, hi2_ref, lo2_ref, qz_ref, lg_ref, m_ref, alpha_ref, acc_ref, *, topk, nkc):
    kb = KEY_BLOCK
    j = pl.program_id(1)
    nk = j + 1
    int_min = jnp.int32(INT_MIN)
    slot = j % 2
    hi_ref, lo_ref = hi2_ref.at[slot], lo2_ref.at[slot]
    hi_next_ref, lo_next_ref = hi2_ref.at[1 - slot], lo2_ref.at[1 - slot]

    zeros = jnp.zeros((HEAD_DIM, kb), BF16)
    for h in range(ATTN_HEADS):
        qh = qt_ref[HEAD_DIM * h:HEAD_DIM * (h + 1), :]
        qz_ref[h] = jnp.concatenate([qh, zeros] if h % 2 == 0 else [zeros, qh], axis=0)

    def score_head(c, h, next_block):
        q3, w = (iq3_next_ref, iw_next_ref) if next_block else (iq3_ref, iw_ref)
        rel = jnp.dot(ik3_ref[c], q3[4 * IDX_DIM * h:4 * IDX_DIM * (h + 1), :], preferred_element_type=F32)
        return w[h:h + 1, :] * jnp.maximum(rel, 0.0)

    def store_keys(c, score, next_block, diagonal):
        bits = pltpu.bitcast(score, jnp.int32)
        key = jnp.where(bits < 0, int_min - bits, bits)
        if diagonal:
            kpos = lax.broadcasted_iota(jnp.int32, (kb, kb), 0)
            qpos = lax.broadcasted_iota(jnp.int32, (kb, kb), 1)
            key = jnp.where(kpos <= qpos, key, int_min)
        hi_dst, lo_dst = (hi_next_ref, lo_next_ref) if next_block else (hi_ref, lo_ref)
        hi_dst[c] = lax.shift_right_arithmetic(key, 16).astype(jnp.int16)
        lo_dst[c] = (key ^ 0x8000).astype(jnp.int16)

    @pl.when(j == 0)
    def _():
        store_keys(0, sum(score_head(0, h, False) for h in range(IDX_HEADS)), False, True)

    one_b, zero_b = jnp.ones((), BF16), jnp.zeros((), BF16)
    one_i, zero_i = jnp.ones((), jnp.int16), jnp.zeros((), jnp.int16)
    rows16 = 16

    hi_ref[nk] = jnp.full((kb, kb), HALF_MIN, jnp.int16)
    lo_ref[nk] = jnp.full((kb, kb), HALF_MIN, jnp.int16)

    def count(hit_fn):
        def fold(hit):
            parts = [hit[rows16 * r:rows16 * (r + 1)] for r in range(kb // rows16)]
            while len(parts) > 1:
                parts = [a + b for a, b in zip(parts[::2], parts[1::2])]
            return parts[0]

        def four_chunks(i, cnt):
            return cnt + ((fold(hit_fn(4 * i)) + fold(hit_fn(4 * i + 1)))
                          + (fold(hit_fn(4 * i + 2)) + fold(hit_fn(4 * i + 3))))

        def two_chunks(i, cnt):
            return cnt + (fold(hit_fn(2 * i)) + fold(hit_fn(2 * i + 1)))

        cnt = lax.fori_loop(0, nk // 4, four_chunks, jnp.zeros((rows16, kb), jnp.int16))
        cnt = lax.fori_loop(2 * (nk // 4), (nk + 1) // 2, two_chunks, cnt)
        return jnp.sum(cnt.astype(jnp.int32), axis=0, keepdims=True).astype(F32)

    def bisect_half(ref, target, first_pass_hits=None):
        def plain_hits(cand):
            return lambda c: jnp.where(ref[c] >= cand, one_i, zero_i)

        def bit_step(i, carry, hits_of=plain_hits):
            thr_u, n_above = carry
            cand_u = thr_u | lax.shift_left(jnp.int32(1), 15 - i)
            cand = (cand_u + HALF_MIN).astype(jnp.int16)
            cnt = count(hits_of(cand))
            accepted = cnt >= target
            return jnp.where(accepted, cand_u, thr_u), jnp.where(accepted, n_above, cnt)

        carry, first = (jnp.zeros((1, kb), jnp.int32), jnp.zeros((1, kb), F32)), 0
        if first_pass_hits is not None:
            carry, first = bit_step(0, carry, first_pass_hits), 1
        thr_u, n_above = lax.fori_loop(first, 16, bit_step, carry)
        return thr_u + HALF_MIN, n_above

    thr_hi32, n_gt_hi = bisect_half(hi_ref, float(topk))
    thr_hi = thr_hi32.astype(jnp.int16)

    def keep_matching_low_and_hit(cand):
        def hit(c):
            matching = jnp.where(hi_ref[c] == thr_hi, lo_ref[c], jnp.int16(HALF_MIN))
            lo_ref[c] = matching
            return jnp.where(matching >= cand, one_i, zero_i)
        return hit

    thr_lo32, n_gt_lo = bisect_half(lo_ref, topk - n_gt_hi, keep_matching_low_and_hit)
    thr_lo = thr_lo32.astype(jnp.int16)
    n_gt = n_gt_hi + n_gt_lo
    is_marker = (thr_hi32 == HALF_MIN) & (thr_lo32 == HALF_MIN)
    need = jnp.where(is_marker, 0.0, topk - n_gt)

    def bias_chunk(c, ties_before):
        hi, lo = hi_ref[c], lo_ref[c]
        above = jnp.where(hi > thr_hi, one_b, jnp.where(lo > thr_lo, one_b, zero_b)).astype(F32)
        eq_b = jnp.where(hi == thr_hi, jnp.where(lo == thr_lo, one_b, zero_b), zero_b)
        eq = eq_b.astype(F32)
        rank = jnp.dot(ltri_ref[...], eq_b, preferred_element_type=F32) + ties_before
        chosen = above + jnp.where(rank < need, eq, 0.0)
        bias = jnp.where(chosen > 0.5, 0.0, NEG_BIG).astype(BF16)
        hi_ref[c] = pltpu.bitcast(bias, jnp.int16)
        return ties_before + jnp.sum(eq, axis=0, keepdims=True)


    def attend(bounded):
        acc_ref[...] = jnp.zeros(acc_ref.shape, F32)
        if not bounded:
            m_ref[...] = jnp.full(m_ref.shape, NEG_BIG, F32)

        def logits_head(c, h, bias):
            kk = k_ref[c, :, LANES * (h // 2):LANES * (h // 2 + 1)]
            lgb = jnp.dot(kk, qz_ref[h], preferred_element_type=F32) + bias
            lg_ref[h] = lgb
            if not bounded:
                m_old = m_ref[h]
                m_new = jnp.maximum(m_old, jnp.max(lgb, axis=0, keepdims=True))
                alpha_ref[h] = jnp.exp2(m_old - m_new)
                m_ref[h] = m_new

        def softmax_pv_head(c, h):
            p = jnp.exp2(lg_ref[h] if bounded else lg_ref[h] - m_ref[h])
            pv = jnp.dot(vt_ref[c, VT_ROWS * h:VT_ROWS * (h + 1), :], p.astype(BF16),
                         preferred_element_type=F32)
            acc_ref[h] = (acc_ref[h] if bounded else alpha_ref[h] * acc_ref[h]) + pv

        def chunk_bias(c):
            return pltpu.bitcast(hi_ref[c], BF16).astype(F32)

        def next_block_scores(c, h, partial):
            if h % (ATTN_HEADS // IDX_HEADS) != ATTN_HEADS // IDX_HEADS - 1:
                return partial
            term = score_head(c, h // (ATTN_HEADS // IDX_HEADS), True)
            return term if partial is None else partial + term

        ties = bias_chunk(0, jnp.zeros((1, kb), F32))
        bias0 = chunk_bias(0)
        partial = None
        for h in range(ATTN_HEADS):
            logits_head(0, h, bias0)
            partial = next_block_scores(0, h, partial)
        store_keys(0, partial, True, False)
        ties = bias_chunk(1, ties)

        def pipelined(c, ties_before):
            bias = chunk_bias(c)
            ties_after = bias_chunk(c + 1, ties_before)
            partial = None
            for h in range(ATTN_HEADS):
                softmax_pv_head(c - 1, h)
                logits_head(c, h, bias)
                partial = next_block_scores(c, h, partial)
            store_keys(c, partial, True, False)
            return ties_after

        lax.fori_loop(1, nk, pipelined, ties)
        c_diag = jnp.minimum(nk, nkc - 1)
        partial = None
        for h in range(ATTN_HEADS):
            softmax_pv_head(nk - 1, h)
            partial = next_block_scores(c_diag, h, partial)
        store_keys(c_diag, partial, True, True)

    logits_bounded = bound_ref[0] < MAX_SAFE_LOGIT

    @pl.when(logits_bounded)
    def _():
        attend(True)

    @pl.when(jnp.logical_not(logits_bounded))
    def _():
        attend(False)

    out_t = [acc_ref[h][:HEAD_DIM] / acc_ref[h][HEAD_DIM:HEAD_DIM + 1] for h in range(ATTN_HEADS)]
    o_ref[...] = jnp.concatenate(out_t, axis=0).T.astype(o_ref.dtype)


def _dsa(logit_bound, iq3t, ik3, iwt, qt, k, vt4, ltri, b, t, topk):
    kb = KEY_BLOCK
    nkc = t // kb
    chunked = lambda a: a.reshape(b * nkc, kb, a.shape[-1])
    per_batch = lambda shape: pl.BlockSpec((nkc,) + shape, lambda i, j: (i, 0, 0), pipeline_mode=pl.Buffered(1))
    q_cols = lambda rows: pl.BlockSpec((rows, kb), lambda i, j: (0, i * nkc + j))
    next_q_cols = lambda rows: pl.BlockSpec((rows, kb), lambda i, j: (0, i * nkc + jnp.minimum(j + 1, nkc - 1)))
    return pl.pallas_call(
        functools.partial(_dsa_kernel, topk=topk, nkc=nkc),
        grid=(b, nkc),
        in_specs=[
            pl.BlockSpec(memory_space=pltpu.SMEM),
            q_cols(IDX_HEADS * 4 * IDX_DIM),
            next_q_cols(IDX_HEADS * 4 * IDX_DIM),
            per_batch((kb, 4 * IDX_DIM)),
            q_cols(SUBLANES),
            next_q_cols(SUBLANES),
            q_cols(ATTN_WIDTH),
            per_batch((kb, ATTN_WIDTH)),
            per_batch((ATTN_HEADS * VT_ROWS, kb)),
            _resident(ltri.shape),
        ],
        out_specs=pl.BlockSpec((kb, ATTN_WIDTH), lambda i, j: (i * nkc + j, 0)),
        out_shape=jax.ShapeDtypeStruct((b * t, ATTN_WIDTH), BF16),
        scratch_shapes=[
            pltpu.VMEM((2, nkc + 1, kb, kb), jnp.int16),
            pltpu.VMEM((2, nkc + 1, kb, kb), jnp.int16),
            pltpu.VMEM((ATTN_HEADS, LANES, kb), BF16),
            pltpu.VMEM((ATTN_HEADS, kb, kb), F32),
            pltpu.VMEM((ATTN_HEADS, 1, kb), F32),
            pltpu.VMEM((ATTN_HEADS, 1, kb), F32),
            pltpu.VMEM((ATTN_HEADS, VT_ROWS, kb), F32),
        ],
        compiler_params=_params(2),
        name="dsa",
    )(logit_bound, iq3t, iq3t, chunked(ik3), iwt, iwt, qt, chunked(k), vt4, ltri)


def _hgrn_kernel(q_ref, lf_ref, k_ref, v_ref, gate_ref, hc_ref, lt_ref, o_ref, st_ref):
    @pl.when(pl.program_id(0) == 0)
    def _():
        st_ref[...] = jnp.zeros(st_ref.shape, F32)

    units = [(bi, ci) for ci in range(HG_STEP_CHUNKS) for bi in range(q_ref.shape[0])]
    rows = lambda ci: slice(HG_CHUNK * ci, HG_CHUNK * (ci + 1))
    cums = {(bi, ci): _dot_exact_lhs(lt_ref[...], lf_ref[bi, rows(ci), :]) for bi, ci in units}

    drops = []
    for b_all in cums.values():
        ends = [b_all[HG_FACT * (i + 1) - 1:HG_FACT * (i + 1), :] for i in range(HG_CHUNK // HG_FACT)]
        drops += [(ends[i - 1] if i > 0 else 0.0) - ends[i] for i in range(len(ends))]
    factored_ok = jnp.max(functools.reduce(jnp.maximum, drops)) < MAX_SAFE_EXPONENT

    def run(factored):
        for bi, ci in units:
            _hgrn_chunk(q_ref[bi, rows(ci), :], cums[(bi, ci)], k_ref[bi, rows(ci), :], v_ref.at[bi, rows(ci), :],
                        gate_ref[bi, rows(ci), :], hc_ref[3:4, :], o_ref.at[bi, rows(ci), :], st_ref.at[bi], factored)

    @pl.when(factored_ok)
    def _():
        run(True)

    @pl.when(jnp.logical_not(factored_ok))
    def _():
        run(False)


def _hgrn_chunk(q_all, b_all, k_all, hi_ref, gate, out_gain, o_ref, st_ref, factored):
    c_len, sub = HG_CHUNK, HG_SUB
    b_last = b_all[c_len - 1:c_len, :]
    q_inter = q_all * jnp.exp(b_all)
    k_carry = k_all * jnp.exp(b_last - b_all)

    n_sub = c_len // sub
    trans_b = (((1,), (1,)), ((), ()))

    def block_start(b, i):
        return b[sub * i - 1:sub * i, :] if i > 0 else jnp.zeros_like(b[0:1, :])

    def intra_factored_all():
        fs = HG_FACT
        heads = [slice(HG_DIM * h, HG_DIM * (h + 1)) for h in range(HG_HEADS)]
        scores, intra = [], []
        for hs in heads:
            b, q, k = b_all[:, hs], q_all[:, hs], k_all[:, hs]
            for i in range(c_len // fs):
                rs, upto = slice(fs * i, fs * (i + 1)), slice(0, fs * (i + 1))
                b_ref = b[fs * i - 1:fs * i, :] if i > 0 else jnp.zeros_like(b[0:1, :])
                q_hat = (q[rs] * jnp.exp(b[rs] - b_ref)).astype(BF16)
                k_hat = (k[upto] * jnp.exp(b_ref - b[upto])).astype(BF16)
                scores.append(lax.dot_general(q_hat, k_hat, trans_b, preferred_element_type=F32))
        for h, hs in enumerate(heads):
            vb = hi_ref[:, hs].astype(BF16)
            blocks = []
            for i in range(c_len // fs):
                a = scores[h * (c_len // fs) + i]
                t_idx = fs * i + lax.broadcasted_iota(jnp.int32, a.shape, 0)
                s_idx = lax.broadcasted_iota(jnp.int32, a.shape, 1)
                a = jnp.where(s_idx <= t_idx, a, 0.0).astype(BF16)
                blocks.append(jnp.dot(a, vb[0:fs * (i + 1)], preferred_element_type=F32))
            intra.append(jnp.concatenate(blocks, axis=0))
        return intra

    def intra_direct(b, q, k, vb):
        row = lax.broadcasted_iota(jnp.int32, (sub, 1), 0)
        col = lax.broadcasted_iota(jnp.int32, (sub, sub), 1)
        blocks = []
        for i in range(n_sub):
            rs = slice(sub * i, sub * (i + 1))
            b_i, q_i, k_i = b[rs], q[rs], k[rs]
            diag = jnp.zeros((sub, sub), F32)
            for s in range(sub):
                decay = jnp.exp(jnp.where(row >= s, b_i - b_i[s:s + 1, :], -jnp.inf))
                a_col = jnp.sum(q_i * k_i[s:s + 1, :] * decay, axis=1, keepdims=True)
                diag = jnp.where(col == s, a_col, diag)
            o_i = jnp.dot(diag.astype(BF16), vb[rs], preferred_element_type=F32)
            if i > 0:
                prev = slice(0, sub * i)
                b_ref = block_start(b, i)
                q_hat = (q_i * jnp.exp(b_i - b_ref)).astype(BF16)
                k_hat = (k[prev] * jnp.exp(b_ref - b[prev])).astype(BF16)
                a_off = lax.dot_general(q_hat, k_hat, trans_b, preferred_element_type=F32)
                o_i = o_i + jnp.dot(a_off.astype(BF16), vb[prev], preferred_element_type=F32)
            blocks.append(o_i)
        return jnp.concatenate(blocks, axis=0)

    if factored:
        intra = intra_factored_all()
    else:
        intra = [intra_direct(b_all[:, hs], q_all[:, hs], k_all[:, hs], hi_ref[:, hs].astype(BF16))
                 for hs in (slice(HG_DIM * h, HG_DIM * (h + 1)) for h in range(HG_HEADS))]

    outs = []
    for h in range(HG_HEADS):
        hs = slice(HG_DIM * h, HG_DIM * (h + 1))
        v = hi_ref[:, hs]
        st = st_ref[h]
        inter = lax.dot_general(q_inter[:, hs].astype(BF16), st.astype(BF16), trans_b,
                                preferred_element_type=F32)
        o = inter + intra[h]
        st_ref[h] = jnp.exp(b_last[:, hs]) * st + jnp.dot(v.T.astype(BF16), k_carry[:, hs].astype(BF16),
                                                         preferred_element_type=F32)
        outs.append(_rmsnorm_rows(o, out_gain[:, hs]) * gate[:, hs])
    o_ref[...] = jnp.concatenate(outs, axis=1).astype(o_ref.dtype)


def _hgrn(hg5, hconst, ltri, b, t):
    rows = HG_CHUNK * HG_STEP_CHUNKS
    streams = hg5.reshape(b, t, HG_STREAMS * HG_WIDTH)
    col = lambda jcol: pl.BlockSpec((b, rows, HG_WIDTH), lambda s, jcol=jcol: (0, s, jcol))
    return pl.pallas_call(
        _hgrn_kernel,
        grid=(t // rows,),
        in_specs=[col(jcol) for jcol in range(HG_STREAMS)] + [_resident(hconst.shape), _resident(ltri.shape)],
        out_specs=pl.BlockSpec((b, rows, HG_WIDTH), lambda s: (0, s, 0)),
        out_shape=jax.ShapeDtypeStruct((b, t, HG_WIDTH), BF16),
        scratch_shapes=[pltpu.VMEM((b, HG_HEADS, HG_DIM, HG_DIM), F32)],
        compiler_params=_params(1),
        name="hgrn",
    )(*([streams] * HG_STREAMS), hconst, ltri).reshape(b * t, HG_WIDTH)


def _merge_kernel(x_ref, ya_ref, yh_ref, g_ref, wg_ref, wpa_ref, wph_ref, wo_ref, o_ref):
    x = x_ref[...]
    hb = _rmsnorm_rows(x, g_ref[...]).astype(BF16)
    gates = lax.dot_general(hb, wg_ref[...], _TRANS_B, preferred_element_type=F32)
    pa = jnp.dot(ya_ref[...], wpa_ref[...], preferred_element_type=F32)
    ph = jnp.dot(yh_ref[...], wph_ref[...], preferred_element_type=F32)
    merged = _sigmoid(gates[:, :D_MODEL]) * pa + _sigmoid(gates[:, D_MODEL:]) * ph
    o_ref[...] = x + jnp.dot(merged.astype(BF16), wo_ref[...], preferred_element_type=F32)


def _merge(x2, ya, yh, gain, wg, wpa, wph, wo, l, tm):
    nt = x2.shape[0]
    row = lambda w: pl.BlockSpec((tm, w), lambda i: (i, 0))
    return pl.pallas_call(
        _merge_kernel,
        grid=(nt // tm,),
        in_specs=[row(D_MODEL), row(ATTN_WIDTH), row(HG_WIDTH), _resident(gain.shape), _layer_of(wg, l),
                  _layer_of(wpa, l), _layer_of(wph, l), _layer_of(wo, l)],
        out_specs=row(D_MODEL),
        out_shape=jax.ShapeDtypeStruct((nt, D_MODEL), F32),
        compiler_params=_params(1),
        name="merge",
    )(x2, ya, yh, gain, wg, wpa, wph, wo)


def _ffn_kernel(x_ref, g_ref, wi_ref, wo_ref, o_ref):
    x = x_ref[...]
    hb = _rmsnorm_rows(x, g_ref[...]).astype(BF16)
    gu = jnp.dot(hb, wi_ref[...], preferred_element_type=F32)
    g = gu[:, :D_FF]
    act = g * _sigmoid(g) * gu[:, D_FF:]
    o_ref[...] = x + jnp.dot(act.astype(BF16), wo_ref[...], preferred_element_type=F32)


def _ffn(x2, gain, wi, wo, l, tm):
    nt = x2.shape[0]
    row = pl.BlockSpec((tm, D_MODEL), lambda i: (i, 0))
    return pl.pallas_call(
        _ffn_kernel,
        grid=(nt // tm,),
        in_specs=[row, _resident(gain.shape), _layer_of(wi, l), _layer_of(wo, l)],
        out_specs=row,
        out_shape=jax.ShapeDtypeStruct((nt, D_MODEL), F32),
        compiler_params=_params(1),
        name="ffn",
    )(x2, gain, wi, wo)


def _layer(x2, b, t, cos1, sin1, consts, w, l):
    hsum, hexp, ltri_attn, ltri_hgrn = consts
    qt, k, vt4, iq3t, ik3, iwt, hg5 = _mix_in(x2, w["norm_mix"], w["wa"], *w["wi"], w["wh"], w["qn"], w["kn"],
                                              cos1, sin1, hsum, hexp, w["hconst"], l, tm=ROW_TILE)
    y_attn = _dsa(w["logit_bound"], iq3t, ik3, iwt, qt, k, vt4, ltri_attn, b, t, topk=min(MAX_TOPK, t // 4))
    y_hgrn = _hgrn(hg5, w["hconst"], ltri_hgrn, b, t)
    x2 = _merge(x2, y_attn, y_hgrn, w["norm_mix"], w["wg"], w["wpa"], w["wph"], w["wout"], l, tm=ROW_TILE)
    return _ffn(x2, w["norm_ffn"], w["wffn_in"], w["wffn_out"], l, tm=ROW_TILE)


IN_WIDTHS = (ATTN_WIDTH, ATTN_WIDTH, ATTN_WIDTH, IDX_HEADS * IDX_DIM, IDX_DIM, IDX_HEADS,
             HG_WIDTH, HG_WIDTH, HG_WIDTH, HG_WIDTH, D_MODEL, D_MODEL)
IN_OFF = tuple(int(v) for v in np.concatenate([[0], np.cumsum(IN_WIDTHS)]))
IDX_COLS = IN_OFF[6] - IN_OFF[3]
IDX_COLS_PADDED = 3 * LANES
PACK_COLS = 256


def _pack_group_kernel(v_ref, *o_refs, depth, lane_tiles, real_cols):
    cb = o_refs[0].shape[1]
    stride = lane_tiles * depth
    for l in range(depth):
        wt = jnp.concatenate([v_ref[pl.ds(l + depth * j, cb, stride=stride), :] for j in range(lane_tiles)], axis=1)
        if real_cols < cb:
            wt = jnp.where(lax.broadcasted_iota(jnp.int32, wt.shape, 0) < real_cols, wt, 0.0)
        hi = wt.astype(BF16)
        o_refs[0][l] = hi
        if len(o_refs) > 1:
            o_refs[1][l] = (wt - hi.astype(F32)).astype(BF16)


def _pack_group(view, depth, d, first_col, n_cols, cols_per_step, real_cols=None, hi_lo=False):
    lane_tiles = d // LANES
    rows_per_col = lane_tiles * depth
    n_out = 2 if hi_lo else 1
    return pl.pallas_call(
        functools.partial(_pack_group_kernel, depth=depth, lane_tiles=lane_tiles,
                          real_cols=cols_per_step if real_cols is None else real_cols),
        grid=(n_cols // cols_per_step,),
        in_specs=[pl.BlockSpec((pl.Element(cols_per_step * rows_per_col), pl.Element(LANES)),
                               lambda i: ((first_col + i * cols_per_step) * rows_per_col, 0))],
        out_specs=[pl.BlockSpec((depth, cols_per_step, d), lambda i: (0, i, 0))] * n_out,
        out_shape=[jax.ShapeDtypeStruct((depth, n_cols, d), BF16)] * n_out,
        compiler_params=_params(1),
        name="pack_w_in",
    )(view)


def _pack_w_in(w_in):
    depth, d, n = w_in.shape
    lane_tiles = d // LANES
    view = w_in.transpose(2, 0, 1).reshape(n, depth, lane_tiles, LANES).transpose(0, 2, 1, 3).reshape(
        n * lane_tiles * depth, LANES)
    group = functools.partial(_pack_group, view, depth, d)
    wa, = group(IN_OFF[0], IN_OFF[3] - IN_OFF[0], PACK_COLS)
    wi_hi, wi_lo = group(IN_OFF[3], IDX_COLS_PADDED, IDX_COLS_PADDED, real_cols=IDX_COLS, hi_lo=True)
    wh, = group(IN_OFF[6], IN_OFF[10] - IN_OFF[6], PACK_COLS)
    wg, = group(IN_OFF[10], IN_OFF[12] - IN_OFF[10], PACK_COLS)
    return wa, wi_hi, wi_lo, wh, wg


ROPE_ROWS = 2048


def _rope_tables_kernel(pos_ref, inv_ref, sign_ref, cos_ref, sin_ref):
    ang = pos_ref[...].astype(F32) * inv_ref[...]
    cos_ref[...] = jnp.cos(ang)
    sin_ref[...] = jnp.sin(ang) * sign_ref[...]


def _rope_tables(positions):
    nt = positions.size
    half = HEAD_DIM // 2
    inv = ROPE_THETA ** (-jnp.arange(0, HEAD_DIM, 2, dtype=F32) / HEAD_DIM)
    inv_row = jnp.tile(inv, LANES // half).reshape(1, LANES)
    sign_row = jnp.tile(jnp.concatenate([-jnp.ones(half, F32), jnp.ones(half, F32)]), LANES // HEAD_DIM
                        ).reshape(1, LANES)
    row = pl.BlockSpec((ROPE_ROWS, LANES), lambda i: (i, 0))
    return pl.pallas_call(
        _rope_tables_kernel,
        grid=(nt // ROPE_ROWS,),
        in_specs=[pl.BlockSpec((ROPE_ROWS, 1), lambda i: (i, 0)), _resident(inv_row.shape),
                  _resident(sign_row.shape)],
        out_specs=[row, row],
        out_shape=[jax.ShapeDtypeStruct((nt, LANES), F32)] * 2,
        compiler_params=_params(1),
        name="rope_tables",
    )(positions.reshape(nt, 1), inv_row, sign_row)


def _constants():
    head_of_lane = np.arange(ATTN_WIDTH) // HEAD_DIM
    hsum = (head_of_lane[:, None] == np.arange(LANES)[None, :]).astype(np.float32)
    hexp = hsum.T.copy()
    r = np.arange(KEY_BLOCK)
    ltri_attn = (r[None, :] < r[:, None]).astype(np.float32)
    r = np.arange(HG_CHUNK)
    ltri_hgrn = (r[None, :] <= r[:, None]).astype(np.float32)
    return jnp.asarray(hsum, BF16), jnp.asarray(hexp, BF16), jnp.asarray(ltri_attn, BF16), jnp.asarray(ltri_hgrn, BF16)


def kernel(x, positions, w_in, w_proj_attn, w_proj_hgrn, w_out, norm_mix, norm_ffn, q_norm, k_norm, hgrn_norm,
           hgrn_lower_bound, w_ffn_in, w_ffn_out):
    b, t, d = x.shape
    depth = w_in.shape[0]
    nt = b * t

    cos1, sin1 = _rope_tables(positions)

    lb_all = jnp.cumsum(jax.nn.softmax(hgrn_lower_bound.astype(F32), axis=0), axis=0)
    lb_all = lb_all - lb_all[:1]

    consts = _constants()
    wa_all, wi_hi_all, wi_lo_all, wh_all, wg_all = _pack_w_in(w_in)
    wpa_all, wph_all, wout_all = w_proj_attn.astype(BF16), w_proj_hgrn.astype(BF16), w_out.astype(BF16)
    wffn_in_all, wffn_out_all = w_ffn_in.astype(BF16), w_ffn_out.astype(BF16)

    x2 = x.reshape(nt, d)
    for l in range(depth):
        lb = lb_all[l]
        hconst = jnp.zeros((SUBLANES, HG_WIDTH), F32)
        hconst = hconst.at[0].set(jnp.log(lb)).at[1].set(jnp.log1p(-lb)).at[2].set(1.0 - lb)
        hconst = hconst.at[3].set(jnp.tile(hgrn_norm[l], HG_HEADS))
        w = {
            "norm_mix": norm_mix[l].reshape(1, d),
            "norm_ffn": norm_ffn[l].reshape(1, d),
            "wa": wa_all,
            "wi": (wi_hi_all, wi_lo_all),
            "wh": wh_all,
            "wg": wg_all,
            "qn": jnp.tile(q_norm[l], ATTN_HEADS).reshape(1, ATTN_WIDTH),
            "kn": jnp.tile(k_norm[l], ATTN_HEADS).reshape(1, ATTN_WIDTH),
            "logit_bound": (HEAD_DIM ** 0.5 * LOG2_E * jnp.max(jnp.abs(q_norm[l])) * jnp.max(jnp.abs(k_norm[l]))
                            ).reshape(1).astype(F32),
            "hconst": hconst,
            "wpa": wpa_all,
            "wph": wph_all,
            "wout": wout_all,
            "wffn_in": wffn_in_all,
            "wffn_out": wffn_out_all,
        }
        x2 = _layer(x2, b, t, cos1, sin1, consts, w, l)
    return x2.reshape(b, t, d)
```

```python
import functools

import numpy as np
import jax
import jax.numpy as jnp
from jax import lax
from jax.experimental import pallas as pl
from jax.experimental.pallas import tpu as pltpu

D_MODEL = 1024
ATTN_HEADS = 8
HEAD_DIM = 64
ATTN_WIDTH = ATTN_HEADS * HEAD_DIM
IDX_HEADS = 4
IDX_DIM = 64
MAX_TOPK = 256
HG_HEADS = 4
HG_DIM = 128
HG_WIDTH = HG_HEADS * HG_DIM
HG_CHUNK = 64
HG_STREAMS = 5
HG_STEP_CHUNKS = 8
HG_SUB = 16
HG_FACT = 32
MAX_SAFE_LOGIT = 60.0
MAX_SAFE_EXPONENT = 80.0
D_FF = 2816
ROPE_THETA = 10000.0
EPS = 1e-6
LOG2_E = 1.4426950408889634

LANES = 128
SUBLANES = 8
ROW_TILE = 512
VMEM_LIMIT_BYTES = 56 * 1024 * 1024

KEY_BLOCK = 256
VT_ROWS = 80
INT_MIN = -(2 ** 31)
HALF_MIN = -(2 ** 15)
NEG_BIG = -1e30

F32 = jnp.float32
BF16 = jnp.bfloat16


def _resident(shape):
    nd = len(shape)
    return pl.BlockSpec(shape, lambda *_: (0,) * nd, pipeline_mode=pl.Buffered(1))


def _layer_of(stacked, l):
    return pl.BlockSpec((None,) + stacked.shape[1:], lambda *_: (l, 0, 0), pipeline_mode=pl.Buffered(1))


def _params(n_axes):
    return pltpu.CompilerParams(dimension_semantics=("arbitrary",) * n_axes,
                                vmem_limit_bytes=VMEM_LIMIT_BYTES)


def _rmsnorm_rows(x, gain):
    return x * lax.rsqrt(jnp.mean(x * x, axis=-1, keepdims=True) + EPS) * gain


def _sigmoid(x):
    return 1.0 / (1.0 + jnp.exp(-x))


def _rope(x, cos, sin_signed):
    w = x.shape[-1]
    lane = lax.broadcasted_iota(jnp.int32, x.shape, 1)
    first_half = (lane & (HEAD_DIM - 1)) < HEAD_DIM // 2
    partner = jnp.where(first_half, pltpu.roll(x, w - HEAD_DIM // 2, 1), pltpu.roll(x, HEAD_DIM // 2, 1))
    return x * cos + partner * sin_signed


_TRANS_B = (((1,), (1,)), ((), ()))


def _dot_hi_lo(a, b_hi, b_lo=None, trans_b=False):
    dims = _TRANS_B if trans_b else (((1,), (0,)), ((), ()))
    dot = lambda x, y: lax.dot_general(x, y, dims, preferred_element_type=F32)
    a_hi = a.astype(BF16)
    a_lo = (a - a_hi.astype(F32)).astype(BF16)
    out = dot(a_hi, b_hi) + dot(a_lo, b_hi)
    if b_lo is not None:
        out = out + dot(a_hi, b_lo)
    return out


def _dot_exact_lhs(a, b):
    b1 = b.astype(BF16)
    r1 = b - b1.astype(F32)
    b2 = r1.astype(BF16)
    b3 = (r1 - b2.astype(F32)).astype(BF16)
    return (jnp.dot(a, b1, preferred_element_type=F32) + jnp.dot(a, b2, preferred_element_type=F32)
            + jnp.dot(a, b3, preferred_element_type=F32))


def _mix_in_kernel(x_ref, g_ref, wa_ref, wi_ref, wil_ref, wh_ref, qn_ref, kn_ref, cos_ref, sin_ref, hsum_ref, hexp_ref,
                   hc_ref, qt_ref, k_ref, vt_ref, iq3_ref, ik3_ref, iw_ref, hg_ref):
    h = _rmsnorm_rows(x_ref[...], g_ref[...])
    hb = h.astype(BF16)
    cos1 = cos_ref[...]
    sin1 = sin_ref[...]
    cos4 = jnp.concatenate([cos1] * 4, axis=1)
    sin4 = jnp.concatenate([sin1] * 4, axis=1)

    def head_norm(a, gain):
        ss = _dot_hi_lo(a * a, hsum_ref[...])
        r = lax.rsqrt(ss * (1.0 / HEAD_DIM) + EPS)
        return a * _dot_hi_lo(r, hexp_ref[...]) * gain

    pa = lax.dot_general(hb, wa_ref[...], _TRANS_B, preferred_element_type=F32)
    q = _rope(head_norm(pa[:, :ATTN_WIDTH], qn_ref[...]), cos4, sin4) * (HEAD_DIM ** -0.5 * LOG2_E)
    k = _rope(head_norm(pa[:, ATTN_WIDTH:2 * ATTN_WIDTH], kn_ref[...]), cos4, sin4)
    qt_ref[...] = q.T.astype(BF16)
    k_ref[...] = k.astype(BF16)
    v = pa[:, 2 * ATTN_WIDTH:]
    ones_pad = jnp.where(lax.broadcasted_iota(jnp.int32, (VT_ROWS - HEAD_DIM, KEY_BLOCK), 0) == 0, 1.0, 0.0)
    for s in range(v.shape[0] // KEY_BLOCK):
        vt = v[KEY_BLOCK * s:KEY_BLOCK * (s + 1)].T
        rows = []
        for hd in range(ATTN_HEADS):
            rows += [vt[HEAD_DIM * hd:HEAD_DIM * (hd + 1)], ones_pad]
        vt_ref[s] = jnp.concatenate(rows, axis=0).astype(BF16)

    pi = _dot_hi_lo(h, wi_ref[...], wil_ref[...], trans_b=True)
    nq = IDX_HEADS * IDX_DIM
    iq = _rope(pi[:, :nq], cos4[:, :nq], sin4[:, :nq]) * (IDX_DIM ** -0.5)
    slab = pi[:, nq:]
    ikw_t = _rope(slab, cos1, sin1).T
    iw_ref[...] = slab.T[IDX_DIM:IDX_DIM + SUBLANES] * (IDX_HEADS ** -0.5)

    def hi_lo(a):
        hi = a.astype(BF16).astype(F32)
        return hi, (a - hi).astype(BF16).astype(F32)

    k_hi, k_lo = hi_lo(ikw_t[:IDX_DIM])
    ik3_ref[...] = jnp.concatenate([k_hi, k_lo, k_hi, jnp.zeros_like(k_hi)], axis=0).T.astype(BF16)
    iq_t = iq.T
    rows = []
    for hd in range(IDX_HEADS):
        q_hi, q_lo = hi_lo(iq_t[IDX_DIM * hd:IDX_DIM * (hd + 1)])
        rows += [q_hi, q_hi, q_lo, jnp.zeros_like(q_hi)]
    iq3_ref[...] = jnp.concatenate(rows, axis=0).astype(BF16)

    ph = lax.dot_general(hb, wh_ref[...], _TRANS_B, preferred_element_type=F32)
    w = HG_WIDTH
    hq, hf, hg = ph[:, :w], ph[:, w:2 * w], ph[:, 3 * w:]
    log_lb, log1m_lb, one_m_lb = hc_ref[0:1, :], hc_ref[1:2, :], hc_ref[2:3, :]
    c = log1m_lb + jnp.minimum(hf, 0.0) - jnp.log1p(jnp.exp(-jnp.abs(hf)))
    hg_ref[:, :w] = hq * _sigmoid(hq)
    hg_ref[:, w:2 * w] = jnp.maximum(log_lb, c) + jnp.log1p(jnp.exp(-jnp.abs(log_lb - c)))
    hg_ref[:, 2 * w:3 * w] = one_m_lb * _sigmoid(-hf)
    hg_ref[:, 3 * w:4 * w] = ph[:, 2 * w:3 * w]
    hg_ref[:, 4 * w:] = hg * _sigmoid(hg)


def _mix_in(x2, gain, wa, wi, wil, wh, qn, kn, cos1, sin1, hsum, hexp, hconst, l, tm):
    nt = x2.shape[0]
    kb = KEY_BLOCK
    row = lambda w: pl.BlockSpec((tm, w), lambda i: (i, 0))
    col = lambda w: pl.BlockSpec((w, tm), lambda i: (0, i))
    return pl.pallas_call(
        _mix_in_kernel,
        grid=(nt // tm,),
        in_specs=[row(D_MODEL), _resident(gain.shape), _layer_of(wa, l), _layer_of(wi, l), _layer_of(wil, l),
                  _layer_of(wh, l),
                  _resident(qn.shape), _resident(kn.shape), row(LANES), row(LANES),
                  _resident(hsum.shape), _resident(hexp.shape), _resident(hconst.shape)],
        out_specs=[col(ATTN_WIDTH), row(ATTN_WIDTH),
                   pl.BlockSpec((tm // kb, ATTN_HEADS * VT_ROWS, kb), lambda i: (i, 0, 0)),
                   col(IDX_HEADS * 4 * IDX_DIM), row(4 * IDX_DIM), col(SUBLANES), row(HG_STREAMS * HG_WIDTH)],
        out_shape=[jax.ShapeDtypeStruct((ATTN_WIDTH, nt), BF16), jax.ShapeDtypeStruct((nt, ATTN_WIDTH), BF16),
                   jax.ShapeDtypeStruct((nt // kb, ATTN_HEADS * VT_ROWS, kb), BF16),
                   jax.ShapeDtypeStruct((IDX_HEADS * 4 * IDX_DIM, nt), BF16),
                   jax.ShapeDtypeStruct((nt, 4 * IDX_DIM), BF16), jax.ShapeDtypeStruct((SUBLANES, nt), F32),
                   jax.ShapeDtypeStruct((nt, HG_STREAMS * HG_WIDTH), F32)],
        compiler_params=_params(1),
        name="mix_in",
    )(x2, gain, wa, wi, wil, wh, qn, kn, cos1, sin1, hsum, hexp, hconst)


def _dsa_kernel(bound_ref, iq3_ref, iq3_next_ref, ik3_ref, iw_ref, iw_next_ref, qt_ref, k_ref, vt_ref, ltri_ref,
                o_ref, hi2_ref, lo2_ref, qz_ref, lg_ref, m_ref, alpha_ref, acc_ref, *, topk, nkc):
    kb = KEY_BLOCK
    j = pl.program_id(1)
    nk = j + 1
    int_min = jnp.int32(INT_MIN)
    slot = j % 2
    hi_ref, lo_ref = hi2_ref.at[slot], lo2_ref.at[slot]
    hi_next_ref, lo_next_ref = hi2_ref.at[1 - slot], lo2_ref.at[1 - slot]

    zeros = jnp.zeros((HEAD_DIM, kb), BF16)
    for h in range(ATTN_HEADS):
        qh = qt_ref[HEAD_DIM * h:HEAD_DIM * (h + 1), :]
        qz_ref[h] = jnp.concatenate([qh, zeros] if h % 2 == 0 else [zeros, qh], axis=0)

    def score_head(c, h, next_block):
        q3, w = (iq3_next_ref, iw_next_ref) if next_block else (iq3_ref, iw_ref)
        rel = jnp.dot(ik3_ref[c], q3[4 * IDX_DIM * h:4 * IDX_DIM * (h + 1), :], preferred_element_type=F32)
        return w[h:h + 1, :] * jnp.maximum(rel, 0.0)

    def store_keys(c, score, next_block, diagonal):
        bits = pltpu.bitcast(score, jnp.int32)
        key = jnp.where(bits < 0, int_min - bits, bits)
        if diagonal:
            kpos = lax.broadcasted_iota(jnp.int32, (kb, kb), 0)
            qpos = lax.broadcasted_iota(jnp.int32, (kb, kb), 1)
            key = jnp.where(kpos <= qpos, key, int_min)
        hi_dst, lo_dst = (hi_next_ref, lo_next_ref) if next_block else (hi_ref, lo_ref)
        hi_dst[c] = lax.shift_right_arithmetic(key, 16).astype(jnp.int16)
        lo_dst[c] = (key ^ 0x8000).astype(jnp.int16)

    @pl.when(j == 0)
    def _():
        store_keys(0, sum(score_head(0, h, False) for h in range(IDX_HEADS)), False, True)

    one_b, zero_b = jnp.ones((), BF16), jnp.zeros((), BF16)
    one_i, zero_i = jnp.ones((), jnp.int16), jnp.zeros((), jnp.int16)
    rows16 = 16

    hi_ref[nk] = jnp.full((kb, kb), HALF_MIN, jnp.int16)
    lo_ref[nk] = jnp.full((kb, kb), HALF_MIN, jnp.int16)

    def count(hit_fn):
        def fold(hit):
            parts = [hit[rows16 * r:rows16 * (r + 1)] for r in range(kb // rows16)]
            while len(parts) > 1:
                parts = [a + b for a, b in zip(parts[::2], parts[1::2])]
            return parts[0]

        def four_chunks(i, cnt):
            return cnt + ((fold(hit_fn(4 * i)) + fold(hit_fn(4 * i + 1)))
                          + (fold(hit_fn(4 * i + 2)) + fold(hit_fn(4 * i + 3))))

        def two_chunks(i, cnt):
            return cnt + (fold(hit_fn(2 * i)) + fold(hit_fn(2 * i + 1)))

        cnt = lax.fori_loop(0, nk // 4, four_chunks, jnp.zeros((rows16, kb), jnp.int16))
        cnt = lax.fori_loop(2 * (nk // 4), (nk + 1) // 2, two_chunks, cnt)
        return jnp.sum(cnt.astype(jnp.int32), axis=0, keepdims=True).astype(F32)

    def bisect_half(ref, target, first_pass_hits=None):
        def plain_hits(cand):
            return lambda c: jnp.where(ref[c] >= cand, one_i, zero_i)

        def bit_step(i, carry, hits_of=plain_hits):
            thr_u, n_above = carry
            cand_u = thr_u | lax.shift_left(jnp.int32(1), 15 - i)
            cand = (cand_u + HALF_MIN).astype(jnp.int16)
            cnt = count(hits_of(cand))
            accepted = cnt >= target
            return jnp.where(accepted, cand_u, thr_u), jnp.where(accepted, n_above, cnt)

        carry, first = (jnp.zeros((1, kb), jnp.int32), jnp.zeros((1, kb), F32)), 0
        if first_pass_hits is not None:
            carry, first = bit_step(0, carry, first_pass_hits), 1
        thr_u, n_above = lax.fori_loop(first, 16, bit_step, carry)
        return thr_u + HALF_MIN, n_above

    thr_hi32, n_gt_hi = bisect_half(hi_ref, float(topk))
    thr_hi = thr_hi32.astype(jnp.int16)

    def keep_matching_low_and_hit(cand):
        def hit(c):
            matching = jnp.where(hi_ref[c] == thr_hi, lo_ref[c], jnp.int16(HALF_MIN))
            lo_ref[c] = matching
            return jnp.where(matching >= cand, one_i, zero_i)
        return hit

    thr_lo32, n_gt_lo = bisect_half(lo_ref, topk - n_gt_hi, keep_matching_low_and_hit)
    thr_lo = thr_lo32.astype(jnp.int16)
    n_gt = n_gt_hi + n_gt_lo
    is_marker = (thr_hi32 == HALF_MIN) & (thr_lo32 == HALF_MIN)
    need = jnp.where(is_marker, 0.0, topk - n_gt)

    def bias_chunk(c, ties_before):
        hi, lo = hi_ref[c], lo_ref[c]
        above = jnp.where(hi > thr_hi, one_b, jnp.where(lo > thr_lo, one_b, zero_b)).astype(F32)
        eq_b = jnp.where(hi == thr_hi, jnp.where(lo == thr_lo, one_b, zero_b), zero_b)
        eq = eq_b.astype(F32)
        rank = jnp.dot(ltri_ref[...], eq_b, preferred_element_type=F32) + ties_before
        chosen = above + jnp.where(rank < need, eq, 0.0)
        bias = jnp.where(chosen > 0.5, 0.0, NEG_BIG).astype(BF16)
        hi_ref[c] = pltpu.bitcast(bias, jnp.int16)
        return ties_before + jnp.sum(eq, axis=0, keepdims=True)


    def attend(bounded):
        acc_ref[...] = jnp.zeros(acc_ref.shape, F32)
        if not bounded:
            m_ref[...] = jnp.full(m_ref.shape, NEG_BIG, F32)

        def logits_head(c, h, bias):
            kk = k_ref[c, :, LANES * (h // 2):LANES * (h // 2 + 1)]
            lgb = jnp.dot(kk, qz_ref[h], preferred_element_type=F32) + bias
            lg_ref[h] = lgb
            if not bounded:
                m_old = m_ref[h]
                m_new = jnp.maximum(m_old, jnp.max(lgb, axis=0, keepdims=True))
                alpha_ref[h] = jnp.exp2(m_old - m_new)
                m_ref[h] = m_new

        def softmax_pv_head(c, h):
            p = jnp.exp2(lg_ref[h] if bounded else lg_ref[h] - m_ref[h])
            pv = jnp.dot(vt_ref[c, VT_ROWS * h:VT_ROWS * (h + 1), :], p.astype(BF16),
                         preferred_element_type=F32)
            acc_ref[h] = (acc_ref[h] if bounded else alpha_ref[h] * acc_ref[h]) + pv

        def chunk_bias(c):
            return pltpu.bitcast(hi_ref[c], BF16).astype(F32)

        def next_block_scores(c, h, partial):
            if h % (ATTN_HEADS // IDX_HEADS) != ATTN_HEADS // IDX_HEADS - 1:
                return partial
            term = score_head(c, h // (ATTN_HEADS // IDX_HEADS), True)
            return term if partial is None else partial + term

        ties = bias_chunk(0, jnp.zeros((1, kb), F32))
        bias0 = chunk_bias(0)
        ties = bias_chunk(1, ties)
        partial = None
        for h in range(ATTN_HEADS):
            logits_head(0, h, bias0)
            partial = next_block_scores(0, h, partial)
        store_keys(0, partial, True, False)

        def pipelined(c, ties_before):
            bias = chunk_bias(c)
            ties_after = bias_chunk(c + 1, ties_before)
            partial = None
            for h in range(ATTN_HEADS):
                softmax_pv_head(c - 1, h)
                logits_head(c, h, bias)
                partial = next_block_scores(c, h, partial)
            store_keys(c, partial, True, False)
            return ties_after

        lax.fori_loop(1, nk, pipelined, ties)
        c_diag = jnp.minimum(nk, nkc - 1)
        partial = None
        for h in range(ATTN_HEADS):
            softmax_pv_head(nk - 1, h)
            partial = next_block_scores(c_diag, h, partial)
        store_keys(c_diag, partial, True, True)

    logits_bounded = bound_ref[0] < MAX_SAFE_LOGIT

    @pl.when(logits_bounded)
    def _():
        attend(True)

    @pl.when(jnp.logical_not(logits_bounded))
    def _():
        attend(False)

    out_t = [acc_ref[h][:HEAD_DIM] / acc_ref[h][HEAD_DIM:HEAD_DIM + 1] for h in range(ATTN_HEADS)]
    o_ref[...] = jnp.concatenate(out_t, axis=0).T.astype(o_ref.dtype)


def _dsa(logit_bound, iq3t, ik3, iwt, qt, k, vt4, ltri, b, t, topk):
    kb = KEY_BLOCK
    nkc = t // kb
    chunked = lambda a: a.reshape(b * nkc, kb, a.shape[-1])
    per_batch = lambda shape: pl.BlockSpec((nkc,) + shape, lambda i, j: (i, 0, 0), pipeline_mode=pl.Buffered(1))
    q_cols = lambda rows: pl.BlockSpec((rows, kb), lambda i, j: (0, i * nkc + j))
    next_q_cols = lambda rows: pl.BlockSpec((rows, kb), lambda i, j: (0, i * nkc + jnp.minimum(j + 1, nkc - 1)))
    return pl.pallas_call(
        functools.partial(_dsa_kernel, topk=topk, nkc=nkc),
        grid=(b, nkc),
        in_specs=[
            pl.BlockSpec(memory_space=pltpu.SMEM),
            q_cols(IDX_HEADS * 4 * IDX_DIM),
            next_q_cols(IDX_HEADS * 4 * IDX_DIM),
            per_batch((kb, 4 * IDX_DIM)),
            q_cols(SUBLANES),
            next_q_cols(SUBLANES),
            q_cols(ATTN_WIDTH),
            per_batch((kb, ATTN_WIDTH)),
            per_batch((ATTN_HEADS * VT_ROWS, kb)),
            _resident(ltri.shape),
        ],
        out_specs=pl.BlockSpec((kb, ATTN_WIDTH), lambda i, j: (i * nkc + j, 0)),
        out_shape=jax.ShapeDtypeStruct((b * t, ATTN_WIDTH), BF16),
        scratch_shapes=[
            pltpu.VMEM((2, nkc + 1, kb, kb), jnp.int16),
            pltpu.VMEM((2, nkc + 1, kb, kb), jnp.int16),
            pltpu.VMEM((ATTN_HEADS, LANES, kb), BF16),
            pltpu.VMEM((ATTN_HEADS, kb, kb), F32),
            pltpu.VMEM((ATTN_HEADS, 1, kb), F32),
            pltpu.VMEM((ATTN_HEADS, 1, kb), F32),
            pltpu.VMEM((ATTN_HEADS, VT_ROWS, kb), F32),
        ],
        compiler_params=_params(2),
        name="dsa",
    )(logit_bound, iq3t, iq3t, chunked(ik3), iwt, iwt, qt, chunked(k), vt4, ltri)


def _hgrn_kernel(q_ref, lf_ref, k_ref, v_ref, gate_ref, hc_ref, lt_ref, o_ref, st_ref):
    @pl.when(pl.program_id(0) == 0)
    def _():
        st_ref[...] = jnp.zeros(st_ref.shape, F32)

    units = [(bi, ci) for ci in range(HG_STEP_CHUNKS) for bi in range(q_ref.shape[0])]
    rows = lambda ci: slice(HG_CHUNK * ci, HG_CHUNK * (ci + 1))
    cums = {(bi, ci): _dot_exact_lhs(lt_ref[...], lf_ref[bi, rows(ci), :]) for bi, ci in units}

    drops = []
    for b_all in cums.values():
        ends = [b_all[HG_FACT * (i + 1) - 1:HG_FACT * (i + 1), :] for i in range(HG_CHUNK // HG_FACT)]
        drops += [(ends[i - 1] if i > 0 else 0.0) - ends[i] for i in range(len(ends))]
    factored_ok = jnp.max(functools.reduce(jnp.maximum, drops)) < MAX_SAFE_EXPONENT

    def run(factored):
        for bi, ci in units:
            _hgrn_chunk(q_ref[bi, rows(ci), :], cums[(bi, ci)], k_ref[bi, rows(ci), :], v_ref.at[bi, rows(ci), :],
                        gate_ref[bi, rows(ci), :], hc_ref[3:4, :], o_ref.at[bi, rows(ci), :], st_ref.at[bi], factored)

    @pl.when(factored_ok)
    def _():
        run(True)

    @pl.when(jnp.logical_not(factored_ok))
    def _():
        run(False)


def _hgrn_chunk(q_all, b_all, k_all, hi_ref, gate, out_gain, o_ref, st_ref, factored):
    c_len, sub = HG_CHUNK, HG_SUB
    b_last = b_all[c_len - 1:c_len, :]
    q_inter = q_all * jnp.exp(b_all)
    k_carry = k_all * jnp.exp(b_last - b_all)

    n_sub = c_len // sub
    trans_b = (((1,), (1,)), ((), ()))

    def block_start(b, i):
        return b[sub * i - 1:sub * i, :] if i > 0 else jnp.zeros_like(b[0:1, :])

    def intra_factored_all():
        fs = HG_FACT
        heads = [slice(HG_DIM * h, HG_DIM * (h + 1)) for h in range(HG_HEADS)]
        scores, intra = [], []
        for hs in heads:
            b, q, k = b_all[:, hs], q_all[:, hs], k_all[:, hs]
            for i in range(c_len // fs):
                rs, upto = slice(fs * i, fs * (i + 1)), slice(0, fs * (i + 1))
                b_ref = b[fs * i - 1:fs * i, :] if i > 0 else jnp.zeros_like(b[0:1, :])
                q_hat = (q[rs] * jnp.exp(b[rs] - b_ref)).astype(BF16)
                k_hat = (k[upto] * jnp.exp(b_ref - b[upto])).astype(BF16)
                scores.append(lax.dot_general(q_hat, k_hat, trans_b, preferred_element_type=F32))
        for h, hs in enumerate(heads):
            vb = hi_ref[:, hs].astype(BF16)
            blocks = []
            for i in range(c_len // fs):
                a = scores[h * (c_len // fs) + i]
                t_idx = fs * i + lax.broadcasted_iota(jnp.int32, a.shape, 0)
                s_idx = lax.broadcasted_iota(jnp.int32, a.shape, 1)
                a = jnp.where(s_idx <= t_idx, a, 0.0).astype(BF16)
                blocks.append(jnp.dot(a, vb[0:fs * (i + 1)], preferred_element_type=F32))
            intra.append(jnp.concatenate(blocks, axis=0))
        return intra

    def intra_direct(b, q, k, vb):
        row = lax.broadcasted_iota(jnp.int32, (sub, 1), 0)
        col = lax.broadcasted_iota(jnp.int32, (sub, sub), 1)
        blocks = []
        for i in range(n_sub):
            rs = slice(sub * i, sub * (i + 1))
            b_i, q_i, k_i = b[rs], q[rs], k[rs]
            diag = jnp.zeros((sub, sub), F32)
            for s in range(sub):
                decay = jnp.exp(jnp.where(row >= s, b_i - b_i[s:s + 1, :], -jnp.inf))
                a_col = jnp.sum(q_i * k_i[s:s + 1, :] * decay, axis=1, keepdims=True)
                diag = jnp.where(col == s, a_col, diag)
            o_i = jnp.dot(diag.astype(BF16), vb[rs], preferred_element_type=F32)
            if i > 0:
                prev = slice(0, sub * i)
                b_ref = block_start(b, i)
                q_hat = (q_i * jnp.exp(b_i - b_ref)).astype(BF16)
                k_hat = (k[prev] * jnp.exp(b_ref - b[prev])).astype(BF16)
                a_off = lax.dot_general(q_hat, k_hat, trans_b, preferred_element_type=F32)
                o_i = o_i + jnp.dot(a_off.astype(BF16), vb[prev], preferred_element_type=F32)
            blocks.append(o_i)
        return jnp.concatenate(blocks, axis=0)

    if factored:
        intra = intra_factored_all()
    else:
        intra = [intra_direct(b_all[:, hs], q_all[:, hs], k_all[:, hs], hi_ref[:, hs].astype(BF16))
                 for hs in (slice(HG_DIM * h, HG_DIM * (h + 1)) for h in range(HG_HEADS))]

    outs = []
    for h in range(HG_HEADS):
        hs = slice(HG_DIM * h, HG_DIM * (h + 1))
        v = hi_ref[:, hs]
        st = st_ref[h]
        inter = lax.dot_general(q_inter[:, hs].astype(BF16), st.astype(BF16), trans_b,
                                preferred_element_type=F32)
        o = inter + intra[h]
        st_ref[h] = jnp.exp(b_last[:, hs]) * st + jnp.dot(v.T.astype(BF16), k_carry[:, hs].astype(BF16),
                                                         preferred_element_type=F32)
        outs.append(_rmsnorm_rows(o, out_gain[:, hs]) * gate[:, hs])
    o_ref[...] = jnp.concatenate(outs, axis=1).astype(o_ref.dtype)


def _hgrn(hg5, hconst, ltri, b, t):
    rows = HG_CHUNK * HG_STEP_CHUNKS
    streams = hg5.reshape(b, t, HG_STREAMS * HG_WIDTH)
    col = lambda jcol: pl.BlockSpec((b, rows, HG_WIDTH), lambda s, jcol=jcol: (0, s, jcol))
    return pl.pallas_call(
        _hgrn_kernel,
        grid=(t // rows,),
        in_specs=[col(jcol) for jcol in range(HG_STREAMS)] + [_resident(hconst.shape), _resident(ltri.shape)],
        out_specs=pl.BlockSpec((b, rows, HG_WIDTH), lambda s: (0, s, 0)),
        out_shape=jax.ShapeDtypeStruct((b, t, HG_WIDTH), BF16),
        scratch_shapes=[pltpu.VMEM((b, HG_HEADS, HG_DIM, HG_DIM), F32)],
        compiler_params=_params(1),
        name="hgrn",
    )(*([streams] * HG_STREAMS), hconst, ltri).reshape(b * t, HG_WIDTH)


def _merge_kernel(x_ref, ya_ref, yh_ref, g_ref, wg_ref, wpa_ref, wph_ref, wo_ref, o_ref):
    x = x_ref[...]
    hb = _rmsnorm_rows(x, g_ref[...]).astype(BF16)
    gates = lax.dot_general(hb, wg_ref[...], _TRANS_B, preferred_element_type=F32)
    pa = jnp.dot(ya_ref[...], wpa_ref[...], preferred_element_type=F32)
    ph = jnp.dot(yh_ref[...], wph_ref[...], preferred_element_type=F32)
    merged = _sigmoid(gates[:, :D_MODEL]) * pa + _sigmoid(gates[:, D_MODEL:]) * ph
    o_ref[...] = x + jnp.dot(merged.astype(BF16), wo_ref[...], preferred_element_type=F32)


def _merge(x2, ya, yh, gain, wg, wpa, wph, wo, l, tm):
    nt = x2.shape[0]
    row = lambda w: pl.BlockSpec((tm, w), lambda i: (i, 0))
    return pl.pallas_call(
        _merge_kernel,
        grid=(nt // tm,),
        in_specs=[row(D_MODEL), row(ATTN_WIDTH), row(HG_WIDTH), _resident(gain.shape), _layer_of(wg, l),
                  _layer_of(wpa, l), _layer_of(wph, l), _layer_of(wo, l)],
        out_specs=row(D_MODEL),
        out_shape=jax.ShapeDtypeStruct((nt, D_MODEL), F32),
        compiler_params=_params(1),
        name="merge",
    )(x2, ya, yh, gain, wg, wpa, wph, wo)


def _ffn_kernel(x_ref, g_ref, wi_ref, wo_ref, o_ref):
    x = x_ref[...]
    hb = _rmsnorm_rows(x, g_ref[...]).astype(BF16)
    gu = jnp.dot(hb, wi_ref[...], preferred_element_type=F32)
    g = gu[:, :D_FF]
    act = g * _sigmoid(g) * gu[:, D_FF:]
    o_ref[...] = x + jnp.dot(act.astype(BF16), wo_ref[...], preferred_element_type=F32)


def _ffn(x2, gain, wi, wo, l, tm):
    nt = x2.shape[0]
    row = pl.BlockSpec((tm, D_MODEL), lambda i: (i, 0))
    return pl.pallas_call(
        _ffn_kernel,
        grid=(nt // tm,),
        in_specs=[row, _resident(gain.shape), _layer_of(wi, l), _layer_of(wo, l)],
        out_specs=row,
        out_shape=jax.ShapeDtypeStruct((nt, D_MODEL), F32),
        compiler_params=_params(1),
        name="ffn",
    )(x2, gain, wi, wo)


def _layer(x2, b, t, cos1, sin1, consts, w, l):
    hsum, hexp, ltri_attn, ltri_hgrn = consts
    qt, k, vt4, iq3t, ik3, iwt, hg5 = _mix_in(x2, w["norm_mix"], w["wa"], *w["wi"], w["wh"], w["qn"], w["kn"],
                                              cos1, sin1, hsum, hexp, w["hconst"], l, tm=ROW_TILE)
    y_attn = _dsa(w["logit_bound"], iq3t, ik3, iwt, qt, k, vt4, ltri_attn, b, t, topk=min(MAX_TOPK, t // 4))
    y_hgrn = _hgrn(hg5, w["hconst"], ltri_hgrn, b, t)
    x2 = _merge(x2, y_attn, y_hgrn, w["norm_mix"], w["wg"], w["wpa"], w["wph"], w["wout"], l, tm=ROW_TILE)
    return _ffn(x2, w["norm_ffn"], w["wffn_in"], w["wffn_out"], l, tm=ROW_TILE)


IN_WIDTHS = (ATTN_WIDTH, ATTN_WIDTH, ATTN_WIDTH, IDX_HEADS * IDX_DIM, IDX_DIM, IDX_HEADS,
             HG_WIDTH, HG_WIDTH, HG_WIDTH, HG_WIDTH, D_MODEL, D_MODEL)
IN_OFF = tuple(int(v) for v in np.concatenate([[0], np.cumsum(IN_WIDTHS)]))
IDX_COLS = IN_OFF[6] - IN_OFF[3]
IDX_COLS_PADDED = 3 * LANES
PACK_COLS = 256


def _pack_group_kernel(v_ref, *o_refs, depth, lane_tiles, real_cols):
    cb = o_refs[0].shape[1]
    stride = lane_tiles * depth
    for l in range(depth):
        wt = jnp.concatenate([v_ref[pl.ds(l + depth * j, cb, stride=stride), :] for j in range(lane_tiles)], axis=1)
        if real_cols < cb:
            wt = jnp.where(lax.broadcasted_iota(jnp.int32, wt.shape, 0) < real_cols, wt, 0.0)
        hi = wt.astype(BF16)
        o_refs[0][l] = hi
        if len(o_refs) > 1:
            o_refs[1][l] = (wt - hi.astype(F32)).astype(BF16)


def _pack_group(view, depth, d, first_col, n_cols, cols_per_step, real_cols=None, hi_lo=False):
    lane_tiles = d // LANES
    rows_per_col = lane_tiles * depth
    n_out = 2 if hi_lo else 1
    return pl.pallas_call(
        functools.partial(_pack_group_kernel, depth=depth, lane_tiles=lane_tiles,
                          real_cols=cols_per_step if real_cols is None else real_cols),
        grid=(n_cols // cols_per_step,),
        in_specs=[pl.BlockSpec((pl.Element(cols_per_step * rows_per_col), pl.Element(LANES)),
                               lambda i: ((first_col + i * cols_per_step) * rows_per_col, 0))],
        out_specs=[pl.BlockSpec((depth, cols_per_step, d), lambda i: (0, i, 0))] * n_out,
        out_shape=[jax.ShapeDtypeStruct((depth, n_cols, d), BF16)] * n_out,
        compiler_params=_params(1),
        name="pack_w_in",
    )(view)


def _pack_w_in(w_in):
    depth, d, n = w_in.shape
    lane_tiles = d // LANES
    view = w_in.transpose(2, 0, 1).reshape(n, depth, lane_tiles, LANES).transpose(0, 2, 1, 3).reshape(
        n * lane_tiles * depth, LANES)
    group = functools.partial(_pack_group, view, depth, d)
    wa, = group(IN_OFF[0], IN_OFF[3] - IN_OFF[0], PACK_COLS)
    wi_hi, wi_lo = group(IN_OFF[3], IDX_COLS_PADDED, IDX_COLS_PADDED, real_cols=IDX_COLS, hi_lo=True)
    wh, = group(IN_OFF[6], IN_OFF[10] - IN_OFF[6], PACK_COLS)
    wg, = group(IN_OFF[10], IN_OFF[12] - IN_OFF[10], PACK_COLS)
    return wa, wi_hi, wi_lo, wh, wg


def _constants():
    head_of_lane = np.arange(ATTN_WIDTH) // HEAD_DIM
    hsum = (head_of_lane[:, None] == np.arange(LANES)[None, :]).astype(np.float32)
    hexp = hsum.T.copy()
    r = np.arange(KEY_BLOCK)
    ltri_attn = (r[None, :] < r[:, None]).astype(np.float32)
    r = np.arange(HG_CHUNK)
    ltri_hgrn = (r[None, :] <= r[:, None]).astype(np.float32)
    return jnp.asarray(hsum, BF16), jnp.asarray(hexp, BF16), jnp.asarray(ltri_attn, BF16), jnp.asarray(ltri_hgrn, BF16)


def kernel(x, positions, w_in, w_proj_attn, w_proj_hgrn, w_out, norm_mix, norm_ffn, q_norm, k_norm, hgrn_norm,
           hgrn_lower_bound, w_ffn_in, w_ffn_out):
    b, t, d = x.shape
    depth = w_in.shape[0]
    nt = b * t

    inv = ROPE_THETA ** (-jnp.arange(0, HEAD_DIM, 2, dtype=F32) / HEAD_DIM)
    ang = positions.astype(F32)[..., None] * inv
    cos, sin = jnp.cos(ang), jnp.sin(ang)
    cos1 = jnp.concatenate([cos, cos, cos, cos], axis=-1).reshape(nt, LANES)
    sin1 = jnp.concatenate([-sin, sin, -sin, sin], axis=-1).reshape(nt, LANES)

    lb_all = jnp.cumsum(jax.nn.softmax(hgrn_lower_bound.astype(F32), axis=0), axis=0)
    lb_all = lb_all - lb_all[:1]

    consts = _constants()
    wa_all, wi_hi_all, wi_lo_all, wh_all, wg_all = _pack_w_in(w_in)
    wpa_all, wph_all, wout_all = w_proj_attn.astype(BF16), w_proj_hgrn.astype(BF16), w_out.astype(BF16)
    wffn_in_all, wffn_out_all = w_ffn_in.astype(BF16), w_ffn_out.astype(BF16)

    x2 = x.reshape(nt, d)
    for l in range(depth):
        lb = lb_all[l]
        hconst = jnp.zeros((SUBLANES, HG_WIDTH), F32)
        hconst = hconst.at[0].set(jnp.log(lb)).at[1].set(jnp.log1p(-lb)).at[2].set(1.0 - lb)
        hconst = hconst.at[3].set(jnp.tile(hgrn_norm[l], HG_HEADS))
        w = {
            "norm_mix": norm_mix[l].reshape(1, d),
            "norm_ffn": norm_ffn[l].reshape(1, d),
            "wa": wa_all,
            "wi": (wi_hi_all, wi_lo_all),
            "wh": wh_all,
            "wg": wg_all,
            "qn": jnp.tile(q_norm[l], ATTN_HEADS).reshape(1, ATTN_WIDTH),
            "kn": jnp.tile(k_norm[l], ATTN_HEADS).reshape(1, ATTN_WIDTH),
            "logit_bound": (HEAD_DIM ** 0.5 * LOG2_E * jnp.max(jnp.abs(q_norm[l])) * jnp.max(jnp.abs(k_norm[l]))
                            ).reshape(1).astype(F32),
            "hconst": hconst,
            "wpa": wpa_all,
            "wph": wph_all,
            "wout": wout_all,
            "wffn_in": wffn_in_all,
            "wffn_out": wffn_out_all,
        }
        x2 = _layer(x2, b, t, cos1, sin1, consts, w, l)
    return x2.reshape(b, t, d)
```
